```python
import math
import jax, jax.numpy as jnp
from jax import lax
import numpy as np

D_MODEL = 1024
BATCH = 2
SEQ = 16384
DEPTH = 2

N_META = 16
BLOCK_Q = 128
ROPE_THETA = 10000.0
RMS_EPS = 1e-6
DA_HEADS = 8
DA_HEAD_DIM = 64
DSA_HEADS = 16
DSA_HEAD_DIM = 64
IDX_HEADS = 8
IDX_DIM = 64
TOPK_MAX = 256
DSA_QKV = DSA_HEADS * DSA_HEAD_DIM
DSA_SPLITS = [DSA_QKV, 2 * DSA_QKV, 3 * DSA_QKV,
              3 * DSA_QKV + IDX_HEADS * IDX_DIM,
              3 * DSA_QKV + IDX_HEADS * IDX_DIM + IDX_DIM]
DSA_IN_WIDTH = DSA_SPLITS[-1] + IDX_HEADS
ROPE_DIM = 64
BIG_SCORE = 1e30
FFN_HIDDEN = 2816
CONV_WIDTH = 3

kernel_name = "hybrid_diffattn_dsa_convffn_meta"


def rms_norm(x, g):
    xf = x.astype(jnp.float32)
    y = xf * lax.rsqrt(jnp.mean(xf * xf, axis=-1, keepdims=True) + RMS_EPS)
    return (y * g.astype(jnp.float32)).astype(x.dtype)


def rope_tables(T, dim):
    inv = ROPE_THETA ** (-jnp.arange(0, dim, 2, dtype=jnp.float32) / dim)
    ang = jnp.arange(T, dtype=jnp.float32)[:, None] * inv[None, :]
    return jnp.cos(ang), jnp.sin(ang)


def apply_rope(x, cos, sin):
    shape = (1, cos.shape[0]) + (1,) * (x.ndim - 3) + (cos.shape[1],)
    c = cos.reshape(shape)
    s = sin.reshape(shape)
    x1, x2 = jnp.split(x.astype(jnp.float32), 2, axis=-1)
    return jnp.concatenate([x1 * c - x2 * s, x2 * c + x1 * s], axis=-1).astype(x.dtype)


def sweep_blocks(fn, T):
    out = lax.map(fn, jnp.arange(T // BLOCK_Q) * BLOCK_Q)
    out = jnp.moveaxis(out, 0, 1)
    return out.reshape((out.shape[0], T) + out.shape[3:])


def diff_attention(h, w_qkv, lq1, lk1, lq2, lk2, subln, w_o, cos, sin, lambda_init):
    B, T, _ = h.shape
    q, k, v = jnp.split(h @ w_qkv, 3, axis=-1)
    q = apply_rope(q.reshape(B, T, DA_HEADS, 2, DA_HEAD_DIM), cos, sin) * (DA_HEAD_DIM ** -0.5)
    k = apply_rope(k.reshape(B, T, DA_HEADS, 2, DA_HEAD_DIM), cos, sin)
    v = v.reshape(B, T, DA_HEADS, 2 * DA_HEAD_DIM)
    f32 = jnp.float32
    lam = (jnp.exp(jnp.sum(lq1.astype(f32) * lk1.astype(f32)))
           - jnp.exp(jnp.sum(lq2.astype(f32) * lk2.astype(f32))) + lambda_init)
    k_pos = jnp.arange(T)

    def block(start):
        q_pos = start + jnp.arange(BLOCK_Q)
        qb = lax.dynamic_slice_in_dim(q, start, BLOCK_Q, axis=1)
        s = jnp.einsum('bqhcd,bkhcd->bhcqk', qb, k).astype(f32)
        s = jnp.where(k_pos[None, :] <= q_pos[:, None], s, -jnp.inf)
        p = jax.nn.softmax(s, axis=-1)
        a = p[:, :, 0] - lam * p[:, :, 1]
        return jnp.einsum('bhqk,bkhe->bqhe', a.astype(v.dtype), v)

    o = sweep_blocks(block, T)
    o = rms_norm(o, subln) * (1.0 - lambda_init)
    return o.reshape(B, T, DA_HEADS * 2 * DA_HEAD_DIM) @ w_o


def dsa_attention(h, w_in, idx_k_norm, w_o, cos, sin, n_keys):
    B, T, _ = h.shape
    q, k, v, qi, ki, wi = jnp.split(h @ w_in, DSA_SPLITS, axis=-1)
    q = apply_rope(q.reshape(B, T, DSA_HEADS, DSA_HEAD_DIM), cos, sin) * (DSA_HEAD_DIM ** -0.5)
    k = apply_rope(k.reshape(B, T, DSA_HEADS, DSA_HEAD_DIM), cos, sin)
    v = v.reshape(B, T, DSA_HEADS, DSA_HEAD_DIM)
    qi = apply_rope(qi.reshape(B, T, IDX_HEADS, IDX_DIM), cos, sin)
    ki = apply_rope(rms_norm(ki, idx_k_norm), cos, sin)
    wi = wi * (IDX_HEADS ** -0.5)
    topk = min(TOPK_MAX, n_keys // 4)
    k_pos = jnp.arange(T)
    f32 = jnp.float32
    gather = jax.vmap(lambda arr, idx: arr[idx])

    def block(start):
        q_pos = start + jnp.arange(BLOCK_Q)
        qb = lax.dynamic_slice_in_dim(q, start, BLOCK_Q, axis=1)
        qib = lax.dynamic_slice_in_dim(qi, start, BLOCK_Q, axis=1)
        wb = lax.dynamic_slice_in_dim(wi, start, BLOCK_Q, axis=1)
        raw = jnp.einsum('bqhd,bkd->bqhk', qib, ki).astype(f32) * (IDX_DIM ** -0.5)
        score = jnp.einsum('bqhk,bqh->bqk', jax.nn.relu(raw), wb.astype(f32))
        causal = k_pos[None, :] <= q_pos[:, None]
        score = jnp.where(causal & (k_pos < N_META)[None, :], BIG_SCORE, score)
        score = jnp.where(causal, score, -jnp.inf)
        top_val, top_idx = lax.top_k(score, topk)
        valid = top_val > -jnp.inf
        ks = gather(k, top_idx)
        vs = gather(v, top_idx)
        s = jnp.einsum('bqhd,bqjhd->bhqj', qb, ks).astype(f32)
        s = jnp.where(valid[:, None], s, -jnp.inf)
        p = jax.nn.softmax(s, axis=-1)
        return jnp.einsum('bhqj,bqjhd->bqhd', p.astype(vs.dtype), vs)

    o = sweep_blocks(block, T)
    return o.reshape(B, T, DSA_QKV) @ w_o


def conv_ffn(h, w_up, conv_w, conv_b, w_down):
    u = h @ w_up
    T = u.shape[1]
    up = jnp.pad(u, ((0, 0), (CONV_WIDTH - 1, 0), (0, 0)))
    c = conv_b + sum(conv_w[j] * up[:, j:j + T] for j in range(CONV_WIDTH))
    g, val = jnp.split(c, 2, axis=-1)
    return (jax.nn.silu(g) * val) @ w_down


def setup_inputs(seed: int = 0) -> dict:
    key = jax.random.key(seed)
    ks = iter(jax.random.split(key, 32))
    n_a = (DEPTH + 1) // 2
    n_b = DEPTH // 2
    D, F = D_MODEL, FFN_HIDDEN

    def nrm(shape, scale):
        return jax.random.normal(next(ks), shape, jnp.float32) * scale

    def gain(shape):
        return 1.0 + nrm(shape, 0.02)

    return {
        "x": nrm((BATCH, SEQ, D), 1.0),
        "meta_tokens": nrm((N_META, D), 1.0),
        "da_norm": gain((n_a, D)),
        "da_w_qkv": nrm((n_a, D, 3 * D), D ** -0.5),
        "da_lambda_q1": nrm((n_a, DA_HEAD_DIM), 0.1),
        "da_lambda_k1": nrm((n_a, DA_HEAD_DIM), 0.1),
        "da_lambda_q2": nrm((n_a, DA_HEAD_DIM), 0.1),
        "da_lambda_k2": nrm((n_a, DA_HEAD_DIM), 0.1),
        "da_subln": gain((n_a, 2 * DA_HEAD_DIM)),
        "da_w_o": nrm((n_a, DA_HEADS * 2 * DA_HEAD_DIM, D), D ** -0.5),
        "dsa_norm": gain((n_b, D)),
        "dsa_w_in": nrm((n_b, D, DSA_IN_WIDTH), D ** -0.5),
        "dsa_idx_k_norm": gain((n_b, IDX_DIM)),
        "dsa_w_o": nrm((n_b, DSA_QKV, D), DSA_QKV ** -0.5),
        "ffn_norm": gain((DEPTH, D)),
        "ffn_w_up": nrm((DEPTH, D, 2 * F), D ** -0.5),
        "ffn_conv_w": nrm((DEPTH, CONV_WIDTH, 2 * F), CONV_WIDTH ** -0.5),
        "ffn_conv_b": nrm((DEPTH, 2 * F), 0.02),
        "ffn_w_down": nrm((DEPTH, F, D), F ** -0.5),
        "final_norm": gain((D,)),
    }


def reference(x, meta_tokens, da_norm, da_w_qkv, da_lambda_q1, da_lambda_k1,
              da_lambda_q2, da_lambda_k2, da_subln, da_w_o, dsa_norm, dsa_w_in,
              dsa_idx_k_norm, dsa_w_o, ffn_norm, ffn_w_up, ffn_conv_w, ffn_conv_b,
              ffn_w_down, final_norm):
    B, L, D = x.shape
    T = L + N_META
    T_pad = ((T + BLOCK_Q - 1) // BLOCK_Q) * BLOCK_Q
    meta = jnp.broadcast_to(meta_tokens[None].astype(x.dtype), (B, N_META, D))
    h = jnp.concatenate([meta, x], axis=1)
    h = jnp.pad(h, ((0, 0), (0, T_pad - T), (0, 0)))
    cos, sin = rope_tables(T_pad, ROPE_DIM)
    for i in range(DEPTH):
        j = i // 2
        if i % 2 == 0:
            lambda_init = 0.8 - 0.6 * math.exp(-0.3 * i)
            h = h + diff_attention(rms_norm(h, da_norm[j]), da_w_qkv[j],
                                   da_lambda_q1[j], da_lambda_k1[j],
                                   da_lambda_q2[j], da_lambda_k2[j],
                                   da_subln[j], da_w_o[j], cos, sin, lambda_init)
        else:
            h = h + dsa_attention(rms_norm(h, dsa_norm[j]), dsa_w_in[j],
                                  dsa_idx_k_norm[j], dsa_w_o[j], cos, sin, L)
        h = h + conv_ffn(rms_norm(h, ffn_norm[i]), ffn_w_up[i], ffn_conv_w[i],
                         ffn_conv_b[i], ffn_w_down[i])
    h = rms_norm(h, final_norm)
    return h[:, N_META:N_META + L]
```

```python
import functools
import math

import numpy as np
import jax
import jax.numpy as jnp
from jax import lax
from jax.experimental import pallas as pl
from jax.experimental.pallas import tpu as pltpu

D_MODEL = 1024
N_META = 16
ROPE_THETA = 10000.0
RMS_EPS = 1e-6
HEAD_DIM = 64
N_PAIRS = 8
IDX_HEADS = 8
IDX_PAIRS = IDX_HEADS // 2
TOPK_MAX = 256
BIG_SCORE = 1e30
FFN_HIDDEN = 2816
CONV_WIDTH = 3

LANES = 128
ROW_TILE = 512
Q_TILE = 256
K_TILE = 512
FFN_TILE = 256
HALO = 16
VMEM_LIMIT = 56 * 1024 * 1024

F32 = jnp.float32
BF16 = jnp.bfloat16
INT_MIN = -2147483648
KEY_NEG_INF = INT_MIN + 0x7FFFFF


def _padded_len(t):
    return ((t + ROW_TILE - 1) // ROW_TILE) * ROW_TILE


def _rms(x, gain):
    ms = jnp.mean(x * x, axis=-1, keepdims=True)
    return x * lax.rsqrt(ms + RMS_EPS) * gain


def _rope_lanes(y, cos, sin, lo):
    sw = jnp.where(lo, pltpu.roll(y, LANES - HEAD_DIM // 2, 1), pltpu.roll(y, HEAD_DIM // 2, 1))
    return y * cos + sw * sin


def _proj_kernel(x_ref, g_ref, w_ref, cos_ref, sin_ref, *rest, kinds, has_tail):
    if has_tail:
        wt_ref, gk_ref, o_ref, t_ref = rest
    else:
        (o_ref,) = rest
    xh = _rms(x_ref[...], g_ref[...]).astype(BF16)
    cos = cos_ref[...]
    sin = sin_ref[...]
    lane = lax.broadcasted_iota(jnp.int32, cos.shape, 1)
    lo = (lane & (HEAD_DIM // 2)) == 0
    n_groups = len(kinds)
    cw = 4 * LANES
    for c in range(n_groups // 4):
        y = jnp.dot(xh, w_ref[:, c * cw:(c + 1) * cw], preferred_element_type=F32)
        for s in range(4):
            g = c * 4 + s
            yg = y[:, s * LANES:(s + 1) * LANES]
            if kinds[g] != "plain":
                yg = _rope_lanes(yg, cos, sin, lo)
                if kinds[g] == "rope_scaled":
                    yg = yg * (HEAD_DIM ** -0.5)
            o_ref[g] = yg.astype(BF16)
    if has_tail:
        t = jnp.dot(xh, wt_ref[...], preferred_element_type=F32)
        is_k = lane < HEAD_DIM
        ms = jnp.sum(jnp.where(is_k, t * t, 0.0), axis=-1, keepdims=True) * (1.0 / HEAD_DIM)
        kn = t * lax.rsqrt(ms + RMS_EPS) * gk_ref[...]
        kn = _rope_lanes(kn, cos, sin, lo)
        t_ref[...] = jnp.where(is_k, kn, t * (IDX_HEADS ** -0.5))


def _project(h, gain, w, cos, sin, kinds, t_pad, w_tail=None, gk=None):
    rows = h.shape[0]
    n_groups = len(kinds)
    blocks_per_batch = t_pad // ROW_TILE
    has_tail = w_tail is not None
    in_specs = [
        pl.BlockSpec((ROW_TILE, D_MODEL), lambda i: (i, 0)),
        pl.BlockSpec((1, D_MODEL), lambda i: (0, 0)),
        pl.BlockSpec(w.shape, lambda i: (0, 0)),
        pl.BlockSpec((ROW_TILE, LANES), lambda i: (i % blocks_per_batch, 0)),
        pl.BlockSpec((ROW_TILE, LANES), lambda i: (i % blocks_per_batch, 0)),
    ]
    out_shape = [jax.ShapeDtypeStruct((n_groups, rows, LANES), BF16)]
    out_specs = [pl.BlockSpec((n_groups, ROW_TILE, LANES), lambda i: (0, i, 0))]
    args = [h, gain.reshape(1, D_MODEL), w, cos, sin]
    if has_tail:
        in_specs += [pl.BlockSpec(w_tail.shape, lambda i: (0, 0)),
                     pl.BlockSpec((1, LANES), lambda i: (0, 0))]
        out_shape.append(jax.ShapeDtypeStruct((rows, LANES), F32))
        out_specs.append(pl.BlockSpec((ROW_TILE, LANES), lambda i: (i, 0)))
        args += [w_tail, gk]
    return pl.pallas_call(
        functools.partial(_proj_kernel, kinds=tuple(kinds), has_tail=has_tail),
        grid=(rows // ROW_TILE,),
        in_specs=in_specs,
        out_specs=out_specs if has_tail else out_specs[0],
        out_shape=out_shape if has_tail else out_shape[0],
        compiler_params=pltpu.CompilerParams(dimension_semantics=("arbitrary",),
                                             vmem_limit_bytes=VMEM_LIMIT),
        name="proj_tail" if has_tail else "proj",
    )(*args)


def _oproj_kernel(h_ref, o_ref, w_ref, out_ref):
    o = jnp.concatenate([o_ref[j] for j in range(N_PAIRS)], axis=1)
    out_ref[...] = h_ref[...] + jnp.dot(o, w_ref[...], preferred_element_type=F32)


def _out_project(h, o, w):
    rows = h.shape[0]
    return pl.pallas_call(
        _oproj_kernel,
        grid=(rows // ROW_TILE,),
        in_specs=[pl.BlockSpec((ROW_TILE, D_MODEL), lambda i: (i, 0)),
                  pl.BlockSpec((N_PAIRS, ROW_TILE, LANES), lambda i: (0, i, 0)),
                  pl.BlockSpec(w.shape, lambda i: (0, 0))],
        out_specs=pl.BlockSpec((ROW_TILE, D_MODEL), lambda i: (i, 0)),
        out_shape=jax.ShapeDtypeStruct(h.shape, F32),
        compiler_params=pltpu.CompilerParams(dimension_semantics=("arbitrary",),
                                             vmem_limit_bytes=VMEM_LIMIT),
        name="oproj",
    )(h, o, w)


def _ffn_kernel(xp_ref, x_ref, g_ref, wup_ref, cw_ref, cb_ref, wdn_ref, fg_ref, o_ref, u_sc,
                *, blocks_per_batch, final):
    i = pl.program_id(0)
    x = x_ref[...]
    xe = jnp.concatenate([xp_ref[...], x], axis=0)
    xh = _rms(xe, g_ref[...])
    row = lax.broadcasted_iota(jnp.int32, (HALO + ROW_TILE, 1), 0)
    keep = jnp.logical_or(row >= HALO, i % blocks_per_batch != 0)
    xh = jnp.where(keep, xh, 0.0).astype(BF16)
    acc = jnp.zeros((ROW_TILE, D_MODEL), F32)
    cw2 = 2 * FFN_TILE
    for c in range(FFN_HIDDEN // FFN_TILE):
        u_sc[...] = jnp.dot(xh, wup_ref[:, c * cw2:(c + 1) * cw2], preferred_element_type=F32)
        cwc = cw_ref[:, c * cw2:(c + 1) * cw2]
        conv = cb_ref[:, c * cw2:(c + 1) * cw2]
        for j in range(CONV_WIDTH):
            conv = conv + cwc[j:j + 1, :] * u_sc[pl.ds(HALO - (CONV_WIDTH - 1) + j, ROW_TILE), :]
        gate = conv[:, :FFN_TILE]
        val = conv[:, FFN_TILE:]
        a = (gate * jax.nn.sigmoid(gate) * val).astype(BF16)
        acc = acc + jnp.dot(a, wdn_ref[c * FFN_TILE:(c + 1) * FFN_TILE, :], preferred_element_type=F32)
    y = x + acc
    if final:
        y = _rms(y, fg_ref[...])
    o_ref[...] = y


def _ffn(h, gain, w_up, conv_w, conv_b, w_down, final_gain, t_pad, final):
    rows = h.shape[0]
    blocks_per_batch = t_pad // ROW_TILE
    halo_blocks = ROW_TILE // HALO
    return pl.pallas_call(
        functools.partial(_ffn_kernel, blocks_per_batch=blocks_per_batch, final=final),
        grid=(rows // ROW_TILE,),
        in_specs=[pl.BlockSpec((HALO, D_MODEL), lambda i: (jnp.maximum(i * halo_blocks - 1, 0), 0)),
                  pl.BlockSpec((ROW_TILE, D_MODEL), lambda i: (i, 0)),
                  pl.BlockSpec((1, D_MODEL), lambda i: (0, 0)),
                  pl.BlockSpec(w_up.shape, lambda i: (0, 0)),
                  pl.BlockSpec(conv_w.shape, lambda i: (0, 0)),
                  pl.BlockSpec(conv_b.shape, lambda i: (0, 0)),
                  pl.BlockSpec(w_down.shape, lambda i: (0, 0)),
                  pl.BlockSpec((1, D_MODEL), lambda i: (0, 0))],
        out_specs=pl.BlockSpec((ROW_TILE, D_MODEL), lambda i: (i, 0)),
        out_shape=jax.ShapeDtypeStruct(h.shape, F32),
        scratch_shapes=[pltpu.VMEM((HALO + ROW_TILE, 2 * FFN_TILE), F32)],
        compiler_params=pltpu.CompilerParams(dimension_semantics=("arbitrary",),
                                             vmem_limit_bytes=VMEM_LIMIT),
        name="ffn_final" if final else "ffn",
    )(h, h, gain.reshape(1, D_MODEL), w_up, conv_w, conv_b, w_down, final_gain.reshape(1, D_MODEL))


def _attention_steps(batch, t_pad):
    nq = t_pad // Q_TILE
    nk = t_pad // K_TILE
    cols = {k: [] for k in ("qrow", "krow", "qloc", "kloc", "nkb", "first", "last", "b")}
    for b in range(batch):
        for qi in range(nq):
            n = ((qi + 1) * Q_TILE + K_TILE - 1) // K_TILE
            for ki in range(n):
                cols["qrow"].append(b * nq + qi)
                cols["krow"].append(b * nk + ki)
                cols["qloc"].append(qi)
                cols["kloc"].append(ki)
                cols["nkb"].append(n)
                cols["first"].append(int(ki == 0))
                cols["last"].append(int(ki == n - 1))
                cols["b"].append(b)
    return {k: jnp.asarray(np.asarray(v, np.int32)) for k, v in cols.items()}


def _online_softmax_update(s, v, h, m_sc, l_sc):
    m_prev = m_sc[h]
    m_new = jnp.maximum(m_prev, jnp.max(s, axis=1, keepdims=True))
    alpha = jnp.exp(m_prev - m_new)
    p = jnp.exp(s - m_new)
    l_sc[h] = alpha * l_sc[h] + jnp.sum(p, axis=1, keepdims=True)
    m_sc[h] = m_new
    return alpha, jnp.dot(p.astype(BF16), v, preferred_element_type=F32)


def _nt_dot(a, b):
    return lax.dot_general(a, b, (((1,), (1,)), ((), ())), preferred_element_type=F32)


def _diff_attn_kernel(qrow, krow, qloc, kloc, first, last,
                      q_ref, k_ref, v_ref, lam_ref, subln_ref, o_ref, m_sc, l_sc, acc_sc,
                      *, lambda_init):
    p = pl.program_id(0)

    @pl.when(first[p] == 1)
    def _():
        m_sc[...] = jnp.full(m_sc.shape, -jnp.inf, F32)
        l_sc[...] = jnp.zeros(l_sc.shape, F32)
        acc_sc[...] = jnp.zeros(acc_sc.shape, F32)

    lane = lax.broadcasted_iota(jnp.int32, (Q_TILE, LANES), 1)
    lo = lane < HEAD_DIM
    qpos = qloc[p] * Q_TILE + lax.broadcasted_iota(jnp.int32, (Q_TILE, K_TILE), 0)
    kpos = kloc[p] * K_TILE + lax.broadcasted_iota(jnp.int32, (Q_TILE, K_TILE), 1)
    bias = jnp.where(kpos <= qpos, 0.0, -jnp.inf).astype(F32)

    def pair(j, carry):
        q = q_ref[j]
        k = k_ref[j]
        v = v_ref[j]
        for c in range(2):
            qc = jnp.where(lo if c == 0 else jnp.logical_not(lo), q, jnp.zeros_like(q))
            s = _nt_dot(qc, k) + bias
            h = 2 * j + c
            alpha, pv = _online_softmax_update(s, v, h, m_sc, l_sc)
            acc_sc[h] = alpha * acc_sc[h] + pv
        return carry

    lax.fori_loop(0, N_PAIRS, pair, 0)

    @pl.when(last[p] == 1)
    def _():
        lam_rows = lam_ref[...]
        lam = (jnp.exp(jnp.sum(lam_rows[0:1] * lam_rows[1:2], axis=-1, keepdims=True))
               - jnp.exp(jnp.sum(lam_rows[2:3] * lam_rows[3:4], axis=-1, keepdims=True))
               + lambda_init)
        for j in range(N_PAIRS):
            o1 = acc_sc[2 * j] / l_sc[2 * j]
            o2 = acc_sc[2 * j + 1] / l_sc[2 * j + 1]
            o = _rms(o1 - lam * o2, subln_ref[...]) * (1.0 - lambda_init)
            o_ref[j] = o.astype(BF16)


def _diff_attention(qkv, lam_rows, subln, steps, lambda_init):
    rows = qkv.shape[1]
    n_steps = steps["qrow"].shape[0]
    grid_spec = pltpu.PrefetchScalarGridSpec(
        num_scalar_prefetch=6,
        grid=(n_steps,),
        in_specs=[
            pl.BlockSpec((N_PAIRS, Q_TILE, LANES), lambda p, qr, kr, ql, kl, f, l: (0, qr[p], 0)),
            pl.BlockSpec((N_PAIRS, K_TILE, LANES), lambda p, qr, kr, ql, kl, f, l: (1, kr[p], 0)),
            pl.BlockSpec((N_PAIRS, K_TILE, LANES), lambda p, qr, kr, ql, kl, f, l: (2, kr[p], 0)),
            pl.BlockSpec((8, LANES), lambda p, qr, kr, ql, kl, f, l: (0, 0)),
            pl.BlockSpec((1, LANES), lambda p, qr, kr, ql, kl, f, l: (0, 0)),
        ],
        out_specs=pl.BlockSpec((N_PAIRS, Q_TILE, LANES), lambda p, qr, kr, ql, kl, f, l: (0, qr[p], 0)),
        scratch_shapes=[pltpu.VMEM((2 * N_PAIRS, Q_TILE, 1), F32),
                        pltpu.VMEM((2 * N_PAIRS, Q_TILE, 1), F32),
                        pltpu.VMEM((2 * N_PAIRS, Q_TILE, LANES), F32)],
    )
    return pl.pallas_call(
        functools.partial(_diff_attn_kernel, lambda_init=lambda_init),
        grid_spec=grid_spec,
        out_shape=jax.ShapeDtypeStruct((N_PAIRS, rows, LANES), BF16),
        compiler_params=pltpu.CompilerParams(dimension_semantics=("arbitrary",),
                                             vmem_limit_bytes=VMEM_LIMIT),
        name="diff_attn",
    )(steps["qrow"], steps["krow"], steps["qloc"], steps["kloc"], steps["first"], steps["last"],
      qkv, qkv, qkv, lam_rows, subln)


BISECT_ROWS = 64


def _dsa_kernel(qrow, krow, qloc, kloc, nkb, first, last, bidx,
                q_ref, k_ref, v_ref, qi_ref, wi_ref, kit_ref, o_ref,
                key_sc, thr_sc, m_sc, l_sc, acc_sc, wb_sc, *, topk):
    p = pl.program_id(0)
    lane = lax.broadcasted_iota(jnp.int32, (Q_TILE, LANES), 1)
    lo = lane < HEAD_DIM

    @pl.when(first[p] == 1)
    def _():
        m_sc[...] = jnp.full(m_sc.shape, -jnp.inf, F32)
        l_sc[...] = jnp.zeros(l_sc.shape, F32)
        acc_sc[...] = jnp.zeros(acc_sc.shape, F32)
        wi = wi_ref[...]
        for h in range(IDX_HEADS):
            wb_sc[h] = jnp.broadcast_to(wi[:, HEAD_DIM + h:HEAD_DIM + h + 1], (Q_TILE, K_TILE))
        n = nkb[p]
        qpos = qloc[p] * Q_TILE + lax.broadcasted_iota(jnp.int32, (Q_TILE, K_TILE), 0)
        kcol = lax.broadcasted_iota(jnp.int32, (Q_TILE, K_TILE), 1)

        def score_chunk(c, carry):
            kt = kit_ref[0, c]
            sc = jnp.zeros((Q_TILE, K_TILE), F32)
            for jp in range(IDX_PAIRS):
                qi = qi_ref[jp]
                for cc in range(2):
                    qh = jnp.where(lo if cc == 0 else jnp.logical_not(lo), qi, jnp.zeros_like(qi))
                    raw = jnp.dot(qh, kt, preferred_element_type=F32)
                    sc = sc + jnp.maximum(raw, 0.0) * wb_sc[2 * jp + cc]
            kpos = c * K_TILE + kcol
            sc = jnp.where(kpos < N_META, BIG_SCORE, sc)
            sc = jnp.where(kpos <= qpos, sc, -jnp.inf)
            bits = lax.bitcast_convert_type(sc, jnp.int32)
            key_sc[c] = bits ^ ((bits >> 31) & 0x7FFFFFFF)
            return carry

        lax.fori_loop(0, n, score_chunk, 0)

        for r in range(Q_TILE // BISECT_ROWS):
            r0 = r * BISECT_ROWS

            def bit_body(b, thr):
                cand = thr + jnp.left_shift(jnp.int32(1), 31 - b)

                def count_chunk(c, cnt):
                    x = key_sc[c, r0:r0 + BISECT_ROWS, :]
                    for g in range(K_TILE // LANES):
                        cnt = cnt + jnp.where(x[:, g * LANES:(g + 1) * LANES] >= cand, 1.0, 0.0)
                    return cnt

                cnt = lax.fori_loop(0, n, count_chunk, jnp.zeros((BISECT_ROWS, LANES), F32))
                tot = jnp.sum(cnt, axis=1, keepdims=True)
                return jnp.where(tot >= float(topk), cand, thr)

            thr = lax.fori_loop(0, 32, bit_body, jnp.full((BISECT_ROWS, 1), INT_MIN, jnp.int32))
            thr_sc[r0:r0 + BISECT_ROWS, :] = jnp.maximum(thr, KEY_NEG_INF + 1)

    bias = jnp.where(key_sc[kloc[p]] >= thr_sc[...], 0.0, -jnp.inf).astype(F32)

    def pair(j, carry):
        q = q_ref[j]
        k = k_ref[j]
        v = v_ref[j]
        upd = []
        for c in range(2):
            qc = jnp.where(lo if c == 0 else jnp.logical_not(lo), q, jnp.zeros_like(q))
            s = _nt_dot(qc, k) + bias
            upd.append(_online_softmax_update(s, v, 2 * j + c, m_sc, l_sc))
        alpha = jnp.where(lo, upd[0][0], upd[1][0])
        acc_sc[j] = alpha * acc_sc[j] + jnp.where(lo, upd[0][1], upd[1][1])
        return carry

    lax.fori_loop(0, N_PAIRS, pair, 0)

    @pl.when(last[p] == 1)
    def _():
        for j in range(N_PAIRS):
            l = jnp.where(lo, l_sc[2 * j], l_sc[2 * j + 1])
            o_ref[j] = (acc_sc[j] / l).astype(BF16)


def _dsa_attention(qkvi, tail, kit, steps, topk, t_pad):
    rows = qkvi.shape[1]
    n_steps = steps["qrow"].shape[0]
    nkb_total = t_pad // K_TILE

    def im(fn):
        return lambda p, qr, kr, ql, kl, nk, f, l, b: fn(p, qr, kr, b)

    grid_spec = pltpu.PrefetchScalarGridSpec(
        num_scalar_prefetch=8,
        grid=(n_steps,),
        in_specs=[
            pl.BlockSpec((N_PAIRS, Q_TILE, LANES), im(lambda p, qr, kr, b: (0, qr[p], 0))),
            pl.BlockSpec((N_PAIRS, K_TILE, LANES), im(lambda p, qr, kr, b: (1, kr[p], 0))),
            pl.BlockSpec((N_PAIRS, K_TILE, LANES), im(lambda p, qr, kr, b: (2, kr[p], 0))),
            pl.BlockSpec((IDX_PAIRS, Q_TILE, LANES), im(lambda p, qr, kr, b: (3 * N_PAIRS // IDX_PAIRS, qr[p], 0))),
            pl.BlockSpec((Q_TILE, LANES), im(lambda p, qr, kr, b: (qr[p], 0))),
            pl.BlockSpec((1, nkb_total, LANES, K_TILE), im(lambda p, qr, kr, b: (b[p], 0, 0, 0))),
        ],
        out_specs=pl.BlockSpec((N_PAIRS, Q_TILE, LANES), im(lambda p, qr, kr, b: (0, qr[p], 0))),
        scratch_shapes=[pltpu.VMEM((nkb_total, Q_TILE, K_TILE), jnp.int32),
                        pltpu.VMEM((Q_TILE, 1), jnp.int32),
                        pltpu.VMEM((2 * N_PAIRS, Q_TILE, 1), F32),
                        pltpu.VMEM((2 * N_PAIRS, Q_TILE, 1), F32),
                        pltpu.VMEM((N_PAIRS, Q_TILE, LANES), F32),
                        pltpu.VMEM((IDX_HEADS, Q_TILE, K_TILE), F32)],
    )
    return pl.pallas_call(
        functools.partial(_dsa_kernel, topk=topk),
        grid_spec=grid_spec,
        out_shape=jax.ShapeDtypeStruct((N_PAIRS, rows, LANES), BF16),
        compiler_params=pltpu.CompilerParams(dimension_semantics=("arbitrary",),
                                             vmem_limit_bytes=VMEM_LIMIT),
        name="dsa_attn",
    )(steps["qrow"], steps["krow"], steps["qloc"], steps["kloc"], steps["nkb"], steps["first"],
      steps["last"], steps["b"], qkvi, qkvi, qkvi, qkvi, tail, kit)


def _rope_tables(t_pad):
    inv = ROPE_THETA ** (-jnp.arange(0, HEAD_DIM, 2, dtype=F32) / HEAD_DIM)
    ang = jnp.arange(t_pad, dtype=F32)[:, None] * inv[None, :]
    cos, sin = jnp.cos(ang), jnp.sin(ang)
    cos128 = jnp.tile(cos, (1, LANES // (HEAD_DIM // 2)))
    sin128 = jnp.tile(jnp.concatenate([-sin, sin], axis=1), (1, LANES // HEAD_DIM))
    return cos128, sin128


def _interleave_gate_val(a):
    lead = a.shape[:-1]
    a = a.reshape(lead + (2, FFN_HIDDEN // FFN_TILE, FFN_TILE))
    a = jnp.swapaxes(a, -3, -2)
    return a.reshape(lead + (2 * FFN_HIDDEN,))


def kernel(x, meta_tokens, da_norm, da_w_qkv, da_lambda_q1, da_lambda_k1, da_lambda_q2, da_lambda_k2, da_subln, da_w_o, dsa_norm, dsa_w_in, dsa_idx_k_norm, dsa_w_o, ffn_norm, ffn_w_up, ffn_conv_w, ffn_conv_b, ffn_w_down, final_norm):
    batch, seq, d = x.shape
    assert d == D_MODEL
    depth = ffn_norm.shape[0]
    t_real = seq + N_META
    t_pad = _padded_len(t_real)
    assert t_pad % K_TILE == 0 and t_pad % Q_TILE == 0

    meta = jnp.broadcast_to(meta_tokens[None].astype(x.dtype), (batch, N_META, d))
    h = jnp.concatenate([meta, x, jnp.zeros((batch, t_pad - t_real, d), x.dtype)], axis=1)
    h = h.reshape(batch * t_pad, d)
    cos128, sin128 = _rope_tables(t_pad)
    steps = _attention_steps(batch, t_pad)

    dsa_qkv = 16 * HEAD_DIM
    idx_cols = IDX_HEADS * HEAD_DIM
    for i in range(depth):
        j = i // 2
        if i % 2 == 0:
            lambda_init = 0.8 - 0.6 * math.exp(-0.3 * i)
            kinds = ["rope_scaled"] * N_PAIRS + ["rope"] * N_PAIRS + ["plain"] * N_PAIRS
            qkv = _project(h, da_norm[j], da_w_qkv[j].astype(BF16), cos128, sin128, kinds, t_pad)
            lam_rows = jnp.zeros((8, LANES), F32).at[0:4, 0:HEAD_DIM].set(
                jnp.stack([da_lambda_q1[j], da_lambda_k1[j], da_lambda_q2[j], da_lambda_k2[j]]).astype(F32))
            o = _diff_attention(qkv, lam_rows, da_subln[j].reshape(1, LANES).astype(F32), steps, lambda_init)
            h = _out_project(h, o, da_w_o[j].astype(BF16))
        else:
            w_in = dsa_w_in[j]
            n_main = 3 * dsa_qkv + idx_cols
            kinds = (["rope_scaled"] * N_PAIRS + ["rope"] * N_PAIRS + ["plain"] * N_PAIRS
                     + ["rope_scaled"] * IDX_PAIRS)
            w_tail = jnp.zeros((d, LANES), w_in.dtype).at[:, :w_in.shape[1] - n_main].set(w_in[:, n_main:])
            gk = jnp.ones((1, LANES), F32).at[0, :HEAD_DIM].set(dsa_idx_k_norm[j].astype(F32))
            qkvi, tail = _project(h, dsa_norm[j], w_in[:, :n_main].astype(BF16), cos128, sin128, kinds, t_pad,
                                  w_tail=w_tail.astype(BF16), gk=gk)
            ki = tail[:, :HEAD_DIM].astype(BF16).reshape(batch, t_pad // K_TILE, K_TILE, HEAD_DIM)
            kit = jnp.swapaxes(ki, 2, 3)
            kit = jnp.concatenate([kit, kit], axis=2)
            topk = min(TOPK_MAX, seq // 4)
            o = _dsa_attention(qkvi, tail, kit, steps, topk, t_pad)
            h = _out_project(h, o, dsa_w_o[j].astype(BF16))
        h = _ffn(h, ffn_norm[i], _interleave_gate_val(ffn_w_up[i]).astype(BF16),
                 _interleave_gate_val(ffn_conv_w[i]).astype(F32),
                 _interleave_gate_val(ffn_conv_b[i]).reshape(1, -1).astype(F32),
                 ffn_w_down[i].astype(BF16), final_norm, t_pad, final=(i == depth - 1))
    return h.reshape(batch, t_pad, d)[:, N_META:N_META + seq]
```

```python
import functools
import math

import numpy as np
import jax
import jax.numpy as jnp
from jax import lax
from jax.experimental import pallas as pl
from jax.experimental.pallas import tpu as pltpu

D_MODEL = 1024
N_META = 16
ROPE_THETA = 10000.0
RMS_EPS = 1e-6
HEAD_DIM = 64
N_PAIRS = 8
N_HEADS = 2 * N_PAIRS
IDX_HEADS = 8
IDX_PAIRS = IDX_HEADS // 2
TOPK_MAX = 256
BIG_SCORE = 1e30
LOG2_E = 1.4426950408889634
FFN_HIDDEN = 2816
CONV_WIDTH = 3

LANES = 128
BF16_ROWS = 16
ROW_TILE = 512
Q_TILE = 256
K_TILE = 512
FFN_TILE = 256
HALO = BF16_ROWS
VMEM_LIMIT = 56 * 1024 * 1024

F32 = jnp.float32
BF16 = jnp.bfloat16
INT_MIN = -2147483648
KEY_NEG_INF = INT_MIN + 0x7FFFFF


def _padded_len(t):
    return ((t + ROW_TILE - 1) // ROW_TILE) * ROW_TILE


def _rms(x, gain):
    ms = jnp.mean(x * x, axis=-1, keepdims=True)
    return x * lax.rsqrt(ms + RMS_EPS) * gain


def _rope_lanes(y, cos, sin, lo):
    sw = jnp.where(lo, pltpu.roll(y, LANES - HEAD_DIM // 2, 1), pltpu.roll(y, HEAD_DIM // 2, 1))
    return y * cos + sw * sin


def _proj_kernel(x_ref, g_ref, w_ref, cos_ref, sin_ref, *rest, kinds, has_tail):
    if has_tail:
        wt_ref, gk_ref, o_ref, t_ref = rest
    else:
        (o_ref,) = rest
    xh = _rms(x_ref[...], g_ref[...]).astype(BF16)
    cos = cos_ref[...]
    sin = sin_ref[...]
    lane = lax.broadcasted_iota(jnp.int32, cos.shape, 1)
    lo = (lane & (HEAD_DIM // 2)) == 0
    n_groups = len(kinds)
    cw = 4 * LANES
    for c in range(n_groups // 4):
        y = jnp.dot(xh, w_ref[:, c * cw:(c + 1) * cw], preferred_element_type=F32)
        for s in range(4):
            g = c * 4 + s
            yg = y[:, s * LANES:(s + 1) * LANES]
            if kinds[g] != "plain":
                yg = _rope_lanes(yg, cos, sin, lo)
                if kinds[g] == "rope_scaled":
                    yg = yg * (HEAD_DIM ** -0.5)
                elif kinds[g] == "rope_softmax_q":
                    yg = yg * (HEAD_DIM ** -0.5 * LOG2_E)
            o_ref[g] = yg.astype(BF16)
    if has_tail:
        t = jnp.dot(xh, wt_ref[...], preferred_element_type=F32)
        is_k = lane < HEAD_DIM
        ms = jnp.sum(jnp.where(is_k, t * t, 0.0), axis=-1, keepdims=True) * (1.0 / HEAD_DIM)
        kn = t * lax.rsqrt(ms + RMS_EPS) * gk_ref[...]
        kn = _rope_lanes(kn, cos, sin, lo)
        t_ref[...] = jnp.where(is_k, kn, t * (IDX_HEADS ** -0.5))


def _project(h, gain, w, cos, sin, kinds, t_pad, w_tail=None, gk=None):
    rows = h.shape[0]
    n_groups = len(kinds)
    blocks_per_batch = t_pad // ROW_TILE
    has_tail = w_tail is not None
    in_specs = [
        pl.BlockSpec((ROW_TILE, D_MODEL), lambda i: (i, 0)),
        pl.BlockSpec((1, D_MODEL), lambda i: (0, 0)),
        pl.BlockSpec(w.shape, lambda i: (0, 0)),
        pl.BlockSpec((ROW_TILE, LANES), lambda i: (i % blocks_per_batch, 0)),
        pl.BlockSpec((ROW_TILE, LANES), lambda i: (i % blocks_per_batch, 0)),
    ]
    out_shape = [jax.ShapeDtypeStruct((n_groups, rows, LANES), BF16)]
    out_specs = [pl.BlockSpec((n_groups, ROW_TILE, LANES), lambda i: (0, i, 0))]
    args = [h, gain.reshape(1, D_MODEL), w, cos, sin]
    if has_tail:
        in_specs += [pl.BlockSpec(w_tail.shape, lambda i: (0, 0)),
                     pl.BlockSpec((1, LANES), lambda i: (0, 0))]
        out_shape.append(jax.ShapeDtypeStruct((rows, LANES), F32))
        out_specs.append(pl.BlockSpec((ROW_TILE, LANES), lambda i: (i, 0)))
        args += [w_tail, gk]
    return pl.pallas_call(
        functools.partial(_proj_kernel, kinds=tuple(kinds), has_tail=has_tail),
        grid=(rows // ROW_TILE,),
        in_specs=in_specs,
        out_specs=out_specs if has_tail else out_specs[0],
        out_shape=out_shape if has_tail else out_shape[0],
        compiler_params=pltpu.CompilerParams(dimension_semantics=("arbitrary",),
                                             vmem_limit_bytes=VMEM_LIMIT),
        name="proj_tail" if has_tail else "proj",
    )(*args)


def _oproj_kernel(h_ref, o_ref, w_ref, out_ref):
    o = jnp.concatenate([o_ref[j] for j in range(N_PAIRS)], axis=1)
    out_ref[...] = h_ref[...] + jnp.dot(o, w_ref[...], preferred_element_type=F32)


def _out_project(h, o, w):
    rows = h.shape[0]
    return pl.pallas_call(
        _oproj_kernel,
        grid=(rows // ROW_TILE,),
        in_specs=[pl.BlockSpec((ROW_TILE, D_MODEL), lambda i: (i, 0)),
                  pl.BlockSpec((N_PAIRS, ROW_TILE, LANES), lambda i: (0, i, 0)),
                  pl.BlockSpec(w.shape, lambda i: (0, 0))],
        out_specs=pl.BlockSpec((ROW_TILE, D_MODEL), lambda i: (i, 0)),
        out_shape=jax.ShapeDtypeStruct(h.shape, F32),
        compiler_params=pltpu.CompilerParams(dimension_semantics=("arbitrary",),
                                             vmem_limit_bytes=VMEM_LIMIT),
        name="oproj",
    )(h, o, w)


def _ffn_kernel(xp_ref, x_ref, g_ref, wup_ref, cw_ref, cb_ref, wdn_ref, fg_ref, o_ref, u_sc,
                *, blocks_per_batch, final):
    i = pl.program_id(0)
    x = x_ref[...]
    xe = jnp.concatenate([xp_ref[...], x], axis=0)
    xh = _rms(xe, g_ref[...])
    row = lax.broadcasted_iota(jnp.int32, (HALO + ROW_TILE, 1), 0)
    keep = jnp.logical_or(row >= HALO, i % blocks_per_batch != 0)
    xh = jnp.where(keep, xh, 0.0).astype(BF16)
    acc = jnp.zeros((ROW_TILE, D_MODEL), F32)
    cw2 = 2 * FFN_TILE
    for c in range(FFN_HIDDEN // FFN_TILE):
        u_sc[...] = jnp.dot(xh, wup_ref[:, c * cw2:(c + 1) * cw2], preferred_element_type=F32)
        cwc = cw_ref[:, c * cw2:(c + 1) * cw2]
        conv = cb_ref[:, c * cw2:(c + 1) * cw2]
        for j in range(CONV_WIDTH):
            conv = conv + cwc[j:j + 1, :] * u_sc[pl.ds(HALO - (CONV_WIDTH - 1) + j, ROW_TILE), :]
        gate = conv[:, :FFN_TILE]
        val = conv[:, FFN_TILE:]
        a = (gate * jax.nn.sigmoid(gate) * val).astype(BF16)
        acc = acc + jnp.dot(a, wdn_ref[c * FFN_TILE:(c + 1) * FFN_TILE, :], preferred_element_type=F32)
    y = x + acc
    if final:
        y = _rms(y, fg_ref[...])
    o_ref[...] = y


def _ffn(h, gain, w_up, conv_w, conv_b, w_down, final_gain, t_pad, final):
    rows = h.shape[0]
    blocks_per_batch = t_pad // ROW_TILE
    halo_blocks = ROW_TILE // HALO
    return pl.pallas_call(
        functools.partial(_ffn_kernel, blocks_per_batch=blocks_per_batch, final=final),
        grid=(rows // ROW_TILE,),
        in_specs=[pl.BlockSpec((HALO, D_MODEL), lambda i: (jnp.maximum(i * halo_blocks - 1, 0), 0)),
                  pl.BlockSpec((ROW_TILE, D_MODEL), lambda i: (i, 0)),
                  pl.BlockSpec((1, D_MODEL), lambda i: (0, 0)),
                  pl.BlockSpec(w_up.shape, lambda i: (0, 0)),
                  pl.BlockSpec(conv_w.shape, lambda i: (0, 0)),
                  pl.BlockSpec(conv_b.shape, lambda i: (0, 0)),
                  pl.BlockSpec(w_down.shape, lambda i: (0, 0)),
                  pl.BlockSpec((1, D_MODEL), lambda i: (0, 0))],
        out_specs=pl.BlockSpec((ROW_TILE, D_MODEL), lambda i: (i, 0)),
        out_shape=jax.ShapeDtypeStruct(h.shape, F32),
        scratch_shapes=[pltpu.VMEM((HALO + ROW_TILE, 2 * FFN_TILE), F32)],
        compiler_params=pltpu.CompilerParams(dimension_semantics=("arbitrary",),
                                             vmem_limit_bytes=VMEM_LIMIT),
        name="ffn_final" if final else "ffn",
    )(h, h, gain.reshape(1, D_MODEL), w_up, conv_w, conv_b, w_down, final_gain.reshape(1, D_MODEL))


def _attention_steps(batch, t_pad):
    nq = t_pad // Q_TILE
    nk = t_pad // K_TILE
    cols = {k: [] for k in ("qrow", "krow", "qloc", "kloc", "nkb", "first", "last", "b")}
    for b in range(batch):
        for qi in range(nq):
            n = ((qi + 1) * Q_TILE + K_TILE - 1) // K_TILE
            for ki in range(n):
                cols["qrow"].append(b * nq + qi)
                cols["krow"].append(b * nk + ki)
                cols["qloc"].append(qi)
                cols["kloc"].append(ki)
                cols["nkb"].append(n)
                cols["first"].append(int(ki == 0))
                cols["last"].append(int(ki == n - 1))
                cols["b"].append(b)
    return {k: jnp.asarray(np.asarray(v, np.int32)) for k, v in cols.items()}


def _half_select(x_t, first_half):
    row = lax.broadcasted_iota(jnp.int32, x_t.shape, 0)
    keep = (row < HEAD_DIM) if first_half else (row >= HEAD_DIM)
    return jnp.where(keep, x_t, jnp.zeros_like(x_t))


def _zero_after(x):
    w = lax.bitcast_convert_type(x, jnp.uint32)
    w = lax.shift_right_logical(lax.shift_right_logical(w, jnp.uint32(16)), jnp.uint32(16))
    return lax.bitcast_convert_type(w, F32)[0:1, :]


def _flash_update(s_t, v_t, h, m_sc, acc_sc, after=None):
    m_prev = m_sc[h]
    if after is not None:
        m_prev = m_prev + after
    m_new = jnp.maximum(m_prev, jnp.max(s_t, axis=0, keepdims=True))
    alpha = jnp.exp2(m_prev - m_new)
    p_t = jnp.exp2(s_t - m_new).astype(BF16)
    acc_sc[h] = alpha * acc_sc[h] + jnp.dot(v_t, p_t, preferred_element_type=F32)
    m_sc[h] = m_new


def _both_halves(x_t):
    return jnp.concatenate([_half_select(x_t, True), _half_select(x_t, False)], axis=1)


def _diff_attn_kernel(qrow, krow, qloc, kloc, first, last,
                      qt_ref, k_ref, vt_ref, lam_ref, subln_ref, o_ref, m_sc, acc_sc,
                      *, lambda_init):
    p = pl.program_id(0)

    @pl.when(first[p] == 1)
    def _():
        m_sc[...] = jnp.full(m_sc.shape, -jnp.inf, F32)
        acc_sc[...] = jnp.zeros(acc_sc.shape, F32)

    kpos = kloc[p] * K_TILE + lax.broadcasted_iota(jnp.int32, (K_TILE, 2 * Q_TILE), 0)
    qcol = lax.broadcasted_iota(jnp.int32, (K_TILE, 2 * Q_TILE), 1)
    qpos = qloc[p] * Q_TILE + jnp.where(qcol >= Q_TILE, qcol - Q_TILE, qcol)
    bias = jnp.where(kpos <= qpos, 0.0, -jnp.inf).astype(F32)

    def scores(j):
        return jnp.dot(k_ref[j], _both_halves(qt_ref[j]), preferred_element_type=F32)

    s_next = scores(0)
    for j in range(N_PAIRS):
        s_t = s_next + bias
        after = None
        if j + 1 < N_PAIRS:
            s_next = scores(j + 1)
            after = _zero_after(s_next[0:8, :])
        _flash_update(s_t, vt_ref[j], j, m_sc, acc_sc, after=after)

    @pl.when(last[p] == 1)
    def _():
        lam_rows = lam_ref[...]
        lam = (jnp.exp(jnp.sum(lam_rows[0:1] * lam_rows[1:2], axis=-1, keepdims=True))
               - jnp.exp(jnp.sum(lam_rows[2:3] * lam_rows[3:4], axis=-1, keepdims=True))
               + lambda_init)
        vdim = 2 * HEAD_DIM
        for j in range(N_PAIRS):
            a = acc_sc[j]
            a1 = a[:, :Q_TILE]
            a2 = a[:, Q_TILE:]
            o = a1[:vdim] / a1[vdim:vdim + 1] - lam * (a2[:vdim] / a2[vdim:vdim + 1])
            ms = jnp.mean(o * o, axis=0, keepdims=True)
            o = o * lax.rsqrt(ms + RMS_EPS) * subln_ref[...] * (1.0 - lambda_init)
            o_ref[j] = o.T.astype(BF16)


def _diff_attention(q_t, qkv, v_t, lam_rows, subln_col, steps, lambda_init):
    rows = qkv.shape[1]
    n_steps = steps["qrow"].shape[0]
    vrows = v_t.shape[1]

    def im(fn):
        return lambda p, qr, kr, ql, kl, f, l: fn(p, qr, kr)

    grid_spec = pltpu.PrefetchScalarGridSpec(
        num_scalar_prefetch=6,
        grid=(n_steps,),
        in_specs=[
            pl.BlockSpec((N_PAIRS, LANES, Q_TILE), im(lambda p, qr, kr: (0, 0, qr[p]))),
            pl.BlockSpec((N_PAIRS, K_TILE, LANES), im(lambda p, qr, kr: (1, kr[p], 0))),
            pl.BlockSpec((N_PAIRS, vrows, K_TILE), im(lambda p, qr, kr: (0, 0, kr[p]))),
            pl.BlockSpec((8, LANES), im(lambda p, qr, kr: (0, 0))),
            pl.BlockSpec((LANES, 1), im(lambda p, qr, kr: (0, 0))),
        ],
        out_specs=pl.BlockSpec((N_PAIRS, Q_TILE, LANES), im(lambda p, qr, kr: (0, qr[p], 0))),
        scratch_shapes=[pltpu.VMEM((N_PAIRS, 1, 2 * Q_TILE), F32),
                        pltpu.VMEM((N_PAIRS, vrows, 2 * Q_TILE), F32)],
    )
    return pl.pallas_call(
        functools.partial(_diff_attn_kernel, lambda_init=lambda_init),
        grid_spec=grid_spec,
        out_shape=jax.ShapeDtypeStruct((N_PAIRS, rows, LANES), BF16),
        compiler_params=pltpu.CompilerParams(dimension_semantics=("arbitrary",),
                                             vmem_limit_bytes=VMEM_LIMIT),
        name="diff_attn",
    )(steps["qrow"], steps["krow"], steps["qloc"], steps["kloc"], steps["first"], steps["last"],
      q_t, qkv, v_t, lam_rows, subln_col)


def _dsa_kernel(qrow, krow, qloc, kloc, nkb, first, last, bidx,
                qt_ref, k_ref, vt_ref, qit_ref, wit_ref, kidx_ref, o_ref,
                key_sc, thr_sc, m_sc, acc_sc, *, topk):
    p = pl.program_id(0)

    @pl.when(first[p] == 1)
    def _():
        m_sc[...] = jnp.full(m_sc.shape, -jnp.inf, F32)
        acc_sc[...] = jnp.zeros(acc_sc.shape, F32)
        n = nkb[p]
        wi = wit_ref[...]
        qpos = qloc[p] * Q_TILE + lax.broadcasted_iota(jnp.int32, (K_TILE, Q_TILE), 1)
        krow_iota = lax.broadcasted_iota(jnp.int32, (K_TILE, Q_TILE), 0)

        def score_chunk(c, carry):
            kt = kidx_ref[pl.ds(pl.multiple_of(c * K_TILE, K_TILE), K_TILE), :]
            sc = jnp.zeros((K_TILE, Q_TILE), F32)
            for jp in range(IDX_PAIRS):
                qi_t = qit_ref[jp]
                for cc in range(2):
                    raw = jnp.dot(kt, _half_select(qi_t, cc == 0), preferred_element_type=F32)
                    hh = 2 * jp + cc
                    sc = sc + jnp.maximum(raw, 0.0) * wi[hh:hh + 1, :]
            kpos = c * K_TILE + krow_iota
            sc = jnp.where(kpos < N_META, BIG_SCORE, sc)
            sc = jnp.where(kpos <= qpos, sc, -jnp.inf)
            bits = lax.bitcast_convert_type(sc, jnp.int32)
            key_sc[c] = bits ^ ((bits >> 31) & 0x7FFFFFFF)
            return carry

        lax.fori_loop(0, n, score_chunk, 0)

        def bit_body(b, thr):
            cand = thr + jnp.left_shift(jnp.int32(1), 31 - b)

            def count_chunk(c, cnt):
                return cnt + jnp.sum(jnp.where(key_sc[c] >= cand, 1.0, 0.0), axis=0, keepdims=True)

            cnt = lax.fori_loop(0, n, count_chunk, jnp.zeros((1, Q_TILE), F32))
            return jnp.where(cnt >= float(topk), cand, thr)

        thr = lax.fori_loop(0, 32, bit_body, jnp.full((1, Q_TILE), INT_MIN, jnp.int32))
        thr_sc[...] = jnp.maximum(thr, KEY_NEG_INF + 1)

    bias1 = jnp.where(key_sc[kloc[p]] >= thr_sc[...], 0.0, -jnp.inf).astype(F32)
    bias = jnp.concatenate([bias1, bias1], axis=1)

    def scores(j):
        return jnp.dot(k_ref[j], _both_halves(qt_ref[j]), preferred_element_type=F32)

    s_next = scores(0)
    for j in range(N_PAIRS):
        s_t = s_next + bias
        after = None
        if j + 1 < N_PAIRS:
            s_next = scores(j + 1)
            after = _zero_after(s_next[0:8, :])
        _flash_update(s_t, vt_ref[j], j, m_sc, acc_sc, after=after)

    @pl.when(last[p] == 1)
    def _():
        hv = vt_ref.shape[1] // 2
        for j in range(N_PAIRS):
            a = acc_sc[j][:hv, :Q_TILE]
            b = acc_sc[j][hv:, Q_TILE:]
            o = jnp.concatenate([a[:HEAD_DIM] / a[HEAD_DIM:HEAD_DIM + 1],
                                 b[:HEAD_DIM] / b[HEAD_DIM:HEAD_DIM + 1]], axis=0)
            o_ref[j] = o.T.astype(BF16)


def _dsa_attention(q_t, qkvi, v_t, qi_t, wi_t, kidx, steps, topk, t_pad):
    rows = qkvi.shape[1]
    n_steps = steps["qrow"].shape[0]
    nkb_total = t_pad // K_TILE
    vrows = v_t.shape[1]

    def im(fn):
        return lambda p, qr, kr, ql, kl, nk, f, l, b: fn(p, qr, kr, b)

    grid_spec = pltpu.PrefetchScalarGridSpec(
        num_scalar_prefetch=8,
        grid=(n_steps,),
        in_specs=[
            pl.BlockSpec((N_PAIRS, LANES, Q_TILE), im(lambda p, qr, kr, b: (0, 0, qr[p]))),
            pl.BlockSpec((N_PAIRS, K_TILE, LANES), im(lambda p, qr, kr, b: (1, kr[p], 0))),
            pl.BlockSpec((N_PAIRS, vrows, K_TILE), im(lambda p, qr, kr, b: (0, 0, kr[p]))),
            pl.BlockSpec((IDX_PAIRS, LANES, Q_TILE), im(lambda p, qr, kr, b: (0, 0, qr[p]))),
            pl.BlockSpec((IDX_HEADS, Q_TILE), im(lambda p, qr, kr, b: (0, qr[p]))),
            pl.BlockSpec((t_pad, LANES), im(lambda p, qr, kr, b: (b[p], 0))),
        ],
        out_specs=pl.BlockSpec((N_PAIRS, Q_TILE, LANES), im(lambda p, qr, kr, b: (0, qr[p], 0))),
        scratch_shapes=[pltpu.VMEM((nkb_total, K_TILE, Q_TILE), jnp.int32),
                        pltpu.VMEM((1, Q_TILE), jnp.int32),
                        pltpu.VMEM((N_PAIRS, 1, 2 * Q_TILE), F32),
                        pltpu.VMEM((N_PAIRS, vrows, 2 * Q_TILE), F32)],
    )
    return pl.pallas_call(
        functools.partial(_dsa_kernel, topk=topk),
        grid_spec=grid_spec,
        out_shape=jax.ShapeDtypeStruct((N_PAIRS, rows, LANES), BF16),
        compiler_params=pltpu.CompilerParams(dimension_semantics=("arbitrary",),
                                             vmem_limit_bytes=VMEM_LIMIT),
        name="dsa_attn",
    )(steps["qrow"], steps["krow"], steps["qloc"], steps["kloc"], steps["nkb"], steps["first"],
      steps["last"], steps["b"], q_t, qkvi, v_t, qi_t, wi_t, kidx)


def _rope_tables(t_pad):
    inv = ROPE_THETA ** (-jnp.arange(0, HEAD_DIM, 2, dtype=F32) / HEAD_DIM)
    ang = jnp.arange(t_pad, dtype=F32)[:, None] * inv[None, :]
    cos, sin = jnp.cos(ang), jnp.sin(ang)
    cos128 = jnp.tile(cos, (1, LANES // (HEAD_DIM // 2)))
    sin128 = jnp.tile(jnp.concatenate([-sin, sin], axis=1), (1, LANES // HEAD_DIM))
    return cos128, sin128


def _interleave_gate_val(a):
    lead = a.shape[:-1]
    a = a.reshape(lead + (2, FFN_HIDDEN // FFN_TILE, FFN_TILE))
    a = jnp.swapaxes(a, -3, -2)
    return a.reshape(lead + (2 * FFN_HIDDEN,))


def _with_ones_rows(v_t):
    ones = jnp.ones((v_t.shape[0], BF16_ROWS, v_t.shape[2]), v_t.dtype)
    return jnp.concatenate([v_t, ones], axis=1)


def kernel(x, meta_tokens, da_norm, da_w_qkv, da_lambda_q1, da_lambda_k1, da_lambda_q2, da_lambda_k2, da_subln, da_w_o, dsa_norm, dsa_w_in, dsa_idx_k_norm, dsa_w_o, ffn_norm, ffn_w_up, ffn_conv_w, ffn_conv_b, ffn_w_down, final_norm):
    batch, seq, d = x.shape
    assert d == D_MODEL
    depth = ffn_norm.shape[0]
    t_real = seq + N_META
    t_pad = _padded_len(t_real)
    assert t_pad % K_TILE == 0 and t_pad % Q_TILE == 0
    rows = batch * t_pad

    meta = jnp.broadcast_to(meta_tokens[None].astype(x.dtype), (batch, N_META, d))
    h = jnp.concatenate([meta, x, jnp.zeros((batch, t_pad - t_real, d), x.dtype)], axis=1)
    h = h.reshape(rows, d)
    cos128, sin128 = _rope_tables(t_pad)
    steps = _attention_steps(batch, t_pad)

    dsa_qkv = N_HEADS * HEAD_DIM
    idx_cols = IDX_HEADS * HEAD_DIM
    for i in range(depth):
        j = i // 2
        if i % 2 == 0:
            lambda_init = 0.8 - 0.6 * math.exp(-0.3 * i)
            kinds = ["rope_softmax_q"] * N_PAIRS + ["rope"] * N_PAIRS + ["plain"] * N_PAIRS
            qkv = _project(h, da_norm[j], da_w_qkv[j].astype(BF16), cos128, sin128, kinds, t_pad)
            q_t = jnp.swapaxes(qkv[:N_PAIRS], 1, 2)
            v_t = _with_ones_rows(jnp.swapaxes(qkv[2 * N_PAIRS:], 1, 2))
            lam_rows = jnp.zeros((8, LANES), F32).at[0:4, 0:HEAD_DIM].set(
                jnp.stack([da_lambda_q1[j], da_lambda_k1[j], da_lambda_q2[j], da_lambda_k2[j]]).astype(F32))
            o = _diff_attention(q_t, qkv, v_t, lam_rows, da_subln[j].reshape(LANES, 1).astype(F32),
                                steps, lambda_init)
            h = _out_project(h, o, da_w_o[j].astype(BF16))
        else:
            w_in = dsa_w_in[j]
            n_main = 3 * dsa_qkv + idx_cols
            kinds = (["rope_softmax_q"] * N_PAIRS + ["rope"] * N_PAIRS + ["plain"] * N_PAIRS
                     + ["rope_scaled"] * IDX_PAIRS)
            w_tail = jnp.zeros((d, LANES), w_in.dtype).at[:, :w_in.shape[1] - n_main].set(w_in[:, n_main:])
            gk = jnp.ones((1, LANES), F32).at[0, :HEAD_DIM].set(dsa_idx_k_norm[j].astype(F32))
            qkvi, tail = _project(h, dsa_norm[j], w_in[:, :n_main].astype(BF16), cos128, sin128, kinds, t_pad,
                                  w_tail=w_tail.astype(BF16), gk=gk)
            q_t = jnp.swapaxes(qkvi[:N_PAIRS], 1, 2)
            v_heads = jnp.swapaxes(qkvi[2 * N_PAIRS:3 * N_PAIRS], 1, 2).reshape(N_HEADS, HEAD_DIM, rows)
            v_t = _with_ones_rows(v_heads)
            v_t = v_t.reshape(N_PAIRS, 2 * v_t.shape[1], rows)
            qi_t = jnp.swapaxes(qkvi[3 * N_PAIRS:], 1, 2)
            wi_t = tail[:, HEAD_DIM:HEAD_DIM + IDX_HEADS].T
            ki = tail[:, :HEAD_DIM].astype(BF16)
            kidx = jnp.concatenate([ki, ki], axis=1)
            topk = min(TOPK_MAX, seq // 4)
            o = _dsa_attention(q_t, qkvi, v_t, qi_t, wi_t, kidx, steps, topk, t_pad)
            h = _out_project(h, o, dsa_w_o[j].astype(BF16))
        h = _ffn(h, ffn_norm[i], _interleave_gate_val(ffn_w_up[i]).astype(BF16),
                 _interleave_gate_val(ffn_conv_w[i]).astype(F32),
                 _interleave_gate_val(ffn_conv_b[i]).reshape(1, -1).astype(F32),
                 ffn_w_down[i].astype(BF16), final_norm, t_pad, final=(i == depth - 1))
    return h.reshape(batch, t_pad, d)[:, N_META:N_META + seq]
```

```python
import functools
import math

import numpy as np
import jax
import jax.numpy as jnp
from jax import lax
from jax.experimental import pallas as pl
from jax.experimental.pallas import tpu as pltpu

D_MODEL = 1024
N_META = 16
ROPE_THETA = 10000.0
RMS_EPS = 1e-6
HEAD_DIM = 64
N_PAIRS = 8
N_HEADS = 2 * N_PAIRS
IDX_HEADS = 8
IDX_PAIRS = IDX_HEADS // 2
TOPK_MAX = 256
BIG_SCORE = 1e30
LOG2_E = 1.4426950408889634
FFN_HIDDEN = 2816
CONV_WIDTH = 3

LANES = 128
BF16_ROWS = 16
ROW_TILE = 512
Q_TILE = 256
K_TILE = 512
FFN_TILE = 256
HALO = BF16_ROWS
VMEM_LIMIT = 56 * 1024 * 1024

F32 = jnp.float32
BF16 = jnp.bfloat16
INT_MIN = -2147483648
KEY_NEG_INF = INT_MIN + 0x7FFFFF


def _padded_len(t):
    return ((t + ROW_TILE - 1) // ROW_TILE) * ROW_TILE


def _rms(x, gain):
    ms = jnp.mean(x * x, axis=-1, keepdims=True)
    return x * lax.rsqrt(ms + RMS_EPS) * gain


def _rope_lanes(y, cos, sin, lo):
    sw = jnp.where(lo, pltpu.roll(y, LANES - HEAD_DIM // 2, 1), pltpu.roll(y, HEAD_DIM // 2, 1))
    return y * cos + sw * sin


def _proj_kernel(x_ref, g_ref, w_ref, cos_ref, sin_ref, *rest, kinds, has_tail):
    if has_tail:
        wt_ref, gk_ref, o_ref, t_ref = rest
    else:
        (o_ref,) = rest
    xh = _rms(x_ref[...], g_ref[...]).astype(BF16)
    cos = cos_ref[...]
    sin = sin_ref[...]
    lane = lax.broadcasted_iota(jnp.int32, cos.shape, 1)
    lo = (lane & (HEAD_DIM // 2)) == 0
    n_groups = len(kinds)
    cw = 4 * LANES
    for c in range(n_groups // 4):
        y = jnp.dot(xh, w_ref[:, c * cw:(c + 1) * cw], preferred_element_type=F32)
        for s in range(4):
            g = c * 4 + s
            yg = y[:, s * LANES:(s + 1) * LANES]
            if kinds[g] != "plain":
                yg = _rope_lanes(yg, cos, sin, lo)
                if kinds[g] == "rope_scaled":
                    yg = yg * (HEAD_DIM ** -0.5)
                elif kinds[g] == "rope_softmax_q":
                    yg = yg * (HEAD_DIM ** -0.5 * LOG2_E)
            o_ref[g] = yg.astype(BF16)
    if has_tail:
        t = jnp.dot(xh, wt_ref[...], preferred_element_type=F32)
        is_k = lane < HEAD_DIM
        ms = jnp.sum(jnp.where(is_k, t * t, 0.0), axis=-1, keepdims=True) * (1.0 / HEAD_DIM)
        kn = t * lax.rsqrt(ms + RMS_EPS) * gk_ref[...]
        kn = _rope_lanes(kn, cos, sin, lo)
        t_ref[...] = jnp.where(is_k, kn, t * (IDX_HEADS ** -0.5))


def _project(h, gain, w, cos, sin, kinds, t_pad, w_tail=None, gk=None):
    rows = h.shape[0]
    n_groups = len(kinds)
    blocks_per_batch = t_pad // ROW_TILE
    has_tail = w_tail is not None
    in_specs = [
        pl.BlockSpec((ROW_TILE, D_MODEL), lambda i: (i, 0)),
        pl.BlockSpec((1, D_MODEL), lambda i: (0, 0)),
        pl.BlockSpec(w.shape, lambda i: (0, 0)),
        pl.BlockSpec((ROW_TILE, LANES), lambda i: (i % blocks_per_batch, 0)),
        pl.BlockSpec((ROW_TILE, LANES), lambda i: (i % blocks_per_batch, 0)),
    ]
    out_shape = [jax.ShapeDtypeStruct((n_groups, rows, LANES), BF16)]
    out_specs = [pl.BlockSpec((n_groups, ROW_TILE, LANES), lambda i: (0, i, 0))]
    args = [h, gain.reshape(1, D_MODEL), w, cos, sin]
    if has_tail:
        in_specs += [pl.BlockSpec(w_tail.shape, lambda i: (0, 0)),
                     pl.BlockSpec((1, LANES), lambda i: (0, 0))]
        out_shape.append(jax.ShapeDtypeStruct((rows, LANES), F32))
        out_specs.append(pl.BlockSpec((ROW_TILE, LANES), lambda i: (i, 0)))
        args += [w_tail, gk]
    return pl.pallas_call(
        functools.partial(_proj_kernel, kinds=tuple(kinds), has_tail=has_tail),
        grid=(rows // ROW_TILE,),
        in_specs=in_specs,
        out_specs=out_specs if has_tail else out_specs[0],
        out_shape=out_shape if has_tail else out_shape[0],
        compiler_params=pltpu.CompilerParams(dimension_semantics=("arbitrary",),
                                             vmem_limit_bytes=VMEM_LIMIT),
        name="proj_tail" if has_tail else "proj",
    )(*args)


def _oproj_kernel(h_ref, o_ref, w_ref, out_ref):
    o = jnp.concatenate([o_ref[j] for j in range(N_PAIRS)], axis=1)
    out_ref[...] = h_ref[...] + jnp.dot(o, w_ref[...], preferred_element_type=F32)


def _out_project(h, o, w):
    rows = h.shape[0]
    return pl.pallas_call(
        _oproj_kernel,
        grid=(rows // ROW_TILE,),
        in_specs=[pl.BlockSpec((ROW_TILE, D_MODEL), lambda i: (i, 0)),
                  pl.BlockSpec((N_PAIRS, ROW_TILE, LANES), lambda i: (0, i, 0)),
                  pl.BlockSpec(w.shape, lambda i: (0, 0))],
        out_specs=pl.BlockSpec((ROW_TILE, D_MODEL), lambda i: (i, 0)),
        out_shape=jax.ShapeDtypeStruct(h.shape, F32),
        compiler_params=pltpu.CompilerParams(dimension_semantics=("arbitrary",),
                                             vmem_limit_bytes=VMEM_LIMIT),
        name="oproj",
    )(h, o, w)


def _ffn_kernel(xp_ref, x_ref, g_ref, wup_ref, cw_ref, cb_ref, wdn_ref, fg_ref, o_ref, u_sc,
                *, blocks_per_batch, final):
    i = pl.program_id(0)
    x = x_ref[...]
    xe = jnp.concatenate([xp_ref[...], x], axis=0)
    xh = _rms(xe, g_ref[...])
    row = lax.broadcasted_iota(jnp.int32, (HALO + ROW_TILE, 1), 0)
    keep = jnp.logical_or(row >= HALO, i % blocks_per_batch != 0)
    xh = jnp.where(keep, xh, 0.0).astype(BF16)
    acc = jnp.zeros((ROW_TILE, D_MODEL), F32)
    cw2 = 2 * FFN_TILE
    for c in range(FFN_HIDDEN // FFN_TILE):
        u_sc[...] = jnp.dot(xh, wup_ref[:, c * cw2:(c + 1) * cw2], preferred_element_type=F32)
        cwc = cw_ref[:, c * cw2:(c + 1) * cw2]
        conv = cb_ref[:, c * cw2:(c + 1) * cw2]
        for j in range(CONV_WIDTH):
            conv = conv + cwc[j:j + 1, :] * u_sc[pl.ds(HALO - (CONV_WIDTH - 1) + j, ROW_TILE), :]
        gate = conv[:, :FFN_TILE]
        val = conv[:, FFN_TILE:]
        a = (gate * jax.nn.sigmoid(gate) * val).astype(BF16)
        acc = acc + jnp.dot(a, wdn_ref[c * FFN_TILE:(c + 1) * FFN_TILE, :], preferred_element_type=F32)
    y = x + acc
    if final:
        y = _rms(y, fg_ref[...])
    o_ref[...] = y


def _ffn(h, gain, w_up, conv_w, conv_b, w_down, final_gain, t_pad, final):
    rows = h.shape[0]
    blocks_per_batch = t_pad // ROW_TILE
    halo_blocks = ROW_TILE // HALO
    return pl.pallas_call(
        functools.partial(_ffn_kernel, blocks_per_batch=blocks_per_batch, final=final),
        grid=(rows // ROW_TILE,),
        in_specs=[pl.BlockSpec((HALO, D_MODEL), lambda i: (jnp.maximum(i * halo_blocks - 1, 0), 0)),
                  pl.BlockSpec((ROW_TILE, D_MODEL), lambda i: (i, 0)),
                  pl.BlockSpec((1, D_MODEL), lambda i: (0, 0)),
                  pl.BlockSpec(w_up.shape, lambda i: (0, 0)),
                  pl.BlockSpec(conv_w.shape, lambda i: (0, 0)),
                  pl.BlockSpec(conv_b.shape, lambda i: (0, 0)),
                  pl.BlockSpec(w_down.shape, lambda i: (0, 0)),
                  pl.BlockSpec((1, D_MODEL), lambda i: (0, 0))],
        out_specs=pl.BlockSpec((ROW_TILE, D_MODEL), lambda i: (i, 0)),
        out_shape=jax.ShapeDtypeStruct(h.shape, F32),
        scratch_shapes=[pltpu.VMEM((HALO + ROW_TILE, 2 * FFN_TILE), F32)],
        compiler_params=pltpu.CompilerParams(dimension_semantics=("arbitrary",),
                                             vmem_limit_bytes=VMEM_LIMIT),
        name="ffn_final" if final else "ffn",
    )(h, h, gain.reshape(1, D_MODEL), w_up, conv_w, conv_b, w_down, final_gain.reshape(1, D_MODEL))


def _attention_steps(batch, t_pad):
    nq = t_pad // Q_TILE
    nk = t_pad // K_TILE
    cols = {k: [] for k in ("qrow", "krow", "qloc", "kloc", "nkb", "first", "last", "b")}
    for b in range(batch):
        for qi in range(nq):
            n = ((qi + 1) * Q_TILE + K_TILE - 1) // K_TILE
            for ki in range(n):
                cols["qrow"].append(b * nq + qi)
                cols["krow"].append(b * nk + ki)
                cols["qloc"].append(qi)
                cols["kloc"].append(ki)
                cols["nkb"].append(n)
                cols["first"].append(int(ki == 0))
                cols["last"].append(int(ki == n - 1))
                cols["b"].append(b)
    return {k: jnp.asarray(np.asarray(v, np.int32)) for k, v in cols.items()}


def _half_select(x_t, first_half):
    row = lax.broadcasted_iota(jnp.int32, x_t.shape, 0)
    keep = (row < HEAD_DIM) if first_half else (row >= HEAD_DIM)
    return jnp.where(keep, x_t, jnp.zeros_like(x_t))


def _zero_after(x):
    w = lax.bitcast_convert_type(x, jnp.uint32)
    w = lax.shift_right_logical(lax.shift_right_logical(w, jnp.uint32(16)), jnp.uint32(16))
    return lax.bitcast_convert_type(w, F32)[0:1, :]


def _flash_update(s_t, v_t, h, m_sc, acc_sc, after=None):
    m_prev = m_sc[h]
    m_new = jnp.maximum(m_prev, jnp.max(s_t, axis=0, keepdims=True))
    alpha = jnp.exp2(m_prev - m_new)
    if after is None:
        p_t = jnp.exp2(s_t - m_new).astype(BF16)
    else:
        cut = (3 * s_t.shape[0]) // 4
        p_t = jnp.concatenate([jnp.exp2(s_t[:cut] - m_new),
                               jnp.exp2(s_t[cut:] - (m_new + after))], axis=0).astype(BF16)
    acc_sc[h] = alpha * acc_sc[h] + jnp.dot(v_t, p_t, preferred_element_type=F32)
    m_sc[h] = m_new


def _both_halves(x_t):
    return jnp.concatenate([_half_select(x_t, True), _half_select(x_t, False)], axis=1)


def _diff_attn_kernel(qrow, krow, qloc, kloc, first, last,
                      qt_ref, k_ref, vt_ref, lam_ref, subln_ref, o_ref, m_sc, acc_sc,
                      *, lambda_init):
    p = pl.program_id(0)

    @pl.when(first[p] == 1)
    def _():
        m_sc[...] = jnp.full(m_sc.shape, -jnp.inf, F32)
        acc_sc[...] = jnp.zeros(acc_sc.shape, F32)

    def scores(j):
        return jnp.dot(k_ref[j], _both_halves(qt_ref[j]), preferred_element_type=F32)

    def sweep(bias):
        ahead = 2
        pending = [scores(j) for j in range(ahead)]
        for j in range(N_PAIRS):
            s_t = pending.pop(0)
            if bias is not None:
                s_t = s_t + bias
            after = None
            if j + ahead < N_PAIRS:
                pending.append(scores(j + ahead))
                after = _zero_after(pending[-1][0:8, :])
            _flash_update(s_t, vt_ref[j], j, m_sc, acc_sc, after=after)

    on_diagonal = kloc[p] * K_TILE + (K_TILE - 1) > qloc[p] * Q_TILE

    @pl.when(on_diagonal)
    def _():
        kpos = kloc[p] * K_TILE + lax.broadcasted_iota(jnp.int32, (K_TILE, 2 * Q_TILE), 0)
        qcol = lax.broadcasted_iota(jnp.int32, (K_TILE, 2 * Q_TILE), 1)
        qpos = qloc[p] * Q_TILE + jnp.where(qcol >= Q_TILE, qcol - Q_TILE, qcol)
        sweep(jnp.where(kpos <= qpos, 0.0, -jnp.inf).astype(F32))

    @pl.when(jnp.logical_not(on_diagonal))
    def _():
        sweep(None)

    @pl.when(last[p] == 1)
    def _():
        lam_rows = lam_ref[...]
        lam = (jnp.exp(jnp.sum(lam_rows[0:1] * lam_rows[1:2], axis=-1, keepdims=True))
               - jnp.exp(jnp.sum(lam_rows[2:3] * lam_rows[3:4], axis=-1, keepdims=True))
               + lambda_init)
        vdim = 2 * HEAD_DIM
        for j in range(N_PAIRS):
            a = acc_sc[j]
            a1 = a[:, :Q_TILE]
            a2 = a[:, Q_TILE:]
            o = a1[:vdim] / a1[vdim:vdim + 1] - lam * (a2[:vdim] / a2[vdim:vdim + 1])
            ms = jnp.mean(o * o, axis=0, keepdims=True)
            o = o * lax.rsqrt(ms + RMS_EPS) * subln_ref[...] * (1.0 - lambda_init)
            o_ref[j] = o.T.astype(BF16)


def _diff_attention(q_t, qkv, v_t, lam_rows, subln_col, steps, lambda_init):
    rows = qkv.shape[1]
    n_steps = steps["qrow"].shape[0]
    vrows = v_t.shape[1]

    def im(fn):
        return lambda p, qr, kr, ql, kl, f, l: fn(p, qr, kr)

    grid_spec = pltpu.PrefetchScalarGridSpec(
        num_scalar_prefetch=6,
        grid=(n_steps,),
        in_specs=[
            pl.BlockSpec((N_PAIRS, LANES, Q_TILE), im(lambda p, qr, kr: (0, 0, qr[p]))),
            pl.BlockSpec((N_PAIRS, K_TILE, LANES), im(lambda p, qr, kr: (1, kr[p], 0))),
            pl.BlockSpec((N_PAIRS, vrows, K_TILE), im(lambda p, qr, kr: (0, 0, kr[p]))),
            pl.BlockSpec((8, LANES), im(lambda p, qr, kr: (0, 0))),
            pl.BlockSpec((LANES, 1), im(lambda p, qr, kr: (0, 0))),
        ],
        out_specs=pl.BlockSpec((N_PAIRS, Q_TILE, LANES), im(lambda p, qr, kr: (0, qr[p], 0))),
        scratch_shapes=[pltpu.VMEM((N_PAIRS, 1, 2 * Q_TILE), F32),
                        pltpu.VMEM((N_PAIRS, vrows, 2 * Q_TILE), F32)],
    )
    return pl.pallas_call(
        functools.partial(_diff_attn_kernel, lambda_init=lambda_init),
        grid_spec=grid_spec,
        out_shape=jax.ShapeDtypeStruct((N_PAIRS, rows, LANES), BF16),
        compiler_params=pltpu.CompilerParams(dimension_semantics=("arbitrary",),
                                             vmem_limit_bytes=VMEM_LIMIT),
        name="diff_attn",
    )(steps["qrow"], steps["krow"], steps["qloc"], steps["kloc"], steps["first"], steps["last"],
      q_t, qkv, v_t, lam_rows, subln_col)


def _dsa_kernel(qrow, krow, qloc, kloc, nkb, first, last, bidx,
                qt_ref, k_ref, vt_ref, qit_ref, wit_ref, kidx_ref, o_ref,
                key_sc, thr_sc, m_sc, acc_sc, *, topk):
    p = pl.program_id(0)

    @pl.when(first[p] == 1)
    def _():
        m_sc[...] = jnp.full(m_sc.shape, -jnp.inf, F32)
        acc_sc[...] = jnp.zeros(acc_sc.shape, F32)
        n = nkb[p]
        wi = wit_ref[...]
        qpos = qloc[p] * Q_TILE + lax.broadcasted_iota(jnp.int32, (K_TILE, Q_TILE), 1)
        krow_iota = lax.broadcasted_iota(jnp.int32, (K_TILE, Q_TILE), 0)

        def score_chunk(c, carry):
            kt = kidx_ref[pl.ds(pl.multiple_of(c * K_TILE, K_TILE), K_TILE), :]
            sc = jnp.zeros((K_TILE, Q_TILE), F32)
            for jp in range(IDX_PAIRS):
                qi_t = qit_ref[jp]
                for cc in range(2):
                    raw = jnp.dot(kt, _half_select(qi_t, cc == 0), preferred_element_type=F32)
                    hh = 2 * jp + cc
                    sc = sc + jnp.maximum(raw, 0.0) * wi[hh:hh + 1, :]
            kpos = c * K_TILE + krow_iota
            sc = jnp.where(kpos < N_META, BIG_SCORE, sc)
            sc = jnp.where(kpos <= qpos, sc, -jnp.inf)
            bits = lax.bitcast_convert_type(sc, jnp.int32)
            key_sc[c] = bits ^ ((bits >> 31) & 0x7FFFFFFF)
            return carry

        lax.fori_loop(0, n, score_chunk, 0)

        q1 = qloc[p] * Q_TILE + lax.broadcasted_iota(jnp.int32, (1, Q_TILE), 1)
        settled0 = (q1 < topk).astype(jnp.int32)

        def unsettled(state):
            b, _, settled = state
            return jnp.logical_and(b < 32, jnp.min(settled) == 0)

        def bit_body(state):
            b, thr, settled = state
            cand = thr + jnp.left_shift(jnp.int32(1), 31 - b)

            def count_chunk(c, cnt8):
                hit = jnp.where(key_sc[c] >= cand, 1.0, 0.0).reshape(8, K_TILE // 64, 8, Q_TILE)
                return cnt8 + jnp.sum(jnp.sum(hit, axis=1), axis=0)

            cnt8 = lax.fori_loop(0, n, count_chunk, jnp.zeros((8, Q_TILE), F32))
            cnt = jnp.sum(cnt8, axis=0, keepdims=True)
            thr = jnp.where(cnt >= float(topk), cand, thr)
            settled = jnp.where(cnt == float(topk), 1, settled)
            return b + 1, thr, settled

        _, thr, _ = lax.while_loop(unsettled, bit_body,
                                   (jnp.int32(0), jnp.full((1, Q_TILE), INT_MIN, jnp.int32), settled0))
        thr_sc[...] = jnp.maximum(thr, KEY_NEG_INF + 1)

    bias1 = jnp.where(key_sc[kloc[p]] >= thr_sc[...], 0.0, -jnp.inf).astype(F32)
    bias = jnp.concatenate([bias1, bias1], axis=1)

    def scores(j):
        return jnp.dot(k_ref[j], _both_halves(qt_ref[j]), preferred_element_type=F32)

    ahead = 2
    pending = [scores(j) for j in range(ahead)]
    for j in range(N_PAIRS):
        s_t = pending.pop(0) + bias
        after = None
        if j + ahead < N_PAIRS:
            pending.append(scores(j + ahead))
            after = _zero_after(pending[-1][0:8, :])
        _flash_update(s_t, vt_ref[j], j, m_sc, acc_sc, after=after)

    @pl.when(last[p] == 1)
    def _():
        hv = vt_ref.shape[1] // 2
        for j in range(N_PAIRS):
            a = acc_sc[j][:hv, :Q_TILE]
            b = acc_sc[j][hv:, Q_TILE:]
            o = jnp.concatenate([a[:HEAD_DIM] / a[HEAD_DIM:HEAD_DIM + 1],
                                 b[:HEAD_DIM] / b[HEAD_DIM:HEAD_DIM + 1]], axis=0)
            o_ref[j] = o.T.astype(BF16)


def _dsa_attention(q_t, qkvi, v_t, qi_t, wi_t, kidx, steps, topk, t_pad):
    rows = qkvi.shape[1]
    n_steps = steps["qrow"].shape[0]
    nkb_total = t_pad // K_TILE
    vrows = v_t.shape[1]

    def im(fn):
        return lambda p, qr, kr, ql, kl, nk, f, l, b: fn(p, qr, kr, b)

    grid_spec = pltpu.PrefetchScalarGridSpec(
        num_scalar_prefetch=8,
        grid=(n_steps,),
        in_specs=[
            pl.BlockSpec((N_PAIRS, LANES, Q_TILE), im(lambda p, qr, kr, b: (0, 0, qr[p]))),
            pl.BlockSpec((N_PAIRS, K_TILE, LANES), im(lambda p, qr, kr, b: (1, kr[p], 0))),
            pl.BlockSpec((N_PAIRS, vrows, K_TILE), im(lambda p, qr, kr, b: (0, 0, kr[p]))),
            pl.BlockSpec((IDX_PAIRS, LANES, Q_TILE), im(lambda p, qr, kr, b: (0, 0, qr[p]))),
            pl.BlockSpec((IDX_HEADS, Q_TILE), im(lambda p, qr, kr, b: (0, qr[p]))),
            pl.BlockSpec((t_pad, LANES), im(lambda p, qr, kr, b: (b[p], 0))),
        ],
        out_specs=pl.BlockSpec((N_PAIRS, Q_TILE, LANES), im(lambda p, qr, kr, b: (0, qr[p], 0))),
        scratch_shapes=[pltpu.VMEM((nkb_total, K_TILE, Q_TILE), jnp.int32),
                        pltpu.VMEM((1, Q_TILE), jnp.int32),
                        pltpu.VMEM((N_PAIRS, 1, 2 * Q_TILE), F32),
                        pltpu.VMEM((N_PAIRS, vrows, 2 * Q_TILE), F32)],
    )
    return pl.pallas_call(
        functools.partial(_dsa_kernel, topk=topk),
        grid_spec=grid_spec,
        out_shape=jax.ShapeDtypeStruct((N_PAIRS, rows, LANES), BF16),
        compiler_params=pltpu.CompilerParams(dimension_semantics=("arbitrary",),
                                             vmem_limit_bytes=VMEM_LIMIT),
        name="dsa_attn",
    )(steps["qrow"], steps["krow"], steps["qloc"], steps["kloc"], steps["nkb"], steps["first"],
      steps["last"], steps["b"], q_t, qkvi, v_t, qi_t, wi_t, kidx)


def _rope_tables(t_pad):
    inv = ROPE_THETA ** (-jnp.arange(0, HEAD_DIM, 2, dtype=F32) / HEAD_DIM)
    ang = jnp.arange(t_pad, dtype=F32)[:, None] * inv[None, :]
    cos, sin = jnp.cos(ang), jnp.sin(ang)
    cos128 = jnp.tile(cos, (1, LANES // (HEAD_DIM // 2)))
    sin128 = jnp.tile(jnp.concatenate([-sin, sin], axis=1), (1, LANES // HEAD_DIM))
    return cos128, sin128


def _interleave_gate_val(a):
    lead = a.shape[:-1]
    a = a.reshape(lead + (2, FFN_HIDDEN // FFN_TILE, FFN_TILE))
    a = jnp.swapaxes(a, -3, -2)
    return a.reshape(lead + (2 * FFN_HIDDEN,))


def _with_ones_rows(v_t):
    ones = jnp.ones((v_t.shape[0], BF16_ROWS, v_t.shape[2]), v_t.dtype)
    return jnp.concatenate([v_t, ones], axis=1)


def kernel(x, meta_tokens, da_norm, da_w_qkv, da_lambda_q1, da_lambda_k1, da_lambda_q2, da_lambda_k2, da_subln, da_w_o, dsa_norm, dsa_w_in, dsa_idx_k_norm, dsa_w_o, ffn_norm, ffn_w_up, ffn_conv_w, ffn_conv_b, ffn_w_down, final_norm):
    batch, seq, d = x.shape
    assert d == D_MODEL
    depth = ffn_norm.shape[0]
    t_real = seq + N_META
    t_pad = _padded_len(t_real)
    assert t_pad % K_TILE == 0 and t_pad % Q_TILE == 0
    rows = batch * t_pad

    meta = jnp.broadcast_to(meta_tokens[None].astype(x.dtype), (batch, N_META, d))
    h = jnp.concatenate([meta, x, jnp.zeros((batch, t_pad - t_real, d), x.dtype)], axis=1)
    h = h.reshape(rows, d)
    cos128, sin128 = _rope_tables(t_pad)
    steps = _attention_steps(batch, t_pad)

    dsa_qkv = N_HEADS * HEAD_DIM
    idx_cols = IDX_HEADS * HEAD_DIM
    for i in range(depth):
        j = i // 2
        if i % 2 == 0:
            lambda_init = 0.8 - 0.6 * math.exp(-0.3 * i)
            kinds = ["rope_softmax_q"] * N_PAIRS + ["rope"] * N_PAIRS + ["plain"] * N_PAIRS
            qkv = _project(h, da_norm[j], da_w_qkv[j].astype(BF16), cos128, sin128, kinds, t_pad)
            q_t = jnp.swapaxes(qkv[:N_PAIRS], 1, 2)
            v_t = _with_ones_rows(jnp.swapaxes(qkv[2 * N_PAIRS:], 1, 2))
            lam_rows = jnp.zeros((8, LANES), F32).at[0:4, 0:HEAD_DIM].set(
                jnp.stack([da_lambda_q1[j], da_lambda_k1[j], da_lambda_q2[j], da_lambda_k2[j]]).astype(F32))
            o = _diff_attention(q_t, qkv, v_t, lam_rows, da_subln[j].reshape(LANES, 1).astype(F32),
                                steps, lambda_init)
            h = _out_project(h, o, da_w_o[j].astype(BF16))
        else:
            w_in = dsa_w_in[j]
            n_main = 3 * dsa_qkv + idx_cols
            kinds = (["rope_softmax_q"] * N_PAIRS + ["rope"] * N_PAIRS + ["plain"] * N_PAIRS
                     + ["rope_scaled"] * IDX_PAIRS)
            w_tail = jnp.zeros((d, LANES), w_in.dtype).at[:, :w_in.shape[1] - n_main].set(w_in[:, n_main:])
            gk = jnp.ones((1, LANES), F32).at[0, :HEAD_DIM].set(dsa_idx_k_norm[j].astype(F32))
            qkvi, tail = _project(h, dsa_norm[j], w_in[:, :n_main].astype(BF16), cos128, sin128, kinds, t_pad,
                                  w_tail=w_tail.astype(BF16), gk=gk)
            q_t = jnp.swapaxes(qkvi[:N_PAIRS], 1, 2)
            v_heads = jnp.swapaxes(qkvi[2 * N_PAIRS:3 * N_PAIRS], 1, 2).reshape(N_HEADS, HEAD_DIM, rows)
            v_t = _with_ones_rows(v_heads)
            v_t = v_t.reshape(N_PAIRS, 2 * v_t.shape[1], rows)
            qi_t = jnp.swapaxes(qkvi[3 * N_PAIRS:], 1, 2)
            wi_t = tail[:, HEAD_DIM:HEAD_DIM + IDX_HEADS].T
            ki = tail[:, :HEAD_DIM].astype(BF16)
            kidx = jnp.concatenate([ki, ki], axis=1)
            topk = min(TOPK_MAX, seq // 4)
            o = _dsa_attention(q_t, qkvi, v_t, qi_t, wi_t, kidx, steps, topk, t_pad)
            h = _out_project(h, o, dsa_w_o[j].astype(BF16))
        h = _ffn(h, ffn_norm[i], _interleave_gate_val(ffn_w_up[i]).astype(BF16),
                 _interleave_gate_val(ffn_conv_w[i]).astype(F32),
                 _interleave_gate_val(ffn_conv_b[i]).reshape(1, -1).astype(F32),
                 ffn_w_down[i].astype(BF16), final_norm, t_pad, final=(i == depth - 1))
    return h.reshape(batch, t_pad, d)[:, N_META:N_META + seq]
```

```python
import functools
import math

import numpy as np
import jax
import jax.numpy as jnp
from jax import lax
from jax.experimental import pallas as pl
from jax.experimental.pallas import tpu as pltpu

D_MODEL = 1024
N_META = 16
ROPE_THETA = 10000.0
RMS_EPS = 1e-6
HEAD_DIM = 64
N_PAIRS = 8
N_HEADS = 2 * N_PAIRS
IDX_HEADS = 8
IDX_PAIRS = IDX_HEADS // 2
TOPK_MAX = 256
BIG_SCORE = 1e30
LOG2_E = 1.4426950408889634
FFN_HIDDEN = 2816
CONV_WIDTH = 3

LANES = 128
BF16_ROWS = 16
ROW_TILE = 512
Q_TILE = 256
K_TILE = 512
K_SUBS = 3
FFN_TILE = 256
HALO = BF16_ROWS
VMEM_LIMIT = 56 * 1024 * 1024

F32 = jnp.float32
BF16 = jnp.bfloat16
INT_MIN = -2147483648
KEY_NEG_INF = INT_MIN + 0x7FFFFF


def _padded_len(t):
    unit = math.lcm(ROW_TILE, K_SUBS * K_TILE, Q_TILE)
    return ((t + unit - 1) // unit) * unit


def _rms(x, gain):
    ms = jnp.mean(x * x, axis=-1, keepdims=True)
    return x * lax.rsqrt(ms + RMS_EPS) * gain


def _rope_lanes(y, cos, sin, lo):
    sw = jnp.where(lo, pltpu.roll(y, LANES - HEAD_DIM // 2, 1), pltpu.roll(y, HEAD_DIM // 2, 1))
    return y * cos + sw * sin


def _proj_kernel(x_ref, g_ref, w_ref, cos_ref, sin_ref, *rest, kinds, has_tail):
    if has_tail:
        wt_ref, gk_ref, o_ref, t_ref = rest
    else:
        (o_ref,) = rest
    xh = _rms(x_ref[...], g_ref[...]).astype(BF16)
    cos = cos_ref[...]
    sin = sin_ref[...]
    lane = lax.broadcasted_iota(jnp.int32, cos.shape, 1)
    lo = (lane & (HEAD_DIM // 2)) == 0
    n_groups = len(kinds)
    cw = 4 * LANES
    for c in range(n_groups // 4):
        y = jnp.dot(xh, w_ref[:, c * cw:(c + 1) * cw], preferred_element_type=F32)
        for s in range(4):
            g = c * 4 + s
            yg = y[:, s * LANES:(s + 1) * LANES]
            if kinds[g] != "plain":
                yg = _rope_lanes(yg, cos, sin, lo)
                if kinds[g] == "rope_scaled":
                    yg = yg * (HEAD_DIM ** -0.5)
                elif kinds[g] == "rope_softmax_q":
                    yg = yg * (HEAD_DIM ** -0.5 * LOG2_E)
            o_ref[g] = yg.astype(BF16)
    if has_tail:
        t = jnp.dot(xh, wt_ref[...], preferred_element_type=F32)
        is_k = lane < HEAD_DIM
        ms = jnp.sum(jnp.where(is_k, t * t, 0.0), axis=-1, keepdims=True) * (1.0 / HEAD_DIM)
        kn = t * lax.rsqrt(ms + RMS_EPS) * gk_ref[...]
        kn = _rope_lanes(kn, cos, sin, lo)
        t_ref[...] = jnp.where(is_k, kn, t * (IDX_HEADS ** -0.5))


def _project(h, gain, w, cos, sin, kinds, t_pad, w_tail=None, gk=None):
    rows = h.shape[0]
    n_groups = len(kinds)
    blocks_per_batch = t_pad // ROW_TILE
    has_tail = w_tail is not None
    in_specs = [
        pl.BlockSpec((ROW_TILE, D_MODEL), lambda i: (i, 0)),
        pl.BlockSpec((1, D_MODEL), lambda i: (0, 0)),
        pl.BlockSpec(w.shape, lambda i: (0, 0)),
        pl.BlockSpec((ROW_TILE, LANES), lambda i: (i % blocks_per_batch, 0)),
        pl.BlockSpec((ROW_TILE, LANES), lambda i: (i % blocks_per_batch, 0)),
    ]
    out_shape = [jax.ShapeDtypeStruct((n_groups, rows, LANES), BF16)]
    out_specs = [pl.BlockSpec((n_groups, ROW_TILE, LANES), lambda i: (0, i, 0))]
    args = [h, gain.reshape(1, D_MODEL), w, cos, sin]
    if has_tail:
        in_specs += [pl.BlockSpec(w_tail.shape, lambda i: (0, 0)),
                     pl.BlockSpec((1, LANES), lambda i: (0, 0))]
        out_shape.append(jax.ShapeDtypeStruct((rows, LANES), F32))
        out_specs.append(pl.BlockSpec((ROW_TILE, LANES), lambda i: (i, 0)))
        args += [w_tail, gk]
    return pl.pallas_call(
        functools.partial(_proj_kernel, kinds=tuple(kinds), has_tail=has_tail),
        grid=(rows // ROW_TILE,),
        in_specs=in_specs,
        out_specs=out_specs if has_tail else out_specs[0],
        out_shape=out_shape if has_tail else out_shape[0],
        compiler_params=pltpu.CompilerParams(dimension_semantics=("arbitrary",),
                                             vmem_limit_bytes=VMEM_LIMIT),
        name="proj_tail" if has_tail else "proj",
    )(*args)


def _oproj_kernel(h_ref, o_ref, w_ref, out_ref):
    o = jnp.concatenate([o_ref[j] for j in range(N_PAIRS)], axis=1)
    out_ref[...] = h_ref[...] + jnp.dot(o, w_ref[...], preferred_element_type=F32)


def _out_project(h, o, w):
    rows = h.shape[0]
    return pl.pallas_call(
        _oproj_kernel,
        grid=(rows // ROW_TILE,),
        in_specs=[pl.BlockSpec((ROW_TILE, D_MODEL), lambda i: (i, 0)),
                  pl.BlockSpec((N_PAIRS, ROW_TILE, LANES), lambda i: (0, i, 0)),
                  pl.BlockSpec(w.shape, lambda i: (0, 0))],
        out_specs=pl.BlockSpec((ROW_TILE, D_MODEL), lambda i: (i, 0)),
        out_shape=jax.ShapeDtypeStruct(h.shape, F32),
        compiler_params=pltpu.CompilerParams(dimension_semantics=("arbitrary",),
                                             vmem_limit_bytes=VMEM_LIMIT),
        name="oproj",
    )(h, o, w)


def _ffn_kernel(xp_ref, x_ref, g_ref, wup_ref, cw_ref, cb_ref, wdn_ref, fg_ref, o_ref, u_sc,
                *, blocks_per_batch, final):
    i = pl.program_id(0)
    x = x_ref[...]
    xe = jnp.concatenate([xp_ref[...], x], axis=0)
    xh = _rms(xe, g_ref[...])
    row = lax.broadcasted_iota(jnp.int32, (HALO + ROW_TILE, 1), 0)
    keep = jnp.logical_or(row >= HALO, i % blocks_per_batch != 0)
    xh = jnp.where(keep, xh, 0.0).astype(BF16)
    acc = jnp.zeros((ROW_TILE, D_MODEL), F32)
    cw2 = 2 * FFN_TILE
    for c in range(FFN_HIDDEN // FFN_TILE):
        u_sc[...] = jnp.dot(xh, wup_ref[:, c * cw2:(c + 1) * cw2], preferred_element_type=F32)
        cwc = cw_ref[:, c * cw2:(c + 1) * cw2]
        conv = cb_ref[:, c * cw2:(c + 1) * cw2]
        for j in range(CONV_WIDTH):
            conv = conv + cwc[j:j + 1, :] * u_sc[pl.ds(HALO - (CONV_WIDTH - 1) + j, ROW_TILE), :]
        gate = conv[:, :FFN_TILE]
        val = conv[:, FFN_TILE:]
        a = (gate * jax.nn.sigmoid(gate) * val).astype(BF16)
        acc = acc + jnp.dot(a, wdn_ref[c * FFN_TILE:(c + 1) * FFN_TILE, :], preferred_element_type=F32)
    y = x + acc
    if final:
        y = _rms(y, fg_ref[...])
    o_ref[...] = y


def _ffn(h, gain, w_up, conv_w, conv_b, w_down, final_gain, t_pad, final):
    rows = h.shape[0]
    blocks_per_batch = t_pad // ROW_TILE
    halo_blocks = ROW_TILE // HALO
    return pl.pallas_call(
        functools.partial(_ffn_kernel, blocks_per_batch=blocks_per_batch, final=final),
        grid=(rows // ROW_TILE,),
        in_specs=[pl.BlockSpec((HALO, D_MODEL), lambda i: (jnp.maximum(i * halo_blocks - 1, 0), 0)),
                  pl.BlockSpec((ROW_TILE, D_MODEL), lambda i: (i, 0)),
                  pl.BlockSpec((1, D_MODEL), lambda i: (0, 0)),
                  pl.BlockSpec(w_up.shape, lambda i: (0, 0)),
                  pl.BlockSpec(conv_w.shape, lambda i: (0, 0)),
                  pl.BlockSpec(conv_b.shape, lambda i: (0, 0)),
                  pl.BlockSpec(w_down.shape, lambda i: (0, 0)),
                  pl.BlockSpec((1, D_MODEL), lambda i: (0, 0))],
        out_specs=pl.BlockSpec((ROW_TILE, D_MODEL), lambda i: (i, 0)),
        out_shape=jax.ShapeDtypeStruct(h.shape, F32),
        scratch_shapes=[pltpu.VMEM((HALO + ROW_TILE, 2 * FFN_TILE), F32)],
        compiler_params=pltpu.CompilerParams(dimension_semantics=("arbitrary",),
                                             vmem_limit_bytes=VMEM_LIMIT),
        name="ffn_final" if final else "ffn",
    )(h, h, gain.reshape(1, D_MODEL), w_up, conv_w, conv_b, w_down, final_gain.reshape(1, D_MODEL))


def _attention_steps(batch, t_pad):
    nq = t_pad // Q_TILE
    nk = t_pad // (K_SUBS * K_TILE)
    cols = {k: [] for k in ("qrow", "krow", "qloc", "kloc", "nkb", "nsub", "first", "last", "b")}
    for b in range(batch):
        for qi in range(nq):
            n_tiles = ((qi + 1) * Q_TILE + K_TILE - 1) // K_TILE
            n = (n_tiles + K_SUBS - 1) // K_SUBS
            for ki in range(n):
                cols["qrow"].append(b * nq + qi)
                cols["krow"].append(b * nk + ki)
                cols["qloc"].append(qi)
                cols["kloc"].append(ki)
                cols["nkb"].append(n_tiles)
                cols["nsub"].append(min(K_SUBS, n_tiles - ki * K_SUBS))
                cols["first"].append(int(ki == 0))
                cols["last"].append(int(ki == n - 1))
                cols["b"].append(b)
    return {k: jnp.asarray(np.asarray(v, np.int32)) for k, v in cols.items()}


def _half_select(x_t, first_half):
    row = lax.broadcasted_iota(jnp.int32, x_t.shape, 0)
    keep = (row < HEAD_DIM) if first_half else (row >= HEAD_DIM)
    return jnp.where(keep, x_t, jnp.zeros_like(x_t))


def _zero_after(x):
    w = lax.bitcast_convert_type(x, jnp.uint32)
    w = lax.shift_right_logical(lax.shift_right_logical(w, jnp.uint32(16)), jnp.uint32(16))
    return lax.bitcast_convert_type(w, F32)[0:1, :]


def _flash_update(s_t, v_t, h, m_sc, acc_sc, after=None):
    m_prev = m_sc[h]
    m_new = jnp.maximum(m_prev, jnp.max(s_t, axis=0, keepdims=True))
    alpha = jnp.exp2(m_prev - m_new)
    if after is None:
        p_t = jnp.exp2(s_t - m_new).astype(BF16)
    else:
        cut = (3 * s_t.shape[0]) // 4
        p_t = jnp.concatenate([jnp.exp2(s_t[:cut] - m_new),
                               jnp.exp2(s_t[cut:] - (m_new + after))], axis=0).astype(BF16)
    acc_sc[h] = alpha * acc_sc[h] + jnp.dot(v_t, p_t, preferred_element_type=F32)
    m_sc[h] = m_new


def _both_halves(x_t):
    return jnp.concatenate([_half_select(x_t, True), _half_select(x_t, False)], axis=1)


def _attention_sweep(qt_ref, k_ref, vt_ref, sub, bias, m_sc, acc_sc):
    rows = pl.ds(pl.multiple_of(sub * K_TILE, K_TILE), K_TILE)

    def scores(j):
        return jnp.dot(k_ref[j, rows, :], _both_halves(qt_ref[j]), preferred_element_type=F32)

    ahead = 2
    pending = [scores(j) for j in range(ahead)]
    for j in range(N_PAIRS):
        s_t = pending.pop(0)
        if bias is not None:
            s_t = s_t + bias
        after = None
        if j + ahead < N_PAIRS:
            pending.append(scores(j + ahead))
            after = _zero_after(pending[-1][0:8, :])
        _flash_update(s_t, vt_ref[j, sub], j, m_sc, acc_sc, after=after)


def _diff_attn_kernel(qrow, krow, qloc, kloc, nsub, first, last,
                      qt_ref, k_ref, vt_ref, lam_ref, subln_ref, o_ref, m_sc, acc_sc,
                      *, lambda_init):
    p = pl.program_id(0)

    @pl.when(first[p] == 1)
    def _():
        m_sc[...] = jnp.full(m_sc.shape, -jnp.inf, F32)
        acc_sc[...] = jnp.zeros(acc_sc.shape, F32)

    def key_tile(sub, carry):
        tile = kloc[p] * K_SUBS + sub
        on_diagonal = tile * K_TILE + (K_TILE - 1) > qloc[p] * Q_TILE

        @pl.when(on_diagonal)
        def _():
            kpos = tile * K_TILE + lax.broadcasted_iota(jnp.int32, (K_TILE, 2 * Q_TILE), 0)
            qcol = lax.broadcasted_iota(jnp.int32, (K_TILE, 2 * Q_TILE), 1)
            qpos = qloc[p] * Q_TILE + jnp.where(qcol >= Q_TILE, qcol - Q_TILE, qcol)
            _attention_sweep(qt_ref, k_ref, vt_ref, sub, jnp.where(kpos <= qpos, 0.0, -jnp.inf).astype(F32),
                             m_sc, acc_sc)

        @pl.when(jnp.logical_not(on_diagonal))
        def _():
            _attention_sweep(qt_ref, k_ref, vt_ref, sub, None, m_sc, acc_sc)

        return carry

    lax.fori_loop(0, nsub[p], key_tile, 0)

    @pl.when(last[p] == 1)
    def _():
        lam_rows = lam_ref[...]
        lam = (jnp.exp(jnp.sum(lam_rows[0:1] * lam_rows[1:2], axis=-1, keepdims=True))
               - jnp.exp(jnp.sum(lam_rows[2:3] * lam_rows[3:4], axis=-1, keepdims=True))
               + lambda_init)
        vdim = 2 * HEAD_DIM
        for j in range(N_PAIRS):
            a = acc_sc[j]
            a1 = a[:, :Q_TILE]
            a2 = a[:, Q_TILE:]
            o = a1[:vdim] / a1[vdim:vdim + 1] - lam * (a2[:vdim] / a2[vdim:vdim + 1])
            ms = jnp.mean(o * o, axis=0, keepdims=True)
            o = o * lax.rsqrt(ms + RMS_EPS) * subln_ref[...] * (1.0 - lambda_init)
            o_ref[j] = o.T.astype(BF16)


def _diff_attention(q_t, qkv, v_t, lam_rows, subln_col, steps, lambda_init):
    rows = qkv.shape[1]
    n_steps = steps["qrow"].shape[0]
    vrows = v_t.shape[2]

    def im(fn):
        return lambda p, qr, kr, ql, kl, ns, f, l: fn(p, qr, kr)

    grid_spec = pltpu.PrefetchScalarGridSpec(
        num_scalar_prefetch=7,
        grid=(n_steps,),
        in_specs=[
            pl.BlockSpec((N_PAIRS, LANES, Q_TILE), im(lambda p, qr, kr: (0, 0, qr[p]))),
            pl.BlockSpec((N_PAIRS, K_SUBS * K_TILE, LANES), im(lambda p, qr, kr: (1, kr[p], 0))),
            pl.BlockSpec((N_PAIRS, K_SUBS, vrows, K_TILE), im(lambda p, qr, kr: (0, kr[p], 0, 0))),
            pl.BlockSpec((8, LANES), im(lambda p, qr, kr: (0, 0))),
            pl.BlockSpec((LANES, 1), im(lambda p, qr, kr: (0, 0))),
        ],
        out_specs=pl.BlockSpec((N_PAIRS, Q_TILE, LANES), im(lambda p, qr, kr: (0, qr[p], 0))),
        scratch_shapes=[pltpu.VMEM((N_PAIRS, 1, 2 * Q_TILE), F32),
                        pltpu.VMEM((N_PAIRS, vrows, 2 * Q_TILE), F32)],
    )
    return pl.pallas_call(
        functools.partial(_diff_attn_kernel, lambda_init=lambda_init),
        grid_spec=grid_spec,
        out_shape=jax.ShapeDtypeStruct((N_PAIRS, rows, LANES), BF16),
        compiler_params=pltpu.CompilerParams(dimension_semantics=("arbitrary",),
                                             vmem_limit_bytes=VMEM_LIMIT),
        name="diff_attn",
    )(steps["qrow"], steps["krow"], steps["qloc"], steps["kloc"], steps["nsub"], steps["first"],
      steps["last"], q_t, qkv, v_t, lam_rows, subln_col)


def _dsa_kernel(qrow, krow, qloc, kloc, nkb, nsub, first, last, bidx,
                qt_ref, k_ref, vt_ref, qit_ref, wit_ref, kidx_ref, o_ref,
                key_sc, thr_sc, m_sc, acc_sc, *, topk):
    p = pl.program_id(0)

    @pl.when(first[p] == 1)
    def _():
        m_sc[...] = jnp.full(m_sc.shape, -jnp.inf, F32)
        acc_sc[...] = jnp.zeros(acc_sc.shape, F32)
        n = nkb[p]
        wi = wit_ref[...]
        qpos = qloc[p] * Q_TILE + lax.broadcasted_iota(jnp.int32, (K_TILE, Q_TILE), 1)
        krow_iota = lax.broadcasted_iota(jnp.int32, (K_TILE, Q_TILE), 0)

        def score_chunk(c, carry):
            kt = kidx_ref[pl.ds(pl.multiple_of(c * K_TILE, K_TILE), K_TILE), :]
            sc = jnp.zeros((K_TILE, Q_TILE), F32)
            for jp in range(IDX_PAIRS):
                qi_t = qit_ref[jp]
                for cc in range(2):
                    raw = jnp.dot(kt, _half_select(qi_t, cc == 0), preferred_element_type=F32)
                    hh = 2 * jp + cc
                    sc = sc + jnp.maximum(raw, 0.0) * wi[hh:hh + 1, :]
            kpos = c * K_TILE + krow_iota
            sc = jnp.where(kpos < N_META, BIG_SCORE, sc)
            sc = jnp.where(kpos <= qpos, sc, -jnp.inf)
            bits = lax.bitcast_convert_type(sc, jnp.int32)
            key_sc[c] = bits ^ ((bits >> 31) & 0x7FFFFFFF)
            return carry

        lax.fori_loop(0, n, score_chunk, 0)

        q1 = qloc[p] * Q_TILE + lax.broadcasted_iota(jnp.int32, (1, Q_TILE), 1)
        settled0 = (q1 < topk).astype(jnp.int32)

        def unsettled(state):
            b, _, settled = state
            return jnp.logical_and(b < 32, jnp.min(settled) == 0)

        def bit_body(state):
            b, thr, settled = state
            cand = thr + jnp.left_shift(jnp.int32(1), 31 - b)

            def count_chunk(c, cnt8):
                hit = jnp.where(key_sc[c] >= cand, 1.0, 0.0).reshape(8, K_TILE // 64, 8, Q_TILE)
                return cnt8 + jnp.sum(jnp.sum(hit, axis=1), axis=0)

            cnt8 = lax.fori_loop(0, n, count_chunk, jnp.zeros((8, Q_TILE), F32))
            cnt = jnp.sum(cnt8, axis=0, keepdims=True)
            thr = jnp.where(cnt >= float(topk), cand, thr)
            settled = jnp.where(cnt == float(topk), 1, settled)
            return b + 1, thr, settled

        _, thr, _ = lax.while_loop(unsettled, bit_body,
                                   (jnp.int32(0), jnp.full((1, Q_TILE), INT_MIN, jnp.int32), settled0))
        thr_sc[...] = jnp.maximum(thr, KEY_NEG_INF + 1)

    def key_tile(sub, carry):
        sel = key_sc[kloc[p] * K_SUBS + sub] >= thr_sc[...]
        bias1 = jnp.where(sel, 0.0, -jnp.inf).astype(F32)
        bias = jnp.concatenate([bias1, bias1], axis=1)
        _attention_sweep(qt_ref, k_ref, vt_ref, sub, bias, m_sc, acc_sc)
        return carry

    lax.fori_loop(0, nsub[p], key_tile, 0)

    @pl.when(last[p] == 1)
    def _():
        hv = vt_ref.shape[2] // 2
        for j in range(N_PAIRS):
            a = acc_sc[j][:hv, :Q_TILE]
            b = acc_sc[j][hv:, Q_TILE:]
            o = jnp.concatenate([a[:HEAD_DIM] / a[HEAD_DIM:HEAD_DIM + 1],
                                 b[:HEAD_DIM] / b[HEAD_DIM:HEAD_DIM + 1]], axis=0)
            o_ref[j] = o.T.astype(BF16)


def _dsa_attention(q_t, qkvi, v_t, qi_t, wi_t, kidx, steps, topk, t_pad):
    rows = qkvi.shape[1]
    n_steps = steps["qrow"].shape[0]
    nkb_total = t_pad // K_TILE
    vrows = v_t.shape[2]

    def im(fn):
        return lambda p, qr, kr, ql, kl, nk, ns, f, l, b: fn(p, qr, kr, b)

    grid_spec = pltpu.PrefetchScalarGridSpec(
        num_scalar_prefetch=9,
        grid=(n_steps,),
        in_specs=[
            pl.BlockSpec((N_PAIRS, LANES, Q_TILE), im(lambda p, qr, kr, b: (0, 0, qr[p]))),
            pl.BlockSpec((N_PAIRS, K_SUBS * K_TILE, LANES), im(lambda p, qr, kr, b: (1, kr[p], 0))),
            pl.BlockSpec((N_PAIRS, K_SUBS, vrows, K_TILE), im(lambda p, qr, kr, b: (0, kr[p], 0, 0))),
            pl.BlockSpec((IDX_PAIRS, LANES, Q_TILE), im(lambda p, qr, kr, b: (0, 0, qr[p]))),
            pl.BlockSpec((IDX_HEADS, Q_TILE), im(lambda p, qr, kr, b: (0, qr[p]))),
            pl.BlockSpec((t_pad, LANES), im(lambda p, qr, kr, b: (b[p], 0))),
        ],
        out_specs=pl.BlockSpec((N_PAIRS, Q_TILE, LANES), im(lambda p, qr, kr, b: (0, qr[p], 0))),
        scratch_shapes=[pltpu.VMEM((nkb_total, K_TILE, Q_TILE), jnp.int32),
                        pltpu.VMEM((1, Q_TILE), jnp.int32),
                        pltpu.VMEM((N_PAIRS, 1, 2 * Q_TILE), F32),
                        pltpu.VMEM((N_PAIRS, vrows, 2 * Q_TILE), F32)],
    )
    return pl.pallas_call(
        functools.partial(_dsa_kernel, topk=topk),
        grid_spec=grid_spec,
        out_shape=jax.ShapeDtypeStruct((N_PAIRS, rows, LANES), BF16),
        compiler_params=pltpu.CompilerParams(dimension_semantics=("arbitrary",),
                                             vmem_limit_bytes=VMEM_LIMIT),
        name="dsa_attn",
    )(steps["qrow"], steps["krow"], steps["qloc"], steps["kloc"], steps["nkb"], steps["nsub"],
      steps["first"], steps["last"], steps["b"], q_t, qkvi, v_t, qi_t, wi_t, kidx)


def _rope_tables(t_pad):
    inv = ROPE_THETA ** (-jnp.arange(0, HEAD_DIM, 2, dtype=F32) / HEAD_DIM)
    ang = jnp.arange(t_pad, dtype=F32)[:, None] * inv[None, :]
    cos, sin = jnp.cos(ang), jnp.sin(ang)
    cos128 = jnp.tile(cos, (1, LANES // (HEAD_DIM // 2)))
    sin128 = jnp.tile(jnp.concatenate([-sin, sin], axis=1), (1, LANES // HEAD_DIM))
    return cos128, sin128


def _interleave_gate_val(a):
    lead = a.shape[:-1]
    a = a.reshape(lead + (2, FFN_HIDDEN // FFN_TILE, FFN_TILE))
    a = jnp.swapaxes(a, -3, -2)
    return a.reshape(lead + (2 * FFN_HIDDEN,))


def _with_ones_rows(v_t):
    ones = jnp.ones((v_t.shape[0], BF16_ROWS, v_t.shape[2]), v_t.dtype)
    return jnp.concatenate([v_t, ones], axis=1)


def _key_tiled(v_t):
    pairs, d, rows = v_t.shape
    return jnp.swapaxes(v_t.reshape(pairs, d, rows // K_TILE, K_TILE), 1, 2)


def kernel(x, meta_tokens, da_norm, da_w_qkv, da_lambda_q1, da_lambda_k1, da_lambda_q2, da_lambda_k2, da_subln, da_w_o, dsa_norm, dsa_w_in, dsa_idx_k_norm, dsa_w_o, ffn_norm, ffn_w_up, ffn_conv_w, ffn_conv_b, ffn_w_down, final_norm):
    batch, seq, d = x.shape
    assert d == D_MODEL
    depth = ffn_norm.shape[0]
    t_real = seq + N_META
    t_pad = _padded_len(t_real)
    assert t_pad % (K_SUBS * K_TILE) == 0 and t_pad % Q_TILE == 0
    rows = batch * t_pad

    meta = jnp.broadcast_to(meta_tokens[None].astype(x.dtype), (batch, N_META, d))
    h = jnp.concatenate([meta, x, jnp.zeros((batch, t_pad - t_real, d), x.dtype)], axis=1)
    h = h.reshape(rows, d)
    cos128, sin128 = _rope_tables(t_pad)
    steps = _attention_steps(batch, t_pad)

    dsa_qkv = N_HEADS * HEAD_DIM
    idx_cols = IDX_HEADS * HEAD_DIM
    for i in range(depth):
        j = i // 2
        if i % 2 == 0:
            lambda_init = 0.8 - 0.6 * math.exp(-0.3 * i)
            kinds = ["rope_softmax_q"] * N_PAIRS + ["rope"] * N_PAIRS + ["plain"] * N_PAIRS
            qkv = _project(h, da_norm[j], da_w_qkv[j].astype(BF16), cos128, sin128, kinds, t_pad)
            q_t = jnp.swapaxes(qkv[:N_PAIRS], 1, 2)
            v_t = _key_tiled(_with_ones_rows(jnp.swapaxes(qkv[2 * N_PAIRS:], 1, 2)))
            lam_rows = jnp.zeros((8, LANES), F32).at[0:4, 0:HEAD_DIM].set(
                jnp.stack([da_lambda_q1[j], da_lambda_k1[j], da_lambda_q2[j], da_lambda_k2[j]]).astype(F32))
            o = _diff_attention(q_t, qkv, v_t, lam_rows, da_subln[j].reshape(LANES, 1).astype(F32),
                                steps, lambda_init)
            h = _out_project(h, o, da_w_o[j].astype(BF16))
        else:
            w_in = dsa_w_in[j]
            n_main = 3 * dsa_qkv + idx_cols
            kinds = (["rope_softmax_q"] * N_PAIRS + ["rope"] * N_PAIRS + ["plain"] * N_PAIRS
                     + ["rope_scaled"] * IDX_PAIRS)
            w_tail = jnp.zeros((d, LANES), w_in.dtype).at[:, :w_in.shape[1] - n_main].set(w_in[:, n_main:])
            gk = jnp.ones((1, LANES), F32).at[0, :HEAD_DIM].set(dsa_idx_k_norm[j].astype(F32))
            qkvi, tail = _project(h, dsa_norm[j], w_in[:, :n_main].astype(BF16), cos128, sin128, kinds, t_pad,
                                  w_tail=w_tail.astype(BF16), gk=gk)
            q_t = jnp.swapaxes(qkvi[:N_PAIRS], 1, 2)
            v_heads = jnp.swapaxes(qkvi[2 * N_PAIRS:3 * N_PAIRS], 1, 2).reshape(N_HEADS, HEAD_DIM, rows)
            v_t = _with_ones_rows(v_heads)
            v_t = _key_tiled(v_t.reshape(N_PAIRS, 2 * v_t.shape[1], rows))
            qi_t = jnp.swapaxes(qkvi[3 * N_PAIRS:], 1, 2)
            wi_t = tail[:, HEAD_DIM:HEAD_DIM + IDX_HEADS].T
            ki = tail[:, :HEAD_DIM].astype(BF16)
            kidx = jnp.concatenate([ki, ki], axis=1)
            topk = min(TOPK_MAX, seq // 4)
            o = _dsa_attention(q_t, qkvi, v_t, qi_t, wi_t, kidx, steps, topk, t_pad)
            h = _out_project(h, o, dsa_w_o[j].astype(BF16))
        h = _ffn(h, ffn_norm[i], _interleave_gate_val(ffn_w_up[i]).astype(BF16),
                 _interleave_gate_val(ffn_conv_w[i]).astype(F32),
                 _interleave_gate_val(ffn_conv_b[i]).reshape(1, -1).astype(F32),
                 ffn_w_down[i].astype(BF16), final_norm, t_pad, final=(i == depth - 1))
    return h.reshape(batch, t_pad, d)[:, N_META:N_META + seq]
```

```python
import functools
import math

import numpy as np
import jax
import jax.numpy as jnp
from jax import lax
from jax.experimental import pallas as pl
from jax.experimental.pallas import tpu as pltpu

D_MODEL = 1024
N_META = 16
ROPE_THETA = 10000.0
RMS_EPS = 1e-6
HEAD_DIM = 64
N_PAIRS = 8
N_HEADS = 2 * N_PAIRS
IDX_HEADS = 8
IDX_PAIRS = IDX_HEADS // 2
TOPK_MAX = 256
BIG_SCORE = 1e30
LOG2_E = 1.4426950408889634
FFN_HIDDEN = 2816
CONV_WIDTH = 3

LANES = 128
BF16_ROWS = 16
ROW_TILE = 512
Q_TILE = 256
K_TILE = 512
K_SUBS = 3
FFN_TILE = 256
HALO = BF16_ROWS
VMEM_LIMIT = 56 * 1024 * 1024

F32 = jnp.float32
BF16 = jnp.bfloat16
INT_MIN = -2147483648
KEY_NEG_INF = INT_MIN + 0x7FFFFF


def _padded_len(t):
    unit = math.lcm(ROW_TILE, K_SUBS * K_TILE, Q_TILE)
    return ((t + unit - 1) // unit) * unit


def _rms(x, gain):
    ms = jnp.mean(x * x, axis=-1, keepdims=True)
    return x * lax.rsqrt(ms + RMS_EPS) * gain


def _rope_lanes(y, cos, sin, lo):
    sw = jnp.where(lo, pltpu.roll(y, LANES - HEAD_DIM // 2, 1), pltpu.roll(y, HEAD_DIM // 2, 1))
    return y * cos + sw * sin


def _proj_kernel(x_ref, g_ref, w_ref, cos_ref, sin_ref, *rest, kinds, has_tail):
    if has_tail:
        wt_ref, gk_ref, o_ref, t_ref = rest
    else:
        (o_ref,) = rest
    xh = _rms(x_ref[...], g_ref[...]).astype(BF16)
    cos = cos_ref[...]
    sin = sin_ref[...]
    lane = lax.broadcasted_iota(jnp.int32, cos.shape, 1)
    lo = (lane & (HEAD_DIM // 2)) == 0
    n_groups = len(kinds)
    cw = 4 * LANES
    for c in range(n_groups // 4):
        y = jnp.dot(xh, w_ref[:, c * cw:(c + 1) * cw], preferred_element_type=F32)
        for s in range(4):
            g = c * 4 + s
            yg = y[:, s * LANES:(s + 1) * LANES]
            if kinds[g] != "plain":
                yg = _rope_lanes(yg, cos, sin, lo)
                if kinds[g] == "rope_scaled":
                    yg = yg * (HEAD_DIM ** -0.5)
                elif kinds[g] == "rope_softmax_q":
                    yg = yg * (HEAD_DIM ** -0.5 * LOG2_E)
            o_ref[g] = yg.astype(BF16)
    if has_tail:
        t = jnp.dot(xh, wt_ref[...], preferred_element_type=F32)
        is_k = lane < HEAD_DIM
        ms = jnp.sum(jnp.where(is_k, t * t, 0.0), axis=-1, keepdims=True) * (1.0 / HEAD_DIM)
        kn = t * lax.rsqrt(ms + RMS_EPS) * gk_ref[...]
        kn = _rope_lanes(kn, cos, sin, lo)
        t_ref[...] = jnp.where(is_k, kn, t * (IDX_HEADS ** -0.5))


def _project(h, gain, w, cos, sin, kinds, t_pad, w_tail=None, gk=None):
    rows = h.shape[0]
    n_groups = len(kinds)
    blocks_per_batch = t_pad // ROW_TILE
    has_tail = w_tail is not None
    in_specs = [
        pl.BlockSpec((ROW_TILE, D_MODEL), lambda i: (i, 0)),
        pl.BlockSpec((1, D_MODEL), lambda i: (0, 0)),
        pl.BlockSpec(w.shape, lambda i: (0, 0)),
        pl.BlockSpec((ROW_TILE, LANES), lambda i: (i % blocks_per_batch, 0)),
        pl.BlockSpec((ROW_TILE, LANES), lambda i: (i % blocks_per_batch, 0)),
    ]
    out_shape = [jax.ShapeDtypeStruct((n_groups, rows, LANES), BF16)]
    out_specs = [pl.BlockSpec((n_groups, ROW_TILE, LANES), lambda i: (0, i, 0))]
    args = [h, gain.reshape(1, D_MODEL), w, cos, sin]
    if has_tail:
        in_specs += [pl.BlockSpec(w_tail.shape, lambda i: (0, 0)),
                     pl.BlockSpec((1, LANES), lambda i: (0, 0))]
        out_shape.append(jax.ShapeDtypeStruct((rows, LANES), F32))
        out_specs.append(pl.BlockSpec((ROW_TILE, LANES), lambda i: (i, 0)))
        args += [w_tail, gk]
    return pl.pallas_call(
        functools.partial(_proj_kernel, kinds=tuple(kinds), has_tail=has_tail),
        grid=(rows // ROW_TILE,),
        in_specs=in_specs,
        out_specs=out_specs if has_tail else out_specs[0],
        out_shape=out_shape if has_tail else out_shape[0],
        compiler_params=pltpu.CompilerParams(dimension_semantics=("arbitrary",),
                                             vmem_limit_bytes=VMEM_LIMIT),
        name="proj_tail" if has_tail else "proj",
    )(*args)


def _oproj_kernel(h_ref, o_ref, w_ref, out_ref):
    o = jnp.concatenate([o_ref[j] for j in range(N_PAIRS)], axis=1)
    out_ref[...] = h_ref[...] + jnp.dot(o, w_ref[...], preferred_element_type=F32)


def _out_project(h, o, w):
    rows = h.shape[0]
    return pl.pallas_call(
        _oproj_kernel,
        grid=(rows // ROW_TILE,),
        in_specs=[pl.BlockSpec((ROW_TILE, D_MODEL), lambda i: (i, 0)),
                  pl.BlockSpec((N_PAIRS, ROW_TILE, LANES), lambda i: (0, i, 0)),
                  pl.BlockSpec(w.shape, lambda i: (0, 0))],
        out_specs=pl.BlockSpec((ROW_TILE, D_MODEL), lambda i: (i, 0)),
        out_shape=jax.ShapeDtypeStruct(h.shape, F32),
        compiler_params=pltpu.CompilerParams(dimension_semantics=("arbitrary",),
                                             vmem_limit_bytes=VMEM_LIMIT),
        name="oproj",
    )(h, o, w)


def _ffn_kernel(xp_ref, x_ref, g_ref, wup_ref, cw_ref, cb_ref, wdn_ref, fg_ref, o_ref, u_sc,
                *, blocks_per_batch, final):
    i = pl.program_id(0)
    x = x_ref[...]
    xe = jnp.concatenate([xp_ref[...], x], axis=0)
    xh = _rms(xe, g_ref[...])
    row = lax.broadcasted_iota(jnp.int32, (HALO + ROW_TILE, 1), 0)
    keep = jnp.logical_or(row >= HALO, i % blocks_per_batch != 0)
    xh = jnp.where(keep, xh, 0.0).astype(BF16)
    acc = jnp.zeros((ROW_TILE, D_MODEL), F32)
    cw2 = 2 * FFN_TILE
    for c in range(FFN_HIDDEN // FFN_TILE):
        u_sc[...] = jnp.dot(xh, wup_ref[:, c * cw2:(c + 1) * cw2], preferred_element_type=F32)
        cwc = cw_ref[:, c * cw2:(c + 1) * cw2]
        conv = cb_ref[:, c * cw2:(c + 1) * cw2]
        for j in range(CONV_WIDTH):
            conv = conv + cwc[j:j + 1, :] * u_sc[pl.ds(HALO - (CONV_WIDTH - 1) + j, ROW_TILE), :]
        gate = conv[:, :FFN_TILE]
        val = conv[:, FFN_TILE:]
        a = (gate * jax.nn.sigmoid(gate) * val).astype(BF16)
        acc = acc + jnp.dot(a, wdn_ref[c * FFN_TILE:(c + 1) * FFN_TILE, :], preferred_element_type=F32)
    y = x + acc
    if final:
        y = _rms(y, fg_ref[...])
    o_ref[...] = y


def _ffn(h, gain, w_up, conv_w, conv_b, w_down, final_gain, t_pad, final):
    rows = h.shape[0]
    blocks_per_batch = t_pad // ROW_TILE
    halo_blocks = ROW_TILE // HALO
    return pl.pallas_call(
        functools.partial(_ffn_kernel, blocks_per_batch=blocks_per_batch, final=final),
        grid=(rows // ROW_TILE,),
        in_specs=[pl.BlockSpec((HALO, D_MODEL), lambda i: (jnp.maximum(i * halo_blocks - 1, 0), 0)),
                  pl.BlockSpec((ROW_TILE, D_MODEL), lambda i: (i, 0)),
                  pl.BlockSpec((1, D_MODEL), lambda i: (0, 0)),
                  pl.BlockSpec(w_up.shape, lambda i: (0, 0)),
                  pl.BlockSpec(conv_w.shape, lambda i: (0, 0)),
                  pl.BlockSpec(conv_b.shape, lambda i: (0, 0)),
                  pl.BlockSpec(w_down.shape, lambda i: (0, 0)),
                  pl.BlockSpec((1, D_MODEL), lambda i: (0, 0))],
        out_specs=pl.BlockSpec((ROW_TILE, D_MODEL), lambda i: (i, 0)),
        out_shape=jax.ShapeDtypeStruct(h.shape, F32),
        scratch_shapes=[pltpu.VMEM((HALO + ROW_TILE, 2 * FFN_TILE), F32)],
        compiler_params=pltpu.CompilerParams(dimension_semantics=("arbitrary",),
                                             vmem_limit_bytes=VMEM_LIMIT),
        name="ffn_final" if final else "ffn",
    )(h, h, gain.reshape(1, D_MODEL), w_up, conv_w, conv_b, w_down, final_gain.reshape(1, D_MODEL))


def _attention_steps(batch, t_pad):
    nq = t_pad // Q_TILE
    nk = t_pad // (K_SUBS * K_TILE)
    cols = {k: [] for k in ("qrow", "krow", "qloc", "kloc", "nkb", "nsub", "full", "first", "last", "b")}
    for b in range(batch):
        for qi in range(nq):
            n_tiles = ((qi + 1) * Q_TILE + K_TILE - 1) // K_TILE
            n = (n_tiles + K_SUBS - 1) // K_SUBS
            for ki in range(n):
                cols["qrow"].append(b * nq + qi)
                cols["krow"].append(b * nk + ki)
                cols["qloc"].append(qi)
                cols["kloc"].append(ki)
                cols["nkb"].append(n_tiles)
                cols["nsub"].append(min(K_SUBS, n_tiles - ki * K_SUBS))
                cols["full"].append(int((ki + 1) * K_SUBS < n_tiles))
                cols["first"].append(int(ki == 0))
                cols["last"].append(int(ki == n - 1))
                cols["b"].append(b)
    return {k: jnp.asarray(np.asarray(v, np.int32)) for k, v in cols.items()}


def _half_select(x_t, first_half):
    row = lax.broadcasted_iota(jnp.int32, x_t.shape, 0)
    keep = (row < HEAD_DIM) if first_half else (row >= HEAD_DIM)
    return jnp.where(keep, x_t, jnp.zeros_like(x_t))


def _zero_after(x):
    w = lax.bitcast_convert_type(x, jnp.uint32)
    w = lax.shift_right_logical(lax.shift_right_logical(w, jnp.uint32(16)), jnp.uint32(16))
    return lax.bitcast_convert_type(w, F32)[0:1, :]


def _flash_update(s_t, v_t, h, m_sc, acc_sc, after=None):
    m_prev = m_sc[h]
    m_new = jnp.maximum(m_prev, jnp.max(s_t, axis=0, keepdims=True))
    alpha = jnp.exp2(m_prev - m_new)
    if after is None:
        p_t = jnp.exp2(s_t - m_new).astype(BF16)
    else:
        cut = (3 * s_t.shape[0]) // 4
        p_t = jnp.concatenate([jnp.exp2(s_t[:cut] - m_new),
                               jnp.exp2(s_t[cut:] - (m_new + after))], axis=0).astype(BF16)
    acc_sc[h] = alpha * acc_sc[h] + jnp.dot(v_t, p_t, preferred_element_type=F32)
    m_sc[h] = m_new


def _both_halves(x_t):
    return jnp.concatenate([_half_select(x_t, True), _half_select(x_t, False)], axis=1)


def _attention_sweep(qt_ref, k_ref, vt_ref, subs, bias_of, m_sc, acc_sc):
    def scores(sub, j):
        start = sub * K_TILE
        if not isinstance(sub, int):
            start = pl.multiple_of(start, K_TILE)
        return jnp.dot(k_ref[j, pl.ds(start, K_TILE), :], _both_halves(qt_ref[j]), preferred_element_type=F32)

    items = [(sub, j) for sub in subs for j in range(N_PAIRS)]
    biases = {}
    ahead = 2
    pending = [scores(*item) for item in items[:ahead]]
    for i, (sub, j) in enumerate(items):
        s_t = pending.pop(0)
        tile_key = sub if isinstance(sub, int) else "traced"
        if tile_key not in biases:
            biases[tile_key] = bias_of(sub)
        if biases[tile_key] is not None:
            s_t = s_t + biases[tile_key]
        after = None
        if i + ahead < len(items):
            pending.append(scores(*items[i + ahead]))
            after = _zero_after(pending[-1][0:8, :])
        _flash_update(s_t, vt_ref[j, sub], j, m_sc, acc_sc, after=after)


def _diff_attn_kernel(qrow, krow, qloc, kloc, nsub, full, first, last,
                      qt_ref, k_ref, vt_ref, lam_ref, subln_ref, o_ref, m_sc, acc_sc,
                      *, lambda_init):
    p = pl.program_id(0)

    @pl.when(first[p] == 1)
    def _():
        m_sc[...] = jnp.full(m_sc.shape, -jnp.inf, F32)
        acc_sc[...] = jnp.zeros(acc_sc.shape, F32)

    def key_tile(sub, carry):
        tile = kloc[p] * K_SUBS + sub
        on_diagonal = tile * K_TILE + (K_TILE - 1) > qloc[p] * Q_TILE

        @pl.when(on_diagonal)
        def _():
            kpos = tile * K_TILE + lax.broadcasted_iota(jnp.int32, (K_TILE, 2 * Q_TILE), 0)
            qcol = lax.broadcasted_iota(jnp.int32, (K_TILE, 2 * Q_TILE), 1)
            qpos = qloc[p] * Q_TILE + jnp.where(qcol >= Q_TILE, qcol - Q_TILE, qcol)
            bias = jnp.where(kpos <= qpos, 0.0, -jnp.inf).astype(F32)
            _attention_sweep(qt_ref, k_ref, vt_ref, (sub,), lambda _: bias, m_sc, acc_sc)

        @pl.when(jnp.logical_not(on_diagonal))
        def _():
            _attention_sweep(qt_ref, k_ref, vt_ref, (sub,), lambda _: None, m_sc, acc_sc)

        return carry

    @pl.when(full[p] == 1)
    def _():
        _attention_sweep(qt_ref, k_ref, vt_ref, tuple(range(K_SUBS)), lambda _: None, m_sc, acc_sc)

    @pl.when(full[p] == 0)
    def _():
        lax.fori_loop(0, nsub[p], key_tile, 0)

    @pl.when(last[p] == 1)
    def _():
        lam_rows = lam_ref[...]
        lam = (jnp.exp(jnp.sum(lam_rows[0:1] * lam_rows[1:2], axis=-1, keepdims=True))
               - jnp.exp(jnp.sum(lam_rows[2:3] * lam_rows[3:4], axis=-1, keepdims=True))
               + lambda_init)
        vdim = 2 * HEAD_DIM
        for j in range(N_PAIRS):
            a = acc_sc[j]
            a1 = a[:, :Q_TILE]
            a2 = a[:, Q_TILE:]
            o = a1[:vdim] / a1[vdim:vdim + 1] - lam * (a2[:vdim] / a2[vdim:vdim + 1])
            ms = jnp.mean(o * o, axis=0, keepdims=True)
            o = o * lax.rsqrt(ms + RMS_EPS) * subln_ref[...] * (1.0 - lambda_init)
            o_ref[j] = o.T.astype(BF16)


def _diff_attention(q_t, qkv, v_t, lam_rows, subln_col, steps, lambda_init):
    rows = qkv.shape[1]
    n_steps = steps["qrow"].shape[0]
    vrows = v_t.shape[2]

    def im(fn):
        return lambda p, qr, kr, ql, kl, ns, fu, f, l: fn(p, qr, kr)

    grid_spec = pltpu.PrefetchScalarGridSpec(
        num_scalar_prefetch=8,
        grid=(n_steps,),
        in_specs=[
            pl.BlockSpec((N_PAIRS, LANES, Q_TILE), im(lambda p, qr, kr: (0, 0, qr[p]))),
            pl.BlockSpec((N_PAIRS, K_SUBS * K_TILE, LANES), im(lambda p, qr, kr: (1, kr[p], 0))),
            pl.BlockSpec((N_PAIRS, K_SUBS, vrows, K_TILE), im(lambda p, qr, kr: (0, kr[p], 0, 0))),
            pl.BlockSpec((8, LANES), im(lambda p, qr, kr: (0, 0))),
            pl.BlockSpec((LANES, 1), im(lambda p, qr, kr: (0, 0))),
        ],
        out_specs=pl.BlockSpec((N_PAIRS, Q_TILE, LANES), im(lambda p, qr, kr: (0, qr[p], 0))),
        scratch_shapes=[pltpu.VMEM((N_PAIRS, 1, 2 * Q_TILE), F32),
                        pltpu.VMEM((N_PAIRS, vrows, 2 * Q_TILE), F32)],
    )
    return pl.pallas_call(
        functools.partial(_diff_attn_kernel, lambda_init=lambda_init),
        grid_spec=grid_spec,
        out_shape=jax.ShapeDtypeStruct((N_PAIRS, rows, LANES), BF16),
        compiler_params=pltpu.CompilerParams(dimension_semantics=("arbitrary",),
                                             vmem_limit_bytes=VMEM_LIMIT),
        name="diff_attn",
    )(steps["qrow"], steps["krow"], steps["qloc"], steps["kloc"], steps["nsub"], steps["full"],
      steps["first"], steps["last"], q_t, qkv, v_t, lam_rows, subln_col)


def _dsa_kernel(qrow, krow, qloc, kloc, nkb, nsub, full, first, last, bidx,
                qt_ref, k_ref, vt_ref, qit_ref, wit_ref, kidx_ref, o_ref,
                key_sc, thr_sc, m_sc, acc_sc, *, topk):
    p = pl.program_id(0)

    @pl.when(first[p] == 1)
    def _():
        m_sc[...] = jnp.full(m_sc.shape, -jnp.inf, F32)
        acc_sc[...] = jnp.zeros(acc_sc.shape, F32)
        n = nkb[p]
        wi = wit_ref[...]
        qpos = qloc[p] * Q_TILE + lax.broadcasted_iota(jnp.int32, (K_TILE, Q_TILE), 1)
        krow_iota = lax.broadcasted_iota(jnp.int32, (K_TILE, Q_TILE), 0)

        def score_chunk(c, carry):
            kt = kidx_ref[pl.ds(pl.multiple_of(c * K_TILE, K_TILE), K_TILE), :]
            sc = jnp.zeros((K_TILE, Q_TILE), F32)
            for jp in range(IDX_PAIRS):
                qi_t = qit_ref[jp]
                for cc in range(2):
                    raw = jnp.dot(kt, _half_select(qi_t, cc == 0), preferred_element_type=F32)
                    hh = 2 * jp + cc
                    sc = sc + jnp.maximum(raw, 0.0) * wi[hh:hh + 1, :]
            kpos = c * K_TILE + krow_iota
            sc = jnp.where(kpos < N_META, BIG_SCORE, sc)
            sc = jnp.where(kpos <= qpos, sc, -jnp.inf)
            bits = lax.bitcast_convert_type(sc, jnp.int32)
            key_sc[c] = bits ^ ((bits >> 31) & 0x7FFFFFFF)
            return carry

        lax.fori_loop(0, n, score_chunk, 0)

        q1 = qloc[p] * Q_TILE + lax.broadcasted_iota(jnp.int32, (1, Q_TILE), 1)
        settled0 = (q1 < topk).astype(jnp.int32)

        def unsettled(state):
            b, _, settled = state
            return jnp.logical_and(b < 32, jnp.min(settled) == 0)

        def bit_body(state):
            b, thr, settled = state
            cand = thr + jnp.left_shift(jnp.int32(1), 31 - b)

            def count_chunk(c, cnt8):
                hit = jnp.where(key_sc[c] >= cand, 1.0, 0.0).reshape(8, K_TILE // 64, 8, Q_TILE)
                return cnt8 + jnp.sum(jnp.sum(hit, axis=1), axis=0)

            cnt8 = lax.fori_loop(0, n, count_chunk, jnp.zeros((8, Q_TILE), F32))
            cnt = jnp.sum(cnt8, axis=0, keepdims=True)
            thr = jnp.where(cnt >= float(topk), cand, thr)
            settled = jnp.where(cnt == float(topk), 1, settled)
            return b + 1, thr, settled

        _, thr, _ = lax.while_loop(unsettled, bit_body,
                                   (jnp.int32(0), jnp.full((1, Q_TILE), INT_MIN, jnp.int32), settled0))
        thr_sc[...] = jnp.maximum(thr, KEY_NEG_INF + 1)

    def mask_bias(sub):
        sel = key_sc[kloc[p] * K_SUBS + sub] >= thr_sc[...]
        bias1 = jnp.where(sel, 0.0, -jnp.inf).astype(F32)
        return jnp.concatenate([bias1, bias1], axis=1)

    def key_tile(sub, carry):
        _attention_sweep(qt_ref, k_ref, vt_ref, (sub,), mask_bias, m_sc, acc_sc)
        return carry

    @pl.when(full[p] == 1)
    def _():
        _attention_sweep(qt_ref, k_ref, vt_ref, tuple(range(K_SUBS)), mask_bias, m_sc, acc_sc)

    @pl.when(full[p] == 0)
    def _():
        lax.fori_loop(0, nsub[p], key_tile, 0)

    @pl.when(last[p] == 1)
    def _():
        ones_row = 2 * HEAD_DIM
        for j in range(N_PAIRS):
            a = acc_sc[j]
            o = jnp.concatenate([a[:HEAD_DIM, :Q_TILE] / a[ones_row:ones_row + 1, :Q_TILE],
                                 a[HEAD_DIM:ones_row, Q_TILE:] / a[ones_row:ones_row + 1, Q_TILE:]],
                                axis=0)
            o_ref[j] = o.T.astype(BF16)


def _dsa_attention(q_t, qkvi, v_t, qi_t, wi_t, kidx, steps, topk, t_pad):
    rows = qkvi.shape[1]
    n_steps = steps["qrow"].shape[0]
    nkb_total = t_pad // K_TILE
    vrows = v_t.shape[2]

    def im(fn):
        return lambda p, qr, kr, ql, kl, nk, ns, fu, f, l, b: fn(p, qr, kr, b)

    grid_spec = pltpu.PrefetchScalarGridSpec(
        num_scalar_prefetch=10,
        grid=(n_steps,),
        in_specs=[
            pl.BlockSpec((N_PAIRS, LANES, Q_TILE), im(lambda p, qr, kr, b: (0, 0, qr[p]))),
            pl.BlockSpec((N_PAIRS, K_SUBS * K_TILE, LANES), im(lambda p, qr, kr, b: (1, kr[p], 0))),
            pl.BlockSpec((N_PAIRS, K_SUBS, vrows, K_TILE), im(lambda p, qr, kr, b: (0, kr[p], 0, 0))),
            pl.BlockSpec((IDX_PAIRS, LANES, Q_TILE), im(lambda p, qr, kr, b: (0, 0, qr[p]))),
            pl.BlockSpec((IDX_HEADS, Q_TILE), im(lambda p, qr, kr, b: (0, qr[p]))),
            pl.BlockSpec((t_pad, LANES), im(lambda p, qr, kr, b: (b[p], 0))),
        ],
        out_specs=pl.BlockSpec((N_PAIRS, Q_TILE, LANES), im(lambda p, qr, kr, b: (0, qr[p], 0))),
        scratch_shapes=[pltpu.VMEM((nkb_total, K_TILE, Q_TILE), jnp.int32),
                        pltpu.VMEM((1, Q_TILE), jnp.int32),
                        pltpu.VMEM((N_PAIRS, 1, 2 * Q_TILE), F32),
                        pltpu.VMEM((N_PAIRS, vrows, 2 * Q_TILE), F32)],
    )
    return pl.pallas_call(
        functools.partial(_dsa_kernel, topk=topk),
        grid_spec=grid_spec,
        out_shape=jax.ShapeDtypeStruct((N_PAIRS, rows, LANES), BF16),
        compiler_params=pltpu.CompilerParams(dimension_semantics=("arbitrary",),
                                             vmem_limit_bytes=VMEM_LIMIT),
        name="dsa_attn",
    )(steps["qrow"], steps["krow"], steps["qloc"], steps["kloc"], steps["nkb"], steps["nsub"],
      steps["full"], steps["first"], steps["last"], steps["b"], q_t, qkvi, v_t, qi_t, wi_t, kidx)


def _rope_tables(t_pad):
    inv = ROPE_THETA ** (-jnp.arange(0, HEAD_DIM, 2, dtype=F32) / HEAD_DIM)
    ang = jnp.arange(t_pad, dtype=F32)[:, None] * inv[None, :]
    cos, sin = jnp.cos(ang), jnp.sin(ang)
    cos128 = jnp.tile(cos, (1, LANES // (HEAD_DIM // 2)))
    sin128 = jnp.tile(jnp.concatenate([-sin, sin], axis=1), (1, LANES // HEAD_DIM))
    return cos128, sin128


def _interleave_gate_val(a):
    lead = a.shape[:-1]
    a = a.reshape(lead + (2, FFN_HIDDEN // FFN_TILE, FFN_TILE))
    a = jnp.swapaxes(a, -3, -2)
    return a.reshape(lead + (2 * FFN_HIDDEN,))


def _with_ones_rows(v_t):
    ones = jnp.ones((v_t.shape[0], BF16_ROWS, v_t.shape[2]), v_t.dtype)
    return jnp.concatenate([v_t, ones], axis=1)


def _key_tiled(v_t):
    pairs, d, rows = v_t.shape
    return jnp.swapaxes(v_t.reshape(pairs, d, rows // K_TILE, K_TILE), 1, 2)


def kernel(x, meta_tokens, da_norm, da_w_qkv, da_lambda_q1, da_lambda_k1, da_lambda_q2, da_lambda_k2, da_subln, da_w_o, dsa_norm, dsa_w_in, dsa_idx_k_norm, dsa_w_o, ffn_norm, ffn_w_up, ffn_conv_w, ffn_conv_b, ffn_w_down, final_norm):
    batch, seq, d = x.shape
    assert d == D_MODEL
    depth = ffn_norm.shape[0]
    t_real = seq + N_META
    t_pad = _padded_len(t_real)
    assert t_pad % (K_SUBS * K_TILE) == 0 and t_pad % Q_TILE == 0
    rows = batch * t_pad

    meta = jnp.broadcast_to(meta_tokens[None].astype(x.dtype), (batch, N_META, d))
    h = jnp.concatenate([meta, x, jnp.zeros((batch, t_pad - t_real, d), x.dtype)], axis=1)
    h = h.reshape(rows, d)
    cos128, sin128 = _rope_tables(t_pad)
    steps = _attention_steps(batch, t_pad)

    dsa_qkv = N_HEADS * HEAD_DIM
    idx_cols = IDX_HEADS * HEAD_DIM
    for i in range(depth):
        j = i // 2
        if i % 2 == 0:
            lambda_init = 0.8 - 0.6 * math.exp(-0.3 * i)
            kinds = ["rope_softmax_q"] * N_PAIRS + ["rope"] * N_PAIRS + ["plain"] * N_PAIRS
            qkv = _project(h, da_norm[j], da_w_qkv[j].astype(BF16), cos128, sin128, kinds, t_pad)
            q_t = jnp.swapaxes(qkv[:N_PAIRS], 1, 2)
            v_t = _key_tiled(_with_ones_rows(jnp.swapaxes(qkv[2 * N_PAIRS:], 1, 2)))
            lam_rows = jnp.zeros((8, LANES), F32).at[0:4, 0:HEAD_DIM].set(
                jnp.stack([da_lambda_q1[j], da_lambda_k1[j], da_lambda_q2[j], da_lambda_k2[j]]).astype(F32))
            o = _diff_attention(q_t, qkv, v_t, lam_rows, da_subln[j].reshape(LANES, 1).astype(F32),
                                steps, lambda_init)
            h = _out_project(h, o, da_w_o[j].astype(BF16))
        else:
            w_in = dsa_w_in[j]
            n_main = 3 * dsa_qkv + idx_cols
            kinds = (["rope_softmax_q"] * N_PAIRS + ["rope"] * N_PAIRS + ["plain"] * N_PAIRS
                     + ["rope_scaled"] * IDX_PAIRS)
            w_tail = jnp.zeros((d, LANES), w_in.dtype).at[:, :w_in.shape[1] - n_main].set(w_in[:, n_main:])
            gk = jnp.ones((1, LANES), F32).at[0, :HEAD_DIM].set(dsa_idx_k_norm[j].astype(F32))
            qkvi, tail = _project(h, dsa_norm[j], w_in[:, :n_main].astype(BF16), cos128, sin128, kinds, t_pad,
                                  w_tail=w_tail.astype(BF16), gk=gk)
            q_t = jnp.swapaxes(qkvi[:N_PAIRS], 1, 2)
            v_t = _key_tiled(_with_ones_rows(jnp.swapaxes(qkvi[2 * N_PAIRS:3 * N_PAIRS], 1, 2)))
            qi_t = jnp.swapaxes(qkvi[3 * N_PAIRS:], 1, 2)
            wi_t = tail[:, HEAD_DIM:HEAD_DIM + IDX_HEADS].T
            ki = tail[:, :HEAD_DIM].astype(BF16)
            kidx = jnp.concatenate([ki, ki], axis=1)
            topk = min(TOPK_MAX, seq // 4)
            o = _dsa_attention(q_t, qkvi, v_t, qi_t, wi_t, kidx, steps, topk, t_pad)
            h = _out_project(h, o, dsa_w_o[j].astype(BF16))
        h = _ffn(h, ffn_norm[i], _interleave_gate_val(ffn_w_up[i]).astype(BF16),
                 _interleave_gate_val(ffn_conv_w[i]).astype(F32),
                 _interleave_gate_val(ffn_conv_b[i]).reshape(1, -1).astype(F32),
                 ffn_w_down[i].astype(BF16), final_norm, t_pad, final=(i == depth - 1))
    return h.reshape(batch, t_pad, d)[:, N_META:N_META + seq]
```

```python
import functools
import math

import numpy as np
import jax
import jax.numpy as jnp
from jax import lax
from jax.experimental import pallas as pl
from jax.experimental.pallas import tpu as pltpu

D_MODEL = 1024
N_META = 16
ROPE_THETA = 10000.0
RMS_EPS = 1e-6
HEAD_DIM = 64
N_PAIRS = 8
N_HEADS = 2 * N_PAIRS
IDX_HEADS = 8
IDX_PAIRS = IDX_HEADS // 2
TOPK_MAX = 256
BIG_SCORE = 1e30
LOG2_E = 1.4426950408889634
FFN_HIDDEN = 2816
CONV_WIDTH = 3

LANES = 128
BF16_ROWS = 16
ROW_TILE = 512
Q_TILE = 256
K_TILE = 512
K_SUBS = 3
FFN_TILE = 256
HALO = BF16_ROWS
VMEM_LIMIT = 56 * 1024 * 1024

F32 = jnp.float32
BF16 = jnp.bfloat16
INT_MIN = -2147483648
KEY_NEG_INF = INT_MIN + 0x7FFFFF


def _padded_len(t):
    unit = math.lcm(ROW_TILE, K_SUBS * K_TILE, Q_TILE)
    return ((t + unit - 1) // unit) * unit


def _rms(x, gain):
    ms = jnp.mean(x * x, axis=-1, keepdims=True)
    return x * lax.rsqrt(ms + RMS_EPS) * gain


def _rope_lanes(y, cos, sin, lo):
    sw = jnp.where(lo, pltpu.roll(y, LANES - HEAD_DIM // 2, 1), pltpu.roll(y, HEAD_DIM // 2, 1))
    return y * cos + sw * sin


def _proj_kernel(x_ref, g_ref, w_ref, cos_ref, sin_ref, *rest, kinds, has_tail):
    if has_tail:
        wt_ref, gk_ref, o_ref, t_ref = rest
    else:
        (o_ref,) = rest
    xh = _rms(x_ref[...], g_ref[...]).astype(BF16)
    cos = cos_ref[...]
    sin = sin_ref[...]
    lane = lax.broadcasted_iota(jnp.int32, cos.shape, 1)
    lo = (lane & (HEAD_DIM // 2)) == 0
    n_groups = len(kinds)
    cw = 4 * LANES
    for c in range(n_groups // 4):
        y = jnp.dot(xh, w_ref[:, c * cw:(c + 1) * cw], preferred_element_type=F32)
        for s in range(4):
            g = c * 4 + s
            yg = y[:, s * LANES:(s + 1) * LANES]
            if kinds[g] != "plain":
                yg = _rope_lanes(yg, cos, sin, lo)
                if kinds[g] == "rope_scaled":
                    yg = yg * (HEAD_DIM ** -0.5)
                elif kinds[g] == "rope_softmax_q":
                    yg = yg * (HEAD_DIM ** -0.5 * LOG2_E)
            o_ref[g] = yg.astype(BF16)
    if has_tail:
        t = jnp.dot(xh, wt_ref[...], preferred_element_type=F32)
        is_k = lane < HEAD_DIM
        ms = jnp.sum(jnp.where(is_k, t * t, 0.0), axis=-1, keepdims=True) * (1.0 / HEAD_DIM)
        kn = t * lax.rsqrt(ms + RMS_EPS) * gk_ref[...]
        kn = _rope_lanes(kn, cos, sin, lo)
        t_ref[...] = jnp.where(is_k, kn, t * (IDX_HEADS ** -0.5))


def _project(h, gain, w, cos, sin, kinds, t_pad, w_tail=None, gk=None):
    rows = h.shape[0]
    n_groups = len(kinds)
    blocks_per_batch = t_pad // ROW_TILE
    has_tail = w_tail is not None
    in_specs = [
        pl.BlockSpec((ROW_TILE, D_MODEL), lambda i: (i, 0)),
        pl.BlockSpec((1, D_MODEL), lambda i: (0, 0)),
        pl.BlockSpec(w.shape, lambda i: (0, 0)),
        pl.BlockSpec((ROW_TILE, LANES), lambda i: (i % blocks_per_batch, 0)),
        pl.BlockSpec((ROW_TILE, LANES), lambda i: (i % blocks_per_batch, 0)),
    ]
    out_shape = [jax.ShapeDtypeStruct((n_groups, rows, LANES), BF16)]
    out_specs = [pl.BlockSpec((n_groups, ROW_TILE, LANES), lambda i: (0, i, 0))]
    args = [h, gain.reshape(1, D_MODEL), w, cos, sin]
    if has_tail:
        in_specs += [pl.BlockSpec(w_tail.shape, lambda i: (0, 0)),
                     pl.BlockSpec((1, LANES), lambda i: (0, 0))]
        out_shape.append(jax.ShapeDtypeStruct((rows, LANES), F32))
        out_specs.append(pl.BlockSpec((ROW_TILE, LANES), lambda i: (i, 0)))
        args += [w_tail, gk]
    return pl.pallas_call(
        functools.partial(_proj_kernel, kinds=tuple(kinds), has_tail=has_tail),
        grid=(rows // ROW_TILE,),
        in_specs=in_specs,
        out_specs=out_specs if has_tail else out_specs[0],
        out_shape=out_shape if has_tail else out_shape[0],
        compiler_params=pltpu.CompilerParams(dimension_semantics=("arbitrary",),
                                             vmem_limit_bytes=VMEM_LIMIT),
        name="proj_tail" if has_tail else "proj",
    )(*args)


def _oproj_kernel(h_ref, o_ref, w_ref, out_ref):
    o = jnp.concatenate([o_ref[j] for j in range(N_PAIRS)], axis=1)
    out_ref[...] = h_ref[...] + jnp.dot(o, w_ref[...], preferred_element_type=F32)


def _out_project(h, o, w):
    rows = h.shape[0]
    return pl.pallas_call(
        _oproj_kernel,
        grid=(rows // ROW_TILE,),
        in_specs=[pl.BlockSpec((ROW_TILE, D_MODEL), lambda i: (i, 0)),
                  pl.BlockSpec((N_PAIRS, ROW_TILE, LANES), lambda i: (0, i, 0)),
                  pl.BlockSpec(w.shape, lambda i: (0, 0))],
        out_specs=pl.BlockSpec((ROW_TILE, D_MODEL), lambda i: (i, 0)),
        out_shape=jax.ShapeDtypeStruct(h.shape, F32),
        compiler_params=pltpu.CompilerParams(dimension_semantics=("arbitrary",),
                                             vmem_limit_bytes=VMEM_LIMIT),
        name="oproj",
    )(h, o, w)


def _ffn_kernel(xp_ref, x_ref, g_ref, wup_ref, cw_ref, cb_ref, wdn_ref, fg_ref, o_ref, u_sc,
                *, blocks_per_batch, final):
    i = pl.program_id(0)
    x = x_ref[...]
    xe = jnp.concatenate([xp_ref[...], x], axis=0)
    xh = _rms(xe, g_ref[...])
    row = lax.broadcasted_iota(jnp.int32, (HALO + ROW_TILE, 1), 0)
    keep = jnp.logical_or(row >= HALO, i % blocks_per_batch != 0)
    xh = jnp.where(keep, xh, 0.0).astype(BF16)
    acc = jnp.zeros((ROW_TILE, D_MODEL), F32)
    cw2 = 2 * FFN_TILE
    for c in range(FFN_HIDDEN // FFN_TILE):
        u_sc[...] = jnp.dot(xh, wup_ref[:, c * cw2:(c + 1) * cw2], preferred_element_type=F32)
        cwc = cw_ref[:, c * cw2:(c + 1) * cw2]
        conv = cb_ref[:, c * cw2:(c + 1) * cw2]
        for j in range(CONV_WIDTH):
            conv = conv + cwc[j:j + 1, :] * u_sc[pl.ds(HALO - (CONV_WIDTH - 1) + j, ROW_TILE), :]
        gate = conv[:, :FFN_TILE]
        val = conv[:, FFN_TILE:]
        a = (gate * jax.nn.sigmoid(gate) * val).astype(BF16)
        acc = acc + jnp.dot(a, wdn_ref[c * FFN_TILE:(c + 1) * FFN_TILE, :], preferred_element_type=F32)
    y = x + acc
    if final:
        y = _rms(y, fg_ref[...])
    o_ref[...] = y


def _ffn(h, gain, w_up, conv_w, conv_b, w_down, final_gain, t_pad, final):
    rows = h.shape[0]
    blocks_per_batch = t_pad // ROW_TILE
    halo_blocks = ROW_TILE // HALO
    return pl.pallas_call(
        functools.partial(_ffn_kernel, blocks_per_batch=blocks_per_batch, final=final),
        grid=(rows // ROW_TILE,),
        in_specs=[pl.BlockSpec((HALO, D_MODEL), lambda i: (jnp.maximum(i * halo_blocks - 1, 0), 0)),
                  pl.BlockSpec((ROW_TILE, D_MODEL), lambda i: (i, 0)),
                  pl.BlockSpec((1, D_MODEL), lambda i: (0, 0)),
                  pl.BlockSpec(w_up.shape, lambda i: (0, 0)),
                  pl.BlockSpec(conv_w.shape, lambda i: (0, 0)),
                  pl.BlockSpec(conv_b.shape, lambda i: (0, 0)),
                  pl.BlockSpec(w_down.shape, lambda i: (0, 0)),
                  pl.BlockSpec((1, D_MODEL), lambda i: (0, 0))],
        out_specs=pl.BlockSpec((ROW_TILE, D_MODEL), lambda i: (i, 0)),
        out_shape=jax.ShapeDtypeStruct(h.shape, F32),
        scratch_shapes=[pltpu.VMEM((HALO + ROW_TILE, 2 * FFN_TILE), F32)],
        compiler_params=pltpu.CompilerParams(dimension_semantics=("arbitrary",),
                                             vmem_limit_bytes=VMEM_LIMIT),
        name="ffn_final" if final else "ffn",
    )(h, h, gain.reshape(1, D_MODEL), w_up, conv_w, conv_b, w_down, final_gain.reshape(1, D_MODEL))


def _attention_steps(batch, t_pad):
    nq = t_pad // Q_TILE
    nk = t_pad // (K_SUBS * K_TILE)
    cols = {k: [] for k in ("qrow", "krow", "qloc", "kloc", "nkb", "nsub", "full", "first", "last", "b")}
    for b in range(batch):
        for qi in range(nq):
            n_tiles = ((qi + 1) * Q_TILE + K_TILE - 1) // K_TILE
            n = (n_tiles + K_SUBS - 1) // K_SUBS
            for ki in range(n):
                cols["qrow"].append(b * nq + qi)
                cols["krow"].append(b * nk + ki)
                cols["qloc"].append(qi)
                cols["kloc"].append(ki)
                cols["nkb"].append(n_tiles)
                cols["nsub"].append(min(K_SUBS, n_tiles - ki * K_SUBS))
                cols["full"].append(int((ki + 1) * K_SUBS < n_tiles))
                cols["first"].append(int(ki == 0))
                cols["last"].append(int(ki == n - 1))
                cols["b"].append(b)
    return {k: jnp.asarray(np.asarray(v, np.int32)) for k, v in cols.items()}


def _half_select(x_t, first_half):
    row = lax.broadcasted_iota(jnp.int32, x_t.shape, 0)
    keep = (row < HEAD_DIM) if first_half else (row >= HEAD_DIM)
    return jnp.where(keep, x_t, jnp.zeros_like(x_t))


def _zero_after(x):
    w = lax.bitcast_convert_type(x, jnp.uint32)
    w = lax.shift_right_logical(lax.shift_right_logical(w, jnp.uint32(16)), jnp.uint32(16))
    return lax.bitcast_convert_type(w, F32)[0:1, :]


def _flash_update(s_t, v_t, h, m_sc, acc_sc, after=None):
    m_prev = m_sc[h]
    m_new = jnp.maximum(m_prev, jnp.max(s_t, axis=0, keepdims=True))
    alpha = jnp.exp2(m_prev - m_new)
    if after is None:
        p_t = jnp.exp2(s_t - m_new).astype(BF16)
    else:
        cut = (3 * s_t.shape[0]) // 4
        p_t = jnp.concatenate([jnp.exp2(s_t[:cut] - m_new),
                               jnp.exp2(s_t[cut:] - (m_new + after))], axis=0).astype(BF16)
    acc_sc[h] = alpha * acc_sc[h] + jnp.dot(v_t, p_t, preferred_element_type=F32)
    m_sc[h] = m_new


def _both_halves(x_t):
    return jnp.concatenate([_half_select(x_t, True), _half_select(x_t, False)], axis=1)


def _attention_sweep(qt_ref, k_ref, vt_ref, subs, bias_of, m_sc, acc_sc):
    def scores(sub, j):
        start = sub * K_TILE
        if not isinstance(sub, int):
            start = pl.multiple_of(start, K_TILE)
        return jnp.dot(k_ref[j, pl.ds(start, K_TILE), :], _both_halves(qt_ref[j]), preferred_element_type=F32)

    items = [(sub, j) for sub in subs for j in range(N_PAIRS)]
    biases = {}
    ahead = 2
    pending = [scores(*item) for item in items[:ahead]]
    for i, (sub, j) in enumerate(items):
        s_t = pending.pop(0)
        tile_key = sub if isinstance(sub, int) else "traced"
        if tile_key not in biases:
            biases[tile_key] = bias_of(sub)
        if biases[tile_key] is not None:
            s_t = s_t + biases[tile_key]
        after = None
        if i + ahead < len(items):
            pending.append(scores(*items[i + ahead]))
            after = _zero_after(pending[-1][0:8, :])
        _flash_update(s_t, vt_ref[j, sub], j, m_sc, acc_sc, after=after)


def _diff_attn_kernel(qrow, krow, qloc, kloc, nsub, full, first, last,
                      qt_ref, k_ref, vt_ref, lam_ref, subln_ref, o_ref, m_sc, acc_sc,
                      *, lambda_init):
    p = pl.program_id(0)

    @pl.when(first[p] == 1)
    def _():
        m_sc[...] = jnp.full(m_sc.shape, -jnp.inf, F32)
        acc_sc[...] = jnp.zeros(acc_sc.shape, F32)

    def key_tile(sub, carry):
        tile = kloc[p] * K_SUBS + sub
        on_diagonal = tile * K_TILE + (K_TILE - 1) > qloc[p] * Q_TILE

        @pl.when(on_diagonal)
        def _():
            kpos = tile * K_TILE + lax.broadcasted_iota(jnp.int32, (K_TILE, 2 * Q_TILE), 0)
            qcol = lax.broadcasted_iota(jnp.int32, (K_TILE, 2 * Q_TILE), 1)
            qpos = qloc[p] * Q_TILE + jnp.where(qcol >= Q_TILE, qcol - Q_TILE, qcol)
            bias = jnp.where(kpos <= qpos, 0.0, -jnp.inf).astype(F32)
            _attention_sweep(qt_ref, k_ref, vt_ref, (sub,), lambda _: bias, m_sc, acc_sc)

        @pl.when(jnp.logical_not(on_diagonal))
        def _():
            _attention_sweep(qt_ref, k_ref, vt_ref, (sub,), lambda _: None, m_sc, acc_sc)

        return carry

    @pl.when(full[p] == 1)
    def _():
        _attention_sweep(qt_ref, k_ref, vt_ref, tuple(range(K_SUBS)), lambda _: None, m_sc, acc_sc)

    @pl.when(full[p] == 0)
    def _():
        lax.fori_loop(0, nsub[p], key_tile, 0)

    @pl.when(last[p] == 1)
    def _():
        lam_rows = lam_ref[...]
        lam = (jnp.exp(jnp.sum(lam_rows[0:1] * lam_rows[1:2], axis=-1, keepdims=True))
               - jnp.exp(jnp.sum(lam_rows[2:3] * lam_rows[3:4], axis=-1, keepdims=True))
               + lambda_init)
        vdim = 2 * HEAD_DIM
        for j in range(N_PAIRS):
            a = acc_sc[j]
            a1 = a[:, :Q_TILE]
            a2 = a[:, Q_TILE:]
            o = a1[:vdim] / a1[vdim:vdim + 1] - lam * (a2[:vdim] / a2[vdim:vdim + 1])
            ms = jnp.mean(o * o, axis=0, keepdims=True)
            o = o * lax.rsqrt(ms + RMS_EPS) * subln_ref[...] * (1.0 - lambda_init)
            o_ref[j] = o.T.astype(BF16)


def _diff_attention(q_t, qkv, v_t, lam_rows, subln_col, steps, lambda_init):
    rows = qkv.shape[1]
    n_steps = steps["qrow"].shape[0]
    vrows = v_t.shape[2]

    def im(fn):
        return lambda p, qr, kr, ql, kl, ns, fu, f, l: fn(p, qr, kr)

    grid_spec = pltpu.PrefetchScalarGridSpec(
        num_scalar_prefetch=8,
        grid=(n_steps,),
        in_specs=[
            pl.BlockSpec((N_PAIRS, LANES, Q_TILE), im(lambda p, qr, kr: (0, 0, qr[p]))),
            pl.BlockSpec((N_PAIRS, K_SUBS * K_TILE, LANES), im(lambda p, qr, kr: (1, kr[p], 0))),
            pl.BlockSpec((N_PAIRS, K_SUBS, vrows, K_TILE), im(lambda p, qr, kr: (0, kr[p], 0, 0))),
            pl.BlockSpec((8, LANES), im(lambda p, qr, kr: (0, 0))),
            pl.BlockSpec((LANES, 1), im(lambda p, qr, kr: (0, 0))),
        ],
        out_specs=pl.BlockSpec((N_PAIRS, Q_TILE, LANES), im(lambda p, qr, kr: (0, qr[p], 0))),
        scratch_shapes=[pltpu.VMEM((N_PAIRS, 1, 2 * Q_TILE), F32),
                        pltpu.VMEM((N_PAIRS, vrows, 2 * Q_TILE), F32)],
    )
    return pl.pallas_call(
        functools.partial(_diff_attn_kernel, lambda_init=lambda_init),
        grid_spec=grid_spec,
        out_shape=jax.ShapeDtypeStruct((N_PAIRS, rows, LANES), BF16),
        compiler_params=pltpu.CompilerParams(dimension_semantics=("arbitrary",),
                                             vmem_limit_bytes=VMEM_LIMIT),
        name="diff_attn",
    )(steps["qrow"], steps["krow"], steps["qloc"], steps["kloc"], steps["nsub"], steps["full"],
      steps["first"], steps["last"], q_t, qkv, v_t, lam_rows, subln_col)


def _dsa_kernel(qrow, krow, qloc, kloc, nkb, nsub, full, first, last, bidx,
                qt_ref, k_ref, vt_ref, qit_ref, wit_ref, kidx_ref, o_ref,
                key_sc, thr_sc, m_sc, acc_sc, *, topk):
    p = pl.program_id(0)

    @pl.when(first[p] == 1)
    def _():
        m_sc[...] = jnp.full(m_sc.shape, -jnp.inf, F32)
        acc_sc[...] = jnp.zeros(acc_sc.shape, F32)
        n = nkb[p]
        wi = wit_ref[...]
        qpos = qloc[p] * Q_TILE + lax.broadcasted_iota(jnp.int32, (K_TILE, Q_TILE), 1)
        krow_iota = lax.broadcasted_iota(jnp.int32, (K_TILE, Q_TILE), 0)

        def score_chunk(c, carry):
            kt = kidx_ref[pl.ds(pl.multiple_of(c * K_TILE, K_TILE), K_TILE), :]
            sc = jnp.zeros((K_TILE, Q_TILE), F32)
            for jp in range(IDX_PAIRS):
                qi_t = qit_ref[jp]
                for cc in range(2):
                    raw = jnp.dot(kt, _half_select(qi_t, cc == 0), preferred_element_type=F32)
                    hh = 2 * jp + cc
                    sc = sc + jnp.maximum(raw, 0.0) * wi[hh:hh + 1, :]
            kpos = c * K_TILE + krow_iota
            sc = jnp.where(kpos < N_META, BIG_SCORE, sc)
            sc = jnp.where(kpos <= qpos, sc, -jnp.inf)
            bits = lax.bitcast_convert_type(sc, jnp.int32)
            key_sc[c] = bits ^ ((bits >> 31) & 0x7FFFFFFF)
            return carry

        lax.fori_loop(0, n, score_chunk, 0)

        kf = float(topk)

        def count_ge(cand):
            def count_chunk(c, cnt8):
                hit = jnp.where(key_sc[c] >= cand, 1.0, 0.0).reshape(8, K_TILE // 64, 8, Q_TILE)
                return cnt8 + jnp.sum(jnp.sum(hit, axis=1), axis=0)

            cnt8 = lax.fori_loop(0, n, count_chunk, jnp.zeros((8, Q_TILE), F32))
            return jnp.sum(cnt8, axis=0, keepdims=True)

        mantissa_bits = 23

        def bit_body(b, state):
            lo, c_lo, c_hi = state
            cand = lo + jnp.left_shift(jnp.int32(1), 31 - b)
            cnt = count_ge(cand)
            ok = cnt >= kf
            return jnp.where(ok, cand, lo), jnp.where(ok, cnt, c_lo), jnp.where(ok, c_hi, cnt)

        lo, c_lo, c_hi = lax.fori_loop(
            0, 32 - mantissa_bits, bit_body,
            (jnp.full((1, Q_TILE), INT_MIN, jnp.int32),
             jnp.full((1, Q_TILE), 1.0, F32) * (n * K_TILE).astype(F32),
             jnp.zeros((1, Q_TILE), F32)))

        q1 = qloc[p] * Q_TILE + lax.broadcasted_iota(jnp.int32, (1, Q_TILE), 1)
        short = q1 < topk

        def active(width, c_lo):
            return jnp.logical_and(jnp.logical_and(c_lo != kf, width > 1), jnp.logical_not(short))

        def unsettled(state):
            _, _, width, c_lo, _ = state
            return jnp.max(active(width, c_lo).astype(jnp.int32)) > 0

        def probe(state):
            it, lo, width, c_lo, c_hi = state
            act = active(width, c_lo)
            frac = (c_lo - kf) / (c_lo - c_hi)
            d_lin = (width.astype(F32) * frac).astype(jnp.int32)
            d = jnp.where(it % 2 == 0, d_lin, lax.shift_right_logical(width, 1))
            d = jnp.maximum(jnp.minimum(d, width - 1), 1)
            cand = lo + d
            cnt = count_ge(cand)
            ok = cnt >= kf
            return (it + 1,
                    jnp.where(jnp.logical_and(act, ok), cand, lo),
                    jnp.where(act, jnp.where(ok, width - d, d), width),
                    jnp.where(jnp.logical_and(act, ok), cnt, c_lo),
                    jnp.where(jnp.logical_and(act, jnp.logical_not(ok)), cnt, c_hi))

        _, thr, _, _, _ = lax.while_loop(
            unsettled, probe,
            (jnp.int32(0), lo, jnp.full((1, Q_TILE), 1 << mantissa_bits, jnp.int32), c_lo, c_hi))
        thr_sc[...] = jnp.maximum(thr, KEY_NEG_INF + 1)

    def mask_bias(sub):
        sel = key_sc[kloc[p] * K_SUBS + sub] >= thr_sc[...]
        bias1 = jnp.where(sel, 0.0, -jnp.inf).astype(F32)
        return jnp.concatenate([bias1, bias1], axis=1)

    def key_tile(sub, carry):
        _attention_sweep(qt_ref, k_ref, vt_ref, (sub,), mask_bias, m_sc, acc_sc)
        return carry

    @pl.when(full[p] == 1)
    def _():
        _attention_sweep(qt_ref, k_ref, vt_ref, tuple(range(K_SUBS)), mask_bias, m_sc, acc_sc)

    @pl.when(full[p] == 0)
    def _():
        lax.fori_loop(0, nsub[p], key_tile, 0)

    @pl.when(last[p] == 1)
    def _():
        ones_row = 2 * HEAD_DIM
        for j in range(N_PAIRS):
            a = acc_sc[j]
            o = jnp.concatenate([a[:HEAD_DIM, :Q_TILE] / a[ones_row:ones_row + 1, :Q_TILE],
                                 a[HEAD_DIM:ones_row, Q_TILE:] / a[ones_row:ones_row + 1, Q_TILE:]],
                                axis=0)
            o_ref[j] = o.T.astype(BF16)


def _dsa_attention(q_t, qkvi, v_t, qi_t, wi_t, kidx, steps, topk, t_pad):
    rows = qkvi.shape[1]
    n_steps = steps["qrow"].shape[0]
    nkb_total = t_pad // K_TILE
    vrows = v_t.shape[2]

    def im(fn):
        return lambda p, qr, kr, ql, kl, nk, ns, fu, f, l, b: fn(p, qr, kr, b)

    grid_spec = pltpu.PrefetchScalarGridSpec(
        num_scalar_prefetch=10,
        grid=(n_steps,),
        in_specs=[
            pl.BlockSpec((N_PAIRS, LANES, Q_TILE), im(lambda p, qr, kr, b: (0, 0, qr[p]))),
            pl.BlockSpec((N_PAIRS, K_SUBS * K_TILE, LANES), im(lambda p, qr, kr, b: (1, kr[p], 0))),
            pl.BlockSpec((N_PAIRS, K_SUBS, vrows, K_TILE), im(lambda p, qr, kr, b: (0, kr[p], 0, 0))),
            pl.BlockSpec((IDX_PAIRS, LANES, Q_TILE), im(lambda p, qr, kr, b: (0, 0, qr[p]))),
            pl.BlockSpec((IDX_HEADS, Q_TILE), im(lambda p, qr, kr, b: (0, qr[p]))),
            pl.BlockSpec((t_pad, LANES), im(lambda p, qr, kr, b: (b[p], 0))),
        ],
        out_specs=pl.BlockSpec((N_PAIRS, Q_TILE, LANES), im(lambda p, qr, kr, b: (0, qr[p], 0))),
        scratch_shapes=[pltpu.VMEM((nkb_total, K_TILE, Q_TILE), jnp.int32),
                        pltpu.VMEM((1, Q_TILE), jnp.int32),
                        pltpu.VMEM((N_PAIRS, 1, 2 * Q_TILE), F32),
                        pltpu.VMEM((N_PAIRS, vrows, 2 * Q_TILE), F32)],
    )
    return pl.pallas_call(
        functools.partial(_dsa_kernel, topk=topk),
        grid_spec=grid_spec,
        out_shape=jax.ShapeDtypeStruct((N_PAIRS, rows, LANES), BF16),
        compiler_params=pltpu.CompilerParams(dimension_semantics=("arbitrary",),
                                             vmem_limit_bytes=VMEM_LIMIT),
        name="dsa_attn",
    )(steps["qrow"], steps["krow"], steps["qloc"], steps["kloc"], steps["nkb"], steps["nsub"],
      steps["full"], steps["first"], steps["last"], steps["b"], q_t, qkvi, v_t, qi_t, wi_t, kidx)


def _rope_tables(t_pad):
    inv = ROPE_THETA ** (-jnp.arange(0, HEAD_DIM, 2, dtype=F32) / HEAD_DIM)
    ang = jnp.arange(t_pad, dtype=F32)[:, None] * inv[None, :]
    cos, sin = jnp.cos(ang), jnp.sin(ang)
    cos128 = jnp.tile(cos, (1, LANES // (HEAD_DIM // 2)))
    sin128 = jnp.tile(jnp.concatenate([-sin, sin], axis=1), (1, LANES // HEAD_DIM))
    return cos128, sin128


def _interleave_gate_val(a):
    lead = a.shape[:-1]
    a = a.reshape(lead + (2, FFN_HIDDEN // FFN_TILE, FFN_TILE))
    a = jnp.swapaxes(a, -3, -2)
    return a.reshape(lead + (2 * FFN_HIDDEN,))


def _with_ones_rows(v_t):
    ones = jnp.ones((v_t.shape[0], BF16_ROWS, v_t.shape[2]), v_t.dtype)
    return jnp.concatenate([v_t, ones], axis=1)


def _key_tiled(v_t):
    pairs, d, rows = v_t.shape
    return jnp.swapaxes(v_t.reshape(pairs, d, rows // K_TILE, K_TILE), 1, 2)


def kernel(x, meta_tokens, da_norm, da_w_qkv, da_lambda_q1, da_lambda_k1, da_lambda_q2, da_lambda_k2, da_subln, da_w_o, dsa_norm, dsa_w_in, dsa_idx_k_norm, dsa_w_o, ffn_norm, ffn_w_up, ffn_conv_w, ffn_conv_b, ffn_w_down, final_norm):
    batch, seq, d = x.shape
    assert d == D_MODEL
    depth = ffn_norm.shape[0]
    t_real = seq + N_META
    t_pad = _padded_len(t_real)
    assert t_pad % (K_SUBS * K_TILE) == 0 and t_pad % Q_TILE == 0
    rows = batch * t_pad

    meta = jnp.broadcast_to(meta_tokens[None].astype(x.dtype), (batch, N_META, d))
    h = jnp.concatenate([meta, x, jnp.zeros((batch, t_pad - t_real, d), x.dtype)], axis=1)
    h = h.reshape(rows, d)
    cos128, sin128 = _rope_tables(t_pad)
    steps = _attention_steps(batch, t_pad)

    dsa_qkv = N_HEADS * HEAD_DIM
    idx_cols = IDX_HEADS * HEAD_DIM
    for i in range(depth):
        j = i // 2
        if i % 2 == 0:
            lambda_init = 0.8 - 0.6 * math.exp(-0.3 * i)
            kinds = ["rope_softmax_q"] * N_PAIRS + ["rope"] * N_PAIRS + ["plain"] * N_PAIRS
            qkv = _project(h, da_norm[j], da_w_qkv[j].astype(BF16), cos128, sin128, kinds, t_pad)
            q_t = jnp.swapaxes(qkv[:N_PAIRS], 1, 2)
            v_t = _key_tiled(_with_ones_rows(jnp.swapaxes(qkv[2 * N_PAIRS:], 1, 2)))
            lam_rows = jnp.zeros((8, LANES), F32).at[0:4, 0:HEAD_DIM].set(
                jnp.stack([da_lambda_q1[j], da_lambda_k1[j], da_lambda_q2[j], da_lambda_k2[j]]).astype(F32))
            o = _diff_attention(q_t, qkv, v_t, lam_rows, da_subln[j].reshape(LANES, 1).astype(F32),
                                steps, lambda_init)
            h = _out_project(h, o, da_w_o[j].astype(BF16))
        else:
            w_in = dsa_w_in[j]
            n_main = 3 * dsa_qkv + idx_cols
            kinds = (["rope_softmax_q"] * N_PAIRS + ["rope"] * N_PAIRS + ["plain"] * N_PAIRS
                     + ["rope_scaled"] * IDX_PAIRS)
            w_tail = jnp.zeros((d, LANES), w_in.dtype).at[:, :w_in.shape[1] - n_main].set(w_in[:, n_main:])
            gk = jnp.ones((1, LANES), F32).at[0, :HEAD_DIM].set(dsa_idx_k_norm[j].astype(F32))
            qkvi, tail = _project(h, dsa_norm[j], w_in[:, :n_main].astype(BF16), cos128, sin128, kinds, t_pad,
                                  w_tail=w_tail.astype(BF16), gk=gk)
            q_t = jnp.swapaxes(qkvi[:N_PAIRS], 1, 2)
            v_t = _key_tiled(_with_ones_rows(jnp.swapaxes(qkvi[2 * N_PAIRS:3 * N_PAIRS], 1, 2)))
            qi_t = jnp.swapaxes(qkvi[3 * N_PAIRS:], 1, 2)
            wi_t = tail[:, HEAD_DIM:HEAD_DIM + IDX_HEADS].T
            ki = tail[:, :HEAD_DIM].astype(BF16)
            kidx = jnp.concatenate([ki, ki], axis=1)
            topk = min(TOPK_MAX, seq // 4)
            o = _dsa_attention(q_t, qkvi, v_t, qi_t, wi_t, kidx, steps, topk, t_pad)
            h = _out_project(h, o, dsa_w_o[j].astype(BF16))
        h = _ffn(h, ffn_norm[i], _interleave_gate_val(ffn_w_up[i]).astype(BF16),
                 _interleave_gate_val(ffn_conv_w[i]).astype(F32),
                 _interleave_gate_val(ffn_conv_b[i]).reshape(1, -1).astype(F32),
                 ffn_w_down[i].astype(BF16), final_norm, t_pad, final=(i == depth - 1))
    return h.reshape(batch, t_pad, d)[:, N_META:N_META + seq]
```

```python
import functools
import math

import numpy as np
import jax
import jax.numpy as jnp
from jax import lax
from jax.experimental import pallas as pl
from jax.experimental.pallas import tpu as pltpu

D_MODEL = 1024
N_META = 16
ROPE_THETA = 10000.0
RMS_EPS = 1e-6
HEAD_DIM = 64
N_PAIRS = 8
N_HEADS = 2 * N_PAIRS
IDX_HEADS = 8
IDX_PAIRS = IDX_HEADS // 2
TOPK_MAX = 256
BIG_SCORE = 1e30
LOG2_E = 1.4426950408889634
FFN_HIDDEN = 2816
CONV_WIDTH = 3

LANES = 128
BF16_ROWS = 16
ROW_TILE = 512
Q_TILE = 256
K_TILE = 512
K_SUBS = 3
FFN_TILE = 256
HALO = BF16_ROWS
VMEM_LIMIT = 56 * 1024 * 1024

F32 = jnp.float32
BF16 = jnp.bfloat16
INT_MIN = -2147483648
KEY_NEG_INF = INT_MIN + 0x7FFFFF


def _padded_len(t):
    unit = math.lcm(ROW_TILE, K_SUBS * K_TILE, Q_TILE)
    return ((t + unit - 1) // unit) * unit


def _rms(x, gain):
    ms = jnp.mean(x * x, axis=-1, keepdims=True)
    return x * lax.rsqrt(ms + RMS_EPS) * gain


def _rope_lanes(y, cos, sin, lo):
    sw = jnp.where(lo, pltpu.roll(y, LANES - HEAD_DIM // 2, 1), pltpu.roll(y, HEAD_DIM // 2, 1))
    return y * cos + sw * sin


def _proj_kernel(x_ref, g_ref, w_ref, cos_ref, sin_ref, *rest, has_idx):
    if has_idx:
        wt_ref, gk_ref, qt_ref, k_ref, vt_ref, qit_ref, kidx_ref, wit_ref = rest
    else:
        qt_ref, k_ref, vt_ref = rest
    xh = _rms(x_ref[...], g_ref[...]).astype(BF16)
    cos = cos_ref[...]
    sin = sin_ref[...]
    lane = lax.broadcasted_iota(jnp.int32, cos.shape, 1)
    lo = (lane & (HEAD_DIM // 2)) == 0
    n_groups = (3 * N_PAIRS + IDX_PAIRS) if has_idx else 3 * N_PAIRS
    cw = 4 * LANES
    ones = jnp.ones((BF16_ROWS, ROW_TILE), BF16)
    for c in range(n_groups // 4):
        y = jnp.dot(xh, w_ref[:, c * cw:(c + 1) * cw], preferred_element_type=F32)
        for s in range(4):
            g = c * 4 + s
            yg = y[:, s * LANES:(s + 1) * LANES]
            if g < N_PAIRS:
                yg = _rope_lanes(yg, cos, sin, lo) * (HEAD_DIM ** -0.5 * LOG2_E)
                qt_ref[g] = yg.T.astype(BF16)
            elif g < 2 * N_PAIRS:
                k_ref[g - N_PAIRS] = _rope_lanes(yg, cos, sin, lo).astype(BF16)
            elif g < 3 * N_PAIRS:
                j = g - 2 * N_PAIRS
                vt_ref[j, 0, 0:2 * HEAD_DIM, :] = yg.T.astype(BF16)
                vt_ref[j, 0, 2 * HEAD_DIM:, :] = ones
            else:
                yg = _rope_lanes(yg, cos, sin, lo) * (HEAD_DIM ** -0.5)
                qit_ref[g - 3 * N_PAIRS] = yg.T.astype(BF16)
    if has_idx:
        t = jnp.dot(xh, wt_ref[...], preferred_element_type=F32)
        is_k = lane < HEAD_DIM
        ms = jnp.sum(jnp.where(is_k, t * t, 0.0), axis=-1, keepdims=True) * (1.0 / HEAD_DIM)
        kn = t * lax.rsqrt(ms + RMS_EPS) * gk_ref[...]
        kn = _rope_lanes(kn, cos, sin, lo)
        kidx_ref[...] = jnp.where(is_k, kn, pltpu.roll(kn, HEAD_DIM, 1)).astype(BF16)
        wit_ref[...] = (t * (IDX_HEADS ** -0.5)).T[HEAD_DIM:HEAD_DIM + IDX_HEADS, :]


def _project(h, gain, w, cos, sin, t_pad, w_tail=None, gk=None):
    assert ROW_TILE == K_TILE
    rows = h.shape[0]
    blocks_per_batch = t_pad // ROW_TILE
    has_idx = w_tail is not None
    vrows = 2 * HEAD_DIM + BF16_ROWS
    in_specs = [
        pl.BlockSpec((ROW_TILE, D_MODEL), lambda i: (i, 0)),
        pl.BlockSpec((1, D_MODEL), lambda i: (0, 0)),
        pl.BlockSpec(w.shape, lambda i: (0, 0)),
        pl.BlockSpec((ROW_TILE, LANES), lambda i: (i % blocks_per_batch, 0)),
        pl.BlockSpec((ROW_TILE, LANES), lambda i: (i % blocks_per_batch, 0)),
    ]
    out_shape = [jax.ShapeDtypeStruct((N_PAIRS, LANES, rows), BF16),
                 jax.ShapeDtypeStruct((N_PAIRS, rows, LANES), BF16),
                 jax.ShapeDtypeStruct((N_PAIRS, rows // K_TILE, vrows, K_TILE), BF16)]
    out_specs = [pl.BlockSpec((N_PAIRS, LANES, ROW_TILE), lambda i: (0, 0, i)),
                 pl.BlockSpec((N_PAIRS, ROW_TILE, LANES), lambda i: (0, i, 0)),
                 pl.BlockSpec((N_PAIRS, 1, vrows, K_TILE), lambda i: (0, i, 0, 0))]
    args = [h, gain.reshape(1, D_MODEL), w, cos, sin]
    if has_idx:
        in_specs += [pl.BlockSpec(w_tail.shape, lambda i: (0, 0)),
                     pl.BlockSpec((1, LANES), lambda i: (0, 0))]
        out_shape += [jax.ShapeDtypeStruct((IDX_PAIRS, LANES, rows), BF16),
                      jax.ShapeDtypeStruct((rows, LANES), BF16),
                      jax.ShapeDtypeStruct((IDX_HEADS, rows), F32)]
        out_specs += [pl.BlockSpec((IDX_PAIRS, LANES, ROW_TILE), lambda i: (0, 0, i)),
                      pl.BlockSpec((ROW_TILE, LANES), lambda i: (i, 0)),
                      pl.BlockSpec((IDX_HEADS, ROW_TILE), lambda i: (0, i))]
        args += [w_tail, gk]
    return pl.pallas_call(
        functools.partial(_proj_kernel, has_idx=has_idx),
        grid=(rows // ROW_TILE,),
        in_specs=in_specs,
        out_specs=out_specs,
        out_shape=out_shape,
        compiler_params=pltpu.CompilerParams(dimension_semantics=("arbitrary",),
                                             vmem_limit_bytes=VMEM_LIMIT),
        name="proj_idx" if has_idx else "proj",
    )(*args)


def _oproj_kernel(h_ref, o_ref, w_ref, out_ref):
    o = jnp.concatenate([o_ref[j] for j in range(N_PAIRS)], axis=1)
    out_ref[...] = h_ref[...] + jnp.dot(o, w_ref[...], preferred_element_type=F32)


def _out_project(h, o, w):
    rows = h.shape[0]
    return pl.pallas_call(
        _oproj_kernel,
        grid=(rows // ROW_TILE,),
        in_specs=[pl.BlockSpec((ROW_TILE, D_MODEL), lambda i: (i, 0)),
                  pl.BlockSpec((N_PAIRS, ROW_TILE, LANES), lambda i: (0, i, 0)),
                  pl.BlockSpec(w.shape, lambda i: (0, 0))],
        out_specs=pl.BlockSpec((ROW_TILE, D_MODEL), lambda i: (i, 0)),
        out_shape=jax.ShapeDtypeStruct(h.shape, F32),
        compiler_params=pltpu.CompilerParams(dimension_semantics=("arbitrary",),
                                             vmem_limit_bytes=VMEM_LIMIT),
        name="oproj",
    )(h, o, w)


def _ffn_kernel(xp_ref, x_ref, g_ref, wup_ref, cw_ref, cb_ref, wdn_ref, fg_ref, o_ref, u_sc,
                *, blocks_per_batch, final):
    i = pl.program_id(0)
    x = x_ref[...]
    xe = jnp.concatenate([xp_ref[...], x], axis=0)
    xh = _rms(xe, g_ref[...])
    row = lax.broadcasted_iota(jnp.int32, (HALO + ROW_TILE, 1), 0)
    keep = jnp.logical_or(row >= HALO, i % blocks_per_batch != 0)
    xh = jnp.where(keep, xh, 0.0).astype(BF16)
    acc = jnp.zeros((ROW_TILE, D_MODEL), F32)
    cw2 = 2 * FFN_TILE
    for c in range(FFN_HIDDEN // FFN_TILE):
        u_sc[...] = jnp.dot(xh, wup_ref[:, c * cw2:(c + 1) * cw2], preferred_element_type=F32)
        cwc = cw_ref[:, c * cw2:(c + 1) * cw2]
        conv = cb_ref[:, c * cw2:(c + 1) * cw2]
        for j in range(CONV_WIDTH):
            conv = conv + cwc[j:j + 1, :] * u_sc[pl.ds(HALO - (CONV_WIDTH - 1) + j, ROW_TILE), :]
        gate = conv[:, :FFN_TILE]
        val = conv[:, FFN_TILE:]
        a = (gate * jax.nn.sigmoid(gate) * val).astype(BF16)
        acc = acc + jnp.dot(a, wdn_ref[c * FFN_TILE:(c + 1) * FFN_TILE, :], preferred_element_type=F32)
    y = x + acc
    if final:
        y = _rms(y, fg_ref[...])
    o_ref[...] = y


def _ffn(h, gain, w_up, conv_w, conv_b, w_down, final_gain, t_pad, final):
    rows = h.shape[0]
    blocks_per_batch = t_pad // ROW_TILE
    halo_blocks = ROW_TILE // HALO
    return pl.pallas_call(
        functools.partial(_ffn_kernel, blocks_per_batch=blocks_per_batch, final=final),
        grid=(rows // ROW_TILE,),
        in_specs=[pl.BlockSpec((HALO, D_MODEL), lambda i: (jnp.maximum(i * halo_blocks - 1, 0), 0)),
                  pl.BlockSpec((ROW_TILE, D_MODEL), lambda i: (i, 0)),
                  pl.BlockSpec((1, D_MODEL), lambda i: (0, 0)),
                  pl.BlockSpec(w_up.shape, lambda i: (0, 0)),
                  pl.BlockSpec(conv_w.shape, lambda i: (0, 0)),
                  pl.BlockSpec(conv_b.shape, lambda i: (0, 0)),
                  pl.BlockSpec(w_down.shape, lambda i: (0, 0)),
                  pl.BlockSpec((1, D_MODEL), lambda i: (0, 0))],
        out_specs=pl.BlockSpec((ROW_TILE, D_MODEL), lambda i: (i, 0)),
        out_shape=jax.ShapeDtypeStruct(h.shape, F32),
        scratch_shapes=[pltpu.VMEM((HALO + ROW_TILE, 2 * FFN_TILE), F32)],
        compiler_params=pltpu.CompilerParams(dimension_semantics=("arbitrary",),
                                             vmem_limit_bytes=VMEM_LIMIT),
        name="ffn_final" if final else "ffn",
    )(h, h, gain.reshape(1, D_MODEL), w_up, conv_w, conv_b, w_down, final_gain.reshape(1, D_MODEL))


def _attention_steps(batch, t_pad):
    nq = t_pad // Q_TILE
    nk = t_pad // (K_SUBS * K_TILE)
    cols = {k: [] for k in ("qrow", "krow", "qloc", "kloc", "nkb", "nsub", "full", "first", "last", "b")}
    for b in range(batch):
        for qi in range(nq):
            n_tiles = ((qi + 1) * Q_TILE + K_TILE - 1) // K_TILE
            n = (n_tiles + K_SUBS - 1) // K_SUBS
            for ki in range(n):
                cols["qrow"].append(b * nq + qi)
                cols["krow"].append(b * nk + ki)
                cols["qloc"].append(qi)
                cols["kloc"].append(ki)
                cols["nkb"].append(n_tiles)
                cols["nsub"].append(min(K_SUBS, n_tiles - ki * K_SUBS))
                cols["full"].append(int((ki + 1) * K_SUBS < n_tiles))
                cols["first"].append(int(ki == 0))
                cols["last"].append(int(ki == n - 1))
                cols["b"].append(b)
    return {k: jnp.asarray(np.asarray(v, np.int32)) for k, v in cols.items()}


def _half_select(x_t, first_half):
    row = lax.broadcasted_iota(jnp.int32, x_t.shape, 0)
    keep = (row < HEAD_DIM) if first_half else (row >= HEAD_DIM)
    return jnp.where(keep, x_t, jnp.zeros_like(x_t))


def _zero_after(x):
    w = lax.bitcast_convert_type(x, jnp.uint32)
    w = lax.shift_right_logical(lax.shift_right_logical(w, jnp.uint32(16)), jnp.uint32(16))
    return lax.bitcast_convert_type(w, F32)[0:1, :]


def _flash_update(s_t, v_t, h, m_sc, acc_sc, after=None):
    m_prev = m_sc[h]
    m_new = jnp.maximum(m_prev, jnp.max(s_t, axis=0, keepdims=True))
    alpha = jnp.exp2(m_prev - m_new)
    if after is None:
        p_t = jnp.exp2(s_t - m_new).astype(BF16)
    else:
        cut = (3 * s_t.shape[0]) // 4
        p_t = jnp.concatenate([jnp.exp2(s_t[:cut] - m_new),
                               jnp.exp2(s_t[cut:] - (m_new + after))], axis=0).astype(BF16)
    acc_sc[h] = alpha * acc_sc[h] + jnp.dot(v_t, p_t, preferred_element_type=F32)
    m_sc[h] = m_new


def _both_halves(x_t):
    return jnp.concatenate([_half_select(x_t, True), _half_select(x_t, False)], axis=1)


def _attention_sweep(qt_ref, k_ref, vt_ref, subs, bias_of, m_sc, acc_sc):
    def scores(sub, j):
        start = sub * K_TILE
        if not isinstance(sub, int):
            start = pl.multiple_of(start, K_TILE)
        return jnp.dot(k_ref[j, pl.ds(start, K_TILE), :], _both_halves(qt_ref[j]), preferred_element_type=F32)

    items = [(sub, j) for sub in subs for j in range(N_PAIRS)]
    biases = {}
    ahead = 2
    pending = [scores(*item) for item in items[:ahead]]
    for i, (sub, j) in enumerate(items):
        s_t = pending.pop(0)
        tile_key = sub if isinstance(sub, int) else "traced"
        if tile_key not in biases:
            biases[tile_key] = bias_of(sub)
        if biases[tile_key] is not None:
            s_t = s_t + biases[tile_key]
        after = None
        if i + ahead < len(items):
            pending.append(scores(*items[i + ahead]))
            after = _zero_after(pending[-1][0:8, :])
        _flash_update(s_t, vt_ref[j, sub], j, m_sc, acc_sc, after=after)


def _diff_attn_kernel(qrow, krow, qloc, kloc, nsub, full, first, last,
                      qt_ref, k_ref, vt_ref, lam_ref, subln_ref, o_ref, m_sc, acc_sc,
                      *, lambda_init):
    p = pl.program_id(0)

    @pl.when(first[p] == 1)
    def _():
        m_sc[...] = jnp.full(m_sc.shape, -jnp.inf, F32)
        acc_sc[...] = jnp.zeros(acc_sc.shape, F32)

    def key_tile(sub, carry):
        tile = kloc[p] * K_SUBS + sub
        on_diagonal = tile * K_TILE + (K_TILE - 1) > qloc[p] * Q_TILE

        @pl.when(on_diagonal)
        def _():
            kpos = tile * K_TILE + lax.broadcasted_iota(jnp.int32, (K_TILE, 2 * Q_TILE), 0)
            qcol = lax.broadcasted_iota(jnp.int32, (K_TILE, 2 * Q_TILE), 1)
            qpos = qloc[p] * Q_TILE + jnp.where(qcol >= Q_TILE, qcol - Q_TILE, qcol)
            bias = jnp.where(kpos <= qpos, 0.0, -jnp.inf).astype(F32)
            _attention_sweep(qt_ref, k_ref, vt_ref, (sub,), lambda _: bias, m_sc, acc_sc)

        @pl.when(jnp.logical_not(on_diagonal))
        def _():
            _attention_sweep(qt_ref, k_ref, vt_ref, (sub,), lambda _: None, m_sc, acc_sc)

        return carry

    @pl.when(full[p] == 1)
    def _():
        _attention_sweep(qt_ref, k_ref, vt_ref, tuple(range(K_SUBS)), lambda _: None, m_sc, acc_sc)

    @pl.when(full[p] == 0)
    def _():
        lax.fori_loop(0, nsub[p], key_tile, 0)

    @pl.when(last[p] == 1)
    def _():
        lam_rows = lam_ref[...]
        lam = (jnp.exp(jnp.sum(lam_rows[0:1] * lam_rows[1:2], axis=-1, keepdims=True))
               - jnp.exp(jnp.sum(lam_rows[2:3] * lam_rows[3:4], axis=-1, keepdims=True))
               + lambda_init)
        vdim = 2 * HEAD_DIM
        for j in range(N_PAIRS):
            a = acc_sc[j]
            a1 = a[:, :Q_TILE]
            a2 = a[:, Q_TILE:]
            o = a1[:vdim] / a1[vdim:vdim + 1] - lam * (a2[:vdim] / a2[vdim:vdim + 1])
            ms = jnp.mean(o * o, axis=0, keepdims=True)
            o = o * lax.rsqrt(ms + RMS_EPS) * subln_ref[...] * (1.0 - lambda_init)
            o_ref[j] = o.T.astype(BF16)


def _diff_attention(q_t, qkv, v_t, lam_rows, subln_col, steps, lambda_init):
    rows = qkv.shape[1]
    n_steps = steps["qrow"].shape[0]
    vrows = v_t.shape[2]

    def im(fn):
        return lambda p, qr, kr, ql, kl, ns, fu, f, l: fn(p, qr, kr)

    grid_spec = pltpu.PrefetchScalarGridSpec(
        num_scalar_prefetch=8,
        grid=(n_steps,),
        in_specs=[
            pl.BlockSpec((N_PAIRS, LANES, Q_TILE), im(lambda p, qr, kr: (0, 0, qr[p]))),
            pl.BlockSpec((N_PAIRS, K_SUBS * K_TILE, LANES), im(lambda p, qr, kr: (0, kr[p], 0))),
            pl.BlockSpec((N_PAIRS, K_SUBS, vrows, K_TILE), im(lambda p, qr, kr: (0, kr[p], 0, 0))),
            pl.BlockSpec((8, LANES), im(lambda p, qr, kr: (0, 0))),
            pl.BlockSpec((LANES, 1), im(lambda p, qr, kr: (0, 0))),
        ],
        out_specs=pl.BlockSpec((N_PAIRS, Q_TILE, LANES), im(lambda p, qr, kr: (0, qr[p], 0))),
        scratch_shapes=[pltpu.VMEM((N_PAIRS, 1, 2 * Q_TILE), F32),
                        pltpu.VMEM((N_PAIRS, vrows, 2 * Q_TILE), F32)],
    )
    return pl.pallas_call(
        functools.partial(_diff_attn_kernel, lambda_init=lambda_init),
        grid_spec=grid_spec,
        out_shape=jax.ShapeDtypeStruct((N_PAIRS, rows, LANES), BF16),
        compiler_params=pltpu.CompilerParams(dimension_semantics=("arbitrary",),
                                             vmem_limit_bytes=VMEM_LIMIT),
        name="diff_attn",
    )(steps["qrow"], steps["krow"], steps["qloc"], steps["kloc"], steps["nsub"], steps["full"],
      steps["first"], steps["last"], q_t, qkv, v_t, lam_rows, subln_col)


def _dsa_kernel(qrow, krow, qloc, kloc, nkb, nsub, full, first, last, bidx,
                qt_ref, k_ref, vt_ref, qit_ref, wit_ref, kidx_ref, o_ref,
                key_sc, thr_sc, m_sc, acc_sc, *, topk):
    p = pl.program_id(0)

    @pl.when(first[p] == 1)
    def _():
        m_sc[...] = jnp.full(m_sc.shape, -jnp.inf, F32)
        acc_sc[...] = jnp.zeros(acc_sc.shape, F32)
        n = nkb[p]
        wi = wit_ref[...]
        qpos = qloc[p] * Q_TILE + lax.broadcasted_iota(jnp.int32, (K_TILE, Q_TILE), 1)
        krow_iota = lax.broadcasted_iota(jnp.int32, (K_TILE, Q_TILE), 0)

        def score_chunk(c, carry):
            kt = kidx_ref[pl.ds(pl.multiple_of(c * K_TILE, K_TILE), K_TILE), :]
            sc = jnp.zeros((K_TILE, Q_TILE), F32)
            for jp in range(IDX_PAIRS):
                qi_t = qit_ref[jp]
                for cc in range(2):
                    raw = jnp.dot(kt, _half_select(qi_t, cc == 0), preferred_element_type=F32)
                    hh = 2 * jp + cc
                    sc = sc + jnp.maximum(raw, 0.0) * wi[hh:hh + 1, :]
            kpos = c * K_TILE + krow_iota
            sc = jnp.where(kpos < N_META, BIG_SCORE, sc)
            sc = jnp.where(kpos <= qpos, sc, -jnp.inf)
            bits = lax.bitcast_convert_type(sc, jnp.int32)
            key_sc[c] = bits ^ ((bits >> 31) & 0x7FFFFFFF)
            return carry

        lax.fori_loop(0, n, score_chunk, 0)

        kf = float(topk)

        def count_ge(cand):
            def count_chunk(c, cnt8):
                hit = jnp.where(key_sc[c] >= cand, 1.0, 0.0).reshape(8, K_TILE // 64, 8, Q_TILE)
                return cnt8 + jnp.sum(jnp.sum(hit, axis=1), axis=0)

            cnt8 = lax.fori_loop(0, n, count_chunk, jnp.zeros((8, Q_TILE), F32))
            return jnp.sum(cnt8, axis=0, keepdims=True)

        mantissa_bits = 23

        def bit_body(b, state):
            lo, c_lo, c_hi = state
            cand = lo + jnp.left_shift(jnp.int32(1), 31 - b)
            cnt = count_ge(cand)
            ok = cnt >= kf
            return jnp.where(ok, cand, lo), jnp.where(ok, cnt, c_lo), jnp.where(ok, c_hi, cnt)

        lo, c_lo, c_hi = lax.fori_loop(
            0, 32 - mantissa_bits, bit_body,
            (jnp.full((1, Q_TILE), INT_MIN, jnp.int32),
             jnp.full((1, Q_TILE), 1.0, F32) * (n * K_TILE).astype(F32),
             jnp.zeros((1, Q_TILE), F32)))

        q1 = qloc[p] * Q_TILE + lax.broadcasted_iota(jnp.int32, (1, Q_TILE), 1)
        short = q1 < topk

        def active(width, c_lo):
            return jnp.logical_and(jnp.logical_and(c_lo != kf, width > 1), jnp.logical_not(short))

        def unsettled(state):
            _, _, width, c_lo, _ = state
            return jnp.max(active(width, c_lo).astype(jnp.int32)) > 0

        def probe(state):
            it, lo, width, c_lo, c_hi = state
            act = active(width, c_lo)
            frac = (c_lo - kf) / (c_lo - c_hi)
            d_lin = (width.astype(F32) * frac).astype(jnp.int32)
            d = jnp.where(it % 2 == 0, d_lin, lax.shift_right_logical(width, 1))
            d = jnp.maximum(jnp.minimum(d, width - 1), 1)
            cand = lo + d
            cnt = count_ge(cand)
            ok = cnt >= kf
            return (it + 1,
                    jnp.where(jnp.logical_and(act, ok), cand, lo),
                    jnp.where(act, jnp.where(ok, width - d, d), width),
                    jnp.where(jnp.logical_and(act, ok), cnt, c_lo),
                    jnp.where(jnp.logical_and(act, jnp.logical_not(ok)), cnt, c_hi))

        _, thr, _, _, _ = lax.while_loop(
            unsettled, probe,
            (jnp.int32(0), lo, jnp.full((1, Q_TILE), 1 << mantissa_bits, jnp.int32), c_lo, c_hi))
        thr_sc[...] = jnp.maximum(thr, KEY_NEG_INF + 1)

    def mask_bias(sub):
        sel = key_sc[kloc[p] * K_SUBS + sub] >= thr_sc[...]
        bias1 = jnp.where(sel, 0.0, -jnp.inf).astype(F32)
        return jnp.concatenate([bias1, bias1], axis=1)

    def key_tile(sub, carry):
        _attention_sweep(qt_ref, k_ref, vt_ref, (sub,), mask_bias, m_sc, acc_sc)
        return carry

    @pl.when(full[p] == 1)
    def _():
        _attention_sweep(qt_ref, k_ref, vt_ref, tuple(range(K_SUBS)), mask_bias, m_sc, acc_sc)

    @pl.when(full[p] == 0)
    def _():
        lax.fori_loop(0, nsub[p], key_tile, 0)

    @pl.when(last[p] == 1)
    def _():
        ones_row = 2 * HEAD_DIM
        for j in range(N_PAIRS):
            a = acc_sc[j]
            o = jnp.concatenate([a[:HEAD_DIM, :Q_TILE] / a[ones_row:ones_row + 1, :Q_TILE],
                                 a[HEAD_DIM:ones_row, Q_TILE:] / a[ones_row:ones_row + 1, Q_TILE:]],
                                axis=0)
            o_ref[j] = o.T.astype(BF16)


def _dsa_attention(q_t, qkvi, v_t, qi_t, wi_t, kidx, steps, topk, t_pad):
    rows = qkvi.shape[1]
    n_steps = steps["qrow"].shape[0]
    nkb_total = t_pad // K_TILE
    vrows = v_t.shape[2]

    def im(fn):
        return lambda p, qr, kr, ql, kl, nk, ns, fu, f, l, b: fn(p, qr, kr, b)

    grid_spec = pltpu.PrefetchScalarGridSpec(
        num_scalar_prefetch=10,
        grid=(n_steps,),
        in_specs=[
            pl.BlockSpec((N_PAIRS, LANES, Q_TILE), im(lambda p, qr, kr, b: (0, 0, qr[p]))),
            pl.BlockSpec((N_PAIRS, K_SUBS * K_TILE, LANES), im(lambda p, qr, kr, b: (0, kr[p], 0))),
            pl.BlockSpec((N_PAIRS, K_SUBS, vrows, K_TILE), im(lambda p, qr, kr, b: (0, kr[p], 0, 0))),
            pl.BlockSpec((IDX_PAIRS, LANES, Q_TILE), im(lambda p, qr, kr, b: (0, 0, qr[p]))),
            pl.BlockSpec((IDX_HEADS, Q_TILE), im(lambda p, qr, kr, b: (0, qr[p]))),
            pl.BlockSpec((t_pad, LANES), im(lambda p, qr, kr, b: (b[p], 0))),
        ],
        out_specs=pl.BlockSpec((N_PAIRS, Q_TILE, LANES), im(lambda p, qr, kr, b: (0, qr[p], 0))),
        scratch_shapes=[pltpu.VMEM((nkb_total, K_TILE, Q_TILE), jnp.int32),
                        pltpu.VMEM((1, Q_TILE), jnp.int32),
                        pltpu.VMEM((N_PAIRS, 1, 2 * Q_TILE), F32),
                        pltpu.VMEM((N_PAIRS, vrows, 2 * Q_TILE), F32)],
    )
    return pl.pallas_call(
        functools.partial(_dsa_kernel, topk=topk),
        grid_spec=grid_spec,
        out_shape=jax.ShapeDtypeStruct((N_PAIRS, rows, LANES), BF16),
        compiler_params=pltpu.CompilerParams(dimension_semantics=("arbitrary",),
                                             vmem_limit_bytes=VMEM_LIMIT),
        name="dsa_attn",
    )(steps["qrow"], steps["krow"], steps["qloc"], steps["kloc"], steps["nkb"], steps["nsub"],
      steps["full"], steps["first"], steps["last"], steps["b"], q_t, qkvi, v_t, qi_t, wi_t, kidx)


def _rope_tables(t_pad):
    inv = ROPE_THETA ** (-jnp.arange(0, HEAD_DIM, 2, dtype=F32) / HEAD_DIM)
    ang = jnp.arange(t_pad, dtype=F32)[:, None] * inv[None, :]
    cos, sin = jnp.cos(ang), jnp.sin(ang)
    cos128 = jnp.tile(cos, (1, LANES // (HEAD_DIM // 2)))
    sin128 = jnp.tile(jnp.concatenate([-sin, sin], axis=1), (1, LANES // HEAD_DIM))
    return cos128, sin128


def _interleave_gate_val(a):
    lead = a.shape[:-1]
    a = a.reshape(lead + (2, FFN_HIDDEN // FFN_TILE, FFN_TILE))
    a = jnp.swapaxes(a, -3, -2)
    return a.reshape(lead + (2 * FFN_HIDDEN,))


def _with_ones_rows(v_t):
    ones = jnp.ones((v_t.shape[0], BF16_ROWS, v_t.shape[2]), v_t.dtype)
    return jnp.concatenate([v_t, ones], axis=1)


def _key_tiled(v_t):
    pairs, d, rows = v_t.shape
    return jnp.swapaxes(v_t.reshape(pairs, d, rows // K_TILE, K_TILE), 1, 2)


def kernel(x, meta_tokens, da_norm, da_w_qkv, da_lambda_q1, da_lambda_k1, da_lambda_q2, da_lambda_k2, da_subln, da_w_o, dsa_norm, dsa_w_in, dsa_idx_k_norm, dsa_w_o, ffn_norm, ffn_w_up, ffn_conv_w, ffn_conv_b, ffn_w_down, final_norm):
    batch, seq, d = x.shape
    assert d == D_MODEL
    depth = ffn_norm.shape[0]
    t_real = seq + N_META
    t_pad = _padded_len(t_real)
    assert t_pad % (K_SUBS * K_TILE) == 0 and t_pad % Q_TILE == 0
    rows = batch * t_pad

    meta = jnp.broadcast_to(meta_tokens[None].astype(x.dtype), (batch, N_META, d))
    h = jnp.concatenate([meta, x, jnp.zeros((batch, t_pad - t_real, d), x.dtype)], axis=1)
    h = h.reshape(rows, d)
    cos128, sin128 = _rope_tables(t_pad)
    steps = _attention_steps(batch, t_pad)

    dsa_qkv = N_HEADS * HEAD_DIM
    idx_cols = IDX_HEADS * HEAD_DIM
    for i in range(depth):
        j = i // 2
        if i % 2 == 0:
            lambda_init = 0.8 - 0.6 * math.exp(-0.3 * i)
            q_t, k, v_t = _project(h, da_norm[j], da_w_qkv[j].astype(BF16), cos128, sin128, t_pad)
            lam_rows = jnp.zeros((8, LANES), F32).at[0:4, 0:HEAD_DIM].set(
                jnp.stack([da_lambda_q1[j], da_lambda_k1[j], da_lambda_q2[j], da_lambda_k2[j]]).astype(F32))
            o = _diff_attention(q_t, k, v_t, lam_rows, da_subln[j].reshape(LANES, 1).astype(F32),
                                steps, lambda_init)
            h = _out_project(h, o, da_w_o[j].astype(BF16))
        else:
            w_in = dsa_w_in[j]
            n_main = 3 * dsa_qkv + idx_cols
            w_tail = jnp.zeros((d, LANES), w_in.dtype).at[:, :w_in.shape[1] - n_main].set(w_in[:, n_main:])
            gk = jnp.ones((1, LANES), F32).at[0, :HEAD_DIM].set(dsa_idx_k_norm[j].astype(F32))
            q_t, k, v_t, qi_t, kidx, wi_t = _project(
                h, dsa_norm[j], w_in[:, :n_main].astype(BF16), cos128, sin128, t_pad,
                w_tail=w_tail.astype(BF16), gk=gk)
            topk = min(TOPK_MAX, seq // 4)
            o = _dsa_attention(q_t, k, v_t, qi_t, wi_t, kidx, steps, topk, t_pad)
            h = _out_project(h, o, dsa_w_o[j].astype(BF16))
        h = _ffn(h, ffn_norm[i], _interleave_gate_val(ffn_w_up[i]).astype(BF16),
                 _interleave_gate_val(ffn_conv_w[i]).astype(F32),
                 _interleave_gate_val(ffn_conv_b[i]).reshape(1, -1).astype(F32),
                 ffn_w_down[i].astype(BF16), final_norm, t_pad, final=(i == depth - 1))
    return h.reshape(batch, t_pad, d)[:, N_META:N_META + seq]
```

```python
import functools
import math

import numpy as np
import jax
import jax.numpy as jnp
from jax import lax
from jax.experimental import pallas as pl
from jax.experimental.pallas import tpu as pltpu

D_MODEL = 1024
N_META = 16
ROPE_THETA = 10000.0
RMS_EPS = 1e-6
HEAD_DIM = 64
N_PAIRS = 8
N_HEADS = 2 * N_PAIRS
IDX_HEADS = 8
IDX_PAIRS = IDX_HEADS // 2
TOPK_MAX = 256
BIG_SCORE = 1e30
LOG2_E = 1.4426950408889634
FFN_HIDDEN = 2816
CONV_WIDTH = 3

LANES = 128
BF16_ROWS = 16
ROW_TILE = 512
Q_TILE = 256
K_TILE = 512
K_SUBS = 3
FFN_TILE = 256
HALO = BF16_ROWS
VMEM_LIMIT = 56 * 1024 * 1024

F32 = jnp.float32
BF16 = jnp.bfloat16
INT_MIN = -2147483648
KEY_NEG_INF = INT_MIN + 0x7FFFFF


def _padded_len(t):
    unit = math.lcm(ROW_TILE, K_SUBS * K_TILE, Q_TILE)
    return ((t + unit - 1) // unit) * unit


def _rms(x, gain):
    ms = jnp.mean(x * x, axis=-1, keepdims=True)
    return x * lax.rsqrt(ms + RMS_EPS) * gain


def _rope_lanes(y, cos, sin, lo):
    sw = jnp.where(lo, pltpu.roll(y, LANES - HEAD_DIM // 2, 1), pltpu.roll(y, HEAD_DIM // 2, 1))
    return y * cos + sw * sin


def _proj_kernel(x_ref, g_ref, w_ref, cos_ref, sin_ref, *rest, has_idx):
    if has_idx:
        wt_ref, gk_ref, qt_ref, k_ref, vt_ref, qit_ref, kidx_ref, wit_ref = rest
    else:
        qt_ref, k_ref, vt_ref = rest
    xh = _rms(x_ref[...], g_ref[...]).astype(BF16)
    cos = cos_ref[...]
    sin = sin_ref[...]
    lane = lax.broadcasted_iota(jnp.int32, cos.shape, 1)
    lo = (lane & (HEAD_DIM // 2)) == 0
    n_groups = (3 * N_PAIRS + IDX_PAIRS) if has_idx else 3 * N_PAIRS
    cw = 4 * LANES
    ones = jnp.ones((BF16_ROWS, ROW_TILE), BF16)
    for c in range(n_groups // 4):
        y = jnp.dot(xh, w_ref[:, c * cw:(c + 1) * cw], preferred_element_type=F32)
        for s in range(4):
            g = c * 4 + s
            yg = y[:, s * LANES:(s + 1) * LANES]
            if g < N_PAIRS:
                yg = _rope_lanes(yg, cos, sin, lo) * (HEAD_DIM ** -0.5 * LOG2_E)
                qt_ref[g] = yg.T.astype(BF16)
            elif g < 2 * N_PAIRS:
                k_ref[g - N_PAIRS] = _rope_lanes(yg, cos, sin, lo).astype(BF16)
            elif g < 3 * N_PAIRS:
                j = g - 2 * N_PAIRS
                vt_ref[j, 0, 0:2 * HEAD_DIM, :] = yg.T.astype(BF16)
                vt_ref[j, 0, 2 * HEAD_DIM:, :] = ones
            else:
                yg = _rope_lanes(yg, cos, sin, lo) * (HEAD_DIM ** -0.5)
                qit_ref[g - 3 * N_PAIRS] = yg.T.astype(BF16)
    if has_idx:
        t = jnp.dot(xh, wt_ref[...], preferred_element_type=F32)
        is_k = lane < HEAD_DIM
        ms = jnp.sum(jnp.where(is_k, t * t, 0.0), axis=-1, keepdims=True) * (1.0 / HEAD_DIM)
        kn = t * lax.rsqrt(ms + RMS_EPS) * gk_ref[...]
        kn = _rope_lanes(kn, cos, sin, lo)
        kidx_ref[...] = jnp.where(is_k, kn, pltpu.roll(kn, HEAD_DIM, 1)).astype(BF16)
        wit_ref[...] = (t * (IDX_HEADS ** -0.5)).T[HEAD_DIM:HEAD_DIM + IDX_HEADS, :]


def _project(h, gain, w, cos, sin, t_pad, w_tail=None, gk=None):
    assert ROW_TILE == K_TILE
    rows = h.shape[0]
    blocks_per_batch = t_pad // ROW_TILE
    has_idx = w_tail is not None
    vrows = 2 * HEAD_DIM + BF16_ROWS
    in_specs = [
        pl.BlockSpec((ROW_TILE, D_MODEL), lambda i: (i, 0)),
        pl.BlockSpec((1, D_MODEL), lambda i: (0, 0)),
        pl.BlockSpec(w.shape, lambda i: (0, 0)),
        pl.BlockSpec((ROW_TILE, LANES), lambda i: (i % blocks_per_batch, 0)),
        pl.BlockSpec((ROW_TILE, LANES), lambda i: (i % blocks_per_batch, 0)),
    ]
    out_shape = [jax.ShapeDtypeStruct((N_PAIRS, LANES, rows), BF16),
                 jax.ShapeDtypeStruct((N_PAIRS, rows, LANES), BF16),
                 jax.ShapeDtypeStruct((N_PAIRS, rows // K_TILE, vrows, K_TILE), BF16)]
    out_specs = [pl.BlockSpec((N_PAIRS, LANES, ROW_TILE), lambda i: (0, 0, i)),
                 pl.BlockSpec((N_PAIRS, ROW_TILE, LANES), lambda i: (0, i, 0)),
                 pl.BlockSpec((N_PAIRS, 1, vrows, K_TILE), lambda i: (0, i, 0, 0))]
    args = [h, gain.reshape(1, D_MODEL), w, cos, sin]
    if has_idx:
        in_specs += [pl.BlockSpec(w_tail.shape, lambda i: (0, 0)),
                     pl.BlockSpec((1, LANES), lambda i: (0, 0))]
        out_shape += [jax.ShapeDtypeStruct((IDX_PAIRS, LANES, rows), BF16),
                      jax.ShapeDtypeStruct((rows, LANES), BF16),
                      jax.ShapeDtypeStruct((IDX_HEADS, rows), F32)]
        out_specs += [pl.BlockSpec((IDX_PAIRS, LANES, ROW_TILE), lambda i: (0, 0, i)),
                      pl.BlockSpec((ROW_TILE, LANES), lambda i: (i, 0)),
                      pl.BlockSpec((IDX_HEADS, ROW_TILE), lambda i: (0, i))]
        args += [w_tail, gk]
    return pl.pallas_call(
        functools.partial(_proj_kernel, has_idx=has_idx),
        grid=(rows // ROW_TILE,),
        in_specs=in_specs,
        out_specs=out_specs,
        out_shape=out_shape,
        compiler_params=pltpu.CompilerParams(dimension_semantics=("arbitrary",),
                                             vmem_limit_bytes=VMEM_LIMIT),
        name="proj_idx" if has_idx else "proj",
    )(*args)


def _oproj_kernel(h_ref, o_ref, w_ref, out_ref):
    o = jnp.concatenate([o_ref[j] for j in range(N_PAIRS)], axis=1)
    out_ref[...] = h_ref[...] + jnp.dot(o, w_ref[...], preferred_element_type=F32)


def _out_project(h, o, w):
    rows = h.shape[0]
    return pl.pallas_call(
        _oproj_kernel,
        grid=(rows // ROW_TILE,),
        in_specs=[pl.BlockSpec((ROW_TILE, D_MODEL), lambda i: (i, 0)),
                  pl.BlockSpec((N_PAIRS, ROW_TILE, LANES), lambda i: (0, i, 0)),
                  pl.BlockSpec(w.shape, lambda i: (0, 0))],
        out_specs=pl.BlockSpec((ROW_TILE, D_MODEL), lambda i: (i, 0)),
        out_shape=jax.ShapeDtypeStruct(h.shape, F32),
        compiler_params=pltpu.CompilerParams(dimension_semantics=("arbitrary",),
                                             vmem_limit_bytes=VMEM_LIMIT),
        name="oproj",
    )(h, o, w)


def _ffn_kernel(xp_ref, x_ref, g_ref, wup_ref, cw_ref, cb_ref, wdn_ref, fg_ref, o_ref, u_sc,
                *, blocks_per_batch, final):
    i = pl.program_id(0)
    x = x_ref[...]
    xe = jnp.concatenate([xp_ref[...], x], axis=0)
    xh = _rms(xe, g_ref[...])
    row = lax.broadcasted_iota(jnp.int32, (HALO + ROW_TILE, 1), 0)
    keep = jnp.logical_or(row >= HALO, i % blocks_per_batch != 0)
    xh = jnp.where(keep, xh, 0.0).astype(BF16)
    acc = jnp.zeros((ROW_TILE, D_MODEL), F32)
    cw2 = 2 * FFN_TILE
    for c in range(FFN_HIDDEN // FFN_TILE):
        u_sc[...] = jnp.dot(xh, wup_ref[:, c * cw2:(c + 1) * cw2], preferred_element_type=F32)
        cwc = cw_ref[:, c * cw2:(c + 1) * cw2]
        conv = cb_ref[:, c * cw2:(c + 1) * cw2]
        for j in range(CONV_WIDTH):
            conv = conv + cwc[j:j + 1, :] * u_sc[pl.ds(HALO - (CONV_WIDTH - 1) + j, ROW_TILE), :]
        gate = conv[:, :FFN_TILE]
        val = conv[:, FFN_TILE:]
        a = (gate * jax.nn.sigmoid(gate) * val).astype(BF16)
        acc = acc + jnp.dot(a, wdn_ref[c * FFN_TILE:(c + 1) * FFN_TILE, :], preferred_element_type=F32)
    y = x + acc
    if final:
        y = _rms(y, fg_ref[...])
    o_ref[...] = y


def _ffn(h, gain, w_up, conv_w, conv_b, w_down, final_gain, t_pad, final):
    rows = h.shape[0]
    blocks_per_batch = t_pad // ROW_TILE
    halo_blocks = ROW_TILE // HALO
    return pl.pallas_call(
        functools.partial(_ffn_kernel, blocks_per_batch=blocks_per_batch, final=final),
        grid=(rows // ROW_TILE,),
        in_specs=[pl.BlockSpec((HALO, D_MODEL), lambda i: (jnp.maximum(i * halo_blocks - 1, 0), 0)),
                  pl.BlockSpec((ROW_TILE, D_MODEL), lambda i: (i, 0)),
                  pl.BlockSpec((1, D_MODEL), lambda i: (0, 0)),
                  pl.BlockSpec(w_up.shape, lambda i: (0, 0)),
                  pl.BlockSpec(conv_w.shape, lambda i: (0, 0)),
                  pl.BlockSpec(conv_b.shape, lambda i: (0, 0)),
                  pl.BlockSpec(w_down.shape, lambda i: (0, 0)),
                  pl.BlockSpec((1, D_MODEL), lambda i: (0, 0))],
        out_specs=pl.BlockSpec((ROW_TILE, D_MODEL), lambda i: (i, 0)),
        out_shape=jax.ShapeDtypeStruct(h.shape, F32),
        scratch_shapes=[pltpu.VMEM((HALO + ROW_TILE, 2 * FFN_TILE), F32)],
        compiler_params=pltpu.CompilerParams(dimension_semantics=("arbitrary",),
                                             vmem_limit_bytes=VMEM_LIMIT),
        name="ffn_final" if final else "ffn",
    )(h, h, gain.reshape(1, D_MODEL), w_up, conv_w, conv_b, w_down, final_gain.reshape(1, D_MODEL))


def _attention_steps(batch, t_pad):
    nq = t_pad // Q_TILE
    nk = t_pad // (K_SUBS * K_TILE)
    cols = {k: [] for k in ("qrow", "krow", "qloc", "kloc", "nkb", "nsub", "full", "first", "last", "b")}
    for b in range(batch):
        for qi in range(nq):
            n_tiles = ((qi + 1) * Q_TILE + K_TILE - 1) // K_TILE
            n = (n_tiles + K_SUBS - 1) // K_SUBS
            for ki in range(n):
                cols["qrow"].append(b * nq + qi)
                cols["krow"].append(b * nk + ki)
                cols["qloc"].append(qi)
                cols["kloc"].append(ki)
                cols["nkb"].append(n_tiles)
                cols["nsub"].append(min(K_SUBS, n_tiles - ki * K_SUBS))
                cols["full"].append(int((ki + 1) * K_SUBS < n_tiles))
                cols["first"].append(int(ki == 0))
                cols["last"].append(int(ki == n - 1))
                cols["b"].append(b)
    return {k: jnp.asarray(np.asarray(v, np.int32)) for k, v in cols.items()}


def _half_select(x_t, first_half):
    row = lax.broadcasted_iota(jnp.int32, x_t.shape, 0)
    keep = (row < HEAD_DIM) if first_half else (row >= HEAD_DIM)
    return jnp.where(keep, x_t, jnp.zeros_like(x_t))


def _zero_after(x):
    w = lax.bitcast_convert_type(x, jnp.uint32)
    w = lax.shift_right_logical(lax.shift_right_logical(w, jnp.uint32(16)), jnp.uint32(16))
    return lax.bitcast_convert_type(w, F32)[0:1, :]


def _flash_update(s_t, v_t, h, m_sc, acc_sc, after=None):
    m_prev = m_sc[h]
    m_new = jnp.maximum(m_prev, jnp.max(s_t, axis=0, keepdims=True))
    alpha = jnp.exp2(m_prev - m_new)
    if after is None:
        p_t = jnp.exp2(s_t - m_new).astype(BF16)
    else:
        cut = (3 * s_t.shape[0]) // 4
        p_t = jnp.concatenate([jnp.exp2(s_t[:cut] - m_new),
                               jnp.exp2(s_t[cut:] - (m_new + after))], axis=0).astype(BF16)
    acc_sc[h] = alpha * acc_sc[h] + jnp.dot(v_t, p_t, preferred_element_type=F32)
    m_sc[h] = m_new


def _both_halves(x_t):
    return jnp.concatenate([_half_select(x_t, True), _half_select(x_t, False)], axis=1)


def _attention_sweep(qt_ref, k_ref, vt_ref, subs, bias_of, m_sc, acc_sc):
    def scores(sub, j):
        start = sub * K_TILE
        if not isinstance(sub, int):
            start = pl.multiple_of(start, K_TILE)
        return jnp.dot(k_ref[j, pl.ds(start, K_TILE), :], _both_halves(qt_ref[j]), preferred_element_type=F32)

    items = [(sub, j) for sub in subs for j in range(N_PAIRS)]
    biases = {}
    ahead = 2
    pending = [scores(*item) for item in items[:ahead]]
    for i, (sub, j) in enumerate(items):
        s_t = pending.pop(0)
        tile_key = sub if isinstance(sub, int) else "traced"
        if tile_key not in biases:
            biases[tile_key] = bias_of(sub)
        if biases[tile_key] is not None:
            s_t = s_t + biases[tile_key]
        after = None
        if i + ahead < len(items):
            pending.append(scores(*items[i + ahead]))
            after = _zero_after(pending[-1][0:8, :])
        _flash_update(s_t, vt_ref[j, sub], j, m_sc, acc_sc, after=after)


def _diff_attn_kernel(qrow, krow, qloc, kloc, nsub, full, first, last,
                      qt_ref, k_ref, vt_ref, lam_ref, subln_ref, o_ref, m_sc, acc_sc,
                      *, lambda_init):
    p = pl.program_id(0)

    @pl.when(first[p] == 1)
    def _():
        m_sc[...] = jnp.full(m_sc.shape, -jnp.inf, F32)
        acc_sc[...] = jnp.zeros(acc_sc.shape, F32)

    def key_tile(sub, carry):
        tile = kloc[p] * K_SUBS + sub
        on_diagonal = tile * K_TILE + (K_TILE - 1) > qloc[p] * Q_TILE

        @pl.when(on_diagonal)
        def _():
            kpos = tile * K_TILE + lax.broadcasted_iota(jnp.int32, (K_TILE, 2 * Q_TILE), 0)
            qcol = lax.broadcasted_iota(jnp.int32, (K_TILE, 2 * Q_TILE), 1)
            qpos = qloc[p] * Q_TILE + jnp.where(qcol >= Q_TILE, qcol - Q_TILE, qcol)
            bias = jnp.where(kpos <= qpos, 0.0, -jnp.inf).astype(F32)
            _attention_sweep(qt_ref, k_ref, vt_ref, (sub,), lambda _: bias, m_sc, acc_sc)

        @pl.when(jnp.logical_not(on_diagonal))
        def _():
            _attention_sweep(qt_ref, k_ref, vt_ref, (sub,), lambda _: None, m_sc, acc_sc)

        return carry

    @pl.when(full[p] == 1)
    def _():
        _attention_sweep(qt_ref, k_ref, vt_ref, tuple(range(K_SUBS)), lambda _: None, m_sc, acc_sc)

    @pl.when(full[p] == 0)
    def _():
        lax.fori_loop(0, nsub[p], key_tile, 0)

    @pl.when(last[p] == 1)
    def _():
        lam_rows = lam_ref[...]
        lam = (jnp.exp(jnp.sum(lam_rows[0:1] * lam_rows[1:2], axis=-1, keepdims=True))
               - jnp.exp(jnp.sum(lam_rows[2:3] * lam_rows[3:4], axis=-1, keepdims=True))
               + lambda_init)
        vdim = 2 * HEAD_DIM
        for j in range(N_PAIRS):
            a = acc_sc[j]
            a1 = a[:, :Q_TILE]
            a2 = a[:, Q_TILE:]
            o = a1[:vdim] / a1[vdim:vdim + 1] - lam * (a2[:vdim] / a2[vdim:vdim + 1])
            ms = jnp.mean(o * o, axis=0, keepdims=True)
            o = o * lax.rsqrt(ms + RMS_EPS) * subln_ref[...] * (1.0 - lambda_init)
            o_ref[j] = o.T.astype(BF16)


def _diff_attention(q_t, qkv, v_t, lam_rows, subln_col, steps, lambda_init):
    rows = qkv.shape[1]
    n_steps = steps["qrow"].shape[0]
    vrows = v_t.shape[2]

    def im(fn):
        return lambda p, qr, kr, ql, kl, ns, fu, f, l: fn(p, qr, kr)

    grid_spec = pltpu.PrefetchScalarGridSpec(
        num_scalar_prefetch=8,
        grid=(n_steps,),
        in_specs=[
            pl.BlockSpec((N_PAIRS, LANES, Q_TILE), im(lambda p, qr, kr: (0, 0, qr[p]))),
            pl.BlockSpec((N_PAIRS, K_SUBS * K_TILE, LANES), im(lambda p, qr, kr: (0, kr[p], 0))),
            pl.BlockSpec((N_PAIRS, K_SUBS, vrows, K_TILE), im(lambda p, qr, kr: (0, kr[p], 0, 0))),
            pl.BlockSpec((8, LANES), im(lambda p, qr, kr: (0, 0))),
            pl.BlockSpec((LANES, 1), im(lambda p, qr, kr: (0, 0))),
        ],
        out_specs=pl.BlockSpec((N_PAIRS, Q_TILE, LANES), im(lambda p, qr, kr: (0, qr[p], 0))),
        scratch_shapes=[pltpu.VMEM((N_PAIRS, 1, 2 * Q_TILE), F32),
                        pltpu.VMEM((N_PAIRS, vrows, 2 * Q_TILE), F32)],
    )
    return pl.pallas_call(
        functools.partial(_diff_attn_kernel, lambda_init=lambda_init),
        grid_spec=grid_spec,
        out_shape=jax.ShapeDtypeStruct((N_PAIRS, rows, LANES), BF16),
        compiler_params=pltpu.CompilerParams(dimension_semantics=("arbitrary",),
                                             vmem_limit_bytes=VMEM_LIMIT),
        name="diff_attn",
    )(steps["qrow"], steps["krow"], steps["qloc"], steps["kloc"], steps["nsub"], steps["full"],
      steps["first"], steps["last"], q_t, qkv, v_t, lam_rows, subln_col)


def _dsa_kernel(qrow, krow, qloc, kloc, nkb, nsub, full, first, last, bidx,
                qt_ref, k_ref, vt_ref, qit_ref, wit_ref, kidx_ref, o_ref,
                key_sc, coarse_sc, thr_sc, m_sc, acc_sc, *, topk):
    p = pl.program_id(0)

    @pl.when(first[p] == 1)
    def _():
        m_sc[...] = jnp.full(m_sc.shape, -jnp.inf, F32)
        acc_sc[...] = jnp.zeros(acc_sc.shape, F32)
        n = nkb[p]
        wi = wit_ref[...]
        qpos = qloc[p] * Q_TILE + lax.broadcasted_iota(jnp.int32, (K_TILE, Q_TILE), 1)
        krow_iota = lax.broadcasted_iota(jnp.int32, (K_TILE, Q_TILE), 0)

        def score_chunk(c, carry):
            kt = kidx_ref[pl.ds(pl.multiple_of(c * K_TILE, K_TILE), K_TILE), :]
            sc = jnp.zeros((K_TILE, Q_TILE), F32)
            for jp in range(IDX_PAIRS):
                qi_t = qit_ref[jp]
                for cc in range(2):
                    raw = jnp.dot(kt, _half_select(qi_t, cc == 0), preferred_element_type=F32)
                    hh = 2 * jp + cc
                    sc = sc + jnp.maximum(raw, 0.0) * wi[hh:hh + 1, :]
            kpos = c * K_TILE + krow_iota
            sc = jnp.where(kpos < N_META, BIG_SCORE, sc)
            sc = jnp.where(kpos <= qpos, sc, -jnp.inf)
            sc = jnp.where(sc == 0.0, 0.0, sc)
            bits = lax.bitcast_convert_type(sc, jnp.int32)
            key_sc[c] = bits ^ ((bits >> 31) & 0x7FFFFFFF)
            coarse_sc[c] = lax.bitcast_convert_type(bits & jnp.int32(-65536), F32).astype(BF16)
            return carry

        lax.fori_loop(0, n, score_chunk, 0)

        kf = float(topk)
        q1 = qloc[p] * Q_TILE + lax.broadcasted_iota(jnp.int32, (1, Q_TILE), 1)
        settled0 = (q1 < topk).astype(jnp.int32)
        coarse_bits = 16

        def coarse_body(b, state):
            lo, settled = state
            cand = lo + jnp.left_shift(jnp.int32(1), 31 - b)
            top = cand >> 16
            cbits = jnp.left_shift(top ^ ((top >> 31) & 0x7FFF), 16)
            cand16 = lax.bitcast_convert_type(cbits, F32).astype(BF16)

            def count_chunk(c, cnt16):
                hit = jnp.where(coarse_sc[c] >= cand16, jnp.ones((), BF16), jnp.zeros((), BF16))
                parts = [hit[i * BF16_ROWS:(i + 1) * BF16_ROWS] for i in range(K_TILE // BF16_ROWS)]
                while len(parts) > 1:
                    parts = [parts[i] + parts[i + 1] for i in range(0, len(parts), 2)]
                return cnt16 + parts[0].astype(F32)

            cnt16 = lax.fori_loop(0, n, count_chunk, jnp.zeros((BF16_ROWS, Q_TILE), F32))
            cnt = jnp.sum(cnt16, axis=0, keepdims=True)
            return jnp.where(cnt >= kf, cand, lo), jnp.where(cnt == kf, 1, settled)

        lo, settled = lax.fori_loop(0, coarse_bits, coarse_body,
                                    (jnp.full((1, Q_TILE), INT_MIN, jnp.int32), settled0))

        def unsettled(state):
            b, _, settled = state
            return jnp.logical_and(b < 32, jnp.min(settled) == 0)

        def fine_body(state):
            b, lo, settled = state
            cand = lo + jnp.left_shift(jnp.int32(1), 31 - b)

            def count_chunk(c, cnt8):
                hit = jnp.where(key_sc[c] >= cand, 1.0, 0.0).reshape(8, K_TILE // 64, 8, Q_TILE)
                return cnt8 + jnp.sum(jnp.sum(hit, axis=1), axis=0)

            cnt8 = lax.fori_loop(0, n, count_chunk, jnp.zeros((8, Q_TILE), F32))
            cnt = jnp.sum(cnt8, axis=0, keepdims=True)
            return b + 1, jnp.where(cnt >= kf, cand, lo), jnp.where(cnt == kf, 1, settled)

        _, thr, _ = lax.while_loop(unsettled, fine_body, (jnp.int32(coarse_bits), lo, settled))
        thr_sc[...] = jnp.maximum(thr, KEY_NEG_INF + 1)

    def mask_bias(sub):
        sel = key_sc[kloc[p] * K_SUBS + sub] >= thr_sc[...]
        bias1 = jnp.where(sel, 0.0, -jnp.inf).astype(F32)
        return jnp.concatenate([bias1, bias1], axis=1)

    def key_tile(sub, carry):
        _attention_sweep(qt_ref, k_ref, vt_ref, (sub,), mask_bias, m_sc, acc_sc)
        return carry

    @pl.when(full[p] == 1)
    def _():
        _attention_sweep(qt_ref, k_ref, vt_ref, tuple(range(K_SUBS)), mask_bias, m_sc, acc_sc)

    @pl.when(full[p] == 0)
    def _():
        lax.fori_loop(0, nsub[p], key_tile, 0)

    @pl.when(last[p] == 1)
    def _():
        ones_row = 2 * HEAD_DIM
        for j in range(N_PAIRS):
            a = acc_sc[j]
            o = jnp.concatenate([a[:HEAD_DIM, :Q_TILE] / a[ones_row:ones_row + 1, :Q_TILE],
                                 a[HEAD_DIM:ones_row, Q_TILE:] / a[ones_row:ones_row + 1, Q_TILE:]],
                                axis=0)
            o_ref[j] = o.T.astype(BF16)


def _dsa_attention(q_t, qkvi, v_t, qi_t, wi_t, kidx, steps, topk, t_pad):
    rows = qkvi.shape[1]
    n_steps = steps["qrow"].shape[0]
    nkb_total = t_pad // K_TILE
    vrows = v_t.shape[2]

    def im(fn):
        return lambda p, qr, kr, ql, kl, nk, ns, fu, f, l, b: fn(p, qr, kr, b)

    grid_spec = pltpu.PrefetchScalarGridSpec(
        num_scalar_prefetch=10,
        grid=(n_steps,),
        in_specs=[
            pl.BlockSpec((N_PAIRS, LANES, Q_TILE), im(lambda p, qr, kr, b: (0, 0, qr[p]))),
            pl.BlockSpec((N_PAIRS, K_SUBS * K_TILE, LANES), im(lambda p, qr, kr, b: (0, kr[p], 0))),
            pl.BlockSpec((N_PAIRS, K_SUBS, vrows, K_TILE), im(lambda p, qr, kr, b: (0, kr[p], 0, 0))),
            pl.BlockSpec((IDX_PAIRS, LANES, Q_TILE), im(lambda p, qr, kr, b: (0, 0, qr[p]))),
            pl.BlockSpec((IDX_HEADS, Q_TILE), im(lambda p, qr, kr, b: (0, qr[p]))),
            pl.BlockSpec((t_pad, LANES), im(lambda p, qr, kr, b: (b[p], 0))),
        ],
        out_specs=pl.BlockSpec((N_PAIRS, Q_TILE, LANES), im(lambda p, qr, kr, b: (0, qr[p], 0))),
        scratch_shapes=[pltpu.VMEM((nkb_total, K_TILE, Q_TILE), jnp.int32),
                        pltpu.VMEM((nkb_total, K_TILE, Q_TILE), BF16),
                        pltpu.VMEM((1, Q_TILE), jnp.int32),
                        pltpu.VMEM((N_PAIRS, 1, 2 * Q_TILE), F32),
                        pltpu.VMEM((N_PAIRS, vrows, 2 * Q_TILE), F32)],
    )
    return pl.pallas_call(
        functools.partial(_dsa_kernel, topk=topk),
        grid_spec=grid_spec,
        out_shape=jax.ShapeDtypeStruct((N_PAIRS, rows, LANES), BF16),
        compiler_params=pltpu.CompilerParams(dimension_semantics=("arbitrary",),
                                             vmem_limit_bytes=VMEM_LIMIT),
        name="dsa_attn",
    )(steps["qrow"], steps["krow"], steps["qloc"], steps["kloc"], steps["nkb"], steps["nsub"],
      steps["full"], steps["first"], steps["last"], steps["b"], q_t, qkvi, v_t, qi_t, wi_t, kidx)


def _rope_tables(t_pad):
    inv = ROPE_THETA ** (-jnp.arange(0, HEAD_DIM, 2, dtype=F32) / HEAD_DIM)
    ang = jnp.arange(t_pad, dtype=F32)[:, None] * inv[None, :]
    cos, sin = jnp.cos(ang), jnp.sin(ang)
    cos128 = jnp.tile(cos, (1, LANES // (HEAD_DIM // 2)))
    sin128 = jnp.tile(jnp.concatenate([-sin, sin], axis=1), (1, LANES // HEAD_DIM))
    return cos128, sin128


def _interleave_gate_val(a):
    lead = a.shape[:-1]
    a = a.reshape(lead + (2, FFN_HIDDEN // FFN_TILE, FFN_TILE))
    a = jnp.swapaxes(a, -3, -2)
    return a.reshape(lead + (2 * FFN_HIDDEN,))


def _with_ones_rows(v_t):
    ones = jnp.ones((v_t.shape[0], BF16_ROWS, v_t.shape[2]), v_t.dtype)
    return jnp.concatenate([v_t, ones], axis=1)


def _key_tiled(v_t):
    pairs, d, rows = v_t.shape
    return jnp.swapaxes(v_t.reshape(pairs, d, rows // K_TILE, K_TILE), 1, 2)


def kernel(x, meta_tokens, da_norm, da_w_qkv, da_lambda_q1, da_lambda_k1, da_lambda_q2, da_lambda_k2, da_subln, da_w_o, dsa_norm, dsa_w_in, dsa_idx_k_norm, dsa_w_o, ffn_norm, ffn_w_up, ffn_conv_w, ffn_conv_b, ffn_w_down, final_norm):
    batch, seq, d = x.shape
    assert d == D_MODEL
    depth = ffn_norm.shape[0]
    t_real = seq + N_META
    t_pad = _padded_len(t_real)
    assert t_pad % (K_SUBS * K_TILE) == 0 and t_pad % Q_TILE == 0
    rows = batch * t_pad

    meta = jnp.broadcast_to(meta_tokens[None].astype(x.dtype), (batch, N_META, d))
    h = jnp.concatenate([meta, x, jnp.zeros((batch, t_pad - t_real, d), x.dtype)], axis=1)
    h = h.reshape(rows, d)
    cos128, sin128 = _rope_tables(t_pad)
    steps = _attention_steps(batch, t_pad)

    dsa_qkv = N_HEADS * HEAD_DIM
    idx_cols = IDX_HEADS * HEAD_DIM
    for i in range(depth):
        j = i // 2
        if i % 2 == 0:
            lambda_init = 0.8 - 0.6 * math.exp(-0.3 * i)
            q_t, k, v_t = _project(h, da_norm[j], da_w_qkv[j].astype(BF16), cos128, sin128, t_pad)
            lam_rows = jnp.zeros((8, LANES), F32).at[0:4, 0:HEAD_DIM].set(
                jnp.stack([da_lambda_q1[j], da_lambda_k1[j], da_lambda_q2[j], da_lambda_k2[j]]).astype(F32))
            o = _diff_attention(q_t, k, v_t, lam_rows, da_subln[j].reshape(LANES, 1).astype(F32),
                                steps, lambda_init)
            h = _out_project(h, o, da_w_o[j].astype(BF16))
        else:
            w_in = dsa_w_in[j]
            n_main = 3 * dsa_qkv + idx_cols
            w_tail = jnp.zeros((d, LANES), w_in.dtype).at[:, :w_in.shape[1] - n_main].set(w_in[:, n_main:])
            gk = jnp.ones((1, LANES), F32).at[0, :HEAD_DIM].set(dsa_idx_k_norm[j].astype(F32))
            q_t, k, v_t, qi_t, kidx, wi_t = _project(
                h, dsa_norm[j], w_in[:, :n_main].astype(BF16), cos128, sin128, t_pad,
                w_tail=w_tail.astype(BF16), gk=gk)
            topk = min(TOPK_MAX, seq // 4)
            o = _dsa_attention(q_t, k, v_t, qi_t, wi_t, kidx, steps, topk, t_pad)
            h = _out_project(h, o, dsa_w_o[j].astype(BF16))
        h = _ffn(h, ffn_norm[i], _interleave_gate_val(ffn_w_up[i]).astype(BF16),
                 _interleave_gate_val(ffn_conv_w[i]).astype(F32),
                 _interleave_gate_val(ffn_conv_b[i]).reshape(1, -1).astype(F32),
                 ffn_w_down[i].astype(BF16), final_norm, t_pad, final=(i == depth - 1))
    return h.reshape(batch, t_pad, d)[:, N_META:N_META + seq]
```

```python
import functools
import math

import numpy as np
import jax
import jax.numpy as jnp
from jax import lax
from jax.experimental import pallas as pl
from jax.experimental.pallas import tpu as pltpu

D_MODEL = 1024
N_META = 16
ROPE_THETA = 10000.0
RMS_EPS = 1e-6
HEAD_DIM = 64
N_PAIRS = 8
N_HEADS = 2 * N_PAIRS
IDX_HEADS = 8
IDX_PAIRS = IDX_HEADS // 2
TOPK_MAX = 256
BIG_SCORE = 1e30
LOG2_E = 1.4426950408889634
FFN_HIDDEN = 2816
CONV_WIDTH = 3

LANES = 128
BF16_ROWS = 16
ROW_TILE = 512
Q_TILE = 256
K_TILE = 512
K_SUBS = 3
FFN_TILE = 256
HALO = BF16_ROWS
VMEM_LIMIT = 56 * 1024 * 1024

F32 = jnp.float32
BF16 = jnp.bfloat16
INT_MIN = -2147483648
KEY_NEG_INF = INT_MIN + 0x7FFFFF


def _padded_len(t):
    unit = math.lcm(ROW_TILE, K_SUBS * K_TILE, Q_TILE)
    return ((t + unit - 1) // unit) * unit


def _rms(x, gain):
    ms = jnp.mean(x * x, axis=-1, keepdims=True)
    return x * lax.rsqrt(ms + RMS_EPS) * gain


def _rope_lanes(y, cos, sin, lo):
    sw = jnp.where(lo, pltpu.roll(y, LANES - HEAD_DIM // 2, 1), pltpu.roll(y, HEAD_DIM // 2, 1))
    return y * cos + sw * sin


def _proj_kernel(x_ref, g_ref, w_ref, cos_ref, sin_ref, *rest, has_idx):
    if has_idx:
        wt_ref, gk_ref, qt_ref, k_ref, vt_ref, qit_ref, kidx_ref, wit_ref = rest
    else:
        qt_ref, k_ref, vt_ref = rest
    xh = _rms(x_ref[...], g_ref[...]).astype(BF16)
    cos = cos_ref[...]
    sin = sin_ref[...]
    lane = lax.broadcasted_iota(jnp.int32, cos.shape, 1)
    lo = (lane & (HEAD_DIM // 2)) == 0
    n_groups = (3 * N_PAIRS + IDX_PAIRS) if has_idx else 3 * N_PAIRS
    cw = 4 * LANES
    ones = jnp.ones((BF16_ROWS, ROW_TILE), BF16)
    for c in range(n_groups // 4):
        y = jnp.dot(xh, w_ref[:, c * cw:(c + 1) * cw], preferred_element_type=F32)
        for s in range(4):
            g = c * 4 + s
            yg = y[:, s * LANES:(s + 1) * LANES]
            if g < N_PAIRS:
                yg = _rope_lanes(yg, cos, sin, lo) * (HEAD_DIM ** -0.5 * LOG2_E)
                qt_ref[g] = yg.T.astype(BF16)
            elif g < 2 * N_PAIRS:
                k_ref[g - N_PAIRS] = _rope_lanes(yg, cos, sin, lo).astype(BF16)
            elif g < 3 * N_PAIRS:
                j = g - 2 * N_PAIRS
                vt_ref[j, 0, 0:2 * HEAD_DIM, :] = yg.T.astype(BF16)
                vt_ref[j, 0, 2 * HEAD_DIM:, :] = ones
            else:
                yg = _rope_lanes(yg, cos, sin, lo) * (HEAD_DIM ** -0.5)
                qit_ref[g - 3 * N_PAIRS] = yg.T.astype(BF16)
    if has_idx:
        t = jnp.dot(xh, wt_ref[...], preferred_element_type=F32)
        is_k = lane < HEAD_DIM
        ms = jnp.sum(jnp.where(is_k, t * t, 0.0), axis=-1, keepdims=True) * (1.0 / HEAD_DIM)
        kn = t * lax.rsqrt(ms + RMS_EPS) * gk_ref[...]
        kn = _rope_lanes(kn, cos, sin, lo)
        kidx_ref[...] = jnp.where(is_k, kn, pltpu.roll(kn, HEAD_DIM, 1)).astype(BF16)
        wit_ref[...] = (t * (IDX_HEADS ** -0.5)).T[HEAD_DIM:HEAD_DIM + IDX_HEADS, :]


def _project(h, gain, w, cos, sin, t_pad, w_tail=None, gk=None):
    assert ROW_TILE == K_TILE
    rows = h.shape[0]
    blocks_per_batch = t_pad // ROW_TILE
    has_idx = w_tail is not None
    vrows = 2 * HEAD_DIM + BF16_ROWS
    in_specs = [
        pl.BlockSpec((ROW_TILE, D_MODEL), lambda i: (i, 0)),
        pl.BlockSpec((1, D_MODEL), lambda i: (0, 0)),
        pl.BlockSpec(w.shape, lambda i: (0, 0)),
        pl.BlockSpec((ROW_TILE, LANES), lambda i: (i % blocks_per_batch, 0)),
        pl.BlockSpec((ROW_TILE, LANES), lambda i: (i % blocks_per_batch, 0)),
    ]
    out_shape = [jax.ShapeDtypeStruct((N_PAIRS, LANES, rows), BF16),
                 jax.ShapeDtypeStruct((N_PAIRS, rows, LANES), BF16),
                 jax.ShapeDtypeStruct((N_PAIRS, rows // K_TILE, vrows, K_TILE), BF16)]
    out_specs = [pl.BlockSpec((N_PAIRS, LANES, ROW_TILE), lambda i: (0, 0, i)),
                 pl.BlockSpec((N_PAIRS, ROW_TILE, LANES), lambda i: (0, i, 0)),
                 pl.BlockSpec((N_PAIRS, 1, vrows, K_TILE), lambda i: (0, i, 0, 0))]
    args = [h, gain.reshape(1, D_MODEL), w, cos, sin]
    if has_idx:
        in_specs += [pl.BlockSpec(w_tail.shape, lambda i: (0, 0)),
                     pl.BlockSpec((1, LANES), lambda i: (0, 0))]
        out_shape += [jax.ShapeDtypeStruct((IDX_PAIRS, LANES, rows), BF16),
                      jax.ShapeDtypeStruct((rows, LANES), BF16),
                      jax.ShapeDtypeStruct((IDX_HEADS, rows), F32)]
        out_specs += [pl.BlockSpec((IDX_PAIRS, LANES, ROW_TILE), lambda i: (0, 0, i)),
                      pl.BlockSpec((ROW_TILE, LANES), lambda i: (i, 0)),
                      pl.BlockSpec((IDX_HEADS, ROW_TILE), lambda i: (0, i))]
        args += [w_tail, gk]
    return pl.pallas_call(
        functools.partial(_proj_kernel, has_idx=has_idx),
        grid=(rows // ROW_TILE,),
        in_specs=in_specs,
        out_specs=out_specs,
        out_shape=out_shape,
        compiler_params=pltpu.CompilerParams(dimension_semantics=("arbitrary",),
                                             vmem_limit_bytes=VMEM_LIMIT),
        name="proj_idx" if has_idx else "proj",
    )(*args)


def _oproj_kernel(h_ref, o_ref, w_ref, out_ref):
    o = jnp.concatenate([o_ref[j] for j in range(N_PAIRS)], axis=1)
    out_ref[...] = h_ref[...] + jnp.dot(o, w_ref[...], preferred_element_type=F32)


def _out_project(h, o, w):
    rows = h.shape[0]
    return pl.pallas_call(
        _oproj_kernel,
        grid=(rows // ROW_TILE,),
        in_specs=[pl.BlockSpec((ROW_TILE, D_MODEL), lambda i: (i, 0)),
                  pl.BlockSpec((N_PAIRS, ROW_TILE, LANES), lambda i: (0, i, 0)),
                  pl.BlockSpec(w.shape, lambda i: (0, 0))],
        out_specs=pl.BlockSpec((ROW_TILE, D_MODEL), lambda i: (i, 0)),
        out_shape=jax.ShapeDtypeStruct(h.shape, F32),
        compiler_params=pltpu.CompilerParams(dimension_semantics=("arbitrary",),
                                             vmem_limit_bytes=VMEM_LIMIT),
        name="oproj",
    )(h, o, w)


def _ffn_kernel(xp_ref, x_ref, g_ref, wup_ref, cw_ref, cb_ref, wdn_ref, fg_ref, o_ref, u_sc,
                *, blocks_per_batch, final):
    i = pl.program_id(0)
    x = x_ref[...]
    xe = jnp.concatenate([xp_ref[...], x], axis=0)
    xh = _rms(xe, g_ref[...])
    row = lax.broadcasted_iota(jnp.int32, (HALO + ROW_TILE, 1), 0)
    keep = jnp.logical_or(row >= HALO, i % blocks_per_batch != 0)
    xh = jnp.where(keep, xh, 0.0).astype(BF16)
    acc = jnp.zeros((ROW_TILE, D_MODEL), F32)
    cw2 = 2 * FFN_TILE
    for c in range(FFN_HIDDEN // FFN_TILE):
        u_sc[...] = jnp.dot(xh, wup_ref[:, c * cw2:(c + 1) * cw2], preferred_element_type=F32)
        cwc = cw_ref[:, c * cw2:(c + 1) * cw2]
        conv = cb_ref[:, c * cw2:(c + 1) * cw2]
        for j in range(CONV_WIDTH):
            conv = conv + cwc[j:j + 1, :] * u_sc[pl.ds(HALO - (CONV_WIDTH - 1) + j, ROW_TILE), :]
        gate = conv[:, :FFN_TILE]
        val = conv[:, FFN_TILE:]
        a = (gate * jax.nn.sigmoid(gate) * val).astype(BF16)
        acc = acc + jnp.dot(a, wdn_ref[c * FFN_TILE:(c + 1) * FFN_TILE, :], preferred_element_type=F32)
    y = x + acc
    if final:
        y = _rms(y, fg_ref[...])
    o_ref[...] = y


def _ffn(h, gain, w_up, conv_w, conv_b, w_down, final_gain, t_pad, final):
    rows = h.shape[0]
    blocks_per_batch = t_pad // ROW_TILE
    halo_blocks = ROW_TILE // HALO
    return pl.pallas_call(
        functools.partial(_ffn_kernel, blocks_per_batch=blocks_per_batch, final=final),
        grid=(rows // ROW_TILE,),
        in_specs=[pl.BlockSpec((HALO, D_MODEL), lambda i: (jnp.maximum(i * halo_blocks - 1, 0), 0)),
                  pl.BlockSpec((ROW_TILE, D_MODEL), lambda i: (i, 0)),
                  pl.BlockSpec((1, D_MODEL), lambda i: (0, 0)),
                  pl.BlockSpec(w_up.shape, lambda i: (0, 0)),
                  pl.BlockSpec(conv_w.shape, lambda i: (0, 0)),
                  pl.BlockSpec(conv_b.shape, lambda i: (0, 0)),
                  pl.BlockSpec(w_down.shape, lambda i: (0, 0)),
                  pl.BlockSpec((1, D_MODEL), lambda i: (0, 0))],
        out_specs=pl.BlockSpec((ROW_TILE, D_MODEL), lambda i: (i, 0)),
        out_shape=jax.ShapeDtypeStruct(h.shape, F32),
        scratch_shapes=[pltpu.VMEM((HALO + ROW_TILE, 2 * FFN_TILE), F32)],
        compiler_params=pltpu.CompilerParams(dimension_semantics=("arbitrary",),
                                             vmem_limit_bytes=VMEM_LIMIT),
        name="ffn_final" if final else "ffn",
    )(h, h, gain.reshape(1, D_MODEL), w_up, conv_w, conv_b, w_down, final_gain.reshape(1, D_MODEL))


def _attention_steps(batch, t_pad):
    nq = t_pad // Q_TILE
    nk = t_pad // (K_SUBS * K_TILE)
    cols = {k: [] for k in ("qrow", "krow", "qloc", "kloc", "nkb", "nsub", "full", "first", "last", "b")}
    for b in range(batch):
        for qi in range(nq):
            n_tiles = ((qi + 1) * Q_TILE + K_TILE - 1) // K_TILE
            n = (n_tiles + K_SUBS - 1) // K_SUBS
            for ki in range(n):
                cols["qrow"].append(b * nq + qi)
                cols["krow"].append(b * nk + ki)
                cols["qloc"].append(qi)
                cols["kloc"].append(ki)
                cols["nkb"].append(n_tiles)
                cols["nsub"].append(min(K_SUBS, n_tiles - ki * K_SUBS))
                cols["full"].append(int((ki + 1) * K_SUBS < n_tiles))
                cols["first"].append(int(ki == 0))
                cols["last"].append(int(ki == n - 1))
                cols["b"].append(b)
    return {k: jnp.asarray(np.asarray(v, np.int32)) for k, v in cols.items()}


def _half_select(x_t, first_half):
    row = lax.broadcasted_iota(jnp.int32, x_t.shape, 0)
    keep = (row < HEAD_DIM) if first_half else (row >= HEAD_DIM)
    return jnp.where(keep, x_t, jnp.zeros_like(x_t))


def _zero_after(x):
    w = lax.bitcast_convert_type(x, jnp.uint32)
    w = lax.shift_right_logical(lax.shift_right_logical(w, jnp.uint32(16)), jnp.uint32(16))
    return lax.bitcast_convert_type(w, F32)[0:1, :]


def _flash_update(s_t, v_t, h, m_sc, acc_sc, after=None):
    m_prev = m_sc[h]
    m_new = jnp.maximum(m_prev, jnp.max(s_t, axis=0, keepdims=True))
    alpha = jnp.exp2(m_prev - m_new)
    if after is None:
        p_t = jnp.exp2(s_t - m_new).astype(BF16)
    else:
        cut = s_t.shape[0] // 2
        p_t = jnp.concatenate([jnp.exp2(s_t[:cut] - m_new),
                               jnp.exp2(s_t[cut:] - (m_new + after))], axis=0).astype(BF16)
    acc_sc[h] = alpha * acc_sc[h] + jnp.dot(v_t, p_t, preferred_element_type=F32)
    m_sc[h] = m_new


def _both_halves(x_t):
    return jnp.concatenate([_half_select(x_t, True), _half_select(x_t, False)], axis=1)


def _attention_sweep(qt_ref, k_ref, vt_ref, subs, bias_of, m_sc, acc_sc):
    def scores(sub, j):
        start = sub * K_TILE
        if not isinstance(sub, int):
            start = pl.multiple_of(start, K_TILE)
        return jnp.dot(k_ref[j, pl.ds(start, K_TILE), :], _both_halves(qt_ref[j]), preferred_element_type=F32)

    items = [(sub, j) for sub in subs for j in range(N_PAIRS)]
    biases = {}
    ahead = 2
    pending = [scores(*item) for item in items[:ahead]]
    for i, (sub, j) in enumerate(items):
        s_t = pending.pop(0)
        tile_key = sub if isinstance(sub, int) else "traced"
        if tile_key not in biases:
            biases[tile_key] = bias_of(sub)
        if biases[tile_key] is not None:
            s_t = s_t + biases[tile_key]
        after = None
        if i + ahead < len(items):
            pending.append(scores(*items[i + ahead]))
            after = _zero_after(pending[-1][0:8, :])
        _flash_update(s_t, vt_ref[j, sub], j, m_sc, acc_sc, after=after)


def _diff_attn_kernel(qrow, krow, qloc, kloc, nsub, full, first, last,
                      qt_ref, k_ref, vt_ref, lam_ref, subln_ref, o_ref, m_sc, acc_sc,
                      *, lambda_init):
    p = pl.program_id(0)

    @pl.when(first[p] == 1)
    def _():
        m_sc[...] = jnp.full(m_sc.shape, -jnp.inf, F32)
        acc_sc[...] = jnp.zeros(acc_sc.shape, F32)

    def key_tile(sub, carry):
        tile = kloc[p] * K_SUBS + sub
        on_diagonal = tile * K_TILE + (K_TILE - 1) > qloc[p] * Q_TILE

        @pl.when(on_diagonal)
        def _():
            kpos = tile * K_TILE + lax.broadcasted_iota(jnp.int32, (K_TILE, 2 * Q_TILE), 0)
            qcol = lax.broadcasted_iota(jnp.int32, (K_TILE, 2 * Q_TILE), 1)
            qpos = qloc[p] * Q_TILE + jnp.where(qcol >= Q_TILE, qcol - Q_TILE, qcol)
            bias = jnp.where(kpos <= qpos, 0.0, -jnp.inf).astype(F32)
            _attention_sweep(qt_ref, k_ref, vt_ref, (sub,), lambda _: bias, m_sc, acc_sc)

        @pl.when(jnp.logical_not(on_diagonal))
        def _():
            _attention_sweep(qt_ref, k_ref, vt_ref, (sub,), lambda _: None, m_sc, acc_sc)

        return carry

    @pl.when(full[p] == 1)
    def _():
        _attention_sweep(qt_ref, k_ref, vt_ref, tuple(range(K_SUBS)), lambda _: None, m_sc, acc_sc)

    @pl.when(full[p] == 0)
    def _():
        lax.fori_loop(0, nsub[p], key_tile, 0)

    @pl.when(last[p] == 1)
    def _():
        lam_rows = lam_ref[...]
        lam = (jnp.exp(jnp.sum(lam_rows[0:1] * lam_rows[1:2], axis=-1, keepdims=True))
               - jnp.exp(jnp.sum(lam_rows[2:3] * lam_rows[3:4], axis=-1, keepdims=True))
               + lambda_init)
        vdim = 2 * HEAD_DIM
        for j in range(N_PAIRS):
            a = acc_sc[j]
            a1 = a[:, :Q_TILE]
            a2 = a[:, Q_TILE:]
            o = a1[:vdim] / a1[vdim:vdim + 1] - lam * (a2[:vdim] / a2[vdim:vdim + 1])
            ms = jnp.mean(o * o, axis=0, keepdims=True)
            o = o * lax.rsqrt(ms + RMS_EPS) * subln_ref[...] * (1.0 - lambda_init)
            o_ref[j] = o.T.astype(BF16)


def _diff_attention(q_t, qkv, v_t, lam_rows, subln_col, steps, lambda_init):
    rows = qkv.shape[1]
    n_steps = steps["qrow"].shape[0]
    vrows = v_t.shape[2]

    def im(fn):
        return lambda p, qr, kr, ql, kl, ns, fu, f, l: fn(p, qr, kr)

    grid_spec = pltpu.PrefetchScalarGridSpec(
        num_scalar_prefetch=8,
        grid=(n_steps,),
        in_specs=[
            pl.BlockSpec((N_PAIRS, LANES, Q_TILE), im(lambda p, qr, kr: (0, 0, qr[p]))),
            pl.BlockSpec((N_PAIRS, K_SUBS * K_TILE, LANES), im(lambda p, qr, kr: (0, kr[p], 0))),
            pl.BlockSpec((N_PAIRS, K_SUBS, vrows, K_TILE), im(lambda p, qr, kr: (0, kr[p], 0, 0))),
            pl.BlockSpec((8, LANES), im(lambda p, qr, kr: (0, 0))),
            pl.BlockSpec((LANES, 1), im(lambda p, qr, kr: (0, 0))),
        ],
        out_specs=pl.BlockSpec((N_PAIRS, Q_TILE, LANES), im(lambda p, qr, kr: (0, qr[p], 0))),
        scratch_shapes=[pltpu.VMEM((N_PAIRS, 1, 2 * Q_TILE), F32),
                        pltpu.VMEM((N_PAIRS, vrows, 2 * Q_TILE), F32)],
    )
    return pl.pallas_call(
        functools.partial(_diff_attn_kernel, lambda_init=lambda_init),
        grid_spec=grid_spec,
        out_shape=jax.ShapeDtypeStruct((N_PAIRS, rows, LANES), BF16),
        compiler_params=pltpu.CompilerParams(dimension_semantics=("arbitrary",),
                                             vmem_limit_bytes=VMEM_LIMIT),
        name="diff_attn",
    )(steps["qrow"], steps["krow"], steps["qloc"], steps["kloc"], steps["nsub"], steps["full"],
      steps["first"], steps["last"], q_t, qkv, v_t, lam_rows, subln_col)


def _dsa_kernel(qrow, krow, qloc, kloc, nkb, nsub, full, first, last, bidx,
                qt_ref, k_ref, vt_ref, qit_ref, wit_ref, kidx_ref, o_ref,
                key_sc, coarse_sc, thr_sc, m_sc, acc_sc, *, topk):
    p = pl.program_id(0)

    @pl.when(first[p] == 1)
    def _():
        m_sc[...] = jnp.full(m_sc.shape, -jnp.inf, F32)
        acc_sc[...] = jnp.zeros(acc_sc.shape, F32)
        n = nkb[p]
        wi = wit_ref[...]
        qpos = qloc[p] * Q_TILE + lax.broadcasted_iota(jnp.int32, (K_TILE, Q_TILE), 1)
        krow_iota = lax.broadcasted_iota(jnp.int32, (K_TILE, Q_TILE), 0)

        def score_chunk(c, carry):
            kt = kidx_ref[pl.ds(pl.multiple_of(c * K_TILE, K_TILE), K_TILE), :]
            sc = jnp.zeros((K_TILE, Q_TILE), F32)
            for jp in range(IDX_PAIRS):
                qi_t = qit_ref[jp]
                for cc in range(2):
                    raw = jnp.dot(kt, _half_select(qi_t, cc == 0), preferred_element_type=F32)
                    hh = 2 * jp + cc
                    sc = sc + jnp.maximum(raw, 0.0) * wi[hh:hh + 1, :]
            kpos = c * K_TILE + krow_iota
            sc = jnp.where(kpos < N_META, BIG_SCORE, sc)
            sc = jnp.where(kpos <= qpos, sc, -jnp.inf)
            sc = jnp.where(sc == 0.0, 0.0, sc)
            bits = lax.bitcast_convert_type(sc, jnp.int32)
            key_sc[c] = bits ^ ((bits >> 31) & 0x7FFFFFFF)
            coarse_sc[c] = lax.bitcast_convert_type(bits & jnp.int32(-65536), F32).astype(BF16)
            return carry

        lax.fori_loop(0, n, score_chunk, 0)

        kf = float(topk)
        q1 = qloc[p] * Q_TILE + lax.broadcasted_iota(jnp.int32, (1, Q_TILE), 1)
        settled0 = (q1 < topk).astype(jnp.int32)
        coarse_bits = 16

        def coarse_body(b, state):
            lo, settled = state
            cand = lo + jnp.left_shift(jnp.int32(1), 31 - b)
            top = cand >> 16
            cbits = jnp.left_shift(top ^ ((top >> 31) & 0x7FFF), 16)
            cand16 = lax.bitcast_convert_type(cbits, F32).astype(BF16)

            def count_chunk(c, cnt16):
                hit = jnp.where(coarse_sc[c] >= cand16, jnp.ones((), BF16), jnp.zeros((), BF16))
                parts = [hit[i * BF16_ROWS:(i + 1) * BF16_ROWS] for i in range(K_TILE // BF16_ROWS)]
                while len(parts) > 1:
                    parts = [parts[i] + parts[i + 1] for i in range(0, len(parts), 2)]
                return cnt16 + parts[0].astype(F32)

            cnt16 = lax.fori_loop(0, n, count_chunk, jnp.zeros((BF16_ROWS, Q_TILE), F32))
            cnt = jnp.sum(cnt16, axis=0, keepdims=True)
            return jnp.where(cnt >= kf, cand, lo), jnp.where(cnt == kf, 1, settled)

        lo, settled = lax.fori_loop(0, coarse_bits, coarse_body,
                                    (jnp.full((1, Q_TILE), INT_MIN, jnp.int32), settled0))

        def unsettled(state):
            b, _, settled = state
            return jnp.logical_and(b < 32, jnp.min(settled) == 0)

        def fine_body(state):
            b, lo, settled = state
            cand = lo + jnp.left_shift(jnp.int32(1), 31 - b)

            def count_chunk(c, cnt8):
                hit = jnp.where(key_sc[c] >= cand, 1.0, 0.0).reshape(8, K_TILE // 64, 8, Q_TILE)
                return cnt8 + jnp.sum(jnp.sum(hit, axis=1), axis=0)

            cnt8 = lax.fori_loop(0, n, count_chunk, jnp.zeros((8, Q_TILE), F32))
            cnt = jnp.sum(cnt8, axis=0, keepdims=True)
            return b + 1, jnp.where(cnt >= kf, cand, lo), jnp.where(cnt == kf, 1, settled)

        _, thr, _ = lax.while_loop(unsettled, fine_body, (jnp.int32(coarse_bits), lo, settled))
        thr_sc[...] = jnp.maximum(thr, KEY_NEG_INF + 1)

    def mask_bias(sub):
        sel = key_sc[kloc[p] * K_SUBS + sub] >= thr_sc[...]
        bias1 = jnp.where(sel, 0.0, -jnp.inf).astype(F32)
        return jnp.concatenate([bias1, bias1], axis=1)

    def key_tile(sub, carry):
        _attention_sweep(qt_ref, k_ref, vt_ref, (sub,), mask_bias, m_sc, acc_sc)
        return carry

    @pl.when(full[p] == 1)
    def _():
        _attention_sweep(qt_ref, k_ref, vt_ref, tuple(range(K_SUBS)), mask_bias, m_sc, acc_sc)

    @pl.when(full[p] == 0)
    def _():
        lax.fori_loop(0, nsub[p], key_tile, 0)

    @pl.when(last[p] == 1)
    def _():
        ones_row = 2 * HEAD_DIM
        for j in range(N_PAIRS):
            a = acc_sc[j]
            o = jnp.concatenate([a[:HEAD_DIM, :Q_TILE] / a[ones_row:ones_row + 1, :Q_TILE],
                                 a[HEAD_DIM:ones_row, Q_TILE:] / a[ones_row:ones_row + 1, Q_TILE:]],
                                axis=0)
            o_ref[j] = o.T.astype(BF16)


def _dsa_attention(q_t, qkvi, v_t, qi_t, wi_t, kidx, steps, topk, t_pad):
    rows = qkvi.shape[1]
    n_steps = steps["qrow"].shape[0]
    nkb_total = t_pad // K_TILE
    vrows = v_t.shape[2]

    def im(fn):
        return lambda p, qr, kr, ql, kl, nk, ns, fu, f, l, b: fn(p, qr, kr, b)

    grid_spec = pltpu.PrefetchScalarGridSpec(
        num_scalar_prefetch=10,
        grid=(n_steps,),
        in_specs=[
            pl.BlockSpec((N_PAIRS, LANES, Q_TILE), im(lambda p, qr, kr, b: (0, 0, qr[p]))),
            pl.BlockSpec((N_PAIRS, K_SUBS * K_TILE, LANES), im(lambda p, qr, kr, b: (0, kr[p], 0))),
            pl.BlockSpec((N_PAIRS, K_SUBS, vrows, K_TILE), im(lambda p, qr, kr, b: (0, kr[p], 0, 0))),
            pl.BlockSpec((IDX_PAIRS, LANES, Q_TILE), im(lambda p, qr, kr, b: (0, 0, qr[p]))),
            pl.BlockSpec((IDX_HEADS, Q_TILE), im(lambda p, qr, kr, b: (0, qr[p]))),
            pl.BlockSpec((t_pad, LANES), im(lambda p, qr, kr, b: (b[p], 0))),
        ],
        out_specs=pl.BlockSpec((N_PAIRS, Q_TILE, LANES), im(lambda p, qr, kr, b: (0, qr[p], 0))),
        scratch_shapes=[pltpu.VMEM((nkb_total, K_TILE, Q_TILE), jnp.int32),
                        pltpu.VMEM((nkb_total, K_TILE, Q_TILE), BF16),
                        pltpu.VMEM((1, Q_TILE), jnp.int32),
                        pltpu.VMEM((N_PAIRS, 1, 2 * Q_TILE), F32),
                        pltpu.VMEM((N_PAIRS, vrows, 2 * Q_TILE), F32)],
    )
    return pl.pallas_call(
        functools.partial(_dsa_kernel, topk=topk),
        grid_spec=grid_spec,
        out_shape=jax.ShapeDtypeStruct((N_PAIRS, rows, LANES), BF16),
        compiler_params=pltpu.CompilerParams(dimension_semantics=("arbitrary",),
                                             vmem_limit_bytes=VMEM_LIMIT),
        name="dsa_attn",
    )(steps["qrow"], steps["krow"], steps["qloc"], steps["kloc"], steps["nkb"], steps["nsub"],
      steps["full"], steps["first"], steps["last"], steps["b"], q_t, qkvi, v_t, qi_t, wi_t, kidx)


def _rope_tables(t_pad):
    inv = ROPE_THETA ** (-jnp.arange(0, HEAD_DIM, 2, dtype=F32) / HEAD_DIM)
    ang = jnp.arange(t_pad, dtype=F32)[:, None] * inv[None, :]
    cos, sin = jnp.cos(ang), jnp.sin(ang)
    cos128 = jnp.tile(cos, (1, LANES // (HEAD_DIM // 2)))
    sin128 = jnp.tile(jnp.concatenate([-sin, sin], axis=1), (1, LANES // HEAD_DIM))
    return cos128, sin128


def _interleave_gate_val(a):
    lead = a.shape[:-1]
    a = a.reshape(lead + (2, FFN_HIDDEN // FFN_TILE, FFN_TILE))
    a = jnp.swapaxes(a, -3, -2)
    return a.reshape(lead + (2 * FFN_HIDDEN,))


def _with_ones_rows(v_t):
    ones = jnp.ones((v_t.shape[0], BF16_ROWS, v_t.shape[2]), v_t.dtype)
    return jnp.concatenate([v_t, ones], axis=1)


def _key_tiled(v_t):
    pairs, d, rows = v_t.shape
    return jnp.swapaxes(v_t.reshape(pairs, d, rows // K_TILE, K_TILE), 1, 2)


def kernel(x, meta_tokens, da_norm, da_w_qkv, da_lambda_q1, da_lambda_k1, da_lambda_q2, da_lambda_k2, da_subln, da_w_o, dsa_norm, dsa_w_in, dsa_idx_k_norm, dsa_w_o, ffn_norm, ffn_w_up, ffn_conv_w, ffn_conv_b, ffn_w_down, final_norm):
    batch, seq, d = x.shape
    assert d == D_MODEL
    depth = ffn_norm.shape[0]
    t_real = seq + N_META
    t_pad = _padded_len(t_real)
    assert t_pad % (K_SUBS * K_TILE) == 0 and t_pad % Q_TILE == 0
    rows = batch * t_pad

    meta = jnp.broadcast_to(meta_tokens[None].astype(x.dtype), (batch, N_META, d))
    h = jnp.concatenate([meta, x, jnp.zeros((batch, t_pad - t_real, d), x.dtype)], axis=1)
    h = h.reshape(rows, d)
    cos128, sin128 = _rope_tables(t_pad)
    steps = _attention_steps(batch, t_pad)

    dsa_qkv = N_HEADS * HEAD_DIM
    idx_cols = IDX_HEADS * HEAD_DIM
    for i in range(depth):
        j = i // 2
        if i % 2 == 0:
            lambda_init = 0.8 - 0.6 * math.exp(-0.3 * i)
            q_t, k, v_t = _project(h, da_norm[j], da_w_qkv[j].astype(BF16), cos128, sin128, t_pad)
            lam_rows = jnp.zeros((8, LANES), F32).at[0:4, 0:HEAD_DIM].set(
                jnp.stack([da_lambda_q1[j], da_lambda_k1[j], da_lambda_q2[j], da_lambda_k2[j]]).astype(F32))
            o = _diff_attention(q_t, k, v_t, lam_rows, da_subln[j].reshape(LANES, 1).astype(F32),
                                steps, lambda_init)
            h = _out_project(h, o, da_w_o[j].astype(BF16))
        else:
            w_in = dsa_w_in[j]
            n_main = 3 * dsa_qkv + idx_cols
            w_tail = jnp.zeros((d, LANES), w_in.dtype).at[:, :w_in.shape[1] - n_main].set(w_in[:, n_main:])
            gk = jnp.ones((1, LANES), F32).at[0, :HEAD_DIM].set(dsa_idx_k_norm[j].astype(F32))
            q_t, k, v_t, qi_t, kidx, wi_t = _project(
                h, dsa_norm[j], w_in[:, :n_main].astype(BF16), cos128, sin128, t_pad,
                w_tail=w_tail.astype(BF16), gk=gk)
            topk = min(TOPK_MAX, seq // 4)
            o = _dsa_attention(q_t, k, v_t, qi_t, wi_t, kidx, steps, topk, t_pad)
            h = _out_project(h, o, dsa_w_o[j].astype(BF16))
        h = _ffn(h, ffn_norm[i], _interleave_gate_val(ffn_w_up[i]).astype(BF16),
                 _interleave_gate_val(ffn_conv_w[i]).astype(F32),
                 _interleave_gate_val(ffn_conv_b[i]).reshape(1, -1).astype(F32),
                 ffn_w_down[i].astype(BF16), final_norm, t_pad, final=(i == depth - 1))
    return h.reshape(batch, t_pad, d)[:, N_META:N_META + seq]
```

```python
import functools
import math

import numpy as np
import jax
import jax.numpy as jnp
from jax import lax
from jax.experimental import pallas as pl
from jax.experimental.pallas import tpu as pltpu

D_MODEL = 1024
N_META = 16
ROPE_THETA = 10000.0
RMS_EPS = 1e-6
HEAD_DIM = 64
N_PAIRS = 8
N_HEADS = 2 * N_PAIRS
IDX_HEADS = 8
IDX_PAIRS = IDX_HEADS // 2
TOPK_MAX = 256
BIG_SCORE = 1e30
LOG2_E = 1.4426950408889634
FFN_HIDDEN = 2816
CONV_WIDTH = 3

LANES = 128
BF16_ROWS = 16
ROW_TILE = 512
Q_TILE = 256
K_TILE = 512
K_SUBS = 3
FFN_TILE = 256
HALO = BF16_ROWS
VMEM_LIMIT = 56 * 1024 * 1024

F32 = jnp.float32
BF16 = jnp.bfloat16
INT_MIN = -2147483648
KEY_NEG_INF = INT_MIN + 0x7FFFFF


def _padded_len(t):
    unit = math.lcm(ROW_TILE, K_SUBS * K_TILE, Q_TILE)
    return ((t + unit - 1) // unit) * unit


def _rms(x, gain):
    ms = jnp.mean(x * x, axis=-1, keepdims=True)
    return x * lax.rsqrt(ms + RMS_EPS) * gain


def _rope_lanes(y, cos, sin, lo):
    sw = jnp.where(lo, pltpu.roll(y, LANES - HEAD_DIM // 2, 1), pltpu.roll(y, HEAD_DIM // 2, 1))
    return y * cos + sw * sin


def _proj_kernel(x_ref, g_ref, w_ref, cos_ref, sin_ref, *rest, has_idx):
    if has_idx:
        wt_ref, gk_ref, qt_ref, k_ref, vt_ref, qit_ref, kidx_ref, wit_ref = rest
    else:
        qt_ref, k_ref, vt_ref = rest
    xh = _rms(x_ref[...], g_ref[...]).astype(BF16)
    cos = cos_ref[...]
    sin = sin_ref[...]
    lane = lax.broadcasted_iota(jnp.int32, cos.shape, 1)
    lo = (lane & (HEAD_DIM // 2)) == 0
    n_groups = (3 * N_PAIRS + IDX_PAIRS) if has_idx else 3 * N_PAIRS
    cw = 4 * LANES
    ones = jnp.ones((BF16_ROWS, ROW_TILE), BF16)
    for c in range(n_groups // 4):
        y = jnp.dot(xh, w_ref[:, c * cw:(c + 1) * cw], preferred_element_type=F32)
        for s in range(4):
            g = c * 4 + s
            yg = y[:, s * LANES:(s + 1) * LANES]
            if g < N_PAIRS:
                yg = _rope_lanes(yg, cos, sin, lo) * (HEAD_DIM ** -0.5 * LOG2_E)
                qt_ref[g] = yg.T.astype(BF16)
            elif g < 2 * N_PAIRS:
                k_ref[g - N_PAIRS] = _rope_lanes(yg, cos, sin, lo).astype(BF16)
            elif g < 3 * N_PAIRS:
                j = g - 2 * N_PAIRS
                vt_ref[j, 0, 0:2 * HEAD_DIM, :] = yg.T.astype(BF16)
                vt_ref[j, 0, 2 * HEAD_DIM:, :] = ones
            else:
                yg = _rope_lanes(yg, cos, sin, lo) * (HEAD_DIM ** -0.5)
                qit_ref[g - 3 * N_PAIRS] = yg.T.astype(BF16)
    if has_idx:
        t = jnp.dot(xh, wt_ref[...], preferred_element_type=F32)
        is_k = lane < HEAD_DIM
        ms = jnp.sum(jnp.where(is_k, t * t, 0.0), axis=-1, keepdims=True) * (1.0 / HEAD_DIM)
        kn = t * lax.rsqrt(ms + RMS_EPS) * gk_ref[...]
        kn = _rope_lanes(kn, cos, sin, lo)
        kidx_ref[...] = jnp.where(is_k, kn, pltpu.roll(kn, HEAD_DIM, 1)).astype(BF16)
        wit_ref[...] = (t * (IDX_HEADS ** -0.5)).T[HEAD_DIM:HEAD_DIM + IDX_HEADS, :]


def _project(h, gain, w, cos, sin, t_pad, w_tail=None, gk=None):
    assert ROW_TILE == K_TILE
    rows = h.shape[0]
    blocks_per_batch = t_pad // ROW_TILE
    has_idx = w_tail is not None
    vrows = 2 * HEAD_DIM + BF16_ROWS
    in_specs = [
        pl.BlockSpec((ROW_TILE, D_MODEL), lambda i: (i, 0)),
        pl.BlockSpec((1, D_MODEL), lambda i: (0, 0)),
        pl.BlockSpec(w.shape, lambda i: (0, 0)),
        pl.BlockSpec((ROW_TILE, LANES), lambda i: (i % blocks_per_batch, 0)),
        pl.BlockSpec((ROW_TILE, LANES), lambda i: (i % blocks_per_batch, 0)),
    ]
    out_shape = [jax.ShapeDtypeStruct((N_PAIRS, LANES, rows), BF16),
                 jax.ShapeDtypeStruct((N_PAIRS, rows, LANES), BF16),
                 jax.ShapeDtypeStruct((N_PAIRS, rows // K_TILE, vrows, K_TILE), BF16)]
    out_specs = [pl.BlockSpec((N_PAIRS, LANES, ROW_TILE), lambda i: (0, 0, i)),
                 pl.BlockSpec((N_PAIRS, ROW_TILE, LANES), lambda i: (0, i, 0)),
                 pl.BlockSpec((N_PAIRS, 1, vrows, K_TILE), lambda i: (0, i, 0, 0))]
    args = [h, gain.reshape(1, D_MODEL), w, cos, sin]
    if has_idx:
        in_specs += [pl.BlockSpec(w_tail.shape, lambda i: (0, 0)),
                     pl.BlockSpec((1, LANES), lambda i: (0, 0))]
        out_shape += [jax.ShapeDtypeStruct((IDX_PAIRS, LANES, rows), BF16),
                      jax.ShapeDtypeStruct((rows, LANES), BF16),
                      jax.ShapeDtypeStruct((IDX_HEADS, rows), F32)]
        out_specs += [pl.BlockSpec((IDX_PAIRS, LANES, ROW_TILE), lambda i: (0, 0, i)),
                      pl.BlockSpec((ROW_TILE, LANES), lambda i: (i, 0)),
                      pl.BlockSpec((IDX_HEADS, ROW_TILE), lambda i: (0, i))]
        args += [w_tail, gk]
    return pl.pallas_call(
        functools.partial(_proj_kernel, has_idx=has_idx),
        grid=(rows // ROW_TILE,),
        in_specs=in_specs,
        out_specs=out_specs,
        out_shape=out_shape,
        compiler_params=pltpu.CompilerParams(dimension_semantics=("arbitrary",),
                                             vmem_limit_bytes=VMEM_LIMIT),
        name="proj_idx" if has_idx else "proj",
    )(*args)


def _oproj_kernel(h_ref, o_ref, w_ref, out_ref):
    o = jnp.concatenate([o_ref[j] for j in range(N_PAIRS)], axis=1)
    out_ref[...] = h_ref[...] + jnp.dot(o, w_ref[...], preferred_element_type=F32)


def _out_project(h, o, w):
    rows = h.shape[0]
    return pl.pallas_call(
        _oproj_kernel,
        grid=(rows // ROW_TILE,),
        in_specs=[pl.BlockSpec((ROW_TILE, D_MODEL), lambda i: (i, 0)),
                  pl.BlockSpec((N_PAIRS, ROW_TILE, LANES), lambda i: (0, i, 0)),
                  pl.BlockSpec(w.shape, lambda i: (0, 0))],
        out_specs=pl.BlockSpec((ROW_TILE, D_MODEL), lambda i: (i, 0)),
        out_shape=jax.ShapeDtypeStruct(h.shape, F32),
        compiler_params=pltpu.CompilerParams(dimension_semantics=("arbitrary",),
                                             vmem_limit_bytes=VMEM_LIMIT),
        name="oproj",
    )(h, o, w)


def _ffn_kernel(xp_ref, x_ref, g_ref, wup_ref, cw_ref, cb_ref, wdn_ref, fg_ref, o_ref, u_sc, a_sc,
                *, blocks_per_batch, final):
    i = pl.program_id(0)
    x = x_ref[...]
    xe = jnp.concatenate([xp_ref[...], x], axis=0)
    xh = _rms(xe, g_ref[...])
    row = lax.broadcasted_iota(jnp.int32, (HALO + ROW_TILE, 1), 0)
    keep = jnp.logical_or(row >= HALO, i % blocks_per_batch != 0)
    xh = jnp.where(keep, xh, 0.0).astype(BF16)
    cw2 = 2 * FFN_TILE
    n_chunks = FFN_HIDDEN // FFN_TILE

    def up(c):
        return jnp.dot(xh, wup_ref[:, c * cw2:(c + 1) * cw2], preferred_element_type=F32)

    u_sc[0] = up(0)
    for c in range(n_chunks):
        cur = c % 2
        after = None
        if c + 1 < n_chunks:
            u_sc[1 - cur] = up(c + 1)
            after = _zero_after(u_sc[1 - cur, 0:8, 0:FFN_TILE])
        cwc = cw_ref[:, c * cw2:(c + 1) * cw2]
        conv = cb_ref[:, c * cw2:(c + 1) * cw2]
        for j in range(CONV_WIDTH):
            conv = conv + cwc[j:j + 1, :] * u_sc[cur, pl.ds(HALO - (CONV_WIDTH - 1) + j, ROW_TILE), :]
        gate = conv[:, :FFN_TILE]
        val = conv[:, FFN_TILE:]
        a = gate * jax.nn.sigmoid(gate) * val
        if after is not None:
            cut = (3 * ROW_TILE) // 4
            a = jnp.concatenate([a[:cut], a[cut:] + after], axis=0)
        a_sc[:, c * FFN_TILE:(c + 1) * FFN_TILE] = a.astype(BF16)
    acc = jnp.dot(a_sc[...], wdn_ref[...], preferred_element_type=F32)
    y = x + acc
    if final:
        y = _rms(y, fg_ref[...])
    o_ref[...] = y


def _ffn(h, gain, w_up, conv_w, conv_b, w_down, final_gain, t_pad, final):
    rows = h.shape[0]
    blocks_per_batch = t_pad // ROW_TILE
    halo_blocks = ROW_TILE // HALO
    return pl.pallas_call(
        functools.partial(_ffn_kernel, blocks_per_batch=blocks_per_batch, final=final),
        grid=(rows // ROW_TILE,),
        in_specs=[pl.BlockSpec((HALO, D_MODEL), lambda i: (jnp.maximum(i * halo_blocks - 1, 0), 0)),
                  pl.BlockSpec((ROW_TILE, D_MODEL), lambda i: (i, 0)),
                  pl.BlockSpec((1, D_MODEL), lambda i: (0, 0)),
                  pl.BlockSpec(w_up.shape, lambda i: (0, 0)),
                  pl.BlockSpec(conv_w.shape, lambda i: (0, 0)),
                  pl.BlockSpec(conv_b.shape, lambda i: (0, 0)),
                  pl.BlockSpec(w_down.shape, lambda i: (0, 0)),
                  pl.BlockSpec((1, D_MODEL), lambda i: (0, 0))],
        out_specs=pl.BlockSpec((ROW_TILE, D_MODEL), lambda i: (i, 0)),
        out_shape=jax.ShapeDtypeStruct(h.shape, F32),
        scratch_shapes=[pltpu.VMEM((2, HALO + ROW_TILE, 2 * FFN_TILE), F32),
                        pltpu.VMEM((ROW_TILE, FFN_HIDDEN), BF16)],
        compiler_params=pltpu.CompilerParams(dimension_semantics=("arbitrary",),
                                             vmem_limit_bytes=VMEM_LIMIT),
        name="ffn_final" if final else "ffn",
    )(h, h, gain.reshape(1, D_MODEL), w_up, conv_w, conv_b, w_down, final_gain.reshape(1, D_MODEL))


def _attention_steps(batch, t_pad):
    nq = t_pad // Q_TILE
    nk = t_pad // (K_SUBS * K_TILE)
    cols = {k: [] for k in ("qrow", "krow", "qloc", "kloc", "nkb", "nsub", "full", "first", "last", "b")}
    for b in range(batch):
        for qi in range(nq):
            n_tiles = ((qi + 1) * Q_TILE + K_TILE - 1) // K_TILE
            n = (n_tiles + K_SUBS - 1) // K_SUBS
            for ki in range(n):
                cols["qrow"].append(b * nq + qi)
                cols["krow"].append(b * nk + ki)
                cols["qloc"].append(qi)
                cols["kloc"].append(ki)
                cols["nkb"].append(n_tiles)
                cols["nsub"].append(min(K_SUBS, n_tiles - ki * K_SUBS))
                cols["full"].append(int((ki + 1) * K_SUBS < n_tiles))
                cols["first"].append(int(ki == 0))
                cols["last"].append(int(ki == n - 1))
                cols["b"].append(b)
    return {k: jnp.asarray(np.asarray(v, np.int32)) for k, v in cols.items()}


def _half_select(x_t, first_half):
    row = lax.broadcasted_iota(jnp.int32, x_t.shape, 0)
    keep = (row < HEAD_DIM) if first_half else (row >= HEAD_DIM)
    return jnp.where(keep, x_t, jnp.zeros_like(x_t))


def _zero_after(x):
    w = lax.bitcast_convert_type(x, jnp.uint32)
    w = lax.shift_right_logical(lax.shift_right_logical(w, jnp.uint32(16)), jnp.uint32(16))
    return lax.bitcast_convert_type(w, F32)[0:1, :]


def _flash_update(s_t, v_t, h, m_sc, acc_sc, after=None):
    m_prev = m_sc[h]
    m_new = jnp.maximum(m_prev, jnp.max(s_t, axis=0, keepdims=True))
    alpha = jnp.exp2(m_prev - m_new)
    if after is None:
        p_t = jnp.exp2(s_t - m_new).astype(BF16)
    else:
        cut = s_t.shape[0] // 2
        p_t = jnp.concatenate([jnp.exp2(s_t[:cut] - m_new),
                               jnp.exp2(s_t[cut:] - (m_new + after))], axis=0).astype(BF16)
    acc_sc[h] = alpha * acc_sc[h] + jnp.dot(v_t, p_t, preferred_element_type=F32)
    m_sc[h] = m_new


def _both_halves(x_t):
    return jnp.concatenate([_half_select(x_t, True), _half_select(x_t, False)], axis=1)


def _attention_sweep(qt_ref, k_ref, vt_ref, subs, bias_of, m_sc, acc_sc):
    def scores(sub, j):
        start = sub * K_TILE
        if not isinstance(sub, int):
            start = pl.multiple_of(start, K_TILE)
        return jnp.dot(k_ref[j, pl.ds(start, K_TILE), :], _both_halves(qt_ref[j]), preferred_element_type=F32)

    items = [(sub, j) for sub in subs for j in range(N_PAIRS)]
    biases = {}
    ahead = 2
    pending = [scores(*item) for item in items[:ahead]]
    for i, (sub, j) in enumerate(items):
        s_t = pending.pop(0)
        tile_key = sub if isinstance(sub, int) else "traced"
        if tile_key not in biases:
            biases[tile_key] = bias_of(sub)
        if biases[tile_key] is not None:
            s_t = s_t + biases[tile_key]
        after = None
        if i + ahead < len(items):
            pending.append(scores(*items[i + ahead]))
            after = _zero_after(pending[-1][0:8, :])
        _flash_update(s_t, vt_ref[j, sub], j, m_sc, acc_sc, after=after)


def _diff_attn_kernel(qrow, krow, qloc, kloc, nsub, full, first, last,
                      qt_ref, k_ref, vt_ref, lam_ref, subln_ref, o_ref, m_sc, acc_sc,
                      *, lambda_init):
    p = pl.program_id(0)

    @pl.when(first[p] == 1)
    def _():
        m_sc[...] = jnp.full(m_sc.shape, -jnp.inf, F32)
        acc_sc[...] = jnp.zeros(acc_sc.shape, F32)

    def key_tile(sub, carry):
        tile = kloc[p] * K_SUBS + sub
        on_diagonal = tile * K_TILE + (K_TILE - 1) > qloc[p] * Q_TILE

        @pl.when(on_diagonal)
        def _():
            kpos = tile * K_TILE + lax.broadcasted_iota(jnp.int32, (K_TILE, 2 * Q_TILE), 0)
            qcol = lax.broadcasted_iota(jnp.int32, (K_TILE, 2 * Q_TILE), 1)
            qpos = qloc[p] * Q_TILE + jnp.where(qcol >= Q_TILE, qcol - Q_TILE, qcol)
            bias = jnp.where(kpos <= qpos, 0.0, -jnp.inf).astype(F32)
            _attention_sweep(qt_ref, k_ref, vt_ref, (sub,), lambda _: bias, m_sc, acc_sc)

        @pl.when(jnp.logical_not(on_diagonal))
        def _():
            _attention_sweep(qt_ref, k_ref, vt_ref, (sub,), lambda _: None, m_sc, acc_sc)

        return carry

    @pl.when(full[p] == 1)
    def _():
        _attention_sweep(qt_ref, k_ref, vt_ref, tuple(range(K_SUBS)), lambda _: None, m_sc, acc_sc)

    @pl.when(full[p] == 0)
    def _():
        lax.fori_loop(0, nsub[p], key_tile, 0)

    @pl.when(last[p] == 1)
    def _():
        lam_rows = lam_ref[...]
        lam = (jnp.exp(jnp.sum(lam_rows[0:1] * lam_rows[1:2], axis=-1, keepdims=True))
               - jnp.exp(jnp.sum(lam_rows[2:3] * lam_rows[3:4], axis=-1, keepdims=True))
               + lambda_init)
        vdim = 2 * HEAD_DIM
        for j in range(N_PAIRS):
            a = acc_sc[j]
            a1 = a[:, :Q_TILE]
            a2 = a[:, Q_TILE:]
            o = a1[:vdim] / a1[vdim:vdim + 1] - lam * (a2[:vdim] / a2[vdim:vdim + 1])
            ms = jnp.mean(o * o, axis=0, keepdims=True)
            o = o * lax.rsqrt(ms + RMS_EPS) * subln_ref[...] * (1.0 - lambda_init)
            o_ref[j] = o.T.astype(BF16)


def _diff_attention(q_t, qkv, v_t, lam_rows, subln_col, steps, lambda_init):
    rows = qkv.shape[1]
    n_steps = steps["qrow"].shape[0]
    vrows = v_t.shape[2]

    def im(fn):
        return lambda p, qr, kr, ql, kl, ns, fu, f, l: fn(p, qr, kr)

    grid_spec = pltpu.PrefetchScalarGridSpec(
        num_scalar_prefetch=8,
        grid=(n_steps,),
        in_specs=[
            pl.BlockSpec((N_PAIRS, LANES, Q_TILE), im(lambda p, qr, kr: (0, 0, qr[p]))),
            pl.BlockSpec((N_PAIRS, K_SUBS * K_TILE, LANES), im(lambda p, qr, kr: (0, kr[p], 0))),
            pl.BlockSpec((N_PAIRS, K_SUBS, vrows, K_TILE), im(lambda p, qr, kr: (0, kr[p], 0, 0))),
            pl.BlockSpec((8, LANES), im(lambda p, qr, kr: (0, 0))),
            pl.BlockSpec((LANES, 1), im(lambda p, qr, kr: (0, 0))),
        ],
        out_specs=pl.BlockSpec((N_PAIRS, Q_TILE, LANES), im(lambda p, qr, kr: (0, qr[p], 0))),
        scratch_shapes=[pltpu.VMEM((N_PAIRS, 1, 2 * Q_TILE), F32),
                        pltpu.VMEM((N_PAIRS, vrows, 2 * Q_TILE), F32)],
    )
    return pl.pallas_call(
        functools.partial(_diff_attn_kernel, lambda_init=lambda_init),
        grid_spec=grid_spec,
        out_shape=jax.ShapeDtypeStruct((N_PAIRS, rows, LANES), BF16),
        compiler_params=pltpu.CompilerParams(dimension_semantics=("arbitrary",),
                                             vmem_limit_bytes=VMEM_LIMIT),
        name="diff_attn",
    )(steps["qrow"], steps["krow"], steps["qloc"], steps["kloc"], steps["nsub"], steps["full"],
      steps["first"], steps["last"], q_t, qkv, v_t, lam_rows, subln_col)


def _dsa_kernel(qrow, krow, qloc, kloc, nkb, nsub, full, first, last, bidx,
                qt_ref, k_ref, vt_ref, qit_ref, wit_ref, kidx_ref, o_ref,
                key_sc, coarse_sc, thr_sc, m_sc, acc_sc, *, topk):
    p = pl.program_id(0)

    @pl.when(first[p] == 1)
    def _():
        m_sc[...] = jnp.full(m_sc.shape, -jnp.inf, F32)
        acc_sc[...] = jnp.zeros(acc_sc.shape, F32)
        n = nkb[p]
        wi = wit_ref[...]
        qpos = qloc[p] * Q_TILE + lax.broadcasted_iota(jnp.int32, (K_TILE, Q_TILE), 1)
        krow_iota = lax.broadcasted_iota(jnp.int32, (K_TILE, Q_TILE), 0)

        def score_chunk(c, carry):
            kt = kidx_ref[pl.ds(pl.multiple_of(c * K_TILE, K_TILE), K_TILE), :]
            sc = jnp.zeros((K_TILE, Q_TILE), F32)
            for jp in range(IDX_PAIRS):
                qi_t = qit_ref[jp]
                for cc in range(2):
                    raw = jnp.dot(kt, _half_select(qi_t, cc == 0), preferred_element_type=F32)
                    hh = 2 * jp + cc
                    sc = sc + jnp.maximum(raw, 0.0) * wi[hh:hh + 1, :]
            kpos = c * K_TILE + krow_iota
            sc = jnp.where(kpos < N_META, BIG_SCORE, sc)
            sc = jnp.where(kpos <= qpos, sc, -jnp.inf)
            sc = jnp.where(sc == 0.0, 0.0, sc)
            bits = lax.bitcast_convert_type(sc, jnp.int32)
            key_sc[c] = bits ^ ((bits >> 31) & 0x7FFFFFFF)
            coarse_sc[c] = lax.bitcast_convert_type(bits & jnp.int32(-65536), F32).astype(BF16)
            return carry

        lax.fori_loop(0, n, score_chunk, 0)

        kf = float(topk)
        q1 = qloc[p] * Q_TILE + lax.broadcasted_iota(jnp.int32, (1, Q_TILE), 1)
        settled0 = (q1 < topk).astype(jnp.int32)
        coarse_bits = 16

        def coarse_body(b, state):
            lo, settled = state
            cand = lo + jnp.left_shift(jnp.int32(1), 31 - b)
            top = cand >> 16
            cbits = jnp.left_shift(top ^ ((top >> 31) & 0x7FFF), 16)
            cand16 = lax.bitcast_convert_type(cbits, F32).astype(BF16)

            def count_chunk(c, cnt16):
                hit = jnp.where(coarse_sc[c] >= cand16, jnp.ones((), BF16), jnp.zeros((), BF16))
                parts = [hit[i * BF16_ROWS:(i + 1) * BF16_ROWS] for i in range(K_TILE // BF16_ROWS)]
                while len(parts) > 1:
                    parts = [parts[i] + parts[i + 1] for i in range(0, len(parts), 2)]
                return cnt16 + parts[0].astype(F32)

            cnt16 = lax.fori_loop(0, n, count_chunk, jnp.zeros((BF16_ROWS, Q_TILE), F32))
            cnt = jnp.sum(cnt16, axis=0, keepdims=True)
            return jnp.where(cnt >= kf, cand, lo), jnp.where(cnt == kf, 1, settled)

        lo, settled = lax.fori_loop(0, coarse_bits, coarse_body,
                                    (jnp.full((1, Q_TILE), INT_MIN, jnp.int32), settled0))

        def unsettled(state):
            b, _, settled = state
            return jnp.logical_and(b < 32, jnp.min(settled) == 0)

        def fine_body(state):
            b, lo, settled = state
            cand = lo + jnp.left_shift(jnp.int32(1), 31 - b)

            def count_chunk(c, cnt8):
                hit = jnp.where(key_sc[c] >= cand, 1.0, 0.0).reshape(8, K_TILE // 64, 8, Q_TILE)
                return cnt8 + jnp.sum(jnp.sum(hit, axis=1), axis=0)

            cnt8 = lax.fori_loop(0, n, count_chunk, jnp.zeros((8, Q_TILE), F32))
            cnt = jnp.sum(cnt8, axis=0, keepdims=True)
            return b + 1, jnp.where(cnt >= kf, cand, lo), jnp.where(cnt == kf, 1, settled)

        _, thr, _ = lax.while_loop(unsettled, fine_body, (jnp.int32(coarse_bits), lo, settled))
        thr_sc[...] = jnp.maximum(thr, KEY_NEG_INF + 1)

    def mask_bias(sub):
        sel = key_sc[kloc[p] * K_SUBS + sub] >= thr_sc[...]
        bias1 = jnp.where(sel, 0.0, -jnp.inf).astype(F32)
        return jnp.concatenate([bias1, bias1], axis=1)

    def key_tile(sub, carry):
        _attention_sweep(qt_ref, k_ref, vt_ref, (sub,), mask_bias, m_sc, acc_sc)
        return carry

    @pl.when(full[p] == 1)
    def _():
        _attention_sweep(qt_ref, k_ref, vt_ref, tuple(range(K_SUBS)), mask_bias, m_sc, acc_sc)

    @pl.when(full[p] == 0)
    def _():
        lax.fori_loop(0, nsub[p], key_tile, 0)

    @pl.when(last[p] == 1)
    def _():
        ones_row = 2 * HEAD_DIM
        for j in range(N_PAIRS):
            a = acc_sc[j]
            o = jnp.concatenate([a[:HEAD_DIM, :Q_TILE] / a[ones_row:ones_row + 1, :Q_TILE],
                                 a[HEAD_DIM:ones_row, Q_TILE:] / a[ones_row:ones_row + 1, Q_TILE:]],
                                axis=0)
            o_ref[j] = o.T.astype(BF16)


def _dsa_attention(q_t, qkvi, v_t, qi_t, wi_t, kidx, steps, topk, t_pad):
    rows = qkvi.shape[1]
    n_steps = steps["qrow"].shape[0]
    nkb_total = t_pad // K_TILE
    vrows = v_t.shape[2]

    def im(fn):
        return lambda p, qr, kr, ql, kl, nk, ns, fu, f, l, b: fn(p, qr, kr, b)

    grid_spec = pltpu.PrefetchScalarGridSpec(
        num_scalar_prefetch=10,
        grid=(n_steps,),
        in_specs=[
            pl.BlockSpec((N_PAIRS, LANES, Q_TILE), im(lambda p, qr, kr, b: (0, 0, qr[p]))),
            pl.BlockSpec((N_PAIRS, K_SUBS * K_TILE, LANES), im(lambda p, qr, kr, b: (0, kr[p], 0))),
            pl.BlockSpec((N_PAIRS, K_SUBS, vrows, K_TILE), im(lambda p, qr, kr, b: (0, kr[p], 0, 0))),
            pl.BlockSpec((IDX_PAIRS, LANES, Q_TILE), im(lambda p, qr, kr, b: (0, 0, qr[p]))),
            pl.BlockSpec((IDX_HEADS, Q_TILE), im(lambda p, qr, kr, b: (0, qr[p]))),
            pl.BlockSpec((t_pad, LANES), im(lambda p, qr, kr, b: (b[p], 0))),
        ],
        out_specs=pl.BlockSpec((N_PAIRS, Q_TILE, LANES), im(lambda p, qr, kr, b: (0, qr[p], 0))),
        scratch_shapes=[pltpu.VMEM((nkb_total, K_TILE, Q_TILE), jnp.int32),
                        pltpu.VMEM((nkb_total, K_TILE, Q_TILE), BF16),
                        pltpu.VMEM((1, Q_TILE), jnp.int32),
                        pltpu.VMEM((N_PAIRS, 1, 2 * Q_TILE), F32),
                        pltpu.VMEM((N_PAIRS, vrows, 2 * Q_TILE), F32)],
    )
    return pl.pallas_call(
        functools.partial(_dsa_kernel, topk=topk),
        grid_spec=grid_spec,
        out_shape=jax.ShapeDtypeStruct((N_PAIRS, rows, LANES), BF16),
        compiler_params=pltpu.CompilerParams(dimension_semantics=("arbitrary",),
                                             vmem_limit_bytes=VMEM_LIMIT),
        name="dsa_attn",
    )(steps["qrow"], steps["krow"], steps["qloc"], steps["kloc"], steps["nkb"], steps["nsub"],
      steps["full"], steps["first"], steps["last"], steps["b"], q_t, qkvi, v_t, qi_t, wi_t, kidx)


def _rope_tables(t_pad):
    inv = ROPE_THETA ** (-jnp.arange(0, HEAD_DIM, 2, dtype=F32) / HEAD_DIM)
    ang = jnp.arange(t_pad, dtype=F32)[:, None] * inv[None, :]
    cos, sin = jnp.cos(ang), jnp.sin(ang)
    cos128 = jnp.tile(cos, (1, LANES // (HEAD_DIM // 2)))
    sin128 = jnp.tile(jnp.concatenate([-sin, sin], axis=1), (1, LANES // HEAD_DIM))
    return cos128, sin128


def _interleave_gate_val(a):
    lead = a.shape[:-1]
    a = a.reshape(lead + (2, FFN_HIDDEN // FFN_TILE, FFN_TILE))
    a = jnp.swapaxes(a, -3, -2)
    return a.reshape(lead + (2 * FFN_HIDDEN,))


def _with_ones_rows(v_t):
    ones = jnp.ones((v_t.shape[0], BF16_ROWS, v_t.shape[2]), v_t.dtype)
    return jnp.concatenate([v_t, ones], axis=1)


def _key_tiled(v_t):
    pairs, d, rows = v_t.shape
    return jnp.swapaxes(v_t.reshape(pairs, d, rows // K_TILE, K_TILE), 1, 2)


def kernel(x, meta_tokens, da_norm, da_w_qkv, da_lambda_q1, da_lambda_k1, da_lambda_q2, da_lambda_k2, da_subln, da_w_o, dsa_norm, dsa_w_in, dsa_idx_k_norm, dsa_w_o, ffn_norm, ffn_w_up, ffn_conv_w, ffn_conv_b, ffn_w_down, final_norm):
    batch, seq, d = x.shape
    assert d == D_MODEL
    depth = ffn_norm.shape[0]
    t_real = seq + N_META
    t_pad = _padded_len(t_real)
    assert t_pad % (K_SUBS * K_TILE) == 0 and t_pad % Q_TILE == 0
    rows = batch * t_pad

    meta = jnp.broadcast_to(meta_tokens[None].astype(x.dtype), (batch, N_META, d))
    h = jnp.concatenate([meta, x, jnp.zeros((batch, t_pad - t_real, d), x.dtype)], axis=1)
    h = h.reshape(rows, d)
    cos128, sin128 = _rope_tables(t_pad)
    steps = _attention_steps(batch, t_pad)

    dsa_qkv = N_HEADS * HEAD_DIM
    idx_cols = IDX_HEADS * HEAD_DIM
    for i in range(depth):
        j = i // 2
        if i % 2 == 0:
            lambda_init = 0.8 - 0.6 * math.exp(-0.3 * i)
            q_t, k, v_t = _project(h, da_norm[j], da_w_qkv[j].astype(BF16), cos128, sin128, t_pad)
            lam_rows = jnp.zeros((8, LANES), F32).at[0:4, 0:HEAD_DIM].set(
                jnp.stack([da_lambda_q1[j], da_lambda_k1[j], da_lambda_q2[j], da_lambda_k2[j]]).astype(F32))
            o = _diff_attention(q_t, k, v_t, lam_rows, da_subln[j].reshape(LANES, 1).astype(F32),
                                steps, lambda_init)
            h = _out_project(h, o, da_w_o[j].astype(BF16))
        else:
            w_in = dsa_w_in[j]
            n_main = 3 * dsa_qkv + idx_cols
            w_tail = jnp.zeros((d, LANES), w_in.dtype).at[:, :w_in.shape[1] - n_main].set(w_in[:, n_main:])
            gk = jnp.ones((1, LANES), F32).at[0, :HEAD_DIM].set(dsa_idx_k_norm[j].astype(F32))
            q_t, k, v_t, qi_t, kidx, wi_t = _project(
                h, dsa_norm[j], w_in[:, :n_main].astype(BF16), cos128, sin128, t_pad,
                w_tail=w_tail.astype(BF16), gk=gk)
            topk = min(TOPK_MAX, seq // 4)
            o = _dsa_attention(q_t, k, v_t, qi_t, wi_t, kidx, steps, topk, t_pad)
            h = _out_project(h, o, dsa_w_o[j].astype(BF16))
        h = _ffn(h, ffn_norm[i], _interleave_gate_val(ffn_w_up[i]).astype(BF16),
                 _interleave_gate_val(ffn_conv_w[i]).astype(F32),
                 _interleave_gate_val(ffn_conv_b[i]).reshape(1, -1).astype(F32),
                 ffn_w_down[i].astype(BF16), final_norm, t_pad, final=(i == depth - 1))
    return h.reshape(batch, t_pad, d)[:, N_META:N_META + seq]
```

```python
import functools
import math

import numpy as np
import jax
import jax.numpy as jnp
from jax import lax
from jax.experimental import pallas as pl
from jax.experimental.pallas import tpu as pltpu

D_MODEL = 1024
N_META = 16
ROPE_THETA = 10000.0
RMS_EPS = 1e-6
HEAD_DIM = 64
N_PAIRS = 8
N_HEADS = 2 * N_PAIRS
IDX_HEADS = 8
IDX_PAIRS = IDX_HEADS // 2
TOPK_MAX = 256
BIG_SCORE = 1e30
LOG2_E = 1.4426950408889634
FFN_HIDDEN = 2816
CONV_WIDTH = 3

LANES = 128
BF16_ROWS = 16
ROW_TILE = 512
Q_TILE = 256
K_TILE = 512
K_SUBS = 3
FFN_TILE = 256
HALO = BF16_ROWS
VMEM_LIMIT = 56 * 1024 * 1024

F32 = jnp.float32
BF16 = jnp.bfloat16
INT_MIN = -2147483648
KEY_NEG_INF = INT_MIN + 0x7FFFFF
BF16_MIN_NORMAL_BITS = 0x0080
F32_MIN_NORMAL = 1.1754943508222875e-38


def _padded_len(t):
    unit = math.lcm(ROW_TILE, K_SUBS * K_TILE, Q_TILE)
    return ((t + unit - 1) // unit) * unit


def _rms(x, gain):
    ms = jnp.mean(x * x, axis=-1, keepdims=True)
    return x * lax.rsqrt(ms + RMS_EPS) * gain


def _rope_lanes(y, cos, sin, lo):
    sw = jnp.where(lo, pltpu.roll(y, LANES - HEAD_DIM // 2, 1), pltpu.roll(y, HEAD_DIM // 2, 1))
    return y * cos + sw * sin


def _proj_kernel(x_ref, g_ref, w_ref, cos_ref, sin_ref, *rest, has_idx):
    if has_idx:
        wt_ref, gk_ref, qt_ref, k_ref, vt_ref, qit_ref, kidx_ref, wit_ref = rest
    else:
        qt_ref, k_ref, vt_ref = rest
    xh = _rms(x_ref[...], g_ref[...]).astype(BF16)
    cos = cos_ref[...]
    sin = sin_ref[...]
    lane = lax.broadcasted_iota(jnp.int32, cos.shape, 1)
    lo = (lane & (HEAD_DIM // 2)) == 0
    n_groups = (3 * N_PAIRS + IDX_PAIRS) if has_idx else 3 * N_PAIRS
    cw = 4 * LANES
    ones = jnp.ones((BF16_ROWS, ROW_TILE), BF16)
    for c in range(n_groups // 4):
        y = jnp.dot(xh, w_ref[:, c * cw:(c + 1) * cw], preferred_element_type=F32)
        for s in range(4):
            g = c * 4 + s
            yg = y[:, s * LANES:(s + 1) * LANES]
            if g < N_PAIRS:
                yg = _rope_lanes(yg, cos, sin, lo) * (HEAD_DIM ** -0.5 * LOG2_E)
                qt_ref[g] = yg.T.astype(BF16)
            elif g < 2 * N_PAIRS:
                k_ref[g - N_PAIRS] = _rope_lanes(yg, cos, sin, lo).astype(BF16)
            elif g < 3 * N_PAIRS:
                j = g - 2 * N_PAIRS
                vt_ref[j, 0, 0:2 * HEAD_DIM, :] = yg.T.astype(BF16)
                vt_ref[j, 0, 2 * HEAD_DIM:, :] = ones
            else:
                yg = _rope_lanes(yg, cos, sin, lo) * (HEAD_DIM ** -0.5)
                qit_ref[g - 3 * N_PAIRS] = yg.T.astype(BF16)
    if has_idx:
        t = jnp.dot(xh, wt_ref[...], preferred_element_type=F32)
        is_k = lane < HEAD_DIM
        ms = jnp.sum(jnp.where(is_k, t * t, 0.0), axis=-1, keepdims=True) * (1.0 / HEAD_DIM)
        kn = t * lax.rsqrt(ms + RMS_EPS) * gk_ref[...]
        kn = _rope_lanes(kn, cos, sin, lo)
        kidx_ref[...] = jnp.where(is_k, kn, pltpu.roll(kn, HEAD_DIM, 1)).astype(BF16)
        wit_ref[...] = (t * (IDX_HEADS ** -0.5)).T[HEAD_DIM:HEAD_DIM + IDX_HEADS, :]


def _project(h, gain, w, cos, sin, t_pad, w_tail=None, gk=None):
    assert ROW_TILE == K_TILE
    rows = h.shape[0]
    blocks_per_batch = t_pad // ROW_TILE
    has_idx = w_tail is not None
    vrows = 2 * HEAD_DIM + BF16_ROWS
    in_specs = [
        pl.BlockSpec((ROW_TILE, D_MODEL), lambda i: (i, 0)),
        pl.BlockSpec((1, D_MODEL), lambda i: (0, 0)),
        pl.BlockSpec(w.shape, lambda i: (0, 0)),
        pl.BlockSpec((ROW_TILE, LANES), lambda i: (i % blocks_per_batch, 0)),
        pl.BlockSpec((ROW_TILE, LANES), lambda i: (i % blocks_per_batch, 0)),
    ]
    out_shape = [jax.ShapeDtypeStruct((N_PAIRS, LANES, rows), BF16),
                 jax.ShapeDtypeStruct((N_PAIRS, rows, LANES), BF16),
                 jax.ShapeDtypeStruct((N_PAIRS, rows // K_TILE, vrows, K_TILE), BF16)]
    out_specs = [pl.BlockSpec((N_PAIRS, LANES, ROW_TILE), lambda i: (0, 0, i)),
                 pl.BlockSpec((N_PAIRS, ROW_TILE, LANES), lambda i: (0, i, 0)),
                 pl.BlockSpec((N_PAIRS, 1, vrows, K_TILE), lambda i: (0, i, 0, 0))]
    args = [h, gain.reshape(1, D_MODEL), w, cos, sin]
    if has_idx:
        in_specs += [pl.BlockSpec(w_tail.shape, lambda i: (0, 0)),
                     pl.BlockSpec((1, LANES), lambda i: (0, 0))]
        out_shape += [jax.ShapeDtypeStruct((IDX_PAIRS, LANES, rows), BF16),
                      jax.ShapeDtypeStruct((rows, LANES), BF16),
                      jax.ShapeDtypeStruct((IDX_HEADS, rows), F32)]
        out_specs += [pl.BlockSpec((IDX_PAIRS, LANES, ROW_TILE), lambda i: (0, 0, i)),
                      pl.BlockSpec((ROW_TILE, LANES), lambda i: (i, 0)),
                      pl.BlockSpec((IDX_HEADS, ROW_TILE), lambda i: (0, i))]
        args += [w_tail, gk]
    return pl.pallas_call(
        functools.partial(_proj_kernel, has_idx=has_idx),
        grid=(rows // ROW_TILE,),
        in_specs=in_specs,
        out_specs=out_specs,
        out_shape=out_shape,
        compiler_params=pltpu.CompilerParams(dimension_semantics=("arbitrary",),
                                             vmem_limit_bytes=VMEM_LIMIT),
        name="proj_idx" if has_idx else "proj",
    )(*args)


def _oproj_kernel(h_ref, o_ref, w_ref, out_ref):
    o = jnp.concatenate([o_ref[j] for j in range(N_PAIRS)], axis=1)
    out_ref[...] = h_ref[...] + jnp.dot(o, w_ref[...], preferred_element_type=F32)


def _out_project(h, o, w):
    rows = h.shape[0]
    return pl.pallas_call(
        _oproj_kernel,
        grid=(rows // ROW_TILE,),
        in_specs=[pl.BlockSpec((ROW_TILE, D_MODEL), lambda i: (i, 0)),
                  pl.BlockSpec((N_PAIRS, ROW_TILE, LANES), lambda i: (0, i, 0)),
                  pl.BlockSpec(w.shape, lambda i: (0, 0))],
        out_specs=pl.BlockSpec((ROW_TILE, D_MODEL), lambda i: (i, 0)),
        out_shape=jax.ShapeDtypeStruct(h.shape, F32),
        compiler_params=pltpu.CompilerParams(dimension_semantics=("arbitrary",),
                                             vmem_limit_bytes=VMEM_LIMIT),
        name="oproj",
    )(h, o, w)


def _ffn_kernel(xp_ref, x_ref, g_ref, wup_ref, cw_ref, cb_ref, wdn_ref, fg_ref, o_ref, u_sc, a_sc,
                *, blocks_per_batch, final):
    i = pl.program_id(0)
    x = x_ref[...]
    xe = jnp.concatenate([xp_ref[...], x], axis=0)
    xh = _rms(xe, g_ref[...])
    row = lax.broadcasted_iota(jnp.int32, (HALO + ROW_TILE, 1), 0)
    keep = jnp.logical_or(row >= HALO, i % blocks_per_batch != 0)
    xh = jnp.where(keep, xh, 0.0).astype(BF16)
    cw2 = 2 * FFN_TILE
    n_chunks = FFN_HIDDEN // FFN_TILE

    def up(c):
        return jnp.dot(xh, wup_ref[:, c * cw2:(c + 1) * cw2], preferred_element_type=F32)

    u_sc[0] = up(0)
    for c in range(n_chunks):
        cur = c % 2
        after = None
        if c + 1 < n_chunks:
            u_sc[1 - cur] = up(c + 1)
            after = _zero_after(u_sc[1 - cur, 0:8, 0:FFN_TILE])
        cwc = cw_ref[:, c * cw2:(c + 1) * cw2]
        conv = cb_ref[:, c * cw2:(c + 1) * cw2]
        for j in range(CONV_WIDTH):
            conv = conv + cwc[j:j + 1, :] * u_sc[cur, pl.ds(HALO - (CONV_WIDTH - 1) + j, ROW_TILE), :]
        gate = conv[:, :FFN_TILE]
        val = conv[:, FFN_TILE:]
        a = gate * jax.nn.sigmoid(gate) * val
        if after is not None:
            cut = (3 * ROW_TILE) // 4
            a = jnp.concatenate([a[:cut], a[cut:] + after], axis=0)
        a_sc[:, c * FFN_TILE:(c + 1) * FFN_TILE] = a.astype(BF16)
    acc = jnp.dot(a_sc[...], wdn_ref[...], preferred_element_type=F32)
    y = x + acc
    if final:
        y = _rms(y, fg_ref[...])
    o_ref[...] = y


def _ffn(h, gain, w_up, conv_w, conv_b, w_down, final_gain, t_pad, final):
    rows = h.shape[0]
    blocks_per_batch = t_pad // ROW_TILE
    halo_blocks = ROW_TILE // HALO
    return pl.pallas_call(
        functools.partial(_ffn_kernel, blocks_per_batch=blocks_per_batch, final=final),
        grid=(rows // ROW_TILE,),
        in_specs=[pl.BlockSpec((HALO, D_MODEL), lambda i: (jnp.maximum(i * halo_blocks - 1, 0), 0)),
                  pl.BlockSpec((ROW_TILE, D_MODEL), lambda i: (i, 0)),
                  pl.BlockSpec((1, D_MODEL), lambda i: (0, 0)),
                  pl.BlockSpec(w_up.shape, lambda i: (0, 0)),
                  pl.BlockSpec(conv_w.shape, lambda i: (0, 0)),
                  pl.BlockSpec(conv_b.shape, lambda i: (0, 0)),
                  pl.BlockSpec(w_down.shape, lambda i: (0, 0)),
                  pl.BlockSpec((1, D_MODEL), lambda i: (0, 0))],
        out_specs=pl.BlockSpec((ROW_TILE, D_MODEL), lambda i: (i, 0)),
        out_shape=jax.ShapeDtypeStruct(h.shape, F32),
        scratch_shapes=[pltpu.VMEM((2, HALO + ROW_TILE, 2 * FFN_TILE), F32),
                        pltpu.VMEM((ROW_TILE, FFN_HIDDEN), BF16)],
        compiler_params=pltpu.CompilerParams(dimension_semantics=("arbitrary",),
                                             vmem_limit_bytes=VMEM_LIMIT),
        name="ffn_final" if final else "ffn",
    )(h, h, gain.reshape(1, D_MODEL), w_up, conv_w, conv_b, w_down, final_gain.reshape(1, D_MODEL))


def _attention_steps(batch, t_pad):
    nq = t_pad // Q_TILE
    nk = t_pad // (K_SUBS * K_TILE)
    cols = {k: [] for k in ("qrow", "krow", "qloc", "kloc", "nkb", "nsub", "full", "first", "last", "b")}
    for b in range(batch):
        for qi in range(nq):
            n_tiles = ((qi + 1) * Q_TILE + K_TILE - 1) // K_TILE
            n = (n_tiles + K_SUBS - 1) // K_SUBS
            for ki in range(n):
                cols["qrow"].append(b * nq + qi)
                cols["krow"].append(b * nk + ki)
                cols["qloc"].append(qi)
                cols["kloc"].append(ki)
                cols["nkb"].append(n_tiles)
                cols["nsub"].append(min(K_SUBS, n_tiles - ki * K_SUBS))
                cols["full"].append(int((ki + 1) * K_SUBS < n_tiles))
                cols["first"].append(int(ki == 0))
                cols["last"].append(int(ki == n - 1))
                cols["b"].append(b)
    return {k: jnp.asarray(np.asarray(v, np.int32)) for k, v in cols.items()}


def _half_select(x_t, first_half):
    row = lax.broadcasted_iota(jnp.int32, x_t.shape, 0)
    keep = (row < HEAD_DIM) if first_half else (row >= HEAD_DIM)
    return jnp.where(keep, x_t, jnp.zeros_like(x_t))


def _zero_after(x):
    w = lax.bitcast_convert_type(x, jnp.uint32)
    w = lax.shift_right_logical(lax.shift_right_logical(w, jnp.uint32(16)), jnp.uint32(16))
    return lax.bitcast_convert_type(w, F32)[0:1, :]


def _flash_update(s_t, v_t, h, m_sc, acc_sc, after=None):
    m_prev = m_sc[h]
    m_new = jnp.maximum(m_prev, jnp.max(s_t, axis=0, keepdims=True))
    alpha = jnp.exp2(m_prev - m_new)
    if after is None:
        p_t = jnp.exp2(s_t - m_new).astype(BF16)
    else:
        cut = s_t.shape[0] // 2
        p_t = jnp.concatenate([jnp.exp2(s_t[:cut] - m_new),
                               jnp.exp2(s_t[cut:] - (m_new + after))], axis=0).astype(BF16)
    acc_sc[h] = alpha * acc_sc[h] + jnp.dot(v_t, p_t, preferred_element_type=F32)
    m_sc[h] = m_new


def _both_halves(x_t):
    return jnp.concatenate([_half_select(x_t, True), _half_select(x_t, False)], axis=1)


def _attention_sweep(qt_ref, k_ref, vt_ref, subs, bias_of, m_sc, acc_sc):
    def scores(sub, j):
        start = sub * K_TILE
        if not isinstance(sub, int):
            start = pl.multiple_of(start, K_TILE)
        return jnp.dot(k_ref[j, pl.ds(start, K_TILE), :], _both_halves(qt_ref[j]), preferred_element_type=F32)

    items = [(sub, j) for sub in subs for j in range(N_PAIRS)]
    biases = {}
    ahead = 2
    pending = [scores(*item) for item in items[:ahead]]
    for i, (sub, j) in enumerate(items):
        s_t = pending.pop(0)
        tile_key = sub if isinstance(sub, int) else "traced"
        if tile_key not in biases:
            biases[tile_key] = bias_of(sub)
        if biases[tile_key] is not None:
            s_t = s_t + biases[tile_key]
        after = None
        if i + ahead < len(items):
            pending.append(scores(*items[i + ahead]))
            after = _zero_after(pending[-1][0:8, :])
        _flash_update(s_t, vt_ref[j, sub], j, m_sc, acc_sc, after=after)


def _diff_attn_kernel(qrow, krow, qloc, kloc, nsub, full, first, last,
                      qt_ref, k_ref, vt_ref, lam_ref, subln_ref, o_ref, m_sc, acc_sc,
                      *, lambda_init):
    p = pl.program_id(0)

    @pl.when(first[p] == 1)
    def _():
        m_sc[...] = jnp.full(m_sc.shape, -jnp.inf, F32)
        acc_sc[...] = jnp.zeros(acc_sc.shape, F32)

    def key_tile(sub, carry):
        tile = kloc[p] * K_SUBS + sub
        on_diagonal = tile * K_TILE + (K_TILE - 1) > qloc[p] * Q_TILE

        @pl.when(on_diagonal)
        def _():
            kpos = tile * K_TILE + lax.broadcasted_iota(jnp.int32, (K_TILE, 2 * Q_TILE), 0)
            qcol = lax.broadcasted_iota(jnp.int32, (K_TILE, 2 * Q_TILE), 1)
            qpos = qloc[p] * Q_TILE + jnp.where(qcol >= Q_TILE, qcol - Q_TILE, qcol)
            bias = jnp.where(kpos <= qpos, 0.0, -jnp.inf).astype(F32)
            _attention_sweep(qt_ref, k_ref, vt_ref, (sub,), lambda _: bias, m_sc, acc_sc)

        @pl.when(jnp.logical_not(on_diagonal))
        def _():
            _attention_sweep(qt_ref, k_ref, vt_ref, (sub,), lambda _: None, m_sc, acc_sc)

        return carry

    @pl.when(full[p] == 1)
    def _():
        _attention_sweep(qt_ref, k_ref, vt_ref, tuple(range(K_SUBS)), lambda _: None, m_sc, acc_sc)

    @pl.when(full[p] == 0)
    def _():
        lax.fori_loop(0, nsub[p], key_tile, 0)

    @pl.when(last[p] == 1)
    def _():
        lam_rows = lam_ref[...]
        lam = (jnp.exp(jnp.sum(lam_rows[0:1] * lam_rows[1:2], axis=-1, keepdims=True))
               - jnp.exp(jnp.sum(lam_rows[2:3] * lam_rows[3:4], axis=-1, keepdims=True))
               + lambda_init)
        vdim = 2 * HEAD_DIM
        for j in range(N_PAIRS):
            a = acc_sc[j]
            a1 = a[:, :Q_TILE]
            a2 = a[:, Q_TILE:]
            o = a1[:vdim] / a1[vdim:vdim + 1] - lam * (a2[:vdim] / a2[vdim:vdim + 1])
            ms = jnp.mean(o * o, axis=0, keepdims=True)
            o = o * lax.rsqrt(ms + RMS_EPS) * subln_ref[...] * (1.0 - lambda_init)
            o_ref[j] = o.T.astype(BF16)


def _diff_attention(q_t, qkv, v_t, lam_rows, subln_col, steps, lambda_init):
    rows = qkv.shape[1]
    n_steps = steps["qrow"].shape[0]
    vrows = v_t.shape[2]

    def im(fn):
        return lambda p, qr, kr, ql, kl, ns, fu, f, l: fn(p, qr, kr)

    grid_spec = pltpu.PrefetchScalarGridSpec(
        num_scalar_prefetch=8,
        grid=(n_steps,),
        in_specs=[
            pl.BlockSpec((N_PAIRS, LANES, Q_TILE), im(lambda p, qr, kr: (0, 0, qr[p]))),
            pl.BlockSpec((N_PAIRS, K_SUBS * K_TILE, LANES), im(lambda p, qr, kr: (0, kr[p], 0))),
            pl.BlockSpec((N_PAIRS, K_SUBS, vrows, K_TILE), im(lambda p, qr, kr: (0, kr[p], 0, 0))),
            pl.BlockSpec((8, LANES), im(lambda p, qr, kr: (0, 0))),
            pl.BlockSpec((LANES, 1), im(lambda p, qr, kr: (0, 0))),
        ],
        out_specs=pl.BlockSpec((N_PAIRS, Q_TILE, LANES), im(lambda p, qr, kr: (0, qr[p], 0))),
        scratch_shapes=[pltpu.VMEM((N_PAIRS, 1, 2 * Q_TILE), F32),
                        pltpu.VMEM((N_PAIRS, vrows, 2 * Q_TILE), F32)],
    )
    return pl.pallas_call(
        functools.partial(_diff_attn_kernel, lambda_init=lambda_init),
        grid_spec=grid_spec,
        out_shape=jax.ShapeDtypeStruct((N_PAIRS, rows, LANES), BF16),
        compiler_params=pltpu.CompilerParams(dimension_semantics=("arbitrary",),
                                             vmem_limit_bytes=VMEM_LIMIT),
        name="diff_attn",
    )(steps["qrow"], steps["krow"], steps["qloc"], steps["kloc"], steps["nsub"], steps["full"],
      steps["first"], steps["last"], q_t, qkv, v_t, lam_rows, subln_col)


def _dsa_kernel(qrow, krow, qloc, kloc, nkb, nsub, full, first, last, bidx,
                qt_ref, k_ref, vt_ref, qit_ref, wit_ref, kidx_ref, o_ref,
                key_sc, coarse_sc, thr_sc, m_sc, acc_sc, *, topk, pos_bits):
    p = pl.program_id(0)

    @pl.when(first[p] == 1)
    def _():
        m_sc[...] = jnp.full(m_sc.shape, -jnp.inf, F32)
        acc_sc[...] = jnp.zeros(acc_sc.shape, F32)
        n = nkb[p]
        wi = wit_ref[...]
        qpos = qloc[p] * Q_TILE + lax.broadcasted_iota(jnp.int32, (K_TILE, Q_TILE), 1)
        krow_iota = lax.broadcasted_iota(jnp.int32, (K_TILE, Q_TILE), 0)

        def score_chunk(c, carry):
            kt = kidx_ref[pl.ds(pl.multiple_of(c * K_TILE, K_TILE), K_TILE), :]
            sc = jnp.zeros((K_TILE, Q_TILE), F32)
            for jp in range(IDX_PAIRS):
                qi_t = qit_ref[jp]
                for cc in range(2):
                    raw = jnp.dot(kt, _half_select(qi_t, cc == 0), preferred_element_type=F32)
                    hh = 2 * jp + cc
                    sc = sc + jnp.maximum(raw, 0.0) * wi[hh:hh + 1, :]
            kpos = c * K_TILE + krow_iota
            sc = jnp.where(kpos < N_META, BIG_SCORE, sc)
            sc = jnp.where(kpos <= qpos, sc, -jnp.inf)
            sc = jnp.where(jnp.abs(sc) < F32_MIN_NORMAL, 0.0, sc)
            bits = lax.bitcast_convert_type(sc, jnp.int32)
            key_sc[c] = bits ^ ((bits >> 31) & 0x7FFFFFFF)
            coarse_sc[c] = lax.bitcast_convert_type(bits & jnp.int32(-65536), F32).astype(BF16)
            return carry

        lax.fori_loop(0, n, score_chunk, 0)

        kf = float(topk)
        q1 = qloc[p] * Q_TILE + lax.broadcasted_iota(jnp.int32, (1, Q_TILE), 1)
        settled0 = (q1 < topk).astype(jnp.int32)
        coarse_bits = 16

        def coarse_body(b, state):
            lo, settled = state
            cand = lo + jnp.left_shift(jnp.int32(1), 31 - b)
            top = cand >> 16
            top = jnp.where(jnp.logical_and(top > 0, top < BF16_MIN_NORMAL_BITS), BF16_MIN_NORMAL_BITS, top)
            cbits = jnp.left_shift(top ^ ((top >> 31) & 0x7FFF), 16)
            cand16 = lax.bitcast_convert_type(cbits, F32).astype(BF16)

            def count_chunk(c, cnt16):
                hit = jnp.where(coarse_sc[c] >= cand16, jnp.ones((), BF16), jnp.zeros((), BF16))
                parts = [hit[i * BF16_ROWS:(i + 1) * BF16_ROWS] for i in range(K_TILE // BF16_ROWS)]
                while len(parts) > 1:
                    parts = [parts[i] + parts[i + 1] for i in range(0, len(parts), 2)]
                return cnt16 + parts[0].astype(F32)

            cnt16 = lax.fori_loop(0, n, count_chunk, jnp.zeros((BF16_ROWS, Q_TILE), F32))
            cnt = jnp.sum(cnt16, axis=0, keepdims=True)
            return jnp.where(cnt >= kf, cand, lo), jnp.where(cnt == kf, 1, settled)

        lo, settled = lax.fori_loop(0, coarse_bits, coarse_body,
                                    (jnp.full((1, Q_TILE), INT_MIN, jnp.int32), settled0))

        def unsettled(state):
            b, _, settled = state
            return jnp.logical_and(b < 32, jnp.min(settled) == 0)

        def count_where(hit_of):
            def count_chunk(c, cnt8):
                hit = jnp.where(hit_of(c, key_sc[c]), 1.0, 0.0).reshape(8, K_TILE // 64, 8, Q_TILE)
                return cnt8 + jnp.sum(jnp.sum(hit, axis=1), axis=0)

            cnt8 = lax.fori_loop(0, n, count_chunk, jnp.zeros((8, Q_TILE), F32))
            return jnp.sum(cnt8, axis=0, keepdims=True)

        def fine_body(state):
            b, lo, settled = state
            cand = lo + jnp.left_shift(jnp.int32(1), 31 - b)
            cnt = count_where(lambda c, keys: keys >= cand)
            return b + 1, jnp.where(cnt >= kf, cand, lo), jnp.where(cnt == kf, 1, settled)

        _, thr, settled = lax.while_loop(unsettled, fine_body, (jnp.int32(coarse_bits), lo, settled))
        thr = jnp.maximum(thr, KEY_NEG_INF + 1)
        thr_sc[...] = thr

        @pl.when(jnp.min(settled) == 0)
        def _():
            tied = settled == 0
            surplus = count_where(lambda c, keys: keys >= thr) - kf

            def is_tie_at_or_after(pos):
                def hit(c, keys):
                    kpos = c * K_TILE + krow_iota
                    return jnp.logical_and(keys == thr, kpos >= pos)
                return hit

            def pos_body(b, pos):
                cand = pos + jnp.left_shift(jnp.int32(1), pos_bits - 1 - b)
                cnt = count_where(is_tie_at_or_after(cand))
                return jnp.where(cnt >= surplus, cand, pos)

            cut = lax.fori_loop(0, pos_bits, pos_body, jnp.zeros((1, Q_TILE), jnp.int32))

            def demote(c, carry):
                keys = key_sc[c]
                drop = jnp.logical_and(is_tie_at_or_after(cut)(c, keys), tied)
                key_sc[c] = jnp.where(drop, keys - 1, keys)
                return carry

            lax.fori_loop(0, n, demote, 0)

    def mask_bias(sub):
        sel = key_sc[kloc[p] * K_SUBS + sub] >= thr_sc[...]
        bias1 = jnp.where(sel, 0.0, -jnp.inf).astype(F32)
        return jnp.concatenate([bias1, bias1], axis=1)

    def key_tile(sub, carry):
        _attention_sweep(qt_ref, k_ref, vt_ref, (sub,), mask_bias, m_sc, acc_sc)
        return carry

    @pl.when(full[p] == 1)
    def _():
        _attention_sweep(qt_ref, k_ref, vt_ref, tuple(range(K_SUBS)), mask_bias, m_sc, acc_sc)

    @pl.when(full[p] == 0)
    def _():
        lax.fori_loop(0, nsub[p], key_tile, 0)

    @pl.when(last[p] == 1)
    def _():
        ones_row = 2 * HEAD_DIM
        for j in range(N_PAIRS):
            a = acc_sc[j]
            o = jnp.concatenate([a[:HEAD_DIM, :Q_TILE] / a[ones_row:ones_row + 1, :Q_TILE],
                                 a[HEAD_DIM:ones_row, Q_TILE:] / a[ones_row:ones_row + 1, Q_TILE:]],
                                axis=0)
            o_ref[j] = o.T.astype(BF16)


def _dsa_attention(q_t, qkvi, v_t, qi_t, wi_t, kidx, steps, topk, t_pad):
    rows = qkvi.shape[1]
    n_steps = steps["qrow"].shape[0]
    nkb_total = t_pad // K_TILE
    vrows = v_t.shape[2]

    def im(fn):
        return lambda p, qr, kr, ql, kl, nk, ns, fu, f, l, b: fn(p, qr, kr, b)

    grid_spec = pltpu.PrefetchScalarGridSpec(
        num_scalar_prefetch=10,
        grid=(n_steps,),
        in_specs=[
            pl.BlockSpec((N_PAIRS, LANES, Q_TILE), im(lambda p, qr, kr, b: (0, 0, qr[p]))),
            pl.BlockSpec((N_PAIRS, K_SUBS * K_TILE, LANES), im(lambda p, qr, kr, b: (0, kr[p], 0))),
            pl.BlockSpec((N_PAIRS, K_SUBS, vrows, K_TILE), im(lambda p, qr, kr, b: (0, kr[p], 0, 0))),
            pl.BlockSpec((IDX_PAIRS, LANES, Q_TILE), im(lambda p, qr, kr, b: (0, 0, qr[p]))),
            pl.BlockSpec((IDX_HEADS, Q_TILE), im(lambda p, qr, kr, b: (0, qr[p]))),
            pl.BlockSpec((t_pad, LANES), im(lambda p, qr, kr, b: (b[p], 0))),
        ],
        out_specs=pl.BlockSpec((N_PAIRS, Q_TILE, LANES), im(lambda p, qr, kr, b: (0, qr[p], 0))),
        scratch_shapes=[pltpu.VMEM((nkb_total, K_TILE, Q_TILE), jnp.int32),
                        pltpu.VMEM((nkb_total, K_TILE, Q_TILE), BF16),
                        pltpu.VMEM((1, Q_TILE), jnp.int32),
                        pltpu.VMEM((N_PAIRS, 1, 2 * Q_TILE), F32),
                        pltpu.VMEM((N_PAIRS, vrows, 2 * Q_TILE), F32)],
    )
    return pl.pallas_call(
        functools.partial(_dsa_kernel, topk=topk, pos_bits=(t_pad - 1).bit_length()),
        grid_spec=grid_spec,
        out_shape=jax.ShapeDtypeStruct((N_PAIRS, rows, LANES), BF16),
        compiler_params=pltpu.CompilerParams(dimension_semantics=("arbitrary",),
                                             vmem_limit_bytes=VMEM_LIMIT),
        name="dsa_attn",
    )(steps["qrow"], steps["krow"], steps["qloc"], steps["kloc"], steps["nkb"], steps["nsub"],
      steps["full"], steps["first"], steps["last"], steps["b"], q_t, qkvi, v_t, qi_t, wi_t, kidx)


def _rope_tables(t_pad):
    inv = ROPE_THETA ** (-jnp.arange(0, HEAD_DIM, 2, dtype=F32) / HEAD_DIM)
    ang = jnp.arange(t_pad, dtype=F32)[:, None] * inv[None, :]
    cos, sin = jnp.cos(ang), jnp.sin(ang)
    cos128 = jnp.tile(cos, (1, LANES // (HEAD_DIM // 2)))
    sin128 = jnp.tile(jnp.concatenate([-sin, sin], axis=1), (1, LANES // HEAD_DIM))
    return cos128, sin128


def _interleave_gate_val(a):
    lead = a.shape[:-1]
    a = a.reshape(lead + (2, FFN_HIDDEN // FFN_TILE, FFN_TILE))
    a = jnp.swapaxes(a, -3, -2)
    return a.reshape(lead + (2 * FFN_HIDDEN,))


def _with_ones_rows(v_t):
    ones = jnp.ones((v_t.shape[0], BF16_ROWS, v_t.shape[2]), v_t.dtype)
    return jnp.concatenate([v_t, ones], axis=1)


def _key_tiled(v_t):
    pairs, d, rows = v_t.shape
    return jnp.swapaxes(v_t.reshape(pairs, d, rows // K_TILE, K_TILE), 1, 2)


def kernel(x, meta_tokens, da_norm, da_w_qkv, da_lambda_q1, da_lambda_k1, da_lambda_q2, da_lambda_k2, da_subln, da_w_o, dsa_norm, dsa_w_in, dsa_idx_k_norm, dsa_w_o, ffn_norm, ffn_w_up, ffn_conv_w, ffn_conv_b, ffn_w_down, final_norm):
    batch, seq, d = x.shape
    assert d == D_MODEL
    depth = ffn_norm.shape[0]
    t_real = seq + N_META
    t_pad = _padded_len(t_real)
    assert t_pad % (K_SUBS * K_TILE) == 0 and t_pad % Q_TILE == 0
    rows = batch * t_pad

    meta = jnp.broadcast_to(meta_tokens[None].astype(x.dtype), (batch, N_META, d))
    h = jnp.concatenate([meta, x, jnp.zeros((batch, t_pad - t_real, d), x.dtype)], axis=1)
    h = h.reshape(rows, d)
    cos128, sin128 = _rope_tables(t_pad)
    steps = _attention_steps(batch, t_pad)

    dsa_qkv = N_HEADS * HEAD_DIM
    idx_cols = IDX_HEADS * HEAD_DIM
    for i in range(depth):
        j = i // 2
        if i % 2 == 0:
            lambda_init = 0.8 - 0.6 * math.exp(-0.3 * i)
            q_t, k, v_t = _project(h, da_norm[j], da_w_qkv[j].astype(BF16), cos128, sin128, t_pad)
            lam_rows = jnp.zeros((8, LANES), F32).at[0:4, 0:HEAD_DIM].set(
                jnp.stack([da_lambda_q1[j], da_lambda_k1[j], da_lambda_q2[j], da_lambda_k2[j]]).astype(F32))
            o = _diff_attention(q_t, k, v_t, lam_rows, da_subln[j].reshape(LANES, 1).astype(F32),
                                steps, lambda_init)
            h = _out_project(h, o, da_w_o[j].astype(BF16))
        else:
            w_in = dsa_w_in[j]
            n_main = 3 * dsa_qkv + idx_cols
            w_tail = jnp.zeros((d, LANES), w_in.dtype).at[:, :w_in.shape[1] - n_main].set(w_in[:, n_main:])
            gk = jnp.ones((1, LANES), F32).at[0, :HEAD_DIM].set(dsa_idx_k_norm[j].astype(F32))
            q_t, k, v_t, qi_t, kidx, wi_t = _project(
                h, dsa_norm[j], w_in[:, :n_main].astype(BF16), cos128, sin128, t_pad,
                w_tail=w_tail.astype(BF16), gk=gk)
            topk = min(TOPK_MAX, seq // 4)
            o = _dsa_attention(q_t, k, v_t, qi_t, wi_t, kidx, steps, topk, t_pad)
            h = _out_project(h, o, dsa_w_o[j].astype(BF16))
        h = _ffn(h, ffn_norm[i], _interleave_gate_val(ffn_w_up[i]).astype(BF16),
                 _interleave_gate_val(ffn_conv_w[i]).astype(F32),
                 _interleave_gate_val(ffn_conv_b[i]).reshape(1, -1).astype(F32),
                 ffn_w_down[i].astype(BF16), final_norm, t_pad, final=(i == depth - 1))
    return h.reshape(batch, t_pad, d)[:, N_META:N_META + seq]
```

```python
import functools
import math

import numpy as np
import jax
import jax.numpy as jnp
from jax import lax
from jax.experimental import pallas as pl
from jax.experimental.pallas import tpu as pltpu

D_MODEL = 1024
N_META = 16
ROPE_THETA = 10000.0
RMS_EPS = 1e-6
HEAD_DIM = 64
N_PAIRS = 8
N_HEADS = 2 * N_PAIRS
IDX_HEADS = 8
IDX_PAIRS = IDX_HEADS // 2
TOPK_MAX = 256
BIG_SCORE = 1e30
LOG2_E = 1.4426950408889634
FFN_HIDDEN = 2816
CONV_WIDTH = 3

LANES = 128
BF16_ROWS = 16
ROW_TILE = 512
Q_TILE = 256
K_TILE = 512
K_SUBS = 3
FFN_TILE = 256
HALO = BF16_ROWS
VMEM_LIMIT = 56 * 1024 * 1024

F32 = jnp.float32
BF16 = jnp.bfloat16
INT_MIN = -2147483648
KEY_NEG_INF = INT_MIN + 0x7FFFFF
BF16_MIN_NORMAL_BITS = 0x0080
F32_MIN_NORMAL = 1.1754943508222875e-38
TIE_PEEL_MAX = 4


def _padded_len(t):
    unit = math.lcm(ROW_TILE, K_SUBS * K_TILE, Q_TILE)
    return ((t + unit - 1) // unit) * unit


def _rms(x, gain):
    ms = jnp.mean(x * x, axis=-1, keepdims=True)
    return x * lax.rsqrt(ms + RMS_EPS) * gain


def _rope_lanes(y, cos, sin, lo):
    sw = jnp.where(lo, pltpu.roll(y, LANES - HEAD_DIM // 2, 1), pltpu.roll(y, HEAD_DIM // 2, 1))
    return y * cos + sw * sin


def _proj_kernel(x_ref, g_ref, w_ref, cos_ref, sin_ref, *rest, has_idx):
    if has_idx:
        wt_ref, gk_ref, qt_ref, k_ref, vt_ref, qit_ref, kidx_ref, wit_ref = rest
    else:
        qt_ref, k_ref, vt_ref = rest
    xh = _rms(x_ref[...], g_ref[...]).astype(BF16)
    cos = cos_ref[...]
    sin = sin_ref[...]
    lane = lax.broadcasted_iota(jnp.int32, cos.shape, 1)
    lo = (lane & (HEAD_DIM // 2)) == 0
    n_groups = (3 * N_PAIRS + IDX_PAIRS) if has_idx else 3 * N_PAIRS
    cw = 4 * LANES
    ones = jnp.ones((BF16_ROWS, ROW_TILE), BF16)
    for c in range(n_groups // 4):
        y = jnp.dot(xh, w_ref[:, c * cw:(c + 1) * cw], preferred_element_type=F32)
        for s in range(4):
            g = c * 4 + s
            yg = y[:, s * LANES:(s + 1) * LANES]
            if g < N_PAIRS:
                yg = _rope_lanes(yg, cos, sin, lo) * (HEAD_DIM ** -0.5 * LOG2_E)
                qt_ref[g] = yg.T.astype(BF16)
            elif g < 2 * N_PAIRS:
                k_ref[g - N_PAIRS] = _rope_lanes(yg, cos, sin, lo).astype(BF16)
            elif g < 3 * N_PAIRS:
                j = g - 2 * N_PAIRS
                vt_ref[j, 0, 0:2 * HEAD_DIM, :] = yg.T.astype(BF16)
                vt_ref[j, 0, 2 * HEAD_DIM:, :] = ones
            else:
                yg = _rope_lanes(yg, cos, sin, lo) * (HEAD_DIM ** -0.5)
                qit_ref[g - 3 * N_PAIRS] = yg.T.astype(BF16)
    if has_idx:
        t = jnp.dot(xh, wt_ref[...], preferred_element_type=F32)
        is_k = lane < HEAD_DIM
        ms = jnp.sum(jnp.where(is_k, t * t, 0.0), axis=-1, keepdims=True) * (1.0 / HEAD_DIM)
        kn = t * lax.rsqrt(ms + RMS_EPS) * gk_ref[...]
        kn = _rope_lanes(kn, cos, sin, lo)
        kidx_ref[...] = jnp.where(is_k, kn, pltpu.roll(kn, HEAD_DIM, 1)).astype(BF16)
        wit_ref[...] = (t * (IDX_HEADS ** -0.5)).T[HEAD_DIM:HEAD_DIM + IDX_HEADS, :]


def _project(h, gain, w, cos, sin, t_pad, w_tail=None, gk=None):
    assert ROW_TILE == K_TILE
    rows = h.shape[0]
    blocks_per_batch = t_pad // ROW_TILE
    has_idx = w_tail is not None
    vrows = 2 * HEAD_DIM + BF16_ROWS
    in_specs = [
        pl.BlockSpec((ROW_TILE, D_MODEL), lambda i: (i, 0)),
        pl.BlockSpec((1, D_MODEL), lambda i: (0, 0)),
        pl.BlockSpec(w.shape, lambda i: (0, 0)),
        pl.BlockSpec((ROW_TILE, LANES), lambda i: (i % blocks_per_batch, 0)),
        pl.BlockSpec((ROW_TILE, LANES), lambda i: (i % blocks_per_batch, 0)),
    ]
    out_shape = [jax.ShapeDtypeStruct((N_PAIRS, LANES, rows), BF16),
                 jax.ShapeDtypeStruct((N_PAIRS, rows, LANES), BF16),
                 jax.ShapeDtypeStruct((N_PAIRS, rows // K_TILE, vrows, K_TILE), BF16)]
    out_specs = [pl.BlockSpec((N_PAIRS, LANES, ROW_TILE), lambda i: (0, 0, i)),
                 pl.BlockSpec((N_PAIRS, ROW_TILE, LANES), lambda i: (0, i, 0)),
                 pl.BlockSpec((N_PAIRS, 1, vrows, K_TILE), lambda i: (0, i, 0, 0))]
    args = [h, gain.reshape(1, D_MODEL), w, cos, sin]
    if has_idx:
        in_specs += [pl.BlockSpec(w_tail.shape, lambda i: (0, 0)),
                     pl.BlockSpec((1, LANES), lambda i: (0, 0))]
        out_shape += [jax.ShapeDtypeStruct((IDX_PAIRS, LANES, rows), BF16),
                      jax.ShapeDtypeStruct((rows, LANES), BF16),
                      jax.ShapeDtypeStruct((IDX_HEADS, rows), F32)]
        out_specs += [pl.BlockSpec((IDX_PAIRS, LANES, ROW_TILE), lambda i: (0, 0, i)),
                      pl.BlockSpec((ROW_TILE, LANES), lambda i: (i, 0)),
                      pl.BlockSpec((IDX_HEADS, ROW_TILE), lambda i: (0, i))]
        args += [w_tail, gk]
    return pl.pallas_call(
        functools.partial(_proj_kernel, has_idx=has_idx),
        grid=(rows // ROW_TILE,),
        in_specs=in_specs,
        out_specs=out_specs,
        out_shape=out_shape,
        compiler_params=pltpu.CompilerParams(dimension_semantics=("arbitrary",),
                                             vmem_limit_bytes=VMEM_LIMIT),
        name="proj_idx" if has_idx else "proj",
    )(*args)


def _oproj_kernel(h_ref, o_ref, w_ref, out_ref):
    o = jnp.concatenate([o_ref[j] for j in range(N_PAIRS)], axis=1)
    out_ref[...] = h_ref[...] + jnp.dot(o, w_ref[...], preferred_element_type=F32)


def _out_project(h, o, w):
    rows = h.shape[0]
    return pl.pallas_call(
        _oproj_kernel,
        grid=(rows // ROW_TILE,),
        in_specs=[pl.BlockSpec((ROW_TILE, D_MODEL), lambda i: (i, 0)),
                  pl.BlockSpec((N_PAIRS, ROW_TILE, LANES), lambda i: (0, i, 0)),
                  pl.BlockSpec(w.shape, lambda i: (0, 0))],
        out_specs=pl.BlockSpec((ROW_TILE, D_MODEL), lambda i: (i, 0)),
        out_shape=jax.ShapeDtypeStruct(h.shape, F32),
        compiler_params=pltpu.CompilerParams(dimension_semantics=("arbitrary",),
                                             vmem_limit_bytes=VMEM_LIMIT),
        name="oproj",
    )(h, o, w)


def _ffn_kernel(xp_ref, x_ref, g_ref, wup_ref, cw_ref, cb_ref, wdn_ref, fg_ref, o_ref, u_sc, a_sc,
                *, blocks_per_batch, final):
    i = pl.program_id(0)
    x = x_ref[...]
    xe = jnp.concatenate([xp_ref[...], x], axis=0)
    xh = _rms(xe, g_ref[...])
    row = lax.broadcasted_iota(jnp.int32, (HALO + ROW_TILE, 1), 0)
    keep = jnp.logical_or(row >= HALO, i % blocks_per_batch != 0)
    xh = jnp.where(keep, xh, 0.0).astype(BF16)
    cw2 = 2 * FFN_TILE
    n_chunks = FFN_HIDDEN // FFN_TILE

    def up(c):
        return jnp.dot(xh, wup_ref[:, c * cw2:(c + 1) * cw2], preferred_element_type=F32)

    u_sc[0] = up(0)
    for c in range(n_chunks):
        cur = c % 2
        after = None
        if c + 1 < n_chunks:
            u_sc[1 - cur] = up(c + 1)
            after = _zero_after(u_sc[1 - cur, 0:8, 0:FFN_TILE])
        cwc = cw_ref[:, c * cw2:(c + 1) * cw2]
        conv = cb_ref[:, c * cw2:(c + 1) * cw2]
        for j in range(CONV_WIDTH):
            conv = conv + cwc[j:j + 1, :] * u_sc[cur, pl.ds(HALO - (CONV_WIDTH - 1) + j, ROW_TILE), :]
        gate = conv[:, :FFN_TILE]
        val = conv[:, FFN_TILE:]
        a = gate * jax.nn.sigmoid(gate) * val
        if after is not None:
            cut = (3 * ROW_TILE) // 4
            a = jnp.concatenate([a[:cut], a[cut:] + after], axis=0)
        a_sc[:, c * FFN_TILE:(c + 1) * FFN_TILE] = a.astype(BF16)
    acc = jnp.dot(a_sc[...], wdn_ref[...], preferred_element_type=F32)
    y = x + acc
    if final:
        y = _rms(y, fg_ref[...])
    o_ref[...] = y


def _ffn(h, gain, w_up, conv_w, conv_b, w_down, final_gain, t_pad, final):
    rows = h.shape[0]
    blocks_per_batch = t_pad // ROW_TILE
    halo_blocks = ROW_TILE // HALO
    return pl.pallas_call(
        functools.partial(_ffn_kernel, blocks_per_batch=blocks_per_batch, final=final),
        grid=(rows // ROW_TILE,),
        in_specs=[pl.BlockSpec((HALO, D_MODEL), lambda i: (jnp.maximum(i * halo_blocks - 1, 0), 0)),
                  pl.BlockSpec((ROW_TILE, D_MODEL), lambda i: (i, 0)),
                  pl.BlockSpec((1, D_MODEL), lambda i: (0, 0)),
                  pl.BlockSpec(w_up.shape, lambda i: (0, 0)),
                  pl.BlockSpec(conv_w.shape, lambda i: (0, 0)),
                  pl.BlockSpec(conv_b.shape, lambda i: (0, 0)),
                  pl.BlockSpec(w_down.shape, lambda i: (0, 0)),
                  pl.BlockSpec((1, D_MODEL), lambda i: (0, 0))],
        out_specs=pl.BlockSpec((ROW_TILE, D_MODEL), lambda i: (i, 0)),
        out_shape=jax.ShapeDtypeStruct(h.shape, F32),
        scratch_shapes=[pltpu.VMEM((2, HALO + ROW_TILE, 2 * FFN_TILE), F32),
                        pltpu.VMEM((ROW_TILE, FFN_HIDDEN), BF16)],
        compiler_params=pltpu.CompilerParams(dimension_semantics=("arbitrary",),
                                             vmem_limit_bytes=VMEM_LIMIT),
        name="ffn_final" if final else "ffn",
    )(h, h, gain.reshape(1, D_MODEL), w_up, conv_w, conv_b, w_down, final_gain.reshape(1, D_MODEL))


def _attention_steps(batch, t_pad):
    nq = t_pad // Q_TILE
    nk = t_pad // (K_SUBS * K_TILE)
    cols = {k: [] for k in ("qrow", "krow", "qloc", "kloc", "nkb", "nsub", "full", "first", "last", "b")}
    for b in range(batch):
        for qi in range(nq):
            n_tiles = ((qi + 1) * Q_TILE + K_TILE - 1) // K_TILE
            n = (n_tiles + K_SUBS - 1) // K_SUBS
            for ki in range(n):
                cols["qrow"].append(b * nq + qi)
                cols["krow"].append(b * nk + ki)
                cols["qloc"].append(qi)
                cols["kloc"].append(ki)
                cols["nkb"].append(n_tiles)
                cols["nsub"].append(min(K_SUBS, n_tiles - ki * K_SUBS))
                cols["full"].append(int((ki + 1) * K_SUBS < n_tiles))
                cols["first"].append(int(ki == 0))
                cols["last"].append(int(ki == n - 1))
                cols["b"].append(b)
    return {k: jnp.asarray(np.asarray(v, np.int32)) for k, v in cols.items()}


def _half_select(x_t, first_half):
    row = lax.broadcasted_iota(jnp.int32, x_t.shape, 0)
    keep = (row < HEAD_DIM) if first_half else (row >= HEAD_DIM)
    return jnp.where(keep, x_t, jnp.zeros_like(x_t))


def _zero_after(x):
    w = lax.bitcast_convert_type(x, jnp.uint32)
    w = lax.shift_right_logical(lax.shift_right_logical(w, jnp.uint32(16)), jnp.uint32(16))
    return lax.bitcast_convert_type(w, F32)[0:1, :]


def _flash_update(s_t, v_t, h, m_sc, acc_sc, after=None):
    m_prev = m_sc[h]
    m_new = jnp.maximum(m_prev, jnp.max(s_t, axis=0, keepdims=True))
    alpha = jnp.exp2(m_prev - m_new)
    if after is None:
        p_t = jnp.exp2(s_t - m_new).astype(BF16)
    else:
        cut = s_t.shape[0] // 2
        p_t = jnp.concatenate([jnp.exp2(s_t[:cut] - m_new),
                               jnp.exp2(s_t[cut:] - (m_new + after))], axis=0).astype(BF16)
    acc_sc[h] = alpha * acc_sc[h] + jnp.dot(v_t, p_t, preferred_element_type=F32)
    m_sc[h] = m_new


def _both_halves(x_t):
    return jnp.concatenate([_half_select(x_t, True), _half_select(x_t, False)], axis=1)


def _attention_sweep(qt_ref, k_ref, vt_ref, subs, bias_of, m_sc, acc_sc):
    def scores(sub, j):
        start = sub * K_TILE
        if not isinstance(sub, int):
            start = pl.multiple_of(start, K_TILE)
        return jnp.dot(k_ref[j, pl.ds(start, K_TILE), :], _both_halves(qt_ref[j]), preferred_element_type=F32)

    items = [(sub, j) for sub in subs for j in range(N_PAIRS)]
    biases = {}
    ahead = 2
    pending = [scores(*item) for item in items[:ahead]]
    for i, (sub, j) in enumerate(items):
        s_t = pending.pop(0)
        tile_key = sub if isinstance(sub, int) else "traced"
        if tile_key not in biases:
            biases[tile_key] = bias_of(sub)
        if biases[tile_key] is not None:
            s_t = s_t + biases[tile_key]
        after = None
        if i + ahead < len(items):
            pending.append(scores(*items[i + ahead]))
            after = _zero_after(pending[-1][0:8, :])
        _flash_update(s_t, vt_ref[j, sub], j, m_sc, acc_sc, after=after)


def _diff_attn_kernel(qrow, krow, qloc, kloc, nsub, full, first, last,
                      qt_ref, k_ref, vt_ref, lam_ref, subln_ref, o_ref, m_sc, acc_sc,
                      *, lambda_init):
    p = pl.program_id(0)

    @pl.when(first[p] == 1)
    def _():
        m_sc[...] = jnp.full(m_sc.shape, -jnp.inf, F32)
        acc_sc[...] = jnp.zeros(acc_sc.shape, F32)

    def key_tile(sub, carry):
        tile = kloc[p] * K_SUBS + sub
        on_diagonal = tile * K_TILE + (K_TILE - 1) > qloc[p] * Q_TILE

        @pl.when(on_diagonal)
        def _():
            kpos = tile * K_TILE + lax.broadcasted_iota(jnp.int32, (K_TILE, 2 * Q_TILE), 0)
            qcol = lax.broadcasted_iota(jnp.int32, (K_TILE, 2 * Q_TILE), 1)
            qpos = qloc[p] * Q_TILE + jnp.where(qcol >= Q_TILE, qcol - Q_TILE, qcol)
            bias = jnp.where(kpos <= qpos, 0.0, -jnp.inf).astype(F32)
            _attention_sweep(qt_ref, k_ref, vt_ref, (sub,), lambda _: bias, m_sc, acc_sc)

        @pl.when(jnp.logical_not(on_diagonal))
        def _():
            _attention_sweep(qt_ref, k_ref, vt_ref, (sub,), lambda _: None, m_sc, acc_sc)

        return carry

    @pl.when(full[p] == 1)
    def _():
        _attention_sweep(qt_ref, k_ref, vt_ref, tuple(range(K_SUBS)), lambda _: None, m_sc, acc_sc)

    @pl.when(full[p] == 0)
    def _():
        lax.fori_loop(0, nsub[p], key_tile, 0)

    @pl.when(last[p] == 1)
    def _():
        lam_rows = lam_ref[...]
        lam = (jnp.exp(jnp.sum(lam_rows[0:1] * lam_rows[1:2], axis=-1, keepdims=True))
               - jnp.exp(jnp.sum(lam_rows[2:3] * lam_rows[3:4], axis=-1, keepdims=True))
               + lambda_init)
        vdim = 2 * HEAD_DIM
        for j in range(N_PAIRS):
            a = acc_sc[j]
            a1 = a[:, :Q_TILE]
            a2 = a[:, Q_TILE:]
            o = a1[:vdim] / a1[vdim:vdim + 1] - lam * (a2[:vdim] / a2[vdim:vdim + 1])
            ms = jnp.mean(o * o, axis=0, keepdims=True)
            o = o * lax.rsqrt(ms + RMS_EPS) * subln_ref[...] * (1.0 - lambda_init)
            o_ref[j] = o.T.astype(BF16)


def _diff_attention(q_t, qkv, v_t, lam_rows, subln_col, steps, lambda_init):
    rows = qkv.shape[1]
    n_steps = steps["qrow"].shape[0]
    vrows = v_t.shape[2]

    def im(fn):
        return lambda p, qr, kr, ql, kl, ns, fu, f, l: fn(p, qr, kr)

    grid_spec = pltpu.PrefetchScalarGridSpec(
        num_scalar_prefetch=8,
        grid=(n_steps,),
        in_specs=[
            pl.BlockSpec((N_PAIRS, LANES, Q_TILE), im(lambda p, qr, kr: (0, 0, qr[p]))),
            pl.BlockSpec((N_PAIRS, K_SUBS * K_TILE, LANES), im(lambda p, qr, kr: (0, kr[p], 0))),
            pl.BlockSpec((N_PAIRS, K_SUBS, vrows, K_TILE), im(lambda p, qr, kr: (0, kr[p], 0, 0))),
            pl.BlockSpec((8, LANES), im(lambda p, qr, kr: (0, 0))),
            pl.BlockSpec((LANES, 1), im(lambda p, qr, kr: (0, 0))),
        ],
        out_specs=pl.BlockSpec((N_PAIRS, Q_TILE, LANES), im(lambda p, qr, kr: (0, qr[p], 0))),
        scratch_shapes=[pltpu.VMEM((N_PAIRS, 1, 2 * Q_TILE), F32),
                        pltpu.VMEM((N_PAIRS, vrows, 2 * Q_TILE), F32)],
    )
    return pl.pallas_call(
        functools.partial(_diff_attn_kernel, lambda_init=lambda_init),
        grid_spec=grid_spec,
        out_shape=jax.ShapeDtypeStruct((N_PAIRS, rows, LANES), BF16),
        compiler_params=pltpu.CompilerParams(dimension_semantics=("arbitrary",),
                                             vmem_limit_bytes=VMEM_LIMIT),
        name="diff_attn",
    )(steps["qrow"], steps["krow"], steps["qloc"], steps["kloc"], steps["nsub"], steps["full"],
      steps["first"], steps["last"], q_t, qkv, v_t, lam_rows, subln_col)


def _dsa_kernel(qrow, krow, qloc, kloc, nkb, nsub, full, first, last, bidx,
                qt_ref, k_ref, vt_ref, qit_ref, wit_ref, kidx_ref, o_ref,
                key_sc, coarse_sc, thr_sc, m_sc, acc_sc, *, topk, pos_bits):
    p = pl.program_id(0)

    @pl.when(first[p] == 1)
    def _():
        m_sc[...] = jnp.full(m_sc.shape, -jnp.inf, F32)
        acc_sc[...] = jnp.zeros(acc_sc.shape, F32)
        n = nkb[p]
        wi = wit_ref[...]
        qpos = qloc[p] * Q_TILE + lax.broadcasted_iota(jnp.int32, (K_TILE, Q_TILE), 1)
        krow_iota = lax.broadcasted_iota(jnp.int32, (K_TILE, Q_TILE), 0)

        def score_chunk(c, carry):
            kt = kidx_ref[pl.ds(pl.multiple_of(c * K_TILE, K_TILE), K_TILE), :]
            sc = jnp.zeros((K_TILE, Q_TILE), F32)
            for jp in range(IDX_PAIRS):
                qi_t = qit_ref[jp]
                for cc in range(2):
                    raw = jnp.dot(kt, _half_select(qi_t, cc == 0), preferred_element_type=F32)
                    hh = 2 * jp + cc
                    sc = sc + jnp.maximum(raw, 0.0) * wi[hh:hh + 1, :]
            kpos = c * K_TILE + krow_iota
            sc = jnp.where(kpos < N_META, BIG_SCORE, sc)
            sc = jnp.where(kpos <= qpos, sc, -jnp.inf)
            sc = jnp.where(jnp.abs(sc) < F32_MIN_NORMAL, 0.0, sc)
            bits = lax.bitcast_convert_type(sc, jnp.int32)
            key_sc[c] = bits ^ ((bits >> 31) & 0x7FFFFFFF)
            coarse_sc[c] = lax.bitcast_convert_type(bits & jnp.int32(-65536), F32).astype(BF16)
            return carry

        lax.fori_loop(0, n, score_chunk, 0)

        kf = float(topk)
        q1 = qloc[p] * Q_TILE + lax.broadcasted_iota(jnp.int32, (1, Q_TILE), 1)
        settled0 = (q1 < topk).astype(jnp.int32)
        coarse_bits = 16

        def coarse_body(b, state):
            lo, settled = state
            cand = lo + jnp.left_shift(jnp.int32(1), 31 - b)
            top = cand >> 16
            top = jnp.where(jnp.logical_and(top > 0, top < BF16_MIN_NORMAL_BITS), BF16_MIN_NORMAL_BITS, top)
            cbits = jnp.left_shift(top ^ ((top >> 31) & 0x7FFF), 16)
            cand16 = lax.bitcast_convert_type(cbits, F32).astype(BF16)

            def count_chunk(c, cnt16):
                hit = jnp.where(coarse_sc[c] >= cand16, jnp.ones((), BF16), jnp.zeros((), BF16))
                parts = [hit[i * BF16_ROWS:(i + 1) * BF16_ROWS] for i in range(K_TILE // BF16_ROWS)]
                while len(parts) > 1:
                    parts = [parts[i] + parts[i + 1] for i in range(0, len(parts), 2)]
                return cnt16 + parts[0].astype(F32)

            cnt16 = lax.fori_loop(0, n, count_chunk, jnp.zeros((BF16_ROWS, Q_TILE), F32))
            cnt = jnp.sum(cnt16, axis=0, keepdims=True)
            return jnp.where(cnt >= kf, cand, lo), jnp.where(cnt == kf, 1, settled)

        lo, settled = lax.fori_loop(0, coarse_bits, coarse_body,
                                    (jnp.full((1, Q_TILE), INT_MIN, jnp.int32), settled0))

        def unsettled(state):
            b, _, settled = state
            return jnp.logical_and(b < 32, jnp.min(settled) == 0)

        def count_where(hit_of):
            def count_chunk(c, cnt8):
                hit = jnp.where(hit_of(c, key_sc[c]), 1.0, 0.0).reshape(8, K_TILE // 64, 8, Q_TILE)
                return cnt8 + jnp.sum(jnp.sum(hit, axis=1), axis=0)

            cnt8 = lax.fori_loop(0, n, count_chunk, jnp.zeros((8, Q_TILE), F32))
            return jnp.sum(cnt8, axis=0, keepdims=True)

        def fine_body(state):
            b, lo, settled = state
            cand = lo + jnp.left_shift(jnp.int32(1), 31 - b)
            cnt = count_where(lambda c, keys: keys >= cand)
            return b + 1, jnp.where(cnt >= kf, cand, lo), jnp.where(cnt == kf, 1, settled)

        _, thr, settled = lax.while_loop(unsettled, fine_body, (jnp.int32(coarse_bits), lo, settled))
        thr = jnp.maximum(thr, KEY_NEG_INF + 1)
        thr_sc[...] = thr

        @pl.when(jnp.min(settled) == 0)
        def _():
            tied = settled == 0
            surplus0 = jnp.where(tied, count_where(lambda c, keys: keys >= thr) - kf, 0.0)

            def demote_ties_from(cut, rows):
                def demote(c, carry):
                    keys = key_sc[c]
                    kpos = c * K_TILE + krow_iota
                    drop = jnp.logical_and(jnp.logical_and(keys == thr, kpos >= cut), rows)
                    key_sc[c] = jnp.where(drop, keys - 1, keys)
                    return carry

                lax.fori_loop(0, n, demote, 0)

            few = jnp.max(surplus0) <= float(TIE_PEEL_MAX)

            @pl.when(few)
            def _():
                def more(surplus):
                    return jnp.max(surplus) > 0.0

                def peel(surplus):
                    def last_tie(c, best8):
                        kpos = c * K_TILE + krow_iota
                        cand = jnp.where(key_sc[c] == thr, kpos, -1).reshape(8, K_TILE // 64, 8, Q_TILE)
                        return jnp.maximum(best8, jnp.max(jnp.max(cand, axis=1), axis=0))

                    best8 = lax.fori_loop(0, n, last_tie, jnp.full((8, Q_TILE), -1, jnp.int32))
                    demote_ties_from(jnp.max(best8, axis=0, keepdims=True), surplus > 0.0)
                    return jnp.maximum(surplus - 1.0, 0.0)

                lax.while_loop(more, peel, surplus0)

            @pl.when(jnp.logical_not(few))
            def _():
                def pos_body(b, pos):
                    cand = pos + jnp.left_shift(jnp.int32(1), pos_bits - 1 - b)
                    cnt = count_where(lambda c, keys: jnp.logical_and(keys == thr,
                                                                      c * K_TILE + krow_iota >= cand))
                    return jnp.where(cnt >= surplus0, cand, pos)

                cut = lax.fori_loop(0, pos_bits, pos_body, jnp.zeros((1, Q_TILE), jnp.int32))
                demote_ties_from(cut, tied)

    def mask_bias(sub):
        sel = key_sc[kloc[p] * K_SUBS + sub] >= thr_sc[...]
        bias1 = jnp.where(sel, 0.0, -jnp.inf).astype(F32)
        return jnp.concatenate([bias1, bias1], axis=1)

    def key_tile(sub, carry):
        _attention_sweep(qt_ref, k_ref, vt_ref, (sub,), mask_bias, m_sc, acc_sc)
        return carry

    @pl.when(full[p] == 1)
    def _():
        _attention_sweep(qt_ref, k_ref, vt_ref, tuple(range(K_SUBS)), mask_bias, m_sc, acc_sc)

    @pl.when(full[p] == 0)
    def _():
        lax.fori_loop(0, nsub[p], key_tile, 0)

    @pl.when(last[p] == 1)
    def _():
        ones_row = 2 * HEAD_DIM
        for j in range(N_PAIRS):
            a = acc_sc[j]
            o = jnp.concatenate([a[:HEAD_DIM, :Q_TILE] / a[ones_row:ones_row + 1, :Q_TILE],
                                 a[HEAD_DIM:ones_row, Q_TILE:] / a[ones_row:ones_row + 1, Q_TILE:]],
                                axis=0)
            o_ref[j] = o.T.astype(BF16)


def _dsa_attention(q_t, qkvi, v_t, qi_t, wi_t, kidx, steps, topk, t_pad):
    rows = qkvi.shape[1]
    n_steps = steps["qrow"].shape[0]
    nkb_total = t_pad // K_TILE
    vrows = v_t.shape[2]

    def im(fn):
        return lambda p, qr, kr, ql, kl, nk, ns, fu, f, l, b: fn(p, qr, kr, b)

    grid_spec = pltpu.PrefetchScalarGridSpec(
        num_scalar_prefetch=10,
        grid=(n_steps,),
        in_specs=[
            pl.BlockSpec((N_PAIRS, LANES, Q_TILE), im(lambda p, qr, kr, b: (0, 0, qr[p]))),
            pl.BlockSpec((N_PAIRS, K_SUBS * K_TILE, LANES), im(lambda p, qr, kr, b: (0, kr[p], 0))),
            pl.BlockSpec((N_PAIRS, K_SUBS, vrows, K_TILE), im(lambda p, qr, kr, b: (0, kr[p], 0, 0))),
            pl.BlockSpec((IDX_PAIRS, LANES, Q_TILE), im(lambda p, qr, kr, b: (0, 0, qr[p]))),
            pl.BlockSpec((IDX_HEADS, Q_TILE), im(lambda p, qr, kr, b: (0, qr[p]))),
            pl.BlockSpec((t_pad, LANES), im(lambda p, qr, kr, b: (b[p], 0))),
        ],
        out_specs=pl.BlockSpec((N_PAIRS, Q_TILE, LANES), im(lambda p, qr, kr, b: (0, qr[p], 0))),
        scratch_shapes=[pltpu.VMEM((nkb_total, K_TILE, Q_TILE), jnp.int32),
                        pltpu.VMEM((nkb_total, K_TILE, Q_TILE), BF16),
                        pltpu.VMEM((1, Q_TILE), jnp.int32),
                        pltpu.VMEM((N_PAIRS, 1, 2 * Q_TILE), F32),
                        pltpu.VMEM((N_PAIRS, vrows, 2 * Q_TILE), F32)],
    )
    return pl.pallas_call(
        functools.partial(_dsa_kernel, topk=topk, pos_bits=(t_pad - 1).bit_length()),
        grid_spec=grid_spec,
        out_shape=jax.ShapeDtypeStruct((N_PAIRS, rows, LANES), BF16),
        compiler_params=pltpu.CompilerParams(dimension_semantics=("arbitrary",),
                                             vmem_limit_bytes=VMEM_LIMIT),
        name="dsa_attn",
    )(steps["qrow"], steps["krow"], steps["qloc"], steps["kloc"], steps["nkb"], steps["nsub"],
      steps["full"], steps["first"], steps["last"], steps["b"], q_t, qkvi, v_t, qi_t, wi_t, kidx)


def _rope_tables(t_pad):
    inv = ROPE_THETA ** (-jnp.arange(0, HEAD_DIM, 2, dtype=F32) / HEAD_DIM)
    ang = jnp.arange(t_pad, dtype=F32)[:, None] * inv[None, :]
    cos, sin = jnp.cos(ang), jnp.sin(ang)
    cos128 = jnp.tile(cos, (1, LANES // (HEAD_DIM // 2)))
    sin128 = jnp.tile(jnp.concatenate([-sin, sin], axis=1), (1, LANES // HEAD_DIM))
    return cos128, sin128


def _interleave_gate_val(a):
    lead = a.shape[:-1]
    a = a.reshape(lead + (2, FFN_HIDDEN // FFN_TILE, FFN_TILE))
    a = jnp.swapaxes(a, -3, -2)
    return a.reshape(lead + (2 * FFN_HIDDEN,))


def _with_ones_rows(v_t):
    ones = jnp.ones((v_t.shape[0], BF16_ROWS, v_t.shape[2]), v_t.dtype)
    return jnp.concatenate([v_t, ones], axis=1)


def _key_tiled(v_t):
    pairs, d, rows = v_t.shape
    return jnp.swapaxes(v_t.reshape(pairs, d, rows // K_TILE, K_TILE), 1, 2)


def kernel(x, meta_tokens, da_norm, da_w_qkv, da_lambda_q1, da_lambda_k1, da_lambda_q2, da_lambda_k2, da_subln, da_w_o, dsa_norm, dsa_w_in, dsa_idx_k_norm, dsa_w_o, ffn_norm, ffn_w_up, ffn_conv_w, ffn_conv_b, ffn_w_down, final_norm):
    batch, seq, d = x.shape
    assert d == D_MODEL
    depth = ffn_norm.shape[0]
    t_real = seq + N_META
    t_pad = _padded_len(t_real)
    assert t_pad % (K_SUBS * K_TILE) == 0 and t_pad % Q_TILE == 0
    rows = batch * t_pad

    meta = jnp.broadcast_to(meta_tokens[None].astype(x.dtype), (batch, N_META, d))
    h = jnp.concatenate([meta, x, jnp.zeros((batch, t_pad - t_real, d), x.dtype)], axis=1)
    h = h.reshape(rows, d)
    cos128, sin128 = _rope_tables(t_pad)
    steps = _attention_steps(batch, t_pad)

    dsa_qkv = N_HEADS * HEAD_DIM
    idx_cols = IDX_HEADS * HEAD_DIM
    for i in range(depth):
        j = i // 2
        if i % 2 == 0:
            lambda_init = 0.8 - 0.6 * math.exp(-0.3 * i)
            q_t, k, v_t = _project(h, da_norm[j], da_w_qkv[j].astype(BF16), cos128, sin128, t_pad)
            lam_rows = jnp.zeros((8, LANES), F32).at[0:4, 0:HEAD_DIM].set(
                jnp.stack([da_lambda_q1[j], da_lambda_k1[j], da_lambda_q2[j], da_lambda_k2[j]]).astype(F32))
            o = _diff_attention(q_t, k, v_t, lam_rows, da_subln[j].reshape(LANES, 1).astype(F32),
                                steps, lambda_init)
            h = _out_project(h, o, da_w_o[j].astype(BF16))
        else:
            w_in = dsa_w_in[j]
            n_main = 3 * dsa_qkv + idx_cols
            w_tail = jnp.zeros((d, LANES), w_in.dtype).at[:, :w_in.shape[1] - n_main].set(w_in[:, n_main:])
            gk = jnp.ones((1, LANES), F32).at[0, :HEAD_DIM].set(dsa_idx_k_norm[j].astype(F32))
            q_t, k, v_t, qi_t, kidx, wi_t = _project(
                h, dsa_norm[j], w_in[:, :n_main].astype(BF16), cos128, sin128, t_pad,
                w_tail=w_tail.astype(BF16), gk=gk)
            topk = min(TOPK_MAX, seq // 4)
            o = _dsa_attention(q_t, k, v_t, qi_t, wi_t, kidx, steps, topk, t_pad)
            h = _out_project(h, o, dsa_w_o[j].astype(BF16))
        h = _ffn(h, ffn_norm[i], _interleave_gate_val(ffn_w_up[i]).astype(BF16),
                 _interleave_gate_val(ffn_conv_w[i]).astype(F32),
                 _interleave_gate_val(ffn_conv_b[i]).reshape(1, -1).astype(F32),
                 ffn_w_down[i].astype(BF16), final_norm, t_pad, final=(i == depth - 1))
    return h.reshape(batch, t_pad, d)[:, N_META:N_META + seq]
```

```python
import functools
import math

import numpy as np
import jax
import jax.numpy as jnp
from jax import lax
from jax.experimental import pallas as pl
from jax.experimental.pallas import tpu as pltpu

D_MODEL = 1024
N_META = 16
ROPE_THETA = 10000.0
RMS_EPS = 1e-6
HEAD_DIM = 64
N_PAIRS = 8
N_HEADS = 2 * N_PAIRS
IDX_HEADS = 8
IDX_PAIRS = IDX_HEADS // 2
TOPK_MAX = 256
BIG_SCORE = 1e30
LOG2_E = 1.4426950408889634
FFN_HIDDEN = 2816
CONV_WIDTH = 3

LANES = 128
BF16_ROWS = 16
ROW_TILE = 512
Q_TILE = 256
K_TILE = 512
K_SUBS = 3
FFN_TILE = 256
HALO = BF16_ROWS
VMEM_LIMIT = 56 * 1024 * 1024

F32 = jnp.float32
BF16 = jnp.bfloat16
INT_MIN = -2147483648
KEY_NEG_INF = INT_MIN + 0x7FFFFF
BF16_MIN_NORMAL_BITS = 0x0080
F32_MIN_NORMAL = 1.1754943508222875e-38


def _padded_len(t):
    unit = math.lcm(ROW_TILE, K_SUBS * K_TILE, Q_TILE)
    return ((t + unit - 1) // unit) * unit


def _rms(x, gain):
    ms = jnp.mean(x * x, axis=-1, keepdims=True)
    return x * lax.rsqrt(ms + RMS_EPS) * gain


def _rope_lanes(y, cos, sin, lo):
    sw = jnp.where(lo, pltpu.roll(y, LANES - HEAD_DIM // 2, 1), pltpu.roll(y, HEAD_DIM // 2, 1))
    return y * cos + sw * sin


def _proj_kernel(x_ref, g_ref, w_ref, cos_ref, sin_ref, *rest, has_idx):
    if has_idx:
        wt_ref, gk_ref, qt_ref, k_ref, vt_ref, qit_ref, kidx_ref, wit_ref = rest
    else:
        qt_ref, k_ref, vt_ref = rest
    xh = _rms(x_ref[...], g_ref[...]).astype(BF16)
    cos = cos_ref[...]
    sin = sin_ref[...]
    lane = lax.broadcasted_iota(jnp.int32, cos.shape, 1)
    lo = (lane & (HEAD_DIM // 2)) == 0
    n_groups = (3 * N_PAIRS + IDX_PAIRS) if has_idx else 3 * N_PAIRS
    cw = 4 * LANES
    ones = jnp.ones((BF16_ROWS, ROW_TILE), BF16)
    for c in range(n_groups // 4):
        y = jnp.dot(xh, w_ref[:, c * cw:(c + 1) * cw], preferred_element_type=F32)
        for s in range(4):
            g = c * 4 + s
            yg = y[:, s * LANES:(s + 1) * LANES]
            if g < N_PAIRS:
                yg = _rope_lanes(yg, cos, sin, lo) * (HEAD_DIM ** -0.5 * LOG2_E)
                qt_ref[g] = yg.T.astype(BF16)
            elif g < 2 * N_PAIRS:
                k_ref[g - N_PAIRS] = _rope_lanes(yg, cos, sin, lo).astype(BF16)
            elif g < 3 * N_PAIRS:
                j = g - 2 * N_PAIRS
                vt_ref[j, 0, 0:2 * HEAD_DIM, :] = yg.T.astype(BF16)
                vt_ref[j, 0, 2 * HEAD_DIM:, :] = ones
            else:
                yg = _rope_lanes(yg, cos, sin, lo) * (HEAD_DIM ** -0.5)
                qit_ref[g - 3 * N_PAIRS] = yg.T.astype(BF16)
    if has_idx:
        t = jnp.dot(xh, wt_ref[...], preferred_element_type=F32)
        is_k = lane < HEAD_DIM
        ms = jnp.sum(jnp.where(is_k, t * t, 0.0), axis=-1, keepdims=True) * (1.0 / HEAD_DIM)
        kn = t * lax.rsqrt(ms + RMS_EPS) * gk_ref[...]
        kn = _rope_lanes(kn, cos, sin, lo)
        kidx_ref[...] = jnp.where(is_k, kn, pltpu.roll(kn, HEAD_DIM, 1)).astype(BF16)
        wit_ref[...] = (t * (IDX_HEADS ** -0.5)).T[HEAD_DIM:HEAD_DIM + IDX_HEADS, :]


def _project(h, gain, w, cos, sin, t_pad, w_tail=None, gk=None):
    assert ROW_TILE == K_TILE
    rows = h.shape[0]
    blocks_per_batch = t_pad // ROW_TILE
    has_idx = w_tail is not None
    vrows = 2 * HEAD_DIM + BF16_ROWS
    in_specs = [
        pl.BlockSpec((ROW_TILE, D_MODEL), lambda i: (i, 0)),
        pl.BlockSpec((1, D_MODEL), lambda i: (0, 0)),
        pl.BlockSpec(w.shape, lambda i: (0, 0)),
        pl.BlockSpec((ROW_TILE, LANES), lambda i: (i % blocks_per_batch, 0)),
        pl.BlockSpec((ROW_TILE, LANES), lambda i: (i % blocks_per_batch, 0)),
    ]
    out_shape = [jax.ShapeDtypeStruct((N_PAIRS, LANES, rows), BF16),
                 jax.ShapeDtypeStruct((N_PAIRS, rows, LANES), BF16),
                 jax.ShapeDtypeStruct((N_PAIRS, rows // K_TILE, vrows, K_TILE), BF16)]
    out_specs = [pl.BlockSpec((N_PAIRS, LANES, ROW_TILE), lambda i: (0, 0, i)),
                 pl.BlockSpec((N_PAIRS, ROW_TILE, LANES), lambda i: (0, i, 0)),
                 pl.BlockSpec((N_PAIRS, 1, vrows, K_TILE), lambda i: (0, i, 0, 0))]
    args = [h, gain.reshape(1, D_MODEL), w, cos, sin]
    if has_idx:
        in_specs += [pl.BlockSpec(w_tail.shape, lambda i: (0, 0)),
                     pl.BlockSpec((1, LANES), lambda i: (0, 0))]
        out_shape += [jax.ShapeDtypeStruct((IDX_PAIRS, LANES, rows), BF16),
                      jax.ShapeDtypeStruct((rows, LANES), BF16),
                      jax.ShapeDtypeStruct((IDX_HEADS, rows), F32)]
        out_specs += [pl.BlockSpec((IDX_PAIRS, LANES, ROW_TILE), lambda i: (0, 0, i)),
                      pl.BlockSpec((ROW_TILE, LANES), lambda i: (i, 0)),
                      pl.BlockSpec((IDX_HEADS, ROW_TILE), lambda i: (0, i))]
        args += [w_tail, gk]
    return pl.pallas_call(
        functools.partial(_proj_kernel, has_idx=has_idx),
        grid=(rows // ROW_TILE,),
        in_specs=in_specs,
        out_specs=out_specs,
        out_shape=out_shape,
        compiler_params=pltpu.CompilerParams(dimension_semantics=("arbitrary",),
                                             vmem_limit_bytes=VMEM_LIMIT),
        name="proj_idx" if has_idx else "proj",
    )(*args)


def _oproj_kernel(h_ref, o_ref, w_ref, out_ref):
    o = jnp.concatenate([o_ref[j] for j in range(N_PAIRS)], axis=1)
    out_ref[...] = h_ref[...] + jnp.dot(o, w_ref[...], preferred_element_type=F32)


def _out_project(h, o, w):
    rows = h.shape[0]
    return pl.pallas_call(
        _oproj_kernel,
        grid=(rows // ROW_TILE,),
        in_specs=[pl.BlockSpec((ROW_TILE, D_MODEL), lambda i: (i, 0)),
                  pl.BlockSpec((N_PAIRS, ROW_TILE, LANES), lambda i: (0, i, 0)),
                  pl.BlockSpec(w.shape, lambda i: (0, 0))],
        out_specs=pl.BlockSpec((ROW_TILE, D_MODEL), lambda i: (i, 0)),
        out_shape=jax.ShapeDtypeStruct(h.shape, F32),
        compiler_params=pltpu.CompilerParams(dimension_semantics=("arbitrary",),
                                             vmem_limit_bytes=VMEM_LIMIT),
        name="oproj",
    )(h, o, w)


def _ffn_kernel(xp_ref, x_ref, g_ref, wup_ref, cw_ref, cb_ref, wdn_ref, fg_ref, o_ref, u_sc, a_sc,
                *, blocks_per_batch, final):
    i = pl.program_id(0)
    x = x_ref[...]
    xe = jnp.concatenate([xp_ref[...], x], axis=0)
    xh = _rms(xe, g_ref[...])
    row = lax.broadcasted_iota(jnp.int32, (HALO + ROW_TILE, 1), 0)
    keep = jnp.logical_or(row >= HALO, i % blocks_per_batch != 0)
    xh = jnp.where(keep, xh, 0.0).astype(BF16)
    cw2 = 2 * FFN_TILE
    n_chunks = FFN_HIDDEN // FFN_TILE

    def up(c):
        return jnp.dot(xh, wup_ref[:, c * cw2:(c + 1) * cw2], preferred_element_type=F32)

    u_sc[0] = up(0)
    for c in range(n_chunks):
        cur = c % 2
        after = None
        if c + 1 < n_chunks:
            u_sc[1 - cur] = up(c + 1)
            after = _zero_after(u_sc[1 - cur, 0:8, 0:FFN_TILE])
        cwc = cw_ref[:, c * cw2:(c + 1) * cw2]
        conv = cb_ref[:, c * cw2:(c + 1) * cw2]
        for j in range(CONV_WIDTH):
            conv = conv + cwc[j:j + 1, :] * u_sc[cur, pl.ds(HALO - (CONV_WIDTH - 1) + j, ROW_TILE), :]
        gate = conv[:, :FFN_TILE]
        val = conv[:, FFN_TILE:]
        a = gate * jax.nn.sigmoid(gate) * val
        if after is not None:
            cut = (3 * ROW_TILE) // 4
            a = jnp.concatenate([a[:cut], a[cut:] + after], axis=0)
        a_sc[:, c * FFN_TILE:(c + 1) * FFN_TILE] = a.astype(BF16)
    acc = jnp.dot(a_sc[...], wdn_ref[...], preferred_element_type=F32)
    y = x + acc
    if final:
        y = _rms(y, fg_ref[...])
    o_ref[...] = y


def _ffn(h, gain, w_up, conv_w, conv_b, w_down, final_gain, t_pad, final):
    rows = h.shape[0]
    blocks_per_batch = t_pad // ROW_TILE
    halo_blocks = ROW_TILE // HALO
    return pl.pallas_call(
        functools.partial(_ffn_kernel, blocks_per_batch=blocks_per_batch, final=final),
        grid=(rows // ROW_TILE,),
        in_specs=[pl.BlockSpec((HALO, D_MODEL), lambda i: (jnp.maximum(i * halo_blocks - 1, 0), 0)),
                  pl.BlockSpec((ROW_TILE, D_MODEL), lambda i: (i, 0)),
                  pl.BlockSpec((1, D_MODEL), lambda i: (0, 0)),
                  pl.BlockSpec(w_up.shape, lambda i: (0, 0)),
                  pl.BlockSpec(conv_w.shape, lambda i: (0, 0)),
                  pl.BlockSpec(conv_b.shape, lambda i: (0, 0)),
                  pl.BlockSpec(w_down.shape, lambda i: (0, 0)),
                  pl.BlockSpec((1, D_MODEL), lambda i: (0, 0))],
        out_specs=pl.BlockSpec((ROW_TILE, D_MODEL), lambda i: (i, 0)),
        out_shape=jax.ShapeDtypeStruct(h.shape, F32),
        scratch_shapes=[pltpu.VMEM((2, HALO + ROW_TILE, 2 * FFN_TILE), F32),
                        pltpu.VMEM((ROW_TILE, FFN_HIDDEN), BF16)],
        compiler_params=pltpu.CompilerParams(dimension_semantics=("arbitrary",),
                                             vmem_limit_bytes=VMEM_LIMIT),
        name="ffn_final" if final else "ffn",
    )(h, h, gain.reshape(1, D_MODEL), w_up, conv_w, conv_b, w_down, final_gain.reshape(1, D_MODEL))


def _attention_steps(batch, t_pad):
    nq = t_pad // Q_TILE
    nk = t_pad // (K_SUBS * K_TILE)
    cols = {k: [] for k in ("qrow", "krow", "qloc", "kloc", "nkb", "nsub", "full", "first", "last", "b")}
    for b in range(batch):
        for qi in range(nq):
            n_tiles = ((qi + 1) * Q_TILE + K_TILE - 1) // K_TILE
            n = (n_tiles + K_SUBS - 1) // K_SUBS
            for ki in range(n):
                cols["qrow"].append(b * nq + qi)
                cols["krow"].append(b * nk + ki)
                cols["qloc"].append(qi)
                cols["kloc"].append(ki)
                cols["nkb"].append(n_tiles)
                cols["nsub"].append(min(K_SUBS, n_tiles - ki * K_SUBS))
                cols["full"].append(int((ki + 1) * K_SUBS < n_tiles))
                cols["first"].append(int(ki == 0))
                cols["last"].append(int(ki == n - 1))
                cols["b"].append(b)
    return {k: jnp.asarray(np.asarray(v, np.int32)) for k, v in cols.items()}


def _half_select(x_t, first_half):
    row = lax.broadcasted_iota(jnp.int32, x_t.shape, 0)
    keep = (row < HEAD_DIM) if first_half else (row >= HEAD_DIM)
    return jnp.where(keep, x_t, jnp.zeros_like(x_t))


def _zero_after(x):
    w = lax.bitcast_convert_type(x, jnp.uint32)
    w = lax.shift_right_logical(lax.shift_right_logical(w, jnp.uint32(16)), jnp.uint32(16))
    return lax.bitcast_convert_type(w, F32)[0:1, :]


def _flash_update(s_t, v_t, h, m_sc, acc_sc, after=None):
    m_prev = m_sc[h]
    m_new = jnp.maximum(m_prev, jnp.max(s_t, axis=0, keepdims=True))
    alpha = jnp.exp2(m_prev - m_new)
    if after is None:
        p_t = jnp.exp2(s_t - m_new).astype(BF16)
    else:
        cut = s_t.shape[0] // 2
        p_t = jnp.concatenate([jnp.exp2(s_t[:cut] - m_new),
                               jnp.exp2(s_t[cut:] - (m_new + after))], axis=0).astype(BF16)
    acc_sc[h] = alpha * acc_sc[h] + jnp.dot(v_t, p_t, preferred_element_type=F32)
    m_sc[h] = m_new


def _both_halves(x_t):
    return jnp.concatenate([_half_select(x_t, True), _half_select(x_t, False)], axis=1)


def _attention_sweep(qt_ref, k_ref, vt_ref, subs, bias_of, m_sc, acc_sc):
    def scores(sub, j):
        start = sub * K_TILE
        if not isinstance(sub, int):
            start = pl.multiple_of(start, K_TILE)
        return jnp.dot(k_ref[j, pl.ds(start, K_TILE), :], _both_halves(qt_ref[j]), preferred_element_type=F32)

    items = [(sub, j) for sub in subs for j in range(N_PAIRS)]
    biases = {}
    ahead = 2
    pending = [scores(*item) for item in items[:ahead]]
    for i, (sub, j) in enumerate(items):
        s_t = pending.pop(0)
        tile_key = sub if isinstance(sub, int) else "traced"
        if tile_key not in biases:
            biases[tile_key] = bias_of(sub)
        if biases[tile_key] is not None:
            s_t = s_t + biases[tile_key]
        after = None
        if i + ahead < len(items):
            pending.append(scores(*items[i + ahead]))
            after = _zero_after(pending[-1][0:8, :])
        _flash_update(s_t, vt_ref[j, sub], j, m_sc, acc_sc, after=after)


def _diff_attn_kernel(qrow, krow, qloc, kloc, nsub, full, first, last,
                      qt_ref, k_ref, vt_ref, lam_ref, subln_ref, o_ref, m_sc, acc_sc,
                      *, lambda_init):
    p = pl.program_id(0)

    @pl.when(first[p] == 1)
    def _():
        m_sc[...] = jnp.full(m_sc.shape, -jnp.inf, F32)
        acc_sc[...] = jnp.zeros(acc_sc.shape, F32)

    def key_tile(sub, carry):
        tile = kloc[p] * K_SUBS + sub
        on_diagonal = tile * K_TILE + (K_TILE - 1) > qloc[p] * Q_TILE

        @pl.when(on_diagonal)
        def _():
            kpos = tile * K_TILE + lax.broadcasted_iota(jnp.int32, (K_TILE, 2 * Q_TILE), 0)
            qcol = lax.broadcasted_iota(jnp.int32, (K_TILE, 2 * Q_TILE), 1)
            qpos = qloc[p] * Q_TILE + jnp.where(qcol >= Q_TILE, qcol - Q_TILE, qcol)
            bias = jnp.where(kpos <= qpos, 0.0, -jnp.inf).astype(F32)
            _attention_sweep(qt_ref, k_ref, vt_ref, (sub,), lambda _: bias, m_sc, acc_sc)

        @pl.when(jnp.logical_not(on_diagonal))
        def _():
            _attention_sweep(qt_ref, k_ref, vt_ref, (sub,), lambda _: None, m_sc, acc_sc)

        return carry

    @pl.when(full[p] == 1)
    def _():
        _attention_sweep(qt_ref, k_ref, vt_ref, tuple(range(K_SUBS)), lambda _: None, m_sc, acc_sc)

    @pl.when(full[p] == 0)
    def _():
        lax.fori_loop(0, nsub[p], key_tile, 0)

    @pl.when(last[p] == 1)
    def _():
        lam_rows = lam_ref[...]
        lam = (jnp.exp(jnp.sum(lam_rows[0:1] * lam_rows[1:2], axis=-1, keepdims=True))
               - jnp.exp(jnp.sum(lam_rows[2:3] * lam_rows[3:4], axis=-1, keepdims=True))
               + lambda_init)
        vdim = 2 * HEAD_DIM
        for j in range(N_PAIRS):
            a = acc_sc[j]
            a1 = a[:, :Q_TILE]
            a2 = a[:, Q_TILE:]
            o = a1[:vdim] / a1[vdim:vdim + 1] - lam * (a2[:vdim] / a2[vdim:vdim + 1])
            ms = jnp.mean(o * o, axis=0, keepdims=True)
            o = o * lax.rsqrt(ms + RMS_EPS) * subln_ref[...] * (1.0 - lambda_init)
            o_ref[j] = o.T.astype(BF16)


def _diff_attention(q_t, qkv, v_t, lam_rows, subln_col, steps, lambda_init):
    rows = qkv.shape[1]
    n_steps = steps["qrow"].shape[0]
    vrows = v_t.shape[2]

    def im(fn):
        return lambda p, qr, kr, ql, kl, ns, fu, f, l: fn(p, qr, kr)

    grid_spec = pltpu.PrefetchScalarGridSpec(
        num_scalar_prefetch=8,
        grid=(n_steps,),
        in_specs=[
            pl.BlockSpec((N_PAIRS, LANES, Q_TILE), im(lambda p, qr, kr: (0, 0, qr[p]))),
            pl.BlockSpec((N_PAIRS, K_SUBS * K_TILE, LANES), im(lambda p, qr, kr: (0, kr[p], 0))),
            pl.BlockSpec((N_PAIRS, K_SUBS, vrows, K_TILE), im(lambda p, qr, kr: (0, kr[p], 0, 0))),
            pl.BlockSpec((8, LANES), im(lambda p, qr, kr: (0, 0))),
            pl.BlockSpec((LANES, 1), im(lambda p, qr, kr: (0, 0))),
        ],
        out_specs=pl.BlockSpec((N_PAIRS, Q_TILE, LANES), im(lambda p, qr, kr: (0, qr[p], 0))),
        scratch_shapes=[pltpu.VMEM((N_PAIRS, 1, 2 * Q_TILE), F32),
                        pltpu.VMEM((N_PAIRS, vrows, 2 * Q_TILE), F32)],
    )
    return pl.pallas_call(
        functools.partial(_diff_attn_kernel, lambda_init=lambda_init),
        grid_spec=grid_spec,
        out_shape=jax.ShapeDtypeStruct((N_PAIRS, rows, LANES), BF16),
        compiler_params=pltpu.CompilerParams(dimension_semantics=("arbitrary",),
                                             vmem_limit_bytes=VMEM_LIMIT),
        name="diff_attn",
    )(steps["qrow"], steps["krow"], steps["qloc"], steps["kloc"], steps["nsub"], steps["full"],
      steps["first"], steps["last"], q_t, qkv, v_t, lam_rows, subln_col)


def _dsa_kernel(qrow, krow, qloc, kloc, nkb, nsub, full, first, last, bidx,
                qt_ref, k_ref, vt_ref, qit_ref, wit_ref, kidx_ref, o_ref,
                key_sc, coarse_sc, thr_sc, m_sc, acc_sc, *, topk):
    p = pl.program_id(0)

    @pl.when(first[p] == 1)
    def _():
        m_sc[...] = jnp.full(m_sc.shape, -jnp.inf, F32)
        acc_sc[...] = jnp.zeros(acc_sc.shape, F32)
        n = nkb[p]
        wi = wit_ref[...]
        qpos = qloc[p] * Q_TILE + lax.broadcasted_iota(jnp.int32, (K_TILE, Q_TILE), 1)
        krow_iota = lax.broadcasted_iota(jnp.int32, (K_TILE, Q_TILE), 0)

        def score_chunk(c, carry):
            kt = kidx_ref[pl.ds(pl.multiple_of(c * K_TILE, K_TILE), K_TILE), :]
            sc = jnp.zeros((K_TILE, Q_TILE), F32)
            for jp in range(IDX_PAIRS):
                qi_t = qit_ref[jp]
                for cc in range(2):
                    raw = jnp.dot(kt, _half_select(qi_t, cc == 0), preferred_element_type=F32)
                    hh = 2 * jp + cc
                    sc = sc + jnp.maximum(raw, 0.0) * wi[hh:hh + 1, :]
            kpos = c * K_TILE + krow_iota
            sc = jnp.where(kpos < N_META, BIG_SCORE, sc)
            sc = jnp.where(kpos <= qpos, sc, -jnp.inf)
            sc = jnp.where(jnp.abs(sc) < F32_MIN_NORMAL, 0.0, sc)
            bits = lax.bitcast_convert_type(sc, jnp.int32)
            key_sc[c] = bits ^ ((bits >> 31) & 0x7FFFFFFF)
            coarse_sc[c] = lax.bitcast_convert_type(bits & jnp.int32(-65536), F32).astype(BF16)
            return carry

        lax.fori_loop(0, n, score_chunk, 0)

        kf = float(topk)
        q1 = qloc[p] * Q_TILE + lax.broadcasted_iota(jnp.int32, (1, Q_TILE), 1)
        settled0 = (q1 < topk).astype(jnp.int32)
        coarse_bits = 16

        def coarse_body(b, state):
            lo, settled = state
            cand = lo + jnp.left_shift(jnp.int32(1), 31 - b)
            top = cand >> 16
            top = jnp.where(jnp.logical_and(top > 0, top < BF16_MIN_NORMAL_BITS), BF16_MIN_NORMAL_BITS, top)
            cbits = jnp.left_shift(top ^ ((top >> 31) & 0x7FFF), 16)
            cand16 = lax.bitcast_convert_type(cbits, F32).astype(BF16)

            def count_chunk(c, cnt16):
                hit = jnp.where(coarse_sc[c] >= cand16, jnp.ones((), BF16), jnp.zeros((), BF16))
                parts = [hit[i * BF16_ROWS:(i + 1) * BF16_ROWS] for i in range(K_TILE // BF16_ROWS)]
                while len(parts) > 1:
                    parts = [parts[i] + parts[i + 1] for i in range(0, len(parts), 2)]
                return cnt16 + parts[0].astype(F32)

            cnt16 = lax.fori_loop(0, n, count_chunk, jnp.zeros((BF16_ROWS, Q_TILE), F32))
            cnt = jnp.sum(cnt16, axis=0, keepdims=True)
            return jnp.where(cnt >= kf, cand, lo), jnp.where(cnt == kf, 1, settled)

        lo, settled = lax.fori_loop(0, coarse_bits, coarse_body,
                                    (jnp.full((1, Q_TILE), INT_MIN, jnp.int32), settled0))

        def unsettled(state):
            b, _, settled = state
            return jnp.logical_and(b < 32, jnp.min(settled) == 0)

        def count_where(hit_of):
            def count_chunk(c, cnt8):
                hit = jnp.where(hit_of(c, key_sc[c]), 1.0, 0.0).reshape(8, K_TILE // 64, 8, Q_TILE)
                return cnt8 + jnp.sum(jnp.sum(hit, axis=1), axis=0)

            cnt8 = lax.fori_loop(0, n, count_chunk, jnp.zeros((8, Q_TILE), F32))
            return jnp.sum(cnt8, axis=0, keepdims=True)

        def fine_body(state):
            b, lo, settled = state
            cand = lo + jnp.left_shift(jnp.int32(1), 31 - b)
            cnt = count_where(lambda c, keys: keys >= cand)
            return b + 1, jnp.where(cnt >= kf, cand, lo), jnp.where(cnt == kf, 1, settled)

        _, thr, settled = lax.while_loop(unsettled, fine_body, (jnp.int32(coarse_bits), lo, settled))
        thr = jnp.maximum(thr, KEY_NEG_INF + 1)
        thr_sc[...] = thr

        @pl.when(jnp.min(settled) == 0)
        def _():
            surplus = count_where(lambda c, keys: keys >= thr) - kf
            kk = lax.broadcasted_iota(jnp.int32, (K_TILE, K_TILE), 0)
            kc = lax.broadcasted_iota(jnp.int32, (K_TILE, K_TILE), 1)
            later_or_same = jnp.where(kc >= kk, 1.0, 0.0).astype(BF16)

            def demote(i, seen):
                c = n - 1 - i
                keys = key_sc[c]
                tie = keys == thr
                from_end = seen + jnp.dot(later_or_same, jnp.where(tie, 1.0, 0.0).astype(BF16),
                                          preferred_element_type=F32)
                drop = jnp.logical_and(tie, from_end <= surplus)
                key_sc[c] = jnp.where(drop, keys - 1, keys)
                return from_end[0:1, :]

            lax.fori_loop(0, n, demote, jnp.zeros((1, Q_TILE), F32))

    def mask_bias(sub):
        sel = key_sc[kloc[p] * K_SUBS + sub] >= thr_sc[...]
        bias1 = jnp.where(sel, 0.0, -jnp.inf).astype(F32)
        return jnp.concatenate([bias1, bias1], axis=1)

    def key_tile(sub, carry):
        _attention_sweep(qt_ref, k_ref, vt_ref, (sub,), mask_bias, m_sc, acc_sc)
        return carry

    @pl.when(full[p] == 1)
    def _():
        _attention_sweep(qt_ref, k_ref, vt_ref, tuple(range(K_SUBS)), mask_bias, m_sc, acc_sc)

    @pl.when(full[p] == 0)
    def _():
        lax.fori_loop(0, nsub[p], key_tile, 0)

    @pl.when(last[p] == 1)
    def _():
        ones_row = 2 * HEAD_DIM
        for j in range(N_PAIRS):
            a = acc_sc[j]
            o = jnp.concatenate([a[:HEAD_DIM, :Q_TILE] / a[ones_row:ones_row + 1, :Q_TILE],
                                 a[HEAD_DIM:ones_row, Q_TILE:] / a[ones_row:ones_row + 1, Q_TILE:]],
                                axis=0)
            o_ref[j] = o.T.astype(BF16)


def _dsa_attention(q_t, qkvi, v_t, qi_t, wi_t, kidx, steps, topk, t_pad):
    rows = qkvi.shape[1]
    n_steps = steps["qrow"].shape[0]
    nkb_total = t_pad // K_TILE
    vrows = v_t.shape[2]

    def im(fn):
        return lambda p, qr, kr, ql, kl, nk, ns, fu, f, l, b: fn(p, qr, kr, b)

    grid_spec = pltpu.PrefetchScalarGridSpec(
        num_scalar_prefetch=10,
        grid=(n_steps,),
        in_specs=[
            pl.BlockSpec((N_PAIRS, LANES, Q_TILE), im(lambda p, qr, kr, b: (0, 0, qr[p]))),
            pl.BlockSpec((N_PAIRS, K_SUBS * K_TILE, LANES), im(lambda p, qr, kr, b: (0, kr[p], 0))),
            pl.BlockSpec((N_PAIRS, K_SUBS, vrows, K_TILE), im(lambda p, qr, kr, b: (0, kr[p], 0, 0))),
            pl.BlockSpec((IDX_PAIRS, LANES, Q_TILE), im(lambda p, qr, kr, b: (0, 0, qr[p]))),
            pl.BlockSpec((IDX_HEADS, Q_TILE), im(lambda p, qr, kr, b: (0, qr[p]))),
            pl.BlockSpec((t_pad, LANES), im(lambda p, qr, kr, b: (b[p], 0))),
        ],
        out_specs=pl.BlockSpec((N_PAIRS, Q_TILE, LANES), im(lambda p, qr, kr, b: (0, qr[p], 0))),
        scratch_shapes=[pltpu.VMEM((nkb_total, K_TILE, Q_TILE), jnp.int32),
                        pltpu.VMEM((nkb_total, K_TILE, Q_TILE), BF16),
                        pltpu.VMEM((1, Q_TILE), jnp.int32),
                        pltpu.VMEM((N_PAIRS, 1, 2 * Q_TILE), F32),
                        pltpu.VMEM((N_PAIRS, vrows, 2 * Q_TILE), F32)],
    )
    return pl.pallas_call(
        functools.partial(_dsa_kernel, topk=topk),
        grid_spec=grid_spec,
        out_shape=jax.ShapeDtypeStruct((N_PAIRS, rows, LANES), BF16),
        compiler_params=pltpu.CompilerParams(dimension_semantics=("arbitrary",),
                                             vmem_limit_bytes=VMEM_LIMIT),
        name="dsa_attn",
    )(steps["qrow"], steps["krow"], steps["qloc"], steps["kloc"], steps["nkb"], steps["nsub"],
      steps["full"], steps["first"], steps["last"], steps["b"], q_t, qkvi, v_t, qi_t, wi_t, kidx)


def _rope_tables(t_pad):
    inv = ROPE_THETA ** (-jnp.arange(0, HEAD_DIM, 2, dtype=F32) / HEAD_DIM)
    ang = jnp.arange(t_pad, dtype=F32)[:, None] * inv[None, :]
    cos, sin = jnp.cos(ang), jnp.sin(ang)
    cos128 = jnp.tile(cos, (1, LANES // (HEAD_DIM // 2)))
    sin128 = jnp.tile(jnp.concatenate([-sin, sin], axis=1), (1, LANES // HEAD_DIM))
    return cos128, sin128


def _interleave_gate_val(a):
    lead = a.shape[:-1]
    a = a.reshape(lead + (2, FFN_HIDDEN // FFN_TILE, FFN_TILE))
    a = jnp.swapaxes(a, -3, -2)
    return a.reshape(lead + (2 * FFN_HIDDEN,))


def _with_ones_rows(v_t):
    ones = jnp.ones((v_t.shape[0], BF16_ROWS, v_t.shape[2]), v_t.dtype)
    return jnp.concatenate([v_t, ones], axis=1)


def _key_tiled(v_t):
    pairs, d, rows = v_t.shape
    return jnp.swapaxes(v_t.reshape(pairs, d, rows // K_TILE, K_TILE), 1, 2)


def kernel(x, meta_tokens, da_norm, da_w_qkv, da_lambda_q1, da_lambda_k1, da_lambda_q2, da_lambda_k2, da_subln, da_w_o, dsa_norm, dsa_w_in, dsa_idx_k_norm, dsa_w_o, ffn_norm, ffn_w_up, ffn_conv_w, ffn_conv_b, ffn_w_down, final_norm):
    batch, seq, d = x.shape
    assert d == D_MODEL
    depth = ffn_norm.shape[0]
    t_real = seq + N_META
    t_pad = _padded_len(t_real)
    assert t_pad % (K_SUBS * K_TILE) == 0 and t_pad % Q_TILE == 0
    rows = batch * t_pad

    meta = jnp.broadcast_to(meta_tokens[None].astype(x.dtype), (batch, N_META, d))
    h = jnp.concatenate([meta, x, jnp.zeros((batch, t_pad - t_real, d), x.dtype)], axis=1)
    h = h.reshape(rows, d)
    cos128, sin128 = _rope_tables(t_pad)
    steps = _attention_steps(batch, t_pad)

    dsa_qkv = N_HEADS * HEAD_DIM
    idx_cols = IDX_HEADS * HEAD_DIM
    for i in range(depth):
        j = i // 2
        if i % 2 == 0:
            lambda_init = 0.8 - 0.6 * math.exp(-0.3 * i)
            q_t, k, v_t = _project(h, da_norm[j], da_w_qkv[j].astype(BF16), cos128, sin128, t_pad)
            lam_rows = jnp.zeros((8, LANES), F32).at[0:4, 0:HEAD_DIM].set(
                jnp.stack([da_lambda_q1[j], da_lambda_k1[j], da_lambda_q2[j], da_lambda_k2[j]]).astype(F32))
            o = _diff_attention(q_t, k, v_t, lam_rows, da_subln[j].reshape(LANES, 1).astype(F32),
                                steps, lambda_init)
            h = _out_project(h, o, da_w_o[j].astype(BF16))
        else:
            w_in = dsa_w_in[j]
            n_main = 3 * dsa_qkv + idx_cols
            w_tail = jnp.zeros((d, LANES), w_in.dtype).at[:, :w_in.shape[1] - n_main].set(w_in[:, n_main:])
            gk = jnp.ones((1, LANES), F32).at[0, :HEAD_DIM].set(dsa_idx_k_norm[j].astype(F32))
            q_t, k, v_t, qi_t, kidx, wi_t = _project(
                h, dsa_norm[j], w_in[:, :n_main].astype(BF16), cos128, sin128, t_pad,
                w_tail=w_tail.astype(BF16), gk=gk)
            topk = min(TOPK_MAX, seq // 4)
            o = _dsa_attention(q_t, k, v_t, qi_t, wi_t, kidx, steps, topk, t_pad)
            h = _out_project(h, o, dsa_w_o[j].astype(BF16))
        h = _ffn(h, ffn_norm[i], _interleave_gate_val(ffn_w_up[i]).astype(BF16),
                 _interleave_gate_val(ffn_conv_w[i]).astype(F32),
                 _interleave_gate_val(ffn_conv_b[i]).reshape(1, -1).astype(F32),
                 ffn_w_down[i].astype(BF16), final_norm, t_pad, final=(i == depth - 1))
    return h.reshape(batch, t_pad, d)[:, N_META:N_META + seq]
```

```python
import functools
import math

import numpy as np
import jax
import jax.numpy as jnp
from jax import lax
from jax.experimental import pallas as pl
from jax.experimental.pallas import tpu as pltpu

D_MODEL = 1024
N_META = 16
ROPE_THETA = 10000.0
RMS_EPS = 1e-6
HEAD_DIM = 64
N_PAIRS = 8
N_HEADS = 2 * N_PAIRS
IDX_HEADS = 8
IDX_PAIRS = IDX_HEADS // 2
TOPK_MAX = 256
BIG_SCORE = 1e30
LOG2_E = 1.4426950408889634
FFN_HIDDEN = 2816
CONV_WIDTH = 3

LANES = 128
BF16_ROWS = 16
ROW_TILE = 512
Q_TILE = 256
K_TILE = 512
K_SUBS = 3
FFN_TILE = 256
HALO = BF16_ROWS
VMEM_LIMIT = 56 * 1024 * 1024

F32 = jnp.float32
BF16 = jnp.bfloat16
INT_MIN = -2147483648
KEY_NEG_INF = INT_MIN + 0x7FFFFF
BF16_MIN_NORMAL_BITS = 0x0080
F32_MIN_NORMAL = 1.1754943508222875e-38


def _padded_len(t):
    unit = math.lcm(ROW_TILE, K_SUBS * K_TILE, Q_TILE)
    return ((t + unit - 1) // unit) * unit


def _rms(x, gain):
    ms = jnp.mean(x * x, axis=-1, keepdims=True)
    return x * lax.rsqrt(ms + RMS_EPS) * gain


def _rope_lanes(y, cos, sin, lo):
    sw = jnp.where(lo, pltpu.roll(y, LANES - HEAD_DIM // 2, 1), pltpu.roll(y, HEAD_DIM // 2, 1))
    return y * cos + sw * sin


def _proj_kernel(x_ref, g_ref, w_ref, cos_ref, sin_ref, *rest, has_idx):
    if has_idx:
        wt_ref, gk_ref, qt_ref, k_ref, vt_ref, qit_ref, kidx_ref, wit_ref = rest
    else:
        qt_ref, k_ref, vt_ref = rest
    xh = _rms(x_ref[...], g_ref[...]).astype(BF16)
    cos = cos_ref[...]
    sin = sin_ref[...]
    lane = lax.broadcasted_iota(jnp.int32, cos.shape, 1)
    lo = (lane & (HEAD_DIM // 2)) == 0
    n_groups = (3 * N_PAIRS + IDX_PAIRS) if has_idx else 3 * N_PAIRS
    cw = 4 * LANES
    ones = jnp.ones((BF16_ROWS, ROW_TILE), BF16)
    for c in range(n_groups // 4):
        y = jnp.dot(xh, w_ref[:, c * cw:(c + 1) * cw], preferred_element_type=F32)
        for s in range(4):
            g = c * 4 + s
            yg = y[:, s * LANES:(s + 1) * LANES]
            if g < N_PAIRS:
                yg = _rope_lanes(yg, cos, sin, lo) * (HEAD_DIM ** -0.5 * LOG2_E)
                qt_ref[g] = yg.T.astype(BF16)
            elif g < 2 * N_PAIRS:
                k_ref[g - N_PAIRS] = _rope_lanes(yg, cos, sin, lo).astype(BF16)
            elif g < 3 * N_PAIRS:
                j = g - 2 * N_PAIRS
                vt_ref[j, 0, 0:2 * HEAD_DIM, :] = yg.T.astype(BF16)
                vt_ref[j, 0, 2 * HEAD_DIM:, :] = ones
            else:
                yg = _rope_lanes(yg, cos, sin, lo) * (HEAD_DIM ** -0.5)
                qit_ref[g - 3 * N_PAIRS] = yg.T.astype(BF16)
    if has_idx:
        t = jnp.dot(xh, wt_ref[...], preferred_element_type=F32)
        is_k = lane < HEAD_DIM
        ms = jnp.sum(jnp.where(is_k, t * t, 0.0), axis=-1, keepdims=True) * (1.0 / HEAD_DIM)
        kn = t * lax.rsqrt(ms + RMS_EPS) * gk_ref[...]
        kn = _rope_lanes(kn, cos, sin, lo)
        kidx_ref[...] = jnp.where(is_k, kn, pltpu.roll(kn, HEAD_DIM, 1)).astype(BF16)
        wit_ref[...] = (t * (IDX_HEADS ** -0.5)).T[HEAD_DIM:HEAD_DIM + IDX_HEADS, :]


def _project(h, gain, w, cos, sin, t_pad, w_tail=None, gk=None):
    assert ROW_TILE == K_TILE
    rows = h.shape[0]
    blocks_per_batch = t_pad // ROW_TILE
    has_idx = w_tail is not None
    vrows = 2 * HEAD_DIM + BF16_ROWS
    in_specs = [
        pl.BlockSpec((ROW_TILE, D_MODEL), lambda i: (i, 0)),
        pl.BlockSpec((1, D_MODEL), lambda i: (0, 0)),
        pl.BlockSpec(w.shape, lambda i: (0, 0)),
        pl.BlockSpec((ROW_TILE, LANES), lambda i: (i % blocks_per_batch, 0)),
        pl.BlockSpec((ROW_TILE, LANES), lambda i: (i % blocks_per_batch, 0)),
    ]
    out_shape = [jax.ShapeDtypeStruct((N_PAIRS, LANES, rows), BF16),
                 jax.ShapeDtypeStruct((N_PAIRS, rows, LANES), BF16),
                 jax.ShapeDtypeStruct((N_PAIRS, rows // K_TILE, vrows, K_TILE), BF16)]
    out_specs = [pl.BlockSpec((N_PAIRS, LANES, ROW_TILE), lambda i: (0, 0, i)),
                 pl.BlockSpec((N_PAIRS, ROW_TILE, LANES), lambda i: (0, i, 0)),
                 pl.BlockSpec((N_PAIRS, 1, vrows, K_TILE), lambda i: (0, i, 0, 0))]
    args = [h, gain.reshape(1, D_MODEL), w, cos, sin]
    if has_idx:
        in_specs += [pl.BlockSpec(w_tail.shape, lambda i: (0, 0)),
                     pl.BlockSpec((1, LANES), lambda i: (0, 0))]
        out_shape += [jax.ShapeDtypeStruct((IDX_PAIRS, LANES, rows), BF16),
                      jax.ShapeDtypeStruct((rows, LANES), BF16),
                      jax.ShapeDtypeStruct((IDX_HEADS, rows), F32)]
        out_specs += [pl.BlockSpec((IDX_PAIRS, LANES, ROW_TILE), lambda i: (0, 0, i)),
                      pl.BlockSpec((ROW_TILE, LANES), lambda i: (i, 0)),
                      pl.BlockSpec((IDX_HEADS, ROW_TILE), lambda i: (0, i))]
        args += [w_tail, gk]
    return pl.pallas_call(
        functools.partial(_proj_kernel, has_idx=has_idx),
        grid=(rows // ROW_TILE,),
        in_specs=in_specs,
        out_specs=out_specs,
        out_shape=out_shape,
        compiler_params=pltpu.CompilerParams(dimension_semantics=("arbitrary",),
                                             vmem_limit_bytes=VMEM_LIMIT),
        name="proj_idx" if has_idx else "proj",
    )(*args)


def _oproj_kernel(h_ref, o_ref, w_ref, out_ref):
    o = jnp.concatenate([o_ref[j] for j in range(N_PAIRS)], axis=1)
    out_ref[...] = h_ref[...] + jnp.dot(o, w_ref[...], preferred_element_type=F32)


def _out_project(h, o, w):
    rows = h.shape[0]
    return pl.pallas_call(
        _oproj_kernel,
        grid=(rows // ROW_TILE,),
        in_specs=[pl.BlockSpec((ROW_TILE, D_MODEL), lambda i: (i, 0)),
                  pl.BlockSpec((N_PAIRS, ROW_TILE, LANES), lambda i: (0, i, 0)),
                  pl.BlockSpec(w.shape, lambda i: (0, 0))],
        out_specs=pl.BlockSpec((ROW_TILE, D_MODEL), lambda i: (i, 0)),
        out_shape=jax.ShapeDtypeStruct(h.shape, F32),
        compiler_params=pltpu.CompilerParams(dimension_semantics=("arbitrary",),
                                             vmem_limit_bytes=VMEM_LIMIT),
        name="oproj",
    )(h, o, w)


def _ffn_kernel(xp_ref, x_ref, g_ref, wup_ref, cw_ref, cb_ref, wdn_ref, fg_ref, o_ref, u_sc, a_sc,
                *, blocks_per_batch, final):
    i = pl.program_id(0)
    x = x_ref[...]
    xe = jnp.concatenate([xp_ref[...], x], axis=0)
    xh = _rms(xe, g_ref[...])
    row = lax.broadcasted_iota(jnp.int32, (HALO + ROW_TILE, 1), 0)
    keep = jnp.logical_or(row >= HALO, i % blocks_per_batch != 0)
    xh = jnp.where(keep, xh, 0.0).astype(BF16)
    cw2 = 2 * FFN_TILE
    n_chunks = FFN_HIDDEN // FFN_TILE

    def up(c):
        return jnp.dot(xh, wup_ref[:, c * cw2:(c + 1) * cw2], preferred_element_type=F32)

    u_sc[0] = up(0)
    for c in range(n_chunks):
        cur = c % 2
        after = None
        if c + 1 < n_chunks:
            u_sc[1 - cur] = up(c + 1)
            after = _zero_after(u_sc[1 - cur, 0:8, 0:FFN_TILE])
        cwc = cw_ref[:, c * cw2:(c + 1) * cw2]
        conv = cb_ref[:, c * cw2:(c + 1) * cw2]
        for j in range(CONV_WIDTH):
            conv = conv + cwc[j:j + 1, :] * u_sc[cur, pl.ds(HALO - (CONV_WIDTH - 1) + j, ROW_TILE), :]
        gate = conv[:, :FFN_TILE]
        val = conv[:, FFN_TILE:]
        a = gate * jax.nn.sigmoid(gate) * val
        if after is not None:
            cut = (3 * ROW_TILE) // 4
            a = jnp.concatenate([a[:cut], a[cut:] + after], axis=0)
        a_sc[:, c * FFN_TILE:(c + 1) * FFN_TILE] = a.astype(BF16)
    acc = jnp.dot(a_sc[...], wdn_ref[...], preferred_element_type=F32)
    y = x + acc
    if final:
        y = _rms(y, fg_ref[...])
    o_ref[...] = y


def _ffn(h, gain, w_up, conv_w, conv_b, w_down, final_gain, t_pad, final):
    rows = h.shape[0]
    blocks_per_batch = t_pad // ROW_TILE
    halo_blocks = ROW_TILE // HALO
    return pl.pallas_call(
        functools.partial(_ffn_kernel, blocks_per_batch=blocks_per_batch, final=final),
        grid=(rows // ROW_TILE,),
        in_specs=[pl.BlockSpec((HALO, D_MODEL), lambda i: (jnp.maximum(i * halo_blocks - 1, 0), 0)),
                  pl.BlockSpec((ROW_TILE, D_MODEL), lambda i: (i, 0)),
                  pl.BlockSpec((1, D_MODEL), lambda i: (0, 0)),
                  pl.BlockSpec(w_up.shape, lambda i: (0, 0)),
                  pl.BlockSpec(conv_w.shape, lambda i: (0, 0)),
                  pl.BlockSpec(conv_b.shape, lambda i: (0, 0)),
                  pl.BlockSpec(w_down.shape, lambda i: (0, 0)),
                  pl.BlockSpec((1, D_MODEL), lambda i: (0, 0))],
        out_specs=pl.BlockSpec((ROW_TILE, D_MODEL), lambda i: (i, 0)),
        out_shape=jax.ShapeDtypeStruct(h.shape, F32),
        scratch_shapes=[pltpu.VMEM((2, HALO + ROW_TILE, 2 * FFN_TILE), F32),
                        pltpu.VMEM((ROW_TILE, FFN_HIDDEN), BF16)],
        compiler_params=pltpu.CompilerParams(dimension_semantics=("arbitrary",),
                                             vmem_limit_bytes=VMEM_LIMIT),
        name="ffn_final" if final else "ffn",
    )(h, h, gain.reshape(1, D_MODEL), w_up, conv_w, conv_b, w_down, final_gain.reshape(1, D_MODEL))


def _attention_steps(batch, t_pad):
    nq = t_pad // Q_TILE
    nk = t_pad // (K_SUBS * K_TILE)
    cols = {k: [] for k in ("qrow", "krow", "qloc", "kloc", "nkb", "nsub", "full", "first", "last", "b")}
    for b in range(batch):
        for qi in range(nq):
            n_tiles = ((qi + 1) * Q_TILE + K_TILE - 1) // K_TILE
            n = (n_tiles + K_SUBS - 1) // K_SUBS
            for ki in range(n):
                cols["qrow"].append(b * nq + qi)
                cols["krow"].append(b * nk + ki)
                cols["qloc"].append(qi)
                cols["kloc"].append(ki)
                cols["nkb"].append(n_tiles)
                cols["nsub"].append(min(K_SUBS, n_tiles - ki * K_SUBS))
                cols["full"].append(int((ki + 1) * K_SUBS < n_tiles))
                cols["first"].append(int(ki == 0))
                cols["last"].append(int(ki == n - 1))
                cols["b"].append(b)
    return {k: jnp.asarray(np.asarray(v, np.int32)) for k, v in cols.items()}


def _half_select(x_t, first_half):
    row = lax.broadcasted_iota(jnp.int32, x_t.shape, 0)
    keep = (row < HEAD_DIM) if first_half else (row >= HEAD_DIM)
    return jnp.where(keep, x_t, jnp.zeros_like(x_t))


COUNT_UNROLL = 4


def _fori_unrolled(n, body, init):
    def group(i, carry):
        for u in range(COUNT_UNROLL):
            carry = body(COUNT_UNROLL * i + u, carry)
        return carry

    groups = n // COUNT_UNROLL if isinstance(n, int) else lax.div(n, jnp.int32(COUNT_UNROLL))
    return lax.fori_loop(COUNT_UNROLL * groups, n, body, lax.fori_loop(0, groups, group, init))


def _zero_after(x):
    w = lax.bitcast_convert_type(x, jnp.uint32)
    w = lax.shift_right_logical(lax.shift_right_logical(w, jnp.uint32(16)), jnp.uint32(16))
    return lax.bitcast_convert_type(w, F32)[0:1, :]


def _flash_update(s_t, v_t, h, m_sc, acc_sc, after=None):
    m_prev = m_sc[h]
    m_new = jnp.maximum(m_prev, jnp.max(s_t, axis=0, keepdims=True))
    alpha = jnp.exp2(m_prev - m_new)
    if after is None:
        p_t = jnp.exp2(s_t - m_new).astype(BF16)
    else:
        cut = s_t.shape[0] // 2
        p_t = jnp.concatenate([jnp.exp2(s_t[:cut] - m_new),
                               jnp.exp2(s_t[cut:] - (m_new + after))], axis=0).astype(BF16)
    acc_sc[h] = alpha * acc_sc[h] + jnp.dot(v_t, p_t, preferred_element_type=F32)
    m_sc[h] = m_new


def _both_halves(x_t):
    return jnp.concatenate([_half_select(x_t, True), _half_select(x_t, False)], axis=1)


def _attention_sweep(qt_ref, k_ref, vt_ref, subs, bias_of, m_sc, acc_sc):
    def scores(sub, j):
        start = sub * K_TILE
        if not isinstance(sub, int):
            start = pl.multiple_of(start, K_TILE)
        return jnp.dot(k_ref[j, pl.ds(start, K_TILE), :], _both_halves(qt_ref[j]), preferred_element_type=F32)

    items = [(sub, j) for sub in subs for j in range(N_PAIRS)]
    biases = {}
    ahead = 2
    pending = [scores(*item) for item in items[:ahead]]
    for i, (sub, j) in enumerate(items):
        s_t = pending.pop(0)
        tile_key = sub if isinstance(sub, int) else "traced"
        if tile_key not in biases:
            biases[tile_key] = bias_of(sub)
        if biases[tile_key] is not None:
            s_t = s_t + biases[tile_key]
        after = None
        if i + ahead < len(items):
            pending.append(scores(*items[i + ahead]))
            after = _zero_after(pending[-1][0:8, :])
        _flash_update(s_t, vt_ref[j, sub], j, m_sc, acc_sc, after=after)


def _diff_attn_kernel(qrow, krow, qloc, kloc, nsub, full, first, last,
                      qt_ref, k_ref, vt_ref, lam_ref, subln_ref, o_ref, m_sc, acc_sc,
                      *, lambda_init):
    p = pl.program_id(0)

    @pl.when(first[p] == 1)
    def _():
        m_sc[...] = jnp.full(m_sc.shape, -jnp.inf, F32)
        acc_sc[...] = jnp.zeros(acc_sc.shape, F32)

    def key_tile(sub, carry):
        tile = kloc[p] * K_SUBS + sub
        on_diagonal = tile * K_TILE + (K_TILE - 1) > qloc[p] * Q_TILE

        @pl.when(on_diagonal)
        def _():
            kpos = tile * K_TILE + lax.broadcasted_iota(jnp.int32, (K_TILE, 2 * Q_TILE), 0)
            qcol = lax.broadcasted_iota(jnp.int32, (K_TILE, 2 * Q_TILE), 1)
            qpos = qloc[p] * Q_TILE + jnp.where(qcol >= Q_TILE, qcol - Q_TILE, qcol)
            bias = jnp.where(kpos <= qpos, 0.0, -jnp.inf).astype(F32)
            _attention_sweep(qt_ref, k_ref, vt_ref, (sub,), lambda _: bias, m_sc, acc_sc)

        @pl.when(jnp.logical_not(on_diagonal))
        def _():
            _attention_sweep(qt_ref, k_ref, vt_ref, (sub,), lambda _: None, m_sc, acc_sc)

        return carry

    @pl.when(full[p] == 1)
    def _():
        _attention_sweep(qt_ref, k_ref, vt_ref, tuple(range(K_SUBS)), lambda _: None, m_sc, acc_sc)

    @pl.when(full[p] == 0)
    def _():
        lax.fori_loop(0, nsub[p], key_tile, 0)

    @pl.when(last[p] == 1)
    def _():
        lam_rows = lam_ref[...]
        lam = (jnp.exp(jnp.sum(lam_rows[0:1] * lam_rows[1:2], axis=-1, keepdims=True))
               - jnp.exp(jnp.sum(lam_rows[2:3] * lam_rows[3:4], axis=-1, keepdims=True))
               + lambda_init)
        vdim = 2 * HEAD_DIM
        for j in range(N_PAIRS):
            a = acc_sc[j]
            a1 = a[:, :Q_TILE]
            a2 = a[:, Q_TILE:]
            o = a1[:vdim] / a1[vdim:vdim + 1] - lam * (a2[:vdim] / a2[vdim:vdim + 1])
            ms = jnp.mean(o * o, axis=0, keepdims=True)
            o = o * lax.rsqrt(ms + RMS_EPS) * subln_ref[...] * (1.0 - lambda_init)
            o_ref[j] = o.T.astype(BF16)


def _diff_attention(q_t, qkv, v_t, lam_rows, subln_col, steps, lambda_init):
    rows = qkv.shape[1]
    n_steps = steps["qrow"].shape[0]
    vrows = v_t.shape[2]

    def im(fn):
        return lambda p, qr, kr, ql, kl, ns, fu, f, l: fn(p, qr, kr)

    grid_spec = pltpu.PrefetchScalarGridSpec(
        num_scalar_prefetch=8,
        grid=(n_steps,),
        in_specs=[
            pl.BlockSpec((N_PAIRS, LANES, Q_TILE), im(lambda p, qr, kr: (0, 0, qr[p]))),
            pl.BlockSpec((N_PAIRS, K_SUBS * K_TILE, LANES), im(lambda p, qr, kr: (0, kr[p], 0))),
            pl.BlockSpec((N_PAIRS, K_SUBS, vrows, K_TILE), im(lambda p, qr, kr: (0, kr[p], 0, 0))),
            pl.BlockSpec((8, LANES), im(lambda p, qr, kr: (0, 0))),
            pl.BlockSpec((LANES, 1), im(lambda p, qr, kr: (0, 0))),
        ],
        out_specs=pl.BlockSpec((N_PAIRS, Q_TILE, LANES), im(lambda p, qr, kr: (0, qr[p], 0))),
        scratch_shapes=[pltpu.VMEM((N_PAIRS, 1, 2 * Q_TILE), F32),
                        pltpu.VMEM((N_PAIRS, vrows, 2 * Q_TILE), F32)],
    )
    return pl.pallas_call(
        functools.partial(_diff_attn_kernel, lambda_init=lambda_init),
        grid_spec=grid_spec,
        out_shape=jax.ShapeDtypeStruct((N_PAIRS, rows, LANES), BF16),
        compiler_params=pltpu.CompilerParams(dimension_semantics=("arbitrary",),
                                             vmem_limit_bytes=VMEM_LIMIT),
        name="diff_attn",
    )(steps["qrow"], steps["krow"], steps["qloc"], steps["kloc"], steps["nsub"], steps["full"],
      steps["first"], steps["last"], q_t, qkv, v_t, lam_rows, subln_col)


def _dsa_kernel(qrow, krow, qloc, kloc, nkb, nsub, full, first, last, bidx,
                qt_ref, k_ref, vt_ref, qit_ref, wit_ref, kidx_ref, o_ref,
                key_sc, coarse_sc, thr_sc, m_sc, acc_sc, *, topk):
    p = pl.program_id(0)

    @pl.when(first[p] == 1)
    def _():
        m_sc[...] = jnp.full(m_sc.shape, -jnp.inf, F32)
        acc_sc[...] = jnp.zeros(acc_sc.shape, F32)
        n = nkb[p]
        wi = wit_ref[...]
        qpos = qloc[p] * Q_TILE + lax.broadcasted_iota(jnp.int32, (K_TILE, Q_TILE), 1)
        krow_iota = lax.broadcasted_iota(jnp.int32, (K_TILE, Q_TILE), 0)

        def score_chunk(c, carry):
            kt = kidx_ref[pl.ds(pl.multiple_of(c * K_TILE, K_TILE), K_TILE), :]
            sc = jnp.zeros((K_TILE, Q_TILE), F32)
            for jp in range(IDX_PAIRS):
                qi_t = qit_ref[jp]
                for cc in range(2):
                    raw = jnp.dot(kt, _half_select(qi_t, cc == 0), preferred_element_type=F32)
                    hh = 2 * jp + cc
                    sc = sc + jnp.maximum(raw, 0.0) * wi[hh:hh + 1, :]
            kpos = c * K_TILE + krow_iota
            sc = jnp.where(kpos < N_META, BIG_SCORE, sc)
            sc = jnp.where(kpos <= qpos, sc, -jnp.inf)
            sc = jnp.where(jnp.abs(sc) < F32_MIN_NORMAL, 0.0, sc)
            bits = lax.bitcast_convert_type(sc, jnp.int32)
            key_sc[c] = bits ^ ((bits >> 31) & 0x7FFFFFFF)
            coarse_sc[c] = lax.bitcast_convert_type(bits & jnp.int32(-65536), F32).astype(BF16)
            return carry

        lax.fori_loop(0, n, score_chunk, 0)

        kf = float(topk)
        q1 = qloc[p] * Q_TILE + lax.broadcasted_iota(jnp.int32, (1, Q_TILE), 1)
        settled0 = (q1 < topk).astype(jnp.int32)
        coarse_bits = 16

        def coarse_body(b, state):
            lo, settled = state
            cand = lo + jnp.left_shift(jnp.int32(1), 31 - b)
            top = cand >> 16
            top = jnp.where(jnp.logical_and(top > 0, top < BF16_MIN_NORMAL_BITS), BF16_MIN_NORMAL_BITS, top)
            cbits = jnp.left_shift(top ^ ((top >> 31) & 0x7FFF), 16)
            cand16 = lax.bitcast_convert_type(cbits, F32).astype(BF16)

            def count_chunk(c, cnt16):
                hit = jnp.where(coarse_sc[c] >= cand16, jnp.ones((), BF16), jnp.zeros((), BF16))
                parts = [hit[i * BF16_ROWS:(i + 1) * BF16_ROWS] for i in range(K_TILE // BF16_ROWS)]
                while len(parts) > 1:
                    parts = [parts[i] + parts[i + 1] for i in range(0, len(parts), 2)]
                return cnt16 + parts[0].astype(F32)

            cnt16 = _fori_unrolled(n, count_chunk, jnp.zeros((BF16_ROWS, Q_TILE), F32))
            cnt = jnp.sum(cnt16, axis=0, keepdims=True)
            return jnp.where(cnt >= kf, cand, lo), jnp.where(cnt == kf, 1, settled)

        lo, settled = lax.fori_loop(0, coarse_bits, coarse_body,
                                    (jnp.full((1, Q_TILE), INT_MIN, jnp.int32), settled0))

        def unsettled(state):
            b, _, settled = state
            return jnp.logical_and(b < 32, jnp.min(settled) == 0)

        def count_where(hit_of):
            def count_chunk(c, cnt8):
                hit = jnp.where(hit_of(c, key_sc[c]), 1.0, 0.0).reshape(8, K_TILE // 64, 8, Q_TILE)
                return cnt8 + jnp.sum(jnp.sum(hit, axis=1), axis=0)

            cnt8 = _fori_unrolled(n, count_chunk, jnp.zeros((8, Q_TILE), F32))
            return jnp.sum(cnt8, axis=0, keepdims=True)

        def fine_body(state):
            b, lo, settled = state
            cand = lo + jnp.left_shift(jnp.int32(1), 31 - b)
            cnt = count_where(lambda c, keys: keys >= cand)
            return b + 1, jnp.where(cnt >= kf, cand, lo), jnp.where(cnt == kf, 1, settled)

        _, thr, settled = lax.while_loop(unsettled, fine_body, (jnp.int32(coarse_bits), lo, settled))
        thr = jnp.maximum(thr, KEY_NEG_INF + 1)
        thr_sc[...] = thr

        @pl.when(jnp.min(settled) == 0)
        def _():
            surplus = count_where(lambda c, keys: keys >= thr) - kf
            kk = lax.broadcasted_iota(jnp.int32, (K_TILE, K_TILE), 0)
            kc = lax.broadcasted_iota(jnp.int32, (K_TILE, K_TILE), 1)
            later_or_same = jnp.where(kc >= kk, 1.0, 0.0).astype(BF16)

            def demote(i, seen):
                c = n - 1 - i
                keys = key_sc[c]
                tie = keys == thr
                from_end = seen + jnp.dot(later_or_same, jnp.where(tie, 1.0, 0.0).astype(BF16),
                                          preferred_element_type=F32)
                drop = jnp.logical_and(tie, from_end <= surplus)
                key_sc[c] = jnp.where(drop, keys - 1, keys)
                return from_end[0:1, :]

            lax.fori_loop(0, n, demote, jnp.zeros((1, Q_TILE), F32))

    def mask_bias(sub):
        sel = key_sc[kloc[p] * K_SUBS + sub] >= thr_sc[...]
        bias1 = jnp.where(sel, 0.0, -jnp.inf).astype(F32)
        return jnp.concatenate([bias1, bias1], axis=1)

    def key_tile(sub, carry):
        _attention_sweep(qt_ref, k_ref, vt_ref, (sub,), mask_bias, m_sc, acc_sc)
        return carry

    @pl.when(full[p] == 1)
    def _():
        _attention_sweep(qt_ref, k_ref, vt_ref, tuple(range(K_SUBS)), mask_bias, m_sc, acc_sc)

    @pl.when(full[p] == 0)
    def _():
        lax.fori_loop(0, nsub[p], key_tile, 0)

    @pl.when(last[p] == 1)
    def _():
        ones_row = 2 * HEAD_DIM
        for j in range(N_PAIRS):
            a = acc_sc[j]
            o = jnp.concatenate([a[:HEAD_DIM, :Q_TILE] / a[ones_row:ones_row + 1, :Q_TILE],
                                 a[HEAD_DIM:ones_row, Q_TILE:] / a[ones_row:ones_row + 1, Q_TILE:]],
                                axis=0)
            o_ref[j] = o.T.astype(BF16)


def _dsa_attention(q_t, qkvi, v_t, qi_t, wi_t, kidx, steps, topk, t_pad):
    rows = qkvi.shape[1]
    n_steps = steps["qrow"].shape[0]
    nkb_total = t_pad // K_TILE
    vrows = v_t.shape[2]

    def im(fn):
        return lambda p, qr, kr, ql, kl, nk, ns, fu, f, l, b: fn(p, qr, kr, b)

    grid_spec = pltpu.PrefetchScalarGridSpec(
        num_scalar_prefetch=10,
        grid=(n_steps,),
        in_specs=[
            pl.BlockSpec((N_PAIRS, LANES, Q_TILE), im(lambda p, qr, kr, b: (0, 0, qr[p]))),
            pl.BlockSpec((N_PAIRS, K_SUBS * K_TILE, LANES), im(lambda p, qr, kr, b: (0, kr[p], 0))),
            pl.BlockSpec((N_PAIRS, K_SUBS, vrows, K_TILE), im(lambda p, qr, kr, b: (0, kr[p], 0, 0))),
            pl.BlockSpec((IDX_PAIRS, LANES, Q_TILE), im(lambda p, qr, kr, b: (0, 0, qr[p]))),
            pl.BlockSpec((IDX_HEADS, Q_TILE), im(lambda p, qr, kr, b: (0, qr[p]))),
            pl.BlockSpec((t_pad, LANES), im(lambda p, qr, kr, b: (b[p], 0))),
        ],
        out_specs=pl.BlockSpec((N_PAIRS, Q_TILE, LANES), im(lambda p, qr, kr, b: (0, qr[p], 0))),
        scratch_shapes=[pltpu.VMEM((nkb_total, K_TILE, Q_TILE), jnp.int32),
                        pltpu.VMEM((nkb_total, K_TILE, Q_TILE), BF16),
                        pltpu.VMEM((1, Q_TILE), jnp.int32),
                        pltpu.VMEM((N_PAIRS, 1, 2 * Q_TILE), F32),
                        pltpu.VMEM((N_PAIRS, vrows, 2 * Q_TILE), F32)],
    )
    return pl.pallas_call(
        functools.partial(_dsa_kernel, topk=topk),
        grid_spec=grid_spec,
        out_shape=jax.ShapeDtypeStruct((N_PAIRS, rows, LANES), BF16),
        compiler_params=pltpu.CompilerParams(dimension_semantics=("arbitrary",),
                                             vmem_limit_bytes=VMEM_LIMIT),
        name="dsa_attn",
    )(steps["qrow"], steps["krow"], steps["qloc"], steps["kloc"], steps["nkb"], steps["nsub"],
      steps["full"], steps["first"], steps["last"], steps["b"], q_t, qkvi, v_t, qi_t, wi_t, kidx)


def _rope_tables(t_pad):
    inv = ROPE_THETA ** (-jnp.arange(0, HEAD_DIM, 2, dtype=F32) / HEAD_DIM)
    ang = jnp.arange(t_pad, dtype=F32)[:, None] * inv[None, :]
    cos, sin = jnp.cos(ang), jnp.sin(ang)
    cos128 = jnp.tile(cos, (1, LANES // (HEAD_DIM // 2)))
    sin128 = jnp.tile(jnp.concatenate([-sin, sin], axis=1), (1, LANES // HEAD_DIM))
    return cos128, sin128


def _interleave_gate_val(a):
    lead = a.shape[:-1]
    a = a.reshape(lead + (2, FFN_HIDDEN // FFN_TILE, FFN_TILE))
    a = jnp.swapaxes(a, -3, -2)
    return a.reshape(lead + (2 * FFN_HIDDEN,))


def _with_ones_rows(v_t):
    ones = jnp.ones((v_t.shape[0], BF16_ROWS, v_t.shape[2]), v_t.dtype)
    return jnp.concatenate([v_t, ones], axis=1)


def _key_tiled(v_t):
    pairs, d, rows = v_t.shape
    return jnp.swapaxes(v_t.reshape(pairs, d, rows // K_TILE, K_TILE), 1, 2)


def kernel(x, meta_tokens, da_norm, da_w_qkv, da_lambda_q1, da_lambda_k1, da_lambda_q2, da_lambda_k2, da_subln, da_w_o, dsa_norm, dsa_w_in, dsa_idx_k_norm, dsa_w_o, ffn_norm, ffn_w_up, ffn_conv_w, ffn_conv_b, ffn_w_down, final_norm):
    batch, seq, d = x.shape
    assert d == D_MODEL
    depth = ffn_norm.shape[0]
    t_real = seq + N_META
    t_pad = _padded_len(t_real)
    assert t_pad % (K_SUBS * K_TILE) == 0 and t_pad % Q_TILE == 0
    rows = batch * t_pad

    meta = jnp.broadcast_to(meta_tokens[None].astype(x.dtype), (batch, N_META, d))
    h = jnp.concatenate([meta, x, jnp.zeros((batch, t_pad - t_real, d), x.dtype)], axis=1)
    h = h.reshape(rows, d)
    cos128, sin128 = _rope_tables(t_pad)
    steps = _attention_steps(batch, t_pad)

    dsa_qkv = N_HEADS * HEAD_DIM
    idx_cols = IDX_HEADS * HEAD_DIM
    for i in range(depth):
        j = i // 2
        if i % 2 == 0:
            lambda_init = 0.8 - 0.6 * math.exp(-0.3 * i)
            q_t, k, v_t = _project(h, da_norm[j], da_w_qkv[j].astype(BF16), cos128, sin128, t_pad)
            lam_rows = jnp.zeros((8, LANES), F32).at[0:4, 0:HEAD_DIM].set(
                jnp.stack([da_lambda_q1[j], da_lambda_k1[j], da_lambda_q2[j], da_lambda_k2[j]]).astype(F32))
            o = _diff_attention(q_t, k, v_t, lam_rows, da_subln[j].reshape(LANES, 1).astype(F32),
                                steps, lambda_init)
            h = _out_project(h, o, da_w_o[j].astype(BF16))
        else:
            w_in = dsa_w_in[j]
            n_main = 3 * dsa_qkv + idx_cols
            w_tail = jnp.zeros((d, LANES), w_in.dtype).at[:, :w_in.shape[1] - n_main].set(w_in[:, n_main:])
            gk = jnp.ones((1, LANES), F32).at[0, :HEAD_DIM].set(dsa_idx_k_norm[j].astype(F32))
            q_t, k, v_t, qi_t, kidx, wi_t = _project(
                h, dsa_norm[j], w_in[:, :n_main].astype(BF16), cos128, sin128, t_pad,
                w_tail=w_tail.astype(BF16), gk=gk)
            topk = min(TOPK_MAX, seq // 4)
            o = _dsa_attention(q_t, k, v_t, qi_t, wi_t, kidx, steps, topk, t_pad)
            h = _out_project(h, o, dsa_w_o[j].astype(BF16))
        h = _ffn(h, ffn_norm[i], _interleave_gate_val(ffn_w_up[i]).astype(BF16),
                 _interleave_gate_val(ffn_conv_w[i]).astype(F32),
                 _interleave_gate_val(ffn_conv_b[i]).reshape(1, -1).astype(F32),
                 ffn_w_down[i].astype(BF16), final_norm, t_pad, final=(i == depth - 1))
    return h.reshape(batch, t_pad, d)[:, N_META:N_META + seq]
```

```python
import functools
import math

import numpy as np
import jax
import jax.numpy as jnp
from jax import lax
from jax.experimental import pallas as pl
from jax.experimental.pallas import tpu as pltpu

D_MODEL = 1024
N_META = 16
ROPE_THETA = 10000.0
RMS_EPS = 1e-6
HEAD_DIM = 64
N_PAIRS = 8
N_HEADS = 2 * N_PAIRS
IDX_HEADS = 8
IDX_PAIRS = IDX_HEADS // 2
TOPK_MAX = 256
BIG_SCORE = 1e30
LOG2_E = 1.4426950408889634
FFN_HIDDEN = 2816
CONV_WIDTH = 3

LANES = 128
BF16_ROWS = 16
ROW_TILE = 512
Q_TILE = 256
K_TILE = 512
K_SUBS = 3
FFN_TILE = 256
HALO = BF16_ROWS
VMEM_LIMIT = 56 * 1024 * 1024

F32 = jnp.float32
BF16 = jnp.bfloat16
INT_MIN = -2147483648
KEY_NEG_INF = INT_MIN + 0x7FFFFF
BF16_MIN_NORMAL_BITS = 0x0080
F32_MIN_NORMAL = 1.1754943508222875e-38


def _padded_len(t):
    unit = math.lcm(ROW_TILE, K_SUBS * K_TILE, Q_TILE)
    return ((t + unit - 1) // unit) * unit


def _rms(x, gain):
    ms = jnp.mean(x * x, axis=-1, keepdims=True)
    return x * lax.rsqrt(ms + RMS_EPS) * gain


def _rope_lanes(y, cos, sin, lo):
    sw = jnp.where(lo, pltpu.roll(y, LANES - HEAD_DIM // 2, 1), pltpu.roll(y, HEAD_DIM // 2, 1))
    return y * cos + sw * sin


def _proj_kernel(x_ref, g_ref, w_ref, cos_ref, sin_ref, *rest, has_idx):
    if has_idx:
        wt_ref, gk_ref, qt_ref, k_ref, vt_ref, qit_ref, kidx_ref, wit_ref = rest
    else:
        qt_ref, k_ref, vt_ref = rest
    xh = _rms(x_ref[...], g_ref[...]).astype(BF16)
    cos = cos_ref[...]
    sin = sin_ref[...]
    lane = lax.broadcasted_iota(jnp.int32, cos.shape, 1)
    lo = (lane & (HEAD_DIM // 2)) == 0
    n_groups = (3 * N_PAIRS + IDX_PAIRS) if has_idx else 3 * N_PAIRS
    cw = 4 * LANES
    ones = jnp.ones((BF16_ROWS, ROW_TILE), BF16)
    for c in range(n_groups // 4):
        y = jnp.dot(xh, w_ref[:, c * cw:(c + 1) * cw], preferred_element_type=F32)
        for s in range(4):
            g = c * 4 + s
            yg = y[:, s * LANES:(s + 1) * LANES]
            if g < N_PAIRS:
                yg = _rope_lanes(yg, cos, sin, lo) * (HEAD_DIM ** -0.5 * LOG2_E)
                qt_ref[g] = yg.T.astype(BF16)
            elif g < 2 * N_PAIRS:
                k_ref[g - N_PAIRS] = _rope_lanes(yg, cos, sin, lo).astype(BF16)
            elif g < 3 * N_PAIRS:
                j = g - 2 * N_PAIRS
                vt_ref[j, 0, 0:2 * HEAD_DIM, :] = yg.T.astype(BF16)
                vt_ref[j, 0, 2 * HEAD_DIM:, :] = ones
            else:
                yg = _rope_lanes(yg, cos, sin, lo) * (HEAD_DIM ** -0.5)
                qit_ref[g - 3 * N_PAIRS] = yg.T.astype(BF16)
    if has_idx:
        t = jnp.dot(xh, wt_ref[...], preferred_element_type=F32)
        is_k = lane < HEAD_DIM
        ms = jnp.sum(jnp.where(is_k, t * t, 0.0), axis=-1, keepdims=True) * (1.0 / HEAD_DIM)
        kn = t * lax.rsqrt(ms + RMS_EPS) * gk_ref[...]
        kn = _rope_lanes(kn, cos, sin, lo)
        kidx_ref[...] = jnp.where(is_k, kn, pltpu.roll(kn, HEAD_DIM, 1)).astype(BF16)
        wit_ref[...] = (t * (IDX_HEADS ** -0.5)).T[HEAD_DIM:HEAD_DIM + IDX_HEADS, :]


def _project(h, gain, w, cos, sin, t_pad, w_tail=None, gk=None):
    assert ROW_TILE == K_TILE
    rows = h.shape[0]
    blocks_per_batch = t_pad // ROW_TILE
    has_idx = w_tail is not None
    vrows = 2 * HEAD_DIM + BF16_ROWS
    in_specs = [
        pl.BlockSpec((ROW_TILE, D_MODEL), lambda i: (i, 0)),
        pl.BlockSpec((1, D_MODEL), lambda i: (0, 0)),
        pl.BlockSpec(w.shape, lambda i: (0, 0)),
        pl.BlockSpec((ROW_TILE, LANES), lambda i: (i % blocks_per_batch, 0)),
        pl.BlockSpec((ROW_TILE, LANES), lambda i: (i % blocks_per_batch, 0)),
    ]
    out_shape = [jax.ShapeDtypeStruct((N_PAIRS, LANES, rows), BF16),
                 jax.ShapeDtypeStruct((N_PAIRS, rows, LANES), BF16),
                 jax.ShapeDtypeStruct((N_PAIRS, rows // K_TILE, vrows, K_TILE), BF16)]
    out_specs = [pl.BlockSpec((N_PAIRS, LANES, ROW_TILE), lambda i: (0, 0, i)),
                 pl.BlockSpec((N_PAIRS, ROW_TILE, LANES), lambda i: (0, i, 0)),
                 pl.BlockSpec((N_PAIRS, 1, vrows, K_TILE), lambda i: (0, i, 0, 0))]
    args = [h, gain.reshape(1, D_MODEL), w, cos, sin]
    if has_idx:
        in_specs += [pl.BlockSpec(w_tail.shape, lambda i: (0, 0)),
                     pl.BlockSpec((1, LANES), lambda i: (0, 0))]
        out_shape += [jax.ShapeDtypeStruct((IDX_PAIRS, LANES, rows), BF16),
                      jax.ShapeDtypeStruct((rows, LANES), BF16),
                      jax.ShapeDtypeStruct((IDX_HEADS, rows), F32)]
        out_specs += [pl.BlockSpec((IDX_PAIRS, LANES, ROW_TILE), lambda i: (0, 0, i)),
                      pl.BlockSpec((ROW_TILE, LANES), lambda i: (i, 0)),
                      pl.BlockSpec((IDX_HEADS, ROW_TILE), lambda i: (0, i))]
        args += [w_tail, gk]
    return pl.pallas_call(
        functools.partial(_proj_kernel, has_idx=has_idx),
        grid=(rows // ROW_TILE,),
        in_specs=in_specs,
        out_specs=out_specs,
        out_shape=out_shape,
        compiler_params=pltpu.CompilerParams(dimension_semantics=("arbitrary",),
                                             vmem_limit_bytes=VMEM_LIMIT),
        name="proj_idx" if has_idx else "proj",
    )(*args)


def _oproj_kernel(h_ref, o_ref, w_ref, out_ref):
    o = jnp.concatenate([o_ref[j] for j in range(N_PAIRS)], axis=1)
    out_ref[...] = h_ref[...] + jnp.dot(o, w_ref[...], preferred_element_type=F32)


def _out_project(h, o, w):
    rows = h.shape[0]
    return pl.pallas_call(
        _oproj_kernel,
        grid=(rows // ROW_TILE,),
        in_specs=[pl.BlockSpec((ROW_TILE, D_MODEL), lambda i: (i, 0)),
                  pl.BlockSpec((N_PAIRS, ROW_TILE, LANES), lambda i: (0, i, 0)),
                  pl.BlockSpec(w.shape, lambda i: (0, 0))],
        out_specs=pl.BlockSpec((ROW_TILE, D_MODEL), lambda i: (i, 0)),
        out_shape=jax.ShapeDtypeStruct(h.shape, F32),
        compiler_params=pltpu.CompilerParams(dimension_semantics=("arbitrary",),
                                             vmem_limit_bytes=VMEM_LIMIT),
        name="oproj",
    )(h, o, w)


def _ffn_kernel(xp_ref, x_ref, g_ref, wup_ref, cw_ref, cb_ref, wdn_ref, fg_ref, o_ref, u_sc, a_sc,
                *, blocks_per_batch, final):
    i = pl.program_id(0)
    x = x_ref[...]
    xe = jnp.concatenate([xp_ref[...], x], axis=0)
    xh = _rms(xe, g_ref[...])
    row = lax.broadcasted_iota(jnp.int32, (HALO + ROW_TILE, 1), 0)
    keep = jnp.logical_or(row >= HALO, i % blocks_per_batch != 0)
    xh = jnp.where(keep, xh, 0.0).astype(BF16)
    cw2 = 2 * FFN_TILE
    n_chunks = FFN_HIDDEN // FFN_TILE

    def up(c):
        return jnp.dot(xh, wup_ref[:, c * cw2:(c + 1) * cw2], preferred_element_type=F32)

    u_sc[0] = up(0)
    for c in range(n_chunks):
        cur = c % 2
        after = None
        if c + 1 < n_chunks:
            u_sc[1 - cur] = up(c + 1)
            after = _zero_after(u_sc[1 - cur, 0:8, 0:FFN_TILE])
        cwc = cw_ref[:, c * cw2:(c + 1) * cw2]
        conv = cb_ref[:, c * cw2:(c + 1) * cw2]
        for j in range(CONV_WIDTH):
            conv = conv + cwc[j:j + 1, :] * u_sc[cur, pl.ds(HALO - (CONV_WIDTH - 1) + j, ROW_TILE), :]
        gate = conv[:, :FFN_TILE]
        val = conv[:, FFN_TILE:]
        a = gate * jax.nn.sigmoid(gate) * val
        if after is not None:
            cut = (3 * ROW_TILE) // 4
            a = jnp.concatenate([a[:cut], a[cut:] + after], axis=0)
        a_sc[:, c * FFN_TILE:(c + 1) * FFN_TILE] = a.astype(BF16)
    acc = jnp.dot(a_sc[...], wdn_ref[...], preferred_element_type=F32)
    y = x + acc
    if final:
        y = _rms(y, fg_ref[...])
    o_ref[...] = y


def _ffn(h, gain, w_up, conv_w, conv_b, w_down, final_gain, t_pad, final):
    rows = h.shape[0]
    blocks_per_batch = t_pad // ROW_TILE
    halo_blocks = ROW_TILE // HALO
    return pl.pallas_call(
        functools.partial(_ffn_kernel, blocks_per_batch=blocks_per_batch, final=final),
        grid=(rows // ROW_TILE,),
        in_specs=[pl.BlockSpec((HALO, D_MODEL), lambda i: (jnp.maximum(i * halo_blocks - 1, 0), 0)),
                  pl.BlockSpec((ROW_TILE, D_MODEL), lambda i: (i, 0)),
                  pl.BlockSpec((1, D_MODEL), lambda i: (0, 0)),
                  pl.BlockSpec(w_up.shape, lambda i: (0, 0)),
                  pl.BlockSpec(conv_w.shape, lambda i: (0, 0)),
                  pl.BlockSpec(conv_b.shape, lambda i: (0, 0)),
                  pl.BlockSpec(w_down.shape, lambda i: (0, 0)),
                  pl.BlockSpec((1, D_MODEL), lambda i: (0, 0))],
        out_specs=pl.BlockSpec((ROW_TILE, D_MODEL), lambda i: (i, 0)),
        out_shape=jax.ShapeDtypeStruct(h.shape, F32),
        scratch_shapes=[pltpu.VMEM((2, HALO + ROW_TILE, 2 * FFN_TILE), F32),
                        pltpu.VMEM((ROW_TILE, FFN_HIDDEN), BF16)],
        compiler_params=pltpu.CompilerParams(dimension_semantics=("arbitrary",),
                                             vmem_limit_bytes=VMEM_LIMIT),
        name="ffn_final" if final else "ffn",
    )(h, h, gain.reshape(1, D_MODEL), w_up, conv_w, conv_b, w_down, final_gain.reshape(1, D_MODEL))


def _attention_steps(batch, t_pad):
    nq = t_pad // Q_TILE
    nk = t_pad // (K_SUBS * K_TILE)
    cols = {k: [] for k in ("qrow", "krow", "qloc", "kloc", "nkb", "nsub", "full", "first", "last", "b")}
    for b in range(batch):
        for qi in range(nq):
            n_tiles = ((qi + 1) * Q_TILE + K_TILE - 1) // K_TILE
            n = (n_tiles + K_SUBS - 1) // K_SUBS
            for ki in range(n):
                cols["qrow"].append(b * nq + qi)
                cols["krow"].append(b * nk + ki)
                cols["qloc"].append(qi)
                cols["kloc"].append(ki)
                cols["nkb"].append(n_tiles)
                cols["nsub"].append(min(K_SUBS, n_tiles - ki * K_SUBS))
                cols["full"].append(int((ki + 1) * K_SUBS < n_tiles))
                cols["first"].append(int(ki == 0))
                cols["last"].append(int(ki == n - 1))
                cols["b"].append(b)
    return {k: jnp.asarray(np.asarray(v, np.int32)) for k, v in cols.items()}


def _half_select(x_t, first_half):
    row = lax.broadcasted_iota(jnp.int32, x_t.shape, 0)
    keep = (row < HEAD_DIM) if first_half else (row >= HEAD_DIM)
    return jnp.where(keep, x_t, jnp.zeros_like(x_t))


COUNT_UNROLL = 4


def _fori_unrolled(n, body, init):
    def group(i, carry):
        for u in range(COUNT_UNROLL):
            carry = body(COUNT_UNROLL * i + u, carry)
        return carry

    groups = n // COUNT_UNROLL if isinstance(n, int) else lax.div(n, jnp.int32(COUNT_UNROLL))
    return lax.fori_loop(COUNT_UNROLL * groups, n, body, lax.fori_loop(0, groups, group, init))


def _zero_after(x):
    w = lax.bitcast_convert_type(x, jnp.uint32)
    w = lax.shift_right_logical(lax.shift_right_logical(w, jnp.uint32(16)), jnp.uint32(16))
    return lax.bitcast_convert_type(w, F32)[0:1, :]


def _flash_update(s_t, v_t, h, m_sc, acc_sc, after=None):
    m_prev = m_sc[h]
    m_new = jnp.maximum(m_prev, jnp.max(s_t, axis=0, keepdims=True))
    alpha = jnp.exp2(m_prev - m_new)
    if after is None:
        p_t = jnp.exp2(s_t - m_new).astype(BF16)
    else:
        cut = s_t.shape[0] // 2
        p_t = jnp.concatenate([jnp.exp2(s_t[:cut] - m_new),
                               jnp.exp2(s_t[cut:] - (m_new + after))], axis=0).astype(BF16)
    acc_sc[h] = alpha * acc_sc[h] + jnp.dot(v_t, p_t, preferred_element_type=F32)
    m_sc[h] = m_new


def _both_halves(x_t):
    return jnp.concatenate([_half_select(x_t, True), _half_select(x_t, False)], axis=1)


def _attention_sweep(qt_ref, k_ref, vt_ref, subs, bias_of, m_sc, acc_sc):
    def scores(sub, j):
        start = sub * K_TILE
        if not isinstance(sub, int):
            start = pl.multiple_of(start, K_TILE)
        return jnp.dot(k_ref[j, pl.ds(start, K_TILE), :], _both_halves(qt_ref[j]), preferred_element_type=F32)

    items = [(sub, j) for sub in subs for j in range(N_PAIRS)]
    biases = {}
    ahead = 2
    pending = [scores(*item) for item in items[:ahead]]
    for i, (sub, j) in enumerate(items):
        s_t = pending.pop(0)
        tile_key = sub if isinstance(sub, int) else "traced"
        if tile_key not in biases:
            biases[tile_key] = bias_of(sub)
        bias = biases[tile_key]
        if bias is not None and bias.shape[1] == Q_TILE:
            s_t = jnp.concatenate([s_t[:, :Q_TILE] + bias, s_t[:, Q_TILE:] + bias], axis=1)
        elif bias is not None:
            s_t = s_t + bias
        after = None
        if i + ahead < len(items):
            pending.append(scores(*items[i + ahead]))
            after = _zero_after(pending[-1][0:8, :])
        _flash_update(s_t, vt_ref[j, sub], j, m_sc, acc_sc, after=after)


def _diff_attn_kernel(qrow, krow, qloc, kloc, nsub, full, first, last,
                      qt_ref, k_ref, vt_ref, lam_ref, subln_ref, o_ref, m_sc, acc_sc,
                      *, lambda_init):
    p = pl.program_id(0)

    @pl.when(first[p] == 1)
    def _():
        m_sc[...] = jnp.full(m_sc.shape, -jnp.inf, F32)
        acc_sc[...] = jnp.zeros(acc_sc.shape, F32)

    def key_tile(sub, carry):
        tile = kloc[p] * K_SUBS + sub
        on_diagonal = tile * K_TILE + (K_TILE - 1) > qloc[p] * Q_TILE

        @pl.when(on_diagonal)
        def _():
            kpos = tile * K_TILE + lax.broadcasted_iota(jnp.int32, (K_TILE, 2 * Q_TILE), 0)
            qcol = lax.broadcasted_iota(jnp.int32, (K_TILE, 2 * Q_TILE), 1)
            qpos = qloc[p] * Q_TILE + jnp.where(qcol >= Q_TILE, qcol - Q_TILE, qcol)
            bias = jnp.where(kpos <= qpos, 0.0, -jnp.inf).astype(F32)
            _attention_sweep(qt_ref, k_ref, vt_ref, (sub,), lambda _: bias, m_sc, acc_sc)

        @pl.when(jnp.logical_not(on_diagonal))
        def _():
            _attention_sweep(qt_ref, k_ref, vt_ref, (sub,), lambda _: None, m_sc, acc_sc)

        return carry

    @pl.when(full[p] == 1)
    def _():
        _attention_sweep(qt_ref, k_ref, vt_ref, tuple(range(K_SUBS)), lambda _: None, m_sc, acc_sc)

    @pl.when(full[p] == 0)
    def _():
        lax.fori_loop(0, nsub[p], key_tile, 0)

    @pl.when(last[p] == 1)
    def _():
        lam_rows = lam_ref[...]
        lam = (jnp.exp(jnp.sum(lam_rows[0:1] * lam_rows[1:2], axis=-1, keepdims=True))
               - jnp.exp(jnp.sum(lam_rows[2:3] * lam_rows[3:4], axis=-1, keepdims=True))
               + lambda_init)
        vdim = 2 * HEAD_DIM
        for j in range(N_PAIRS):
            a = acc_sc[j]
            a1 = a[:, :Q_TILE]
            a2 = a[:, Q_TILE:]
            o = a1[:vdim] / a1[vdim:vdim + 1] - lam * (a2[:vdim] / a2[vdim:vdim + 1])
            ms = jnp.mean(o * o, axis=0, keepdims=True)
            o = o * lax.rsqrt(ms + RMS_EPS) * subln_ref[...] * (1.0 - lambda_init)
            o_ref[j] = o.T.astype(BF16)


def _diff_attention(q_t, k, v_t, lam_rows, subln_col, steps, lambda_init):
    rows = k.shape[1]
    n_steps = steps["qrow"].shape[0]
    vrows = v_t.shape[2]

    def im(fn):
        return lambda p, qr, kr, ql, kl, ns, fu, f, l: fn(p, qr, kr)

    grid_spec = pltpu.PrefetchScalarGridSpec(
        num_scalar_prefetch=8,
        grid=(n_steps,),
        in_specs=[
            pl.BlockSpec((N_PAIRS, LANES, Q_TILE), im(lambda p, qr, kr: (0, 0, qr[p]))),
            pl.BlockSpec((N_PAIRS, K_SUBS * K_TILE, LANES), im(lambda p, qr, kr: (0, kr[p], 0))),
            pl.BlockSpec((N_PAIRS, K_SUBS, vrows, K_TILE), im(lambda p, qr, kr: (0, kr[p], 0, 0))),
            pl.BlockSpec((8, LANES), im(lambda p, qr, kr: (0, 0))),
            pl.BlockSpec((LANES, 1), im(lambda p, qr, kr: (0, 0))),
        ],
        out_specs=pl.BlockSpec((N_PAIRS, Q_TILE, LANES), im(lambda p, qr, kr: (0, qr[p], 0))),
        scratch_shapes=[pltpu.VMEM((N_PAIRS, 1, 2 * Q_TILE), F32),
                        pltpu.VMEM((N_PAIRS, vrows, 2 * Q_TILE), F32)],
    )
    return pl.pallas_call(
        functools.partial(_diff_attn_kernel, lambda_init=lambda_init),
        grid_spec=grid_spec,
        out_shape=jax.ShapeDtypeStruct((N_PAIRS, rows, LANES), BF16),
        compiler_params=pltpu.CompilerParams(dimension_semantics=("arbitrary",),
                                             vmem_limit_bytes=VMEM_LIMIT),
        name="diff_attn",
    )(steps["qrow"], steps["krow"], steps["qloc"], steps["kloc"], steps["nsub"], steps["full"],
      steps["first"], steps["last"], q_t, k, v_t, lam_rows, subln_col)


def _dsa_kernel(qrow, krow, qloc, kloc, nkb, nsub, full, first, last, bidx,
                qt_ref, k_ref, vt_ref, qit_ref, wit_ref, kidx_ref, o_ref,
                key_sc, coarse_sc, thr_sc, m_sc, acc_sc, *, topk):
    p = pl.program_id(0)

    @pl.when(first[p] == 1)
    def _():
        m_sc[...] = jnp.full(m_sc.shape, -jnp.inf, F32)
        acc_sc[...] = jnp.zeros(acc_sc.shape, F32)
        n = nkb[p]
        wi = wit_ref[...]
        qpos = qloc[p] * Q_TILE + lax.broadcasted_iota(jnp.int32, (K_TILE, Q_TILE), 1)
        krow_iota = lax.broadcasted_iota(jnp.int32, (K_TILE, Q_TILE), 0)

        def score_chunk(c, carry):
            kt = kidx_ref[pl.ds(pl.multiple_of(c * K_TILE, K_TILE), K_TILE), :]
            sc = jnp.zeros((K_TILE, Q_TILE), F32)
            for jp in range(IDX_PAIRS):
                qi_t = qit_ref[jp]
                for cc in range(2):
                    raw = jnp.dot(kt, _half_select(qi_t, cc == 0), preferred_element_type=F32)
                    hh = 2 * jp + cc
                    sc = sc + jnp.maximum(raw, 0.0) * wi[hh:hh + 1, :]
            kpos = c * K_TILE + krow_iota
            sc = jnp.where(kpos < N_META, BIG_SCORE, sc)
            sc = jnp.where(kpos <= qpos, sc, -jnp.inf)
            sc = jnp.where(jnp.abs(sc) < F32_MIN_NORMAL, 0.0, sc)
            bits = lax.bitcast_convert_type(sc, jnp.int32)
            key_sc[c] = bits ^ ((bits >> 31) & 0x7FFFFFFF)
            coarse_sc[c] = lax.bitcast_convert_type(bits & jnp.int32(-65536), F32).astype(BF16)
            return carry

        lax.fori_loop(0, n, score_chunk, 0)

        kf = float(topk)
        q1 = qloc[p] * Q_TILE + lax.broadcasted_iota(jnp.int32, (1, Q_TILE), 1)
        settled0 = (q1 < topk).astype(jnp.int32)
        coarse_bits = 16

        def coarse_body(b, state):
            lo, settled = state
            cand = lo + jnp.left_shift(jnp.int32(1), 31 - b)
            top = cand >> 16
            top = jnp.where(jnp.logical_and(top > 0, top < BF16_MIN_NORMAL_BITS), BF16_MIN_NORMAL_BITS, top)
            cbits = jnp.left_shift(top ^ ((top >> 31) & 0x7FFF), 16)
            cand16 = lax.bitcast_convert_type(cbits, F32).astype(BF16)

            def count_chunk(c, cnt16):
                hit = jnp.where(coarse_sc[c] >= cand16, jnp.ones((), BF16), jnp.zeros((), BF16))
                parts = [hit[i * BF16_ROWS:(i + 1) * BF16_ROWS] for i in range(K_TILE // BF16_ROWS)]
                while len(parts) > 1:
                    parts = [parts[i] + parts[i + 1] for i in range(0, len(parts), 2)]
                return cnt16 + parts[0].astype(F32)

            cnt16 = _fori_unrolled(n, count_chunk, jnp.zeros((BF16_ROWS, Q_TILE), F32))
            cnt = jnp.sum(cnt16, axis=0, keepdims=True)
            return jnp.where(cnt >= kf, cand, lo), jnp.where(cnt == kf, 1, settled)

        lo, settled = lax.fori_loop(0, coarse_bits, coarse_body,
                                    (jnp.full((1, Q_TILE), INT_MIN, jnp.int32), settled0))

        def unsettled(state):
            b, _, settled = state
            return jnp.logical_and(b < 32, jnp.min(settled) == 0)

        def count_where(hit_of):
            def count_chunk(c, cnt8):
                hit = jnp.where(hit_of(c, key_sc[c]), 1.0, 0.0).reshape(8, K_TILE // 64, 8, Q_TILE)
                return cnt8 + jnp.sum(jnp.sum(hit, axis=1), axis=0)

            cnt8 = _fori_unrolled(n, count_chunk, jnp.zeros((8, Q_TILE), F32))
            return jnp.sum(cnt8, axis=0, keepdims=True)

        def fine_body(state):
            b, lo, settled = state
            cand = lo + jnp.left_shift(jnp.int32(1), 31 - b)
            cnt = count_where(lambda c, keys: keys >= cand)
            return b + 1, jnp.where(cnt >= kf, cand, lo), jnp.where(cnt == kf, 1, settled)

        _, thr, settled = lax.while_loop(unsettled, fine_body, (jnp.int32(coarse_bits), lo, settled))
        thr = jnp.maximum(thr, KEY_NEG_INF + 1)
        thr_sc[...] = thr

        @pl.when(jnp.min(settled) == 0)
        def _():
            surplus = count_where(lambda c, keys: keys >= thr) - kf
            kk = lax.broadcasted_iota(jnp.int32, (K_TILE, K_TILE), 0)
            kc = lax.broadcasted_iota(jnp.int32, (K_TILE, K_TILE), 1)
            later_or_same = jnp.where(kc >= kk, 1.0, 0.0).astype(BF16)

            def demote(i, seen):
                c = n - 1 - i
                keys = key_sc[c]
                tie = keys == thr
                from_end = seen + jnp.dot(later_or_same, jnp.where(tie, 1.0, 0.0).astype(BF16),
                                          preferred_element_type=F32)
                drop = jnp.logical_and(tie, from_end <= surplus)
                key_sc[c] = jnp.where(drop, keys - 1, keys)
                return from_end[0:1, :]

            lax.fori_loop(0, n, demote, jnp.zeros((1, Q_TILE), F32))

    def mask_bias(sub):
        sel = key_sc[kloc[p] * K_SUBS + sub] >= thr_sc[...]
        return jnp.where(sel, 0.0, -jnp.inf).astype(F32)

    def key_tile(sub, carry):
        _attention_sweep(qt_ref, k_ref, vt_ref, (sub,), mask_bias, m_sc, acc_sc)
        return carry

    @pl.when(full[p] == 1)
    def _():
        _attention_sweep(qt_ref, k_ref, vt_ref, tuple(range(K_SUBS)), mask_bias, m_sc, acc_sc)

    @pl.when(full[p] == 0)
    def _():
        lax.fori_loop(0, nsub[p], key_tile, 0)

    @pl.when(last[p] == 1)
    def _():
        ones_row = 2 * HEAD_DIM
        for j in range(N_PAIRS):
            a = acc_sc[j]
            o = jnp.concatenate([a[:HEAD_DIM, :Q_TILE] / a[ones_row:ones_row + 1, :Q_TILE],
                                 a[HEAD_DIM:ones_row, Q_TILE:] / a[ones_row:ones_row + 1, Q_TILE:]],
                                axis=0)
            o_ref[j] = o.T.astype(BF16)


def _dsa_attention(q_t, k, v_t, qi_t, wi_t, kidx, steps, topk, t_pad):
    rows = k.shape[1]
    n_steps = steps["qrow"].shape[0]
    nkb_total = t_pad // K_TILE
    vrows = v_t.shape[2]

    def im(fn):
        return lambda p, qr, kr, ql, kl, nk, ns, fu, f, l, b: fn(p, qr, kr, b)

    grid_spec = pltpu.PrefetchScalarGridSpec(
        num_scalar_prefetch=10,
        grid=(n_steps,),
        in_specs=[
            pl.BlockSpec((N_PAIRS, LANES, Q_TILE), im(lambda p, qr, kr, b: (0, 0, qr[p]))),
            pl.BlockSpec((N_PAIRS, K_SUBS * K_TILE, LANES), im(lambda p, qr, kr, b: (0, kr[p], 0))),
            pl.BlockSpec((N_PAIRS, K_SUBS, vrows, K_TILE), im(lambda p, qr, kr, b: (0, kr[p], 0, 0))),
            pl.BlockSpec((IDX_PAIRS, LANES, Q_TILE), im(lambda p, qr, kr, b: (0, 0, qr[p]))),
            pl.BlockSpec((IDX_HEADS, Q_TILE), im(lambda p, qr, kr, b: (0, qr[p]))),
            pl.BlockSpec((t_pad, LANES), im(lambda p, qr, kr, b: (b[p], 0))),
        ],
        out_specs=pl.BlockSpec((N_PAIRS, Q_TILE, LANES), im(lambda p, qr, kr, b: (0, qr[p], 0))),
        scratch_shapes=[pltpu.VMEM((nkb_total, K_TILE, Q_TILE), jnp.int32),
                        pltpu.VMEM((nkb_total, K_TILE, Q_TILE), BF16),
                        pltpu.VMEM((1, Q_TILE), jnp.int32),
                        pltpu.VMEM((N_PAIRS, 1, 2 * Q_TILE), F32),
                        pltpu.VMEM((N_PAIRS, vrows, 2 * Q_TILE), F32)],
    )
    return pl.pallas_call(
        functools.partial(_dsa_kernel, topk=topk),
        grid_spec=grid_spec,
        out_shape=jax.ShapeDtypeStruct((N_PAIRS, rows, LANES), BF16),
        compiler_params=pltpu.CompilerParams(dimension_semantics=("arbitrary",),
                                             vmem_limit_bytes=VMEM_LIMIT),
        name="dsa_attn",
    )(steps["qrow"], steps["krow"], steps["qloc"], steps["kloc"], steps["nkb"], steps["nsub"],
      steps["full"], steps["first"], steps["last"], steps["b"], q_t, k, v_t, qi_t, wi_t, kidx)


def _rope_tables(t_pad):
    inv = ROPE_THETA ** (-jnp.arange(0, HEAD_DIM, 2, dtype=F32) / HEAD_DIM)
    ang = jnp.arange(t_pad, dtype=F32)[:, None] * inv[None, :]
    cos, sin = jnp.cos(ang), jnp.sin(ang)
    cos128 = jnp.tile(cos, (1, LANES // (HEAD_DIM // 2)))
    sin128 = jnp.tile(jnp.concatenate([-sin, sin], axis=1), (1, LANES // HEAD_DIM))
    return cos128, sin128


def _interleave_gate_val(a):
    lead = a.shape[:-1]
    a = a.reshape(lead + (2, FFN_HIDDEN // FFN_TILE, FFN_TILE))
    a = jnp.swapaxes(a, -3, -2)
    return a.reshape(lead + (2 * FFN_HIDDEN,))


def kernel(x, meta_tokens, da_norm, da_w_qkv, da_lambda_q1, da_lambda_k1, da_lambda_q2, da_lambda_k2, da_subln, da_w_o, dsa_norm, dsa_w_in, dsa_idx_k_norm, dsa_w_o, ffn_norm, ffn_w_up, ffn_conv_w, ffn_conv_b, ffn_w_down, final_norm):
    batch, seq, d = x.shape
    assert d == D_MODEL
    depth = ffn_norm.shape[0]
    t_real = seq + N_META
    t_pad = _padded_len(t_real)
    assert t_pad % (K_SUBS * K_TILE) == 0 and t_pad % Q_TILE == 0
    rows = batch * t_pad

    meta = jnp.broadcast_to(meta_tokens[None].astype(x.dtype), (batch, N_META, d))
    h = jnp.concatenate([meta, x, jnp.zeros((batch, t_pad - t_real, d), x.dtype)], axis=1)
    h = h.reshape(rows, d)
    cos128, sin128 = _rope_tables(t_pad)
    steps = _attention_steps(batch, t_pad)

    dsa_qkv = N_HEADS * HEAD_DIM
    idx_cols = IDX_HEADS * HEAD_DIM
    for i in range(depth):
        j = i // 2
        if i % 2 == 0:
            lambda_init = 0.8 - 0.6 * math.exp(-0.3 * i)
            q_t, k, v_t = _project(h, da_norm[j], da_w_qkv[j].astype(BF16), cos128, sin128, t_pad)
            lam_rows = jnp.zeros((8, LANES), F32).at[0:4, 0:HEAD_DIM].set(
                jnp.stack([da_lambda_q1[j], da_lambda_k1[j], da_lambda_q2[j], da_lambda_k2[j]]).astype(F32))
            o = _diff_attention(q_t, k, v_t, lam_rows, da_subln[j].reshape(LANES, 1).astype(F32),
                                steps, lambda_init)
            h = _out_project(h, o, da_w_o[j].astype(BF16))
        else:
            w_in = dsa_w_in[j]
            n_main = 3 * dsa_qkv + idx_cols
            w_tail = jnp.zeros((d, LANES), w_in.dtype).at[:, :w_in.shape[1] - n_main].set(w_in[:, n_main:])
            gk = jnp.ones((1, LANES), F32).at[0, :HEAD_DIM].set(dsa_idx_k_norm[j].astype(F32))
            q_t, k, v_t, qi_t, kidx, wi_t = _project(
                h, dsa_norm[j], w_in[:, :n_main].astype(BF16), cos128, sin128, t_pad,
                w_tail=w_tail.astype(BF16), gk=gk)
            topk = min(TOPK_MAX, seq // 4)
            o = _dsa_attention(q_t, k, v_t, qi_t, wi_t, kidx, steps, topk, t_pad)
            h = _out_project(h, o, dsa_w_o[j].astype(BF16))
        h = _ffn(h, ffn_norm[i], _interleave_gate_val(ffn_w_up[i]).astype(BF16),
                 _interleave_gate_val(ffn_conv_w[i]).astype(F32),
                 _interleave_gate_val(ffn_conv_b[i]).reshape(1, -1).astype(F32),
                 ffn_w_down[i].astype(BF16), final_norm, t_pad, final=(i == depth - 1))
    return h.reshape(batch, t_pad, d)[:, N_META:N_META + seq]
```

```python
import functools
import math

import numpy as np
import jax
import jax.numpy as jnp
from jax import lax
from jax.experimental import pallas as pl
from jax.experimental.pallas import tpu as pltpu

D_MODEL = 1024
N_META = 16
ROPE_THETA = 10000.0
RMS_EPS = 1e-6
HEAD_DIM = 64
N_PAIRS = 8
N_HEADS = 2 * N_PAIRS
IDX_HEADS = 8
IDX_PAIRS = IDX_HEADS // 2
TOPK_MAX = 256
BIG_SCORE = 1e30
LOG2_E = 1.4426950408889634
FFN_HIDDEN = 2816
CONV_WIDTH = 3

LANES = 128
BF16_ROWS = 16
ROW_TILE = 512
Q_TILE = 256
K_TILE = 512
K_SUBS = 3
FFN_TILE = 256
HALO = BF16_ROWS
VMEM_LIMIT = 56 * 1024 * 1024

F32 = jnp.float32
BF16 = jnp.bfloat16
INT_MIN = -2147483648
KEY_NEG_INF = INT_MIN + 0x7FFFFF
BF16_MIN_NORMAL_BITS = 0x0080
F32_MIN_NORMAL = 1.1754943508222875e-38


def _padded_len(t):
    unit = math.lcm(ROW_TILE, K_SUBS * K_TILE, Q_TILE)
    return ((t + unit - 1) // unit) * unit


def _rms(x, gain):
    ms = jnp.mean(x * x, axis=-1, keepdims=True)
    return x * lax.rsqrt(ms + RMS_EPS) * gain


def _rope_lanes(y, cos, sin, lo):
    sw = jnp.where(lo, pltpu.roll(y, LANES - HEAD_DIM // 2, 1), pltpu.roll(y, HEAD_DIM // 2, 1))
    return y * cos + sw * sin


def _proj_kernel(x_ref, g_ref, w_ref, cos_ref, sin_ref, *rest, has_idx):
    if has_idx:
        wt_ref, gk_ref, qt_ref, k_ref, vt_ref, qit_ref, kidx_ref, wit_ref = rest
    else:
        qt_ref, k_ref, vt_ref = rest
    xh = _rms(x_ref[...], g_ref[...]).astype(BF16)
    cos = cos_ref[...]
    sin = sin_ref[...]
    lane = lax.broadcasted_iota(jnp.int32, cos.shape, 1)
    lo = (lane & (HEAD_DIM // 2)) == 0
    n_groups = (3 * N_PAIRS + IDX_PAIRS) if has_idx else 3 * N_PAIRS
    cw = 4 * LANES
    ones = jnp.ones((BF16_ROWS, ROW_TILE), BF16)
    for c in range(n_groups // 4):
        y = jnp.dot(xh, w_ref[:, c * cw:(c + 1) * cw], preferred_element_type=F32)
        for s in range(4):
            g = c * 4 + s
            yg = y[:, s * LANES:(s + 1) * LANES]
            if g < N_PAIRS:
                yg = _rope_lanes(yg, cos, sin, lo) * (HEAD_DIM ** -0.5 * LOG2_E)
                qt_ref[g] = yg.T.astype(BF16)
            elif g < 2 * N_PAIRS:
                k_ref[g - N_PAIRS] = _rope_lanes(yg, cos, sin, lo).astype(BF16)
            elif g < 3 * N_PAIRS:
                j = g - 2 * N_PAIRS
                vt_ref[j, 0, 0:2 * HEAD_DIM, :] = yg.T.astype(BF16)
                vt_ref[j, 0, 2 * HEAD_DIM:, :] = ones
            else:
                yg = _rope_lanes(yg, cos, sin, lo) * (HEAD_DIM ** -0.5)
                qit_ref[g - 3 * N_PAIRS] = yg.T.astype(BF16)
    if has_idx:
        t = jnp.dot(xh, wt_ref[...], preferred_element_type=F32)
        is_k = lane < HEAD_DIM
        ms = jnp.sum(jnp.where(is_k, t * t, 0.0), axis=-1, keepdims=True) * (1.0 / HEAD_DIM)
        kn = t * lax.rsqrt(ms + RMS_EPS) * gk_ref[...]
        kn = _rope_lanes(kn, cos, sin, lo)
        kidx_ref[...] = jnp.where(is_k, kn, pltpu.roll(kn, HEAD_DIM, 1)).astype(BF16)
        wit_ref[...] = (t * (IDX_HEADS ** -0.5)).T[HEAD_DIM:HEAD_DIM + IDX_HEADS, :]


def _project(h, gain, w, cos, sin, t_pad, w_tail=None, gk=None):
    assert ROW_TILE == K_TILE
    rows = h.shape[0]
    blocks_per_batch = t_pad // ROW_TILE
    has_idx = w_tail is not None
    vrows = 2 * HEAD_DIM + BF16_ROWS
    in_specs = [
        pl.BlockSpec((ROW_TILE, D_MODEL), lambda i: (i, 0)),
        pl.BlockSpec((1, D_MODEL), lambda i: (0, 0)),
        pl.BlockSpec(w.shape, lambda i: (0, 0)),
        pl.BlockSpec((ROW_TILE, LANES), lambda i: (i % blocks_per_batch, 0)),
        pl.BlockSpec((ROW_TILE, LANES), lambda i: (i % blocks_per_batch, 0)),
    ]
    out_shape = [jax.ShapeDtypeStruct((N_PAIRS, LANES, rows), BF16),
                 jax.ShapeDtypeStruct((N_PAIRS, rows, LANES), BF16),
                 jax.ShapeDtypeStruct((N_PAIRS, rows // K_TILE, vrows, K_TILE), BF16)]
    out_specs = [pl.BlockSpec((N_PAIRS, LANES, ROW_TILE), lambda i: (0, 0, i)),
                 pl.BlockSpec((N_PAIRS, ROW_TILE, LANES), lambda i: (0, i, 0)),
                 pl.BlockSpec((N_PAIRS, 1, vrows, K_TILE), lambda i: (0, i, 0, 0))]
    args = [h, gain.reshape(1, D_MODEL), w, cos, sin]
    if has_idx:
        in_specs += [pl.BlockSpec(w_tail.shape, lambda i: (0, 0)),
                     pl.BlockSpec((1, LANES), lambda i: (0, 0))]
        out_shape += [jax.ShapeDtypeStruct((IDX_PAIRS, LANES, rows), BF16),
                      jax.ShapeDtypeStruct((rows, LANES), BF16),
                      jax.ShapeDtypeStruct((IDX_HEADS, rows), F32)]
        out_specs += [pl.BlockSpec((IDX_PAIRS, LANES, ROW_TILE), lambda i: (0, 0, i)),
                      pl.BlockSpec((ROW_TILE, LANES), lambda i: (i, 0)),
                      pl.BlockSpec((IDX_HEADS, ROW_TILE), lambda i: (0, i))]
        args += [w_tail, gk]
    return pl.pallas_call(
        functools.partial(_proj_kernel, has_idx=has_idx),
        grid=(rows // ROW_TILE,),
        in_specs=in_specs,
        out_specs=out_specs,
        out_shape=out_shape,
        compiler_params=pltpu.CompilerParams(dimension_semantics=("arbitrary",),
                                             vmem_limit_bytes=VMEM_LIMIT),
        name="proj_idx" if has_idx else "proj",
    )(*args)


def _oproj_kernel(h_ref, o_ref, w_ref, out_ref):
    o = jnp.concatenate([o_ref[j] for j in range(N_PAIRS)], axis=1)
    out_ref[...] = h_ref[...] + jnp.dot(o, w_ref[...], preferred_element_type=F32)


def _out_project(h, o, w):
    rows = h.shape[0]
    return pl.pallas_call(
        _oproj_kernel,
        grid=(rows // ROW_TILE,),
        in_specs=[pl.BlockSpec((ROW_TILE, D_MODEL), lambda i: (i, 0)),
                  pl.BlockSpec((N_PAIRS, ROW_TILE, LANES), lambda i: (0, i, 0)),
                  pl.BlockSpec(w.shape, lambda i: (0, 0))],
        out_specs=pl.BlockSpec((ROW_TILE, D_MODEL), lambda i: (i, 0)),
        out_shape=jax.ShapeDtypeStruct(h.shape, F32),
        compiler_params=pltpu.CompilerParams(dimension_semantics=("arbitrary",),
                                             vmem_limit_bytes=VMEM_LIMIT),
        name="oproj",
    )(h, o, w)


def _ffn_kernel(xp_ref, x_ref, g_ref, wup_ref, cw_ref, cb_ref, wdn_ref, fg_ref, o_ref, u_sc, a_sc,
                *, blocks_per_batch, final):
    i = pl.program_id(0)
    x = x_ref[...]
    xe = jnp.concatenate([xp_ref[...], x], axis=0)
    xh = _rms(xe, g_ref[...])
    row = lax.broadcasted_iota(jnp.int32, (HALO + ROW_TILE, 1), 0)
    keep = jnp.logical_or(row >= HALO, i % blocks_per_batch != 0)
    xh = jnp.where(keep, xh, 0.0).astype(BF16)
    cw2 = 2 * FFN_TILE
    n_chunks = FFN_HIDDEN // FFN_TILE

    def up(c):
        return jnp.dot(xh, wup_ref[:, c * cw2:(c + 1) * cw2], preferred_element_type=F32)

    u_sc[0] = up(0)
    for c in range(n_chunks):
        cur = c % 2
        after = None
        if c + 1 < n_chunks:
            u_sc[1 - cur] = up(c + 1)
            after = _zero_after(u_sc[1 - cur, 0:8, 0:FFN_TILE])
        cwc = cw_ref[:, c * cw2:(c + 1) * cw2]
        conv = cb_ref[:, c * cw2:(c + 1) * cw2]
        for j in range(CONV_WIDTH):
            conv = conv + cwc[j:j + 1, :] * u_sc[cur, pl.ds(HALO - (CONV_WIDTH - 1) + j, ROW_TILE), :]
        gate = conv[:, :FFN_TILE]
        val = conv[:, FFN_TILE:]
        a = gate * jax.nn.sigmoid(gate) * val
        if after is not None:
            cut = ROW_TILE // 2
            a = jnp.concatenate([a[:cut], a[cut:] + after], axis=0)
        a_sc[:, c * FFN_TILE:(c + 1) * FFN_TILE] = a.astype(BF16)
    acc = jnp.dot(a_sc[...], wdn_ref[...], preferred_element_type=F32)
    y = x + acc
    if final:
        y = _rms(y, fg_ref[...])
    o_ref[...] = y


def _ffn(h, gain, w_up, conv_w, conv_b, w_down, final_gain, t_pad, final):
    rows = h.shape[0]
    blocks_per_batch = t_pad // ROW_TILE
    halo_blocks = ROW_TILE // HALO
    return pl.pallas_call(
        functools.partial(_ffn_kernel, blocks_per_batch=blocks_per_batch, final=final),
        grid=(rows // ROW_TILE,),
        in_specs=[pl.BlockSpec((HALO, D_MODEL), lambda i: (jnp.maximum(i * halo_blocks - 1, 0), 0)),
                  pl.BlockSpec((ROW_TILE, D_MODEL), lambda i: (i, 0)),
                  pl.BlockSpec((1, D_MODEL), lambda i: (0, 0)),
                  pl.BlockSpec(w_up.shape, lambda i: (0, 0)),
                  pl.BlockSpec(conv_w.shape, lambda i: (0, 0)),
                  pl.BlockSpec(conv_b.shape, lambda i: (0, 0)),
                  pl.BlockSpec(w_down.shape, lambda i: (0, 0)),
                  pl.BlockSpec((1, D_MODEL), lambda i: (0, 0))],
        out_specs=pl.BlockSpec((ROW_TILE, D_MODEL), lambda i: (i, 0)),
        out_shape=jax.ShapeDtypeStruct(h.shape, F32),
        scratch_shapes=[pltpu.VMEM((2, HALO + ROW_TILE, 2 * FFN_TILE), F32),
                        pltpu.VMEM((ROW_TILE, FFN_HIDDEN), BF16)],
        compiler_params=pltpu.CompilerParams(dimension_semantics=("arbitrary",),
                                             vmem_limit_bytes=VMEM_LIMIT),
        name="ffn_final" if final else "ffn",
    )(h, h, gain.reshape(1, D_MODEL), w_up, conv_w, conv_b, w_down, final_gain.reshape(1, D_MODEL))


def _attention_steps(batch, t_pad):
    nq = t_pad // Q_TILE
    nk = t_pad // (K_SUBS * K_TILE)
    cols = {k: [] for k in ("qrow", "krow", "qloc", "kloc", "nkb", "nsub", "full", "first", "last", "b")}
    for b in range(batch):
        for qi in range(nq):
            n_tiles = ((qi + 1) * Q_TILE + K_TILE - 1) // K_TILE
            n = (n_tiles + K_SUBS - 1) // K_SUBS
            for ki in range(n):
                cols["qrow"].append(b * nq + qi)
                cols["krow"].append(b * nk + ki)
                cols["qloc"].append(qi)
                cols["kloc"].append(ki)
                cols["nkb"].append(n_tiles)
                cols["nsub"].append(min(K_SUBS, n_tiles - ki * K_SUBS))
                cols["full"].append(int((ki + 1) * K_SUBS < n_tiles))
                cols["first"].append(int(ki == 0))
                cols["last"].append(int(ki == n - 1))
                cols["b"].append(b)
    return {k: jnp.asarray(np.asarray(v, np.int32)) for k, v in cols.items()}


def _half_select(x_t, first_half):
    row = lax.broadcasted_iota(jnp.int32, x_t.shape, 0)
    keep = (row < HEAD_DIM) if first_half else (row >= HEAD_DIM)
    return jnp.where(keep, x_t, jnp.zeros_like(x_t))


COUNT_UNROLL = 4


def _fori_unrolled(n, body, init):
    def group(i, carry):
        for u in range(COUNT_UNROLL):
            carry = body(COUNT_UNROLL * i + u, carry)
        return carry

    groups = n // COUNT_UNROLL if isinstance(n, int) else lax.div(n, jnp.int32(COUNT_UNROLL))
    return lax.fori_loop(COUNT_UNROLL * groups, n, body, lax.fori_loop(0, groups, group, init))


def _zero_after(x):
    w = lax.bitcast_convert_type(x, jnp.uint32)
    w = lax.shift_right_logical(lax.shift_right_logical(w, jnp.uint32(16)), jnp.uint32(16))
    return lax.bitcast_convert_type(w, F32)[0:1, :]


def _flash_update(s_t, v_t, h, m_sc, acc_sc, after=None):
    m_prev = m_sc[h]
    m_new = jnp.maximum(m_prev, jnp.max(s_t, axis=0, keepdims=True))
    alpha = jnp.exp2(m_prev - m_new)
    if after is None:
        p_t = jnp.exp2(s_t - m_new).astype(BF16)
    else:
        cut = s_t.shape[0] // 2
        p_t = jnp.concatenate([jnp.exp2(s_t[:cut] - m_new),
                               jnp.exp2(s_t[cut:] - (m_new + after))], axis=0).astype(BF16)
    acc_sc[h] = alpha * acc_sc[h] + jnp.dot(v_t, p_t, preferred_element_type=F32)
    m_sc[h] = m_new


def _both_halves(x_t):
    return jnp.concatenate([_half_select(x_t, True), _half_select(x_t, False)], axis=1)


def _attention_sweep(qt_ref, k_ref, vt_ref, subs, bias_of, m_sc, acc_sc):
    def scores(sub, j):
        start = sub * K_TILE
        if not isinstance(sub, int):
            start = pl.multiple_of(start, K_TILE)
        return jnp.dot(k_ref[j, pl.ds(start, K_TILE), :], _both_halves(qt_ref[j]), preferred_element_type=F32)

    items = [(sub, j) for sub in subs for j in range(N_PAIRS)]
    biases = {}
    ahead = 2
    pending = [scores(*item) for item in items[:ahead]]
    for i, (sub, j) in enumerate(items):
        s_t = pending.pop(0)
        tile_key = sub if isinstance(sub, int) else "traced"
        if tile_key not in biases:
            biases[tile_key] = bias_of(sub)
        bias = biases[tile_key]
        if bias is not None and bias.shape[1] == Q_TILE:
            s_t = jnp.concatenate([s_t[:, :Q_TILE] + bias, s_t[:, Q_TILE:] + bias], axis=1)
        elif bias is not None:
            s_t = s_t + bias
        after = None
        if i + ahead < len(items):
            pending.append(scores(*items[i + ahead]))
            after = _zero_after(pending[-1][0:8, :])
        _flash_update(s_t, vt_ref[j, sub], j, m_sc, acc_sc, after=after)


def _diff_attn_kernel(qrow, krow, qloc, kloc, nsub, full, first, last,
                      qt_ref, k_ref, vt_ref, lam_ref, subln_ref, o_ref, m_sc, acc_sc,
                      *, lambda_init):
    p = pl.program_id(0)

    @pl.when(first[p] == 1)
    def _():
        m_sc[...] = jnp.full(m_sc.shape, -jnp.inf, F32)
        acc_sc[...] = jnp.zeros(acc_sc.shape, F32)

    def key_tile(sub, carry):
        tile = kloc[p] * K_SUBS + sub
        on_diagonal = tile * K_TILE + (K_TILE - 1) > qloc[p] * Q_TILE

        @pl.when(on_diagonal)
        def _():
            kpos = tile * K_TILE + lax.broadcasted_iota(jnp.int32, (K_TILE, 2 * Q_TILE), 0)
            qcol = lax.broadcasted_iota(jnp.int32, (K_TILE, 2 * Q_TILE), 1)
            qpos = qloc[p] * Q_TILE + jnp.where(qcol >= Q_TILE, qcol - Q_TILE, qcol)
            bias = jnp.where(kpos <= qpos, 0.0, -jnp.inf).astype(F32)
            _attention_sweep(qt_ref, k_ref, vt_ref, (sub,), lambda _: bias, m_sc, acc_sc)

        @pl.when(jnp.logical_not(on_diagonal))
        def _():
            _attention_sweep(qt_ref, k_ref, vt_ref, (sub,), lambda _: None, m_sc, acc_sc)

        return carry

    @pl.when(full[p] == 1)
    def _():
        _attention_sweep(qt_ref, k_ref, vt_ref, tuple(range(K_SUBS)), lambda _: None, m_sc, acc_sc)

    @pl.when(full[p] == 0)
    def _():
        lax.fori_loop(0, nsub[p], key_tile, 0)

    @pl.when(last[p] == 1)
    def _():
        lam_rows = lam_ref[...]
        lam = (jnp.exp(jnp.sum(lam_rows[0:1] * lam_rows[1:2], axis=-1, keepdims=True))
               - jnp.exp(jnp.sum(lam_rows[2:3] * lam_rows[3:4], axis=-1, keepdims=True))
               + lambda_init)
        vdim = 2 * HEAD_DIM
        for j in range(N_PAIRS):
            a = acc_sc[j]
            a1 = a[:, :Q_TILE]
            a2 = a[:, Q_TILE:]
            o = a1[:vdim] / a1[vdim:vdim + 1] - lam * (a2[:vdim] / a2[vdim:vdim + 1])
            ms = jnp.mean(o * o, axis=0, keepdims=True)
            o = o * lax.rsqrt(ms + RMS_EPS) * subln_ref[...] * (1.0 - lambda_init)
            o_ref[j] = o.T.astype(BF16)


def _diff_attention(q_t, k, v_t, lam_rows, subln_col, steps, lambda_init):
    rows = k.shape[1]
    n_steps = steps["qrow"].shape[0]
    vrows = v_t.shape[2]

    def im(fn):
        return lambda p, qr, kr, ql, kl, ns, fu, f, l: fn(p, qr, kr)

    grid_spec = pltpu.PrefetchScalarGridSpec(
        num_scalar_prefetch=8,
        grid=(n_steps,),
        in_specs=[
            pl.BlockSpec((N_PAIRS, LANES, Q_TILE), im(lambda p, qr, kr: (0, 0, qr[p]))),
            pl.BlockSpec((N_PAIRS, K_SUBS * K_TILE, LANES), im(lambda p, qr, kr: (0, kr[p], 0))),
            pl.BlockSpec((N_PAIRS, K_SUBS, vrows, K_TILE), im(lambda p, qr, kr: (0, kr[p], 0, 0))),
            pl.BlockSpec((8, LANES), im(lambda p, qr, kr: (0, 0))),
            pl.BlockSpec((LANES, 1), im(lambda p, qr, kr: (0, 0))),
        ],
        out_specs=pl.BlockSpec((N_PAIRS, Q_TILE, LANES), im(lambda p, qr, kr: (0, qr[p], 0))),
        scratch_shapes=[pltpu.VMEM((N_PAIRS, 1, 2 * Q_TILE), F32),
                        pltpu.VMEM((N_PAIRS, vrows, 2 * Q_TILE), F32)],
    )
    return pl.pallas_call(
        functools.partial(_diff_attn_kernel, lambda_init=lambda_init),
        grid_spec=grid_spec,
        out_shape=jax.ShapeDtypeStruct((N_PAIRS, rows, LANES), BF16),
        compiler_params=pltpu.CompilerParams(dimension_semantics=("arbitrary",),
                                             vmem_limit_bytes=VMEM_LIMIT),
        name="diff_attn",
    )(steps["qrow"], steps["krow"], steps["qloc"], steps["kloc"], steps["nsub"], steps["full"],
      steps["first"], steps["last"], q_t, k, v_t, lam_rows, subln_col)


def _dsa_kernel(qrow, krow, qloc, kloc, nkb, nsub, full, first, last, bidx,
                qt_ref, k_ref, vt_ref, qit_ref, wit_ref, kidx_ref, o_ref,
                key_sc, coarse_sc, thr_sc, m_sc, acc_sc, *, topk):
    p = pl.program_id(0)

    @pl.when(first[p] == 1)
    def _():
        m_sc[...] = jnp.full(m_sc.shape, -jnp.inf, F32)
        acc_sc[...] = jnp.zeros(acc_sc.shape, F32)
        n = nkb[p]
        wi = wit_ref[...]
        qpos = qloc[p] * Q_TILE + lax.broadcasted_iota(jnp.int32, (K_TILE, Q_TILE), 1)
        krow_iota = lax.broadcasted_iota(jnp.int32, (K_TILE, Q_TILE), 0)

        def score_chunk(c, carry):
            kt = kidx_ref[pl.ds(pl.multiple_of(c * K_TILE, K_TILE), K_TILE), :]
            sc = jnp.zeros((K_TILE, Q_TILE), F32)
            for jp in range(IDX_PAIRS):
                qi_t = qit_ref[jp]
                for cc in range(2):
                    raw = jnp.dot(kt, _half_select(qi_t, cc == 0), preferred_element_type=F32)
                    hh = 2 * jp + cc
                    sc = sc + jnp.maximum(raw, 0.0) * wi[hh:hh + 1, :]
            kpos = c * K_TILE + krow_iota
            sc = jnp.where(kpos < N_META, BIG_SCORE, sc)
            sc = jnp.where(kpos <= qpos, sc, -jnp.inf)
            sc = jnp.where(jnp.abs(sc) < F32_MIN_NORMAL, 0.0, sc)
            bits = lax.bitcast_convert_type(sc, jnp.int32)
            key_sc[c] = bits ^ ((bits >> 31) & 0x7FFFFFFF)
            coarse_sc[c] = lax.bitcast_convert_type(bits & jnp.int32(-65536), F32).astype(BF16)
            return carry

        lax.fori_loop(0, n, score_chunk, 0)

        kf = float(topk)
        q1 = qloc[p] * Q_TILE + lax.broadcasted_iota(jnp.int32, (1, Q_TILE), 1)
        settled0 = (q1 < topk).astype(jnp.int32)
        coarse_bits = 16

        def coarse_body(b, state):
            lo, settled = state
            cand = lo + jnp.left_shift(jnp.int32(1), 31 - b)
            top = cand >> 16
            top = jnp.where(jnp.logical_and(top > 0, top < BF16_MIN_NORMAL_BITS), BF16_MIN_NORMAL_BITS, top)
            cbits = jnp.left_shift(top ^ ((top >> 31) & 0x7FFF), 16)
            cand16 = lax.bitcast_convert_type(cbits, F32).astype(BF16)

            def count_chunk(c, cnt16):
                hit = jnp.where(coarse_sc[c] >= cand16, jnp.ones((), BF16), jnp.zeros((), BF16))
                parts = [hit[i * BF16_ROWS:(i + 1) * BF16_ROWS] for i in range(K_TILE // BF16_ROWS)]
                while len(parts) > 1:
                    parts = [parts[i] + parts[i + 1] for i in range(0, len(parts), 2)]
                return cnt16 + parts[0].astype(F32)

            cnt16 = _fori_unrolled(n, count_chunk, jnp.zeros((BF16_ROWS, Q_TILE), F32))
            cnt = jnp.sum(cnt16, axis=0, keepdims=True)
            return jnp.where(cnt >= kf, cand, lo), jnp.where(cnt == kf, 1, settled)

        lo, settled = lax.fori_loop(0, coarse_bits, coarse_body,
                                    (jnp.full((1, Q_TILE), INT_MIN, jnp.int32), settled0))

        def unsettled(state):
            b, _, settled = state
            return jnp.logical_and(b < 32, jnp.min(settled) == 0)

        def count_where(hit_of):
            def count_chunk(c, cnt8):
                hit = jnp.where(hit_of(c, key_sc[c]), 1.0, 0.0).reshape(8, K_TILE // 64, 8, Q_TILE)
                return cnt8 + jnp.sum(jnp.sum(hit, axis=1), axis=0)

            cnt8 = _fori_unrolled(n, count_chunk, jnp.zeros((8, Q_TILE), F32))
            return jnp.sum(cnt8, axis=0, keepdims=True)

        def fine_body(state):
            b, lo, settled = state
            cand = lo + jnp.left_shift(jnp.int32(1), 31 - b)
            cnt = count_where(lambda c, keys: keys >= cand)
            return b + 1, jnp.where(cnt >= kf, cand, lo), jnp.where(cnt == kf, 1, settled)

        _, thr, settled = lax.while_loop(unsettled, fine_body, (jnp.int32(coarse_bits), lo, settled))
        thr = jnp.maximum(thr, KEY_NEG_INF + 1)
        thr_sc[...] = thr

        @pl.when(jnp.min(settled) == 0)
        def _():
            surplus = count_where(lambda c, keys: keys >= thr) - kf
            kk = lax.broadcasted_iota(jnp.int32, (K_TILE, K_TILE), 0)
            kc = lax.broadcasted_iota(jnp.int32, (K_TILE, K_TILE), 1)
            later_or_same = jnp.where(kc >= kk, 1.0, 0.0).astype(BF16)

            def demote(i, seen):
                c = n - 1 - i
                keys = key_sc[c]
                tie = keys == thr
                from_end = seen + jnp.dot(later_or_same, jnp.where(tie, 1.0, 0.0).astype(BF16),
                                          preferred_element_type=F32)
                drop = jnp.logical_and(tie, from_end <= surplus)
                key_sc[c] = jnp.where(drop, keys - 1, keys)
                return from_end[0:1, :]

            lax.fori_loop(0, n, demote, jnp.zeros((1, Q_TILE), F32))

    def mask_bias(sub):
        sel = key_sc[kloc[p] * K_SUBS + sub] >= thr_sc[...]
        return jnp.where(sel, 0.0, -jnp.inf).astype(F32)

    def key_tile(sub, carry):
        _attention_sweep(qt_ref, k_ref, vt_ref, (sub,), mask_bias, m_sc, acc_sc)
        return carry

    @pl.when(full[p] == 1)
    def _():
        _attention_sweep(qt_ref, k_ref, vt_ref, tuple(range(K_SUBS)), mask_bias, m_sc, acc_sc)

    @pl.when(full[p] == 0)
    def _():
        lax.fori_loop(0, nsub[p], key_tile, 0)

    @pl.when(last[p] == 1)
    def _():
        ones_row = 2 * HEAD_DIM
        for j in range(N_PAIRS):
            a = acc_sc[j]
            o = jnp.concatenate([a[:HEAD_DIM, :Q_TILE] / a[ones_row:ones_row + 1, :Q_TILE],
                                 a[HEAD_DIM:ones_row, Q_TILE:] / a[ones_row:ones_row + 1, Q_TILE:]],
                                axis=0)
            o_ref[j] = o.T.astype(BF16)


def _dsa_attention(q_t, k, v_t, qi_t, wi_t, kidx, steps, topk, t_pad):
    rows = k.shape[1]
    n_steps = steps["qrow"].shape[0]
    nkb_total = t_pad // K_TILE
    vrows = v_t.shape[2]

    def im(fn):
        return lambda p, qr, kr, ql, kl, nk, ns, fu, f, l, b: fn(p, qr, kr, b)

    grid_spec = pltpu.PrefetchScalarGridSpec(
        num_scalar_prefetch=10,
        grid=(n_steps,),
        in_specs=[
            pl.BlockSpec((N_PAIRS, LANES, Q_TILE), im(lambda p, qr, kr, b: (0, 0, qr[p]))),
            pl.BlockSpec((N_PAIRS, K_SUBS * K_TILE, LANES), im(lambda p, qr, kr, b: (0, kr[p], 0))),
            pl.BlockSpec((N_PAIRS, K_SUBS, vrows, K_TILE), im(lambda p, qr, kr, b: (0, kr[p], 0, 0))),
            pl.BlockSpec((IDX_PAIRS, LANES, Q_TILE), im(lambda p, qr, kr, b: (0, 0, qr[p]))),
            pl.BlockSpec((IDX_HEADS, Q_TILE), im(lambda p, qr, kr, b: (0, qr[p]))),
            pl.BlockSpec((t_pad, LANES), im(lambda p, qr, kr, b: (b[p], 0))),
        ],
        out_specs=pl.BlockSpec((N_PAIRS, Q_TILE, LANES), im(lambda p, qr, kr, b: (0, qr[p], 0))),
        scratch_shapes=[pltpu.VMEM((nkb_total, K_TILE, Q_TILE), jnp.int32),
                        pltpu.VMEM((nkb_total, K_TILE, Q_TILE), BF16),
                        pltpu.VMEM((1, Q_TILE), jnp.int32),
                        pltpu.VMEM((N_PAIRS, 1, 2 * Q_TILE), F32),
                        pltpu.VMEM((N_PAIRS, vrows, 2 * Q_TILE), F32)],
    )
    return pl.pallas_call(
        functools.partial(_dsa_kernel, topk=topk),
        grid_spec=grid_spec,
        out_shape=jax.ShapeDtypeStruct((N_PAIRS, rows, LANES), BF16),
        compiler_params=pltpu.CompilerParams(dimension_semantics=("arbitrary",),
                                             vmem_limit_bytes=VMEM_LIMIT),
        name="dsa_attn",
    )(steps["qrow"], steps["krow"], steps["qloc"], steps["kloc"], steps["nkb"], steps["nsub"],
      steps["full"], steps["first"], steps["last"], steps["b"], q_t, k, v_t, qi_t, wi_t, kidx)


def _rope_tables(t_pad):
    inv = ROPE_THETA ** (-jnp.arange(0, HEAD_DIM, 2, dtype=F32) / HEAD_DIM)
    ang = jnp.arange(t_pad, dtype=F32)[:, None] * inv[None, :]
    cos, sin = jnp.cos(ang), jnp.sin(ang)
    cos128 = jnp.tile(cos, (1, LANES // (HEAD_DIM // 2)))
    sin128 = jnp.tile(jnp.concatenate([-sin, sin], axis=1), (1, LANES // HEAD_DIM))
    return cos128, sin128


def _interleave_gate_val(a):
    lead = a.shape[:-1]
    a = a.reshape(lead + (2, FFN_HIDDEN // FFN_TILE, FFN_TILE))
    a = jnp.swapaxes(a, -3, -2)
    return a.reshape(lead + (2 * FFN_HIDDEN,))


def kernel(x, meta_tokens, da_norm, da_w_qkv, da_lambda_q1, da_lambda_k1, da_lambda_q2, da_lambda_k2, da_subln, da_w_o, dsa_norm, dsa_w_in, dsa_idx_k_norm, dsa_w_o, ffn_norm, ffn_w_up, ffn_conv_w, ffn_conv_b, ffn_w_down, final_norm):
    batch, seq, d = x.shape
    assert d == D_MODEL
    depth = ffn_norm.shape[0]
    t_real = seq + N_META
    t_pad = _padded_len(t_real)
    assert t_pad % (K_SUBS * K_TILE) == 0 and t_pad % Q_TILE == 0
    rows = batch * t_pad

    meta = jnp.broadcast_to(meta_tokens[None].astype(x.dtype), (batch, N_META, d))
    h = jnp.concatenate([meta, x, jnp.zeros((batch, t_pad - t_real, d), x.dtype)], axis=1)
    h = h.reshape(rows, d)
    cos128, sin128 = _rope_tables(t_pad)
    steps = _attention_steps(batch, t_pad)

    dsa_qkv = N_HEADS * HEAD_DIM
    idx_cols = IDX_HEADS * HEAD_DIM
    for i in range(depth):
        j = i // 2
        if i % 2 == 0:
            lambda_init = 0.8 - 0.6 * math.exp(-0.3 * i)
            q_t, k, v_t = _project(h, da_norm[j], da_w_qkv[j].astype(BF16), cos128, sin128, t_pad)
            lam_rows = jnp.zeros((8, LANES), F32).at[0:4, 0:HEAD_DIM].set(
                jnp.stack([da_lambda_q1[j], da_lambda_k1[j], da_lambda_q2[j], da_lambda_k2[j]]).astype(F32))
            o = _diff_attention(q_t, k, v_t, lam_rows, da_subln[j].reshape(LANES, 1).astype(F32),
                                steps, lambda_init)
            h = _out_project(h, o, da_w_o[j].astype(BF16))
        else:
            w_in = dsa_w_in[j]
            n_main = 3 * dsa_qkv + idx_cols
            w_tail = jnp.zeros((d, LANES), w_in.dtype).at[:, :w_in.shape[1] - n_main].set(w_in[:, n_main:])
            gk = jnp.ones((1, LANES), F32).at[0, :HEAD_DIM].set(dsa_idx_k_norm[j].astype(F32))
            q_t, k, v_t, qi_t, kidx, wi_t = _project(
                h, dsa_norm[j], w_in[:, :n_main].astype(BF16), cos128, sin128, t_pad,
                w_tail=w_tail.astype(BF16), gk=gk)
            topk = min(TOPK_MAX, seq // 4)
            o = _dsa_attention(q_t, k, v_t, qi_t, wi_t, kidx, steps, topk, t_pad)
            h = _out_project(h, o, dsa_w_o[j].astype(BF16))
        h = _ffn(h, ffn_norm[i], _interleave_gate_val(ffn_w_up[i]).astype(BF16),
                 _interleave_gate_val(ffn_conv_w[i]).astype(F32),
                 _interleave_gate_val(ffn_conv_b[i]).reshape(1, -1).astype(F32),
                 ffn_w_down[i].astype(BF16), final_norm, t_pad, final=(i == depth - 1))
    return h.reshape(batch, t_pad, d)[:, N_META:N_META + seq]
```

```python
import functools
import math

import numpy as np
import jax
import jax.numpy as jnp
from jax import lax
from jax.experimental import pallas as pl
from jax.experimental.pallas import tpu as pltpu

D_MODEL = 1024
N_META = 16
ROPE_THETA = 10000.0
RMS_EPS = 1e-6
HEAD_DIM = 64
N_PAIRS = 8
N_HEADS = 2 * N_PAIRS
IDX_HEADS = 8
IDX_PAIRS = IDX_HEADS // 2
TOPK_MAX = 256
BIG_SCORE = 1e30
LOG2_E = 1.4426950408889634
FFN_HIDDEN = 2816
CONV_WIDTH = 3

LANES = 128
BF16_ROWS = 16
ROW_TILE = 512
Q_TILE = 256
K_TILE = 512
K_SUBS = 3
FFN_TILE = 256
HALO = BF16_ROWS
VMEM_LIMIT = 56 * 1024 * 1024

F32 = jnp.float32
BF16 = jnp.bfloat16
INT_MIN = -2147483648
KEY_NEG_INF = INT_MIN + 0x7FFFFF
BF16_MIN_NORMAL_BITS = 0x0080
F32_MIN_NORMAL = 1.1754943508222875e-38


def _padded_len(t):
    unit = math.lcm(ROW_TILE, K_SUBS * K_TILE, Q_TILE)
    return ((t + unit - 1) // unit) * unit


def _rms(x, gain):
    ms = jnp.mean(x * x, axis=-1, keepdims=True)
    return x * lax.rsqrt(ms + RMS_EPS) * gain


def _rope_lanes(y, cos, sin, lo):
    sw = jnp.where(lo, pltpu.roll(y, LANES - HEAD_DIM // 2, 1), pltpu.roll(y, HEAD_DIM // 2, 1))
    return y * cos + sw * sin


def _proj_kernel(x_ref, g_ref, w_ref, cos_ref, sin_ref, *rest, has_idx):
    if has_idx:
        wt_ref, gk_ref, qt_ref, k_ref, vt_ref, qit_ref, kidx_ref, wit_ref = rest
    else:
        qt_ref, k_ref, vt_ref = rest
    xh = _rms(x_ref[...], g_ref[...]).astype(BF16)
    cos = cos_ref[...]
    sin = sin_ref[...]
    lane = lax.broadcasted_iota(jnp.int32, cos.shape, 1)
    lo = (lane & (HEAD_DIM // 2)) == 0
    n_groups = (3 * N_PAIRS + IDX_PAIRS) if has_idx else 3 * N_PAIRS
    cw = 4 * LANES
    ones = jnp.ones((BF16_ROWS, ROW_TILE), BF16)
    for c in range(n_groups // 4):
        y = jnp.dot(xh, w_ref[:, c * cw:(c + 1) * cw], preferred_element_type=F32)
        for s in range(4):
            g = c * 4 + s
            yg = y[:, s * LANES:(s + 1) * LANES]
            if g < N_PAIRS:
                yg = _rope_lanes(yg, cos, sin, lo) * (HEAD_DIM ** -0.5 * LOG2_E)
                qt_ref[g] = yg.T.astype(BF16)
            elif g < 2 * N_PAIRS:
                k_ref[g - N_PAIRS] = _rope_lanes(yg, cos, sin, lo).astype(BF16)
            elif g < 3 * N_PAIRS:
                j = g - 2 * N_PAIRS
                vt_ref[j, 0, 0:2 * HEAD_DIM, :] = yg.T.astype(BF16)
                vt_ref[j, 0, 2 * HEAD_DIM:, :] = ones
            else:
                yg = _rope_lanes(yg, cos, sin, lo) * (HEAD_DIM ** -0.5)
                qit_ref[g - 3 * N_PAIRS] = yg.T.astype(BF16)
    if has_idx:
        t = jnp.dot(xh, wt_ref[...], preferred_element_type=F32)
        is_k = lane < HEAD_DIM
        ms = jnp.sum(jnp.where(is_k, t * t, 0.0), axis=-1, keepdims=True) * (1.0 / HEAD_DIM)
        kn = t * lax.rsqrt(ms + RMS_EPS) * gk_ref[...]
        kn = _rope_lanes(kn, cos, sin, lo)
        kidx_ref[...] = jnp.where(is_k, kn, pltpu.roll(kn, HEAD_DIM, 1)).astype(BF16)
        wit_ref[...] = (t * (IDX_HEADS ** -0.5)).T[HEAD_DIM:HEAD_DIM + IDX_HEADS, :]


def _project(h, gain, w, cos, sin, t_pad, w_tail=None, gk=None):
    assert ROW_TILE == K_TILE
    rows = h.shape[0]
    blocks_per_batch = t_pad // ROW_TILE
    has_idx = w_tail is not None
    vrows = 2 * HEAD_DIM + BF16_ROWS
    in_specs = [
        pl.BlockSpec((ROW_TILE, D_MODEL), lambda i: (i, 0)),
        pl.BlockSpec((1, D_MODEL), lambda i: (0, 0)),
        pl.BlockSpec(w.shape, lambda i: (0, 0)),
        pl.BlockSpec((ROW_TILE, LANES), lambda i: (i % blocks_per_batch, 0)),
        pl.BlockSpec((ROW_TILE, LANES), lambda i: (i % blocks_per_batch, 0)),
    ]
    out_shape = [jax.ShapeDtypeStruct((N_PAIRS, LANES, rows), BF16),
                 jax.ShapeDtypeStruct((N_PAIRS, rows, LANES), BF16),
                 jax.ShapeDtypeStruct((N_PAIRS, rows // K_TILE, vrows, K_TILE), BF16)]
    out_specs = [pl.BlockSpec((N_PAIRS, LANES, ROW_TILE), lambda i: (0, 0, i)),
                 pl.BlockSpec((N_PAIRS, ROW_TILE, LANES), lambda i: (0, i, 0)),
                 pl.BlockSpec((N_PAIRS, 1, vrows, K_TILE), lambda i: (0, i, 0, 0))]
    args = [h, gain.reshape(1, D_MODEL), w, cos, sin]
    if has_idx:
        in_specs += [pl.BlockSpec(w_tail.shape, lambda i: (0, 0)),
                     pl.BlockSpec((1, LANES), lambda i: (0, 0))]
        out_shape += [jax.ShapeDtypeStruct((IDX_PAIRS, LANES, rows), BF16),
                      jax.ShapeDtypeStruct((rows, LANES), BF16),
                      jax.ShapeDtypeStruct((IDX_HEADS, rows), F32)]
        out_specs += [pl.BlockSpec((IDX_PAIRS, LANES, ROW_TILE), lambda i: (0, 0, i)),
                      pl.BlockSpec((ROW_TILE, LANES), lambda i: (i, 0)),
                      pl.BlockSpec((IDX_HEADS, ROW_TILE), lambda i: (0, i))]
        args += [w_tail, gk]
    return pl.pallas_call(
        functools.partial(_proj_kernel, has_idx=has_idx),
        grid=(rows // ROW_TILE,),
        in_specs=in_specs,
        out_specs=out_specs,
        out_shape=out_shape,
        compiler_params=pltpu.CompilerParams(dimension_semantics=("arbitrary",),
                                             vmem_limit_bytes=VMEM_LIMIT),
        name="proj_idx" if has_idx else "proj",
    )(*args)


def _oproj_kernel(h_ref, o_ref, w_ref, out_ref):
    o = jnp.concatenate([o_ref[j] for j in range(N_PAIRS)], axis=1)
    out_ref[...] = h_ref[...] + jnp.dot(o, w_ref[...], preferred_element_type=F32)


def _out_project(h, o, w):
    rows = h.shape[0]
    return pl.pallas_call(
        _oproj_kernel,
        grid=(rows // ROW_TILE,),
        in_specs=[pl.BlockSpec((ROW_TILE, D_MODEL), lambda i: (i, 0)),
                  pl.BlockSpec((N_PAIRS, ROW_TILE, LANES), lambda i: (0, i, 0)),
                  pl.BlockSpec(w.shape, lambda i: (0, 0))],
        out_specs=pl.BlockSpec((ROW_TILE, D_MODEL), lambda i: (i, 0)),
        out_shape=jax.ShapeDtypeStruct(h.shape, F32),
        compiler_params=pltpu.CompilerParams(dimension_semantics=("arbitrary",),
                                             vmem_limit_bytes=VMEM_LIMIT),
        name="oproj",
    )(h, o, w)


def _ffn_kernel(xp_ref, x_ref, g_ref, wup_ref, cw_ref, cb_ref, wdn_ref, fg_ref, o_ref, u_sc, a_sc,
                *, blocks_per_batch, final):
    i = pl.program_id(0)
    x = x_ref[...]
    xe = jnp.concatenate([xp_ref[...], x], axis=0)
    xh = _rms(xe, g_ref[...])
    row = lax.broadcasted_iota(jnp.int32, (HALO + ROW_TILE, 1), 0)
    keep = jnp.logical_or(row >= HALO, i % blocks_per_batch != 0)
    xh = jnp.where(keep, xh, 0.0).astype(BF16)
    cw2 = 2 * FFN_TILE
    n_chunks = FFN_HIDDEN // FFN_TILE

    def up(c):
        return jnp.dot(xh, wup_ref[:, c * cw2:(c + 1) * cw2], preferred_element_type=F32)

    u_sc[0] = up(0)
    for c in range(n_chunks):
        cur = c % 2
        after = None
        if c + 1 < n_chunks:
            u_sc[1 - cur] = up(c + 1)
            after = _zero_after(u_sc[1 - cur, 0:8, 0:FFN_TILE])
        cwc = cw_ref[:, c * cw2:(c + 1) * cw2]
        conv = cb_ref[:, c * cw2:(c + 1) * cw2]
        for j in range(CONV_WIDTH):
            conv = conv + cwc[j:j + 1, :] * u_sc[cur, pl.ds(HALO - (CONV_WIDTH - 1) + j, ROW_TILE), :]
        gate = conv[:, :FFN_TILE]
        val = conv[:, FFN_TILE:]
        a = gate * jax.nn.sigmoid(gate) * val
        if after is not None:
            cut = ROW_TILE // 2
            a = jnp.concatenate([a[:cut], a[cut:] + after], axis=0)
        a_sc[:, c * FFN_TILE:(c + 1) * FFN_TILE] = a.astype(BF16)
    acc = jnp.dot(a_sc[...], wdn_ref[...], preferred_element_type=F32)
    y = x + acc
    if final:
        y = _rms(y, fg_ref[...])
    o_ref[...] = y


def _ffn(h, gain, w_up, conv_w, conv_b, w_down, final_gain, t_pad, final):
    rows = h.shape[0]
    blocks_per_batch = t_pad // ROW_TILE
    halo_blocks = ROW_TILE // HALO
    return pl.pallas_call(
        functools.partial(_ffn_kernel, blocks_per_batch=blocks_per_batch, final=final),
        grid=(rows // ROW_TILE,),
        in_specs=[pl.BlockSpec((HALO, D_MODEL), lambda i: (jnp.maximum(i * halo_blocks - 1, 0), 0)),
                  pl.BlockSpec((ROW_TILE, D_MODEL), lambda i: (i, 0)),
                  pl.BlockSpec((1, D_MODEL), lambda i: (0, 0)),
                  pl.BlockSpec(w_up.shape, lambda i: (0, 0)),
                  pl.BlockSpec(conv_w.shape, lambda i: (0, 0)),
                  pl.BlockSpec(conv_b.shape, lambda i: (0, 0)),
                  pl.BlockSpec(w_down.shape, lambda i: (0, 0)),
                  pl.BlockSpec((1, D_MODEL), lambda i: (0, 0))],
        out_specs=pl.BlockSpec((ROW_TILE, D_MODEL), lambda i: (i, 0)),
        out_shape=jax.ShapeDtypeStruct(h.shape, F32),
        scratch_shapes=[pltpu.VMEM((2, HALO + ROW_TILE, 2 * FFN_TILE), F32),
                        pltpu.VMEM((ROW_TILE, FFN_HIDDEN), BF16)],
        compiler_params=pltpu.CompilerParams(dimension_semantics=("arbitrary",),
                                             vmem_limit_bytes=VMEM_LIMIT),
        name="ffn_final" if final else "ffn",
    )(h, h, gain.reshape(1, D_MODEL), w_up, conv_w, conv_b, w_down, final_gain.reshape(1, D_MODEL))


def _attention_steps(batch, t_pad):
    nq = t_pad // Q_TILE
    nk = t_pad // (K_SUBS * K_TILE)
    cols = {k: [] for k in ("qrow", "krow", "qloc", "kloc", "nkb", "nsub", "full", "first", "last", "b")}
    for b in range(batch):
        for qi in range(nq):
            n_tiles = ((qi + 1) * Q_TILE + K_TILE - 1) // K_TILE
            n = (n_tiles + K_SUBS - 1) // K_SUBS
            for ki in range(n):
                cols["qrow"].append(b * nq + qi)
                cols["krow"].append(b * nk + ki)
                cols["qloc"].append(qi)
                cols["kloc"].append(ki)
                cols["nkb"].append(n_tiles)
                cols["nsub"].append(min(K_SUBS, n_tiles - ki * K_SUBS))
                cols["full"].append(int((ki + 1) * K_SUBS < n_tiles))
                cols["first"].append(int(ki == 0))
                cols["last"].append(int(ki == n - 1))
                cols["b"].append(b)
    return {k: jnp.asarray(np.asarray(v, np.int32)) for k, v in cols.items()}


def _half_select(x_t, first_half):
    row = lax.broadcasted_iota(jnp.int32, x_t.shape, 0)
    keep = (row < HEAD_DIM) if first_half else (row >= HEAD_DIM)
    return jnp.where(keep, x_t, jnp.zeros_like(x_t))


COUNT_UNROLL = 4


def _fori_unrolled(n, body, init):
    def group(i, carry):
        for u in range(COUNT_UNROLL):
            carry = body(COUNT_UNROLL * i + u, carry)
        return carry

    groups = n // COUNT_UNROLL if isinstance(n, int) else lax.div(n, jnp.int32(COUNT_UNROLL))
    return lax.fori_loop(COUNT_UNROLL * groups, n, body, lax.fori_loop(0, groups, group, init))


def _zero_after(x):
    w = lax.bitcast_convert_type(x, jnp.uint32)
    w = lax.shift_right_logical(lax.shift_right_logical(w, jnp.uint32(16)), jnp.uint32(16))
    return lax.bitcast_convert_type(w, F32)[0:1, :]


def _flash_update(s_t, v_t, h, m_sc, acc_sc, after=None):
    m_prev = m_sc[h]
    m_new = jnp.maximum(m_prev, jnp.max(s_t, axis=0, keepdims=True))
    alpha = jnp.exp2(m_prev - m_new)
    if after is None:
        p_t = jnp.exp2(s_t - m_new).astype(BF16)
    else:
        cut = s_t.shape[0] // 2
        p_t = jnp.concatenate([jnp.exp2(s_t[:cut] - m_new),
                               jnp.exp2(s_t[cut:] - (m_new + after))], axis=0).astype(BF16)
    acc_sc[h] = alpha * acc_sc[h] + jnp.dot(v_t, p_t, preferred_element_type=F32)
    m_sc[h] = m_new


def _both_halves(x_t):
    return jnp.concatenate([_half_select(x_t, True), _half_select(x_t, False)], axis=1)


def _attention_sweep(qt_ref, k_ref, vt_ref, subs, bias_of, m_sc, acc_sc):
    def scores(sub, j):
        start = sub * K_TILE
        if not isinstance(sub, int):
            start = pl.multiple_of(start, K_TILE)
        return jnp.dot(k_ref[j, pl.ds(start, K_TILE), :], _both_halves(qt_ref[j]), preferred_element_type=F32)

    items = [(sub, j) for sub in subs for j in range(N_PAIRS)]
    biases = {}
    ahead = 2
    pending = [scores(*item) for item in items[:ahead]]
    for i, (sub, j) in enumerate(items):
        s_t = pending.pop(0)
        tile_key = sub if isinstance(sub, int) else "traced"
        if tile_key not in biases:
            biases[tile_key] = bias_of(sub)
        bias = biases[tile_key]
        if bias is not None and bias.shape[1] == Q_TILE:
            s_t = jnp.concatenate([s_t[:, :Q_TILE] + bias, s_t[:, Q_TILE:] + bias], axis=1)
        elif bias is not None:
            s_t = s_t + bias
        after = None
        if i + ahead < len(items):
            pending.append(scores(*items[i + ahead]))
            after = _zero_after(pending[-1][0:8, :])
        _flash_update(s_t, vt_ref[j, sub], j, m_sc, acc_sc, after=after)


def _diff_attn_kernel(qrow, krow, qloc, kloc, nsub, full, first, last,
                      qt_ref, k_ref, vt_ref, lam_ref, subln_ref, o_ref, m_sc, acc_sc,
                      *, lambda_init):
    p = pl.program_id(0)

    @pl.when(first[p] == 1)
    def _():
        m_sc[...] = jnp.full(m_sc.shape, -jnp.inf, F32)
        acc_sc[...] = jnp.zeros(acc_sc.shape, F32)

    def key_tile(sub, carry):
        tile = kloc[p] * K_SUBS + sub
        on_diagonal = tile * K_TILE + (K_TILE - 1) > qloc[p] * Q_TILE

        @pl.when(on_diagonal)
        def _():
            kpos = tile * K_TILE + lax.broadcasted_iota(jnp.int32, (K_TILE, 2 * Q_TILE), 0)
            qcol = lax.broadcasted_iota(jnp.int32, (K_TILE, 2 * Q_TILE), 1)
            qpos = qloc[p] * Q_TILE + jnp.where(qcol >= Q_TILE, qcol - Q_TILE, qcol)
            bias = jnp.where(kpos <= qpos, 0.0, -jnp.inf).astype(F32)
            _attention_sweep(qt_ref, k_ref, vt_ref, (sub,), lambda _: bias, m_sc, acc_sc)

        @pl.when(jnp.logical_not(on_diagonal))
        def _():
            _attention_sweep(qt_ref, k_ref, vt_ref, (sub,), lambda _: None, m_sc, acc_sc)

        return carry

    @pl.when(full[p] == 1)
    def _():
        _attention_sweep(qt_ref, k_ref, vt_ref, tuple(range(K_SUBS)), lambda _: None, m_sc, acc_sc)

    @pl.when(full[p] == 0)
    def _():
        lax.fori_loop(0, nsub[p], key_tile, 0)

    @pl.when(last[p] == 1)
    def _():
        lam_rows = lam_ref[...]
        lam = (jnp.exp(jnp.sum(lam_rows[0:1] * lam_rows[1:2], axis=-1, keepdims=True))
               - jnp.exp(jnp.sum(lam_rows[2:3] * lam_rows[3:4], axis=-1, keepdims=True))
               + lambda_init)
        vdim = 2 * HEAD_DIM
        for j in range(N_PAIRS):
            a = acc_sc[j]
            a1 = a[:, :Q_TILE]
            a2 = a[:, Q_TILE:]
            o = a1[:vdim] / a1[vdim:vdim + 1] - lam * (a2[:vdim] / a2[vdim:vdim + 1])
            ms = jnp.mean(o * o, axis=0, keepdims=True)
            o = o * lax.rsqrt(ms + RMS_EPS) * subln_ref[...] * (1.0 - lambda_init)
            o_ref[j] = o.T.astype(BF16)


def _diff_attention(q_t, k, v_t, lam_rows, subln_col, steps, lambda_init):
    rows = k.shape[1]
    n_steps = steps["qrow"].shape[0]
    vrows = v_t.shape[2]

    def im(fn):
        return lambda p, qr, kr, ql, kl, ns, fu, f, l: fn(p, qr, kr)

    grid_spec = pltpu.PrefetchScalarGridSpec(
        num_scalar_prefetch=8,
        grid=(n_steps,),
        in_specs=[
            pl.BlockSpec((N_PAIRS, LANES, Q_TILE), im(lambda p, qr, kr: (0, 0, qr[p]))),
            pl.BlockSpec((N_PAIRS, K_SUBS * K_TILE, LANES), im(lambda p, qr, kr: (0, kr[p], 0))),
            pl.BlockSpec((N_PAIRS, K_SUBS, vrows, K_TILE), im(lambda p, qr, kr: (0, kr[p], 0, 0))),
            pl.BlockSpec((8, LANES), im(lambda p, qr, kr: (0, 0))),
            pl.BlockSpec((LANES, 1), im(lambda p, qr, kr: (0, 0))),
        ],
        out_specs=pl.BlockSpec((N_PAIRS, Q_TILE, LANES), im(lambda p, qr, kr: (0, qr[p], 0))),
        scratch_shapes=[pltpu.VMEM((N_PAIRS, 1, 2 * Q_TILE), F32),
                        pltpu.VMEM((N_PAIRS, vrows, 2 * Q_TILE), F32)],
    )
    return pl.pallas_call(
        functools.partial(_diff_attn_kernel, lambda_init=lambda_init),
        grid_spec=grid_spec,
        out_shape=jax.ShapeDtypeStruct((N_PAIRS, rows, LANES), BF16),
        compiler_params=pltpu.CompilerParams(dimension_semantics=("arbitrary",),
                                             vmem_limit_bytes=VMEM_LIMIT),
        name="diff_attn",
    )(steps["qrow"], steps["krow"], steps["qloc"], steps["kloc"], steps["nsub"], steps["full"],
      steps["first"], steps["last"], q_t, k, v_t, lam_rows, subln_col)


def _dsa_kernel(qrow, krow, qloc, kloc, nkb, nsub, full, first, last, bidx,
                qt_ref, k_ref, vt_ref, qit_ref, wit_ref, kidx_ref, o_ref,
                key_sc, coarse_sc, thr_sc, m_sc, acc_sc, *, topk):
    p = pl.program_id(0)

    @pl.when(first[p] == 1)
    def _():
        m_sc[...] = jnp.full(m_sc.shape, -jnp.inf, F32)
        acc_sc[...] = jnp.zeros(acc_sc.shape, F32)
        n = nkb[p]
        wi = wit_ref[...]
        qpos = qloc[p] * Q_TILE + lax.broadcasted_iota(jnp.int32, (K_TILE, Q_TILE), 1)
        krow_iota = lax.broadcasted_iota(jnp.int32, (K_TILE, Q_TILE), 0)

        def score_chunk(c, carry):
            kt = kidx_ref[pl.ds(pl.multiple_of(c * K_TILE, K_TILE), K_TILE), :]
            sc = jnp.zeros((K_TILE, Q_TILE), F32)
            for jp in range(IDX_PAIRS):
                qi_t = qit_ref[jp]
                for cc in range(2):
                    raw = jnp.dot(kt, _half_select(qi_t, cc == 0), preferred_element_type=F32)
                    hh = 2 * jp + cc
                    sc = sc + jnp.maximum(raw, 0.0) * wi[hh:hh + 1, :]
            kpos = c * K_TILE + krow_iota
            sc = jnp.where(kpos < N_META, BIG_SCORE, sc)
            sc = jnp.where(kpos <= qpos, sc, -jnp.inf)
            sc = jnp.where(jnp.abs(sc) < F32_MIN_NORMAL, 0.0, sc)
            bits = lax.bitcast_convert_type(sc, jnp.int32)
            key_sc[c] = bits ^ ((bits >> 31) & 0x7FFFFFFF)
            coarse_sc[c] = lax.bitcast_convert_type(bits & jnp.int32(-65536), F32).astype(BF16)
            return carry

        def score_pair(i, carry):
            return score_chunk(2 * i + 1, score_chunk(2 * i, carry))

        pairs = lax.shift_right_logical(n, 1)
        lax.fori_loop(2 * pairs, n, score_chunk, lax.fori_loop(0, pairs, score_pair, 0))

        kf = float(topk)
        q1 = qloc[p] * Q_TILE + lax.broadcasted_iota(jnp.int32, (1, Q_TILE), 1)
        settled0 = (q1 < topk).astype(jnp.int32)
        coarse_bits = 16

        def coarse_body(b, state):
            lo, settled = state
            cand = lo + jnp.left_shift(jnp.int32(1), 31 - b)
            top = cand >> 16
            top = jnp.where(jnp.logical_and(top > 0, top < BF16_MIN_NORMAL_BITS), BF16_MIN_NORMAL_BITS, top)
            cbits = jnp.left_shift(top ^ ((top >> 31) & 0x7FFF), 16)
            cand16 = lax.bitcast_convert_type(cbits, F32).astype(BF16)

            def count_chunk(c, cnt16):
                hit = jnp.where(coarse_sc[c] >= cand16, jnp.ones((), BF16), jnp.zeros((), BF16))
                parts = [hit[i * BF16_ROWS:(i + 1) * BF16_ROWS] for i in range(K_TILE // BF16_ROWS)]
                while len(parts) > 1:
                    parts = [parts[i] + parts[i + 1] for i in range(0, len(parts), 2)]
                return cnt16 + parts[0].astype(F32)

            cnt16 = _fori_unrolled(n, count_chunk, jnp.zeros((BF16_ROWS, Q_TILE), F32))
            cnt = jnp.sum(cnt16, axis=0, keepdims=True)
            return jnp.where(cnt >= kf, cand, lo), jnp.where(cnt == kf, 1, settled)

        lo, settled = lax.fori_loop(0, coarse_bits, coarse_body,
                                    (jnp.full((1, Q_TILE), INT_MIN, jnp.int32), settled0))

        def unsettled(state):
            b, _, settled = state
            return jnp.logical_and(b < 32, jnp.min(settled) == 0)

        def count_where(hit_of):
            def count_chunk(c, cnt8):
                hit = jnp.where(hit_of(c, key_sc[c]), 1.0, 0.0).reshape(8, K_TILE // 64, 8, Q_TILE)
                return cnt8 + jnp.sum(jnp.sum(hit, axis=1), axis=0)

            cnt8 = _fori_unrolled(n, count_chunk, jnp.zeros((8, Q_TILE), F32))
            return jnp.sum(cnt8, axis=0, keepdims=True)

        def fine_body(state):
            b, lo, settled = state
            cand = lo + jnp.left_shift(jnp.int32(1), 31 - b)
            cnt = count_where(lambda c, keys: keys >= cand)
            return b + 1, jnp.where(cnt >= kf, cand, lo), jnp.where(cnt == kf, 1, settled)

        _, thr, settled = lax.while_loop(unsettled, fine_body, (jnp.int32(coarse_bits), lo, settled))
        thr = jnp.maximum(thr, KEY_NEG_INF + 1)
        thr_sc[...] = thr

        @pl.when(jnp.min(settled) == 0)
        def _():
            surplus = count_where(lambda c, keys: keys >= thr) - kf
            kk = lax.broadcasted_iota(jnp.int32, (K_TILE, K_TILE), 0)
            kc = lax.broadcasted_iota(jnp.int32, (K_TILE, K_TILE), 1)
            later_or_same = jnp.where(kc >= kk, 1.0, 0.0).astype(BF16)

            def demote(i, seen):
                c = n - 1 - i
                keys = key_sc[c]
                tie = keys == thr
                from_end = seen + jnp.dot(later_or_same, jnp.where(tie, 1.0, 0.0).astype(BF16),
                                          preferred_element_type=F32)
                drop = jnp.logical_and(tie, from_end <= surplus)
                key_sc[c] = jnp.where(drop, keys - 1, keys)
                return from_end[0:1, :]

            lax.fori_loop(0, n, demote, jnp.zeros((1, Q_TILE), F32))

    def mask_bias(sub):
        sel = key_sc[kloc[p] * K_SUBS + sub] >= thr_sc[...]
        return jnp.where(sel, 0.0, -jnp.inf).astype(F32)

    def key_tile(sub, carry):
        _attention_sweep(qt_ref, k_ref, vt_ref, (sub,), mask_bias, m_sc, acc_sc)
        return carry

    @pl.when(full[p] == 1)
    def _():
        _attention_sweep(qt_ref, k_ref, vt_ref, tuple(range(K_SUBS)), mask_bias, m_sc, acc_sc)

    @pl.when(full[p] == 0)
    def _():
        lax.fori_loop(0, nsub[p], key_tile, 0)

    @pl.when(last[p] == 1)
    def _():
        ones_row = 2 * HEAD_DIM
        for j in range(N_PAIRS):
            a = acc_sc[j]
            o = jnp.concatenate([a[:HEAD_DIM, :Q_TILE] / a[ones_row:ones_row + 1, :Q_TILE],
                                 a[HEAD_DIM:ones_row, Q_TILE:] / a[ones_row:ones_row + 1, Q_TILE:]],
                                axis=0)
            o_ref[j] = o.T.astype(BF16)


def _dsa_attention(q_t, k, v_t, qi_t, wi_t, kidx, steps, topk, t_pad):
    rows = k.shape[1]
    n_steps = steps["qrow"].shape[0]
    nkb_total = t_pad // K_TILE
    vrows = v_t.shape[2]

    def im(fn):
        return lambda p, qr, kr, ql, kl, nk, ns, fu, f, l, b: fn(p, qr, kr, b)

    grid_spec = pltpu.PrefetchScalarGridSpec(
        num_scalar_prefetch=10,
        grid=(n_steps,),
        in_specs=[
            pl.BlockSpec((N_PAIRS, LANES, Q_TILE), im(lambda p, qr, kr, b: (0, 0, qr[p]))),
            pl.BlockSpec((N_PAIRS, K_SUBS * K_TILE, LANES), im(lambda p, qr, kr, b: (0, kr[p], 0))),
            pl.BlockSpec((N_PAIRS, K_SUBS, vrows, K_TILE), im(lambda p, qr, kr, b: (0, kr[p], 0, 0))),
            pl.BlockSpec((IDX_PAIRS, LANES, Q_TILE), im(lambda p, qr, kr, b: (0, 0, qr[p]))),
            pl.BlockSpec((IDX_HEADS, Q_TILE), im(lambda p, qr, kr, b: (0, qr[p]))),
            pl.BlockSpec((t_pad, LANES), im(lambda p, qr, kr, b: (b[p], 0))),
        ],
        out_specs=pl.BlockSpec((N_PAIRS, Q_TILE, LANES), im(lambda p, qr, kr, b: (0, qr[p], 0))),
        scratch_shapes=[pltpu.VMEM((nkb_total, K_TILE, Q_TILE), jnp.int32),
                        pltpu.VMEM((nkb_total, K_TILE, Q_TILE), BF16),
                        pltpu.VMEM((1, Q_TILE), jnp.int32),
                        pltpu.VMEM((N_PAIRS, 1, 2 * Q_TILE), F32),
                        pltpu.VMEM((N_PAIRS, vrows, 2 * Q_TILE), F32)],
    )
    return pl.pallas_call(
        functools.partial(_dsa_kernel, topk=topk),
        grid_spec=grid_spec,
        out_shape=jax.ShapeDtypeStruct((N_PAIRS, rows, LANES), BF16),
        compiler_params=pltpu.CompilerParams(dimension_semantics=("arbitrary",),
                                             vmem_limit_bytes=VMEM_LIMIT),
        name="dsa_attn",
    )(steps["qrow"], steps["krow"], steps["qloc"], steps["kloc"], steps["nkb"], steps["nsub"],
      steps["full"], steps["first"], steps["last"], steps["b"], q_t, k, v_t, qi_t, wi_t, kidx)


def _rope_tables(t_pad):
    inv = ROPE_THETA ** (-jnp.arange(0, HEAD_DIM, 2, dtype=F32) / HEAD_DIM)
    ang = jnp.arange(t_pad, dtype=F32)[:, None] * inv[None, :]
    cos, sin = jnp.cos(ang), jnp.sin(ang)
    cos128 = jnp.tile(cos, (1, LANES // (HEAD_DIM // 2)))
    sin128 = jnp.tile(jnp.concatenate([-sin, sin], axis=1), (1, LANES // HEAD_DIM))
    return cos128, sin128


def _interleave_gate_val(a):
    lead = a.shape[:-1]
    a = a.reshape(lead + (2, FFN_HIDDEN // FFN_TILE, FFN_TILE))
    a = jnp.swapaxes(a, -3, -2)
    return a.reshape(lead + (2 * FFN_HIDDEN,))


def kernel(x, meta_tokens, da_norm, da_w_qkv, da_lambda_q1, da_lambda_k1, da_lambda_q2, da_lambda_k2, da_subln, da_w_o, dsa_norm, dsa_w_in, dsa_idx_k_norm, dsa_w_o, ffn_norm, ffn_w_up, ffn_conv_w, ffn_conv_b, ffn_w_down, final_norm):
    batch, seq, d = x.shape
    assert d == D_MODEL
    depth = ffn_norm.shape[0]
    t_real = seq + N_META
    t_pad = _padded_len(t_real)
    assert t_pad % (K_SUBS * K_TILE) == 0 and t_pad % Q_TILE == 0
    rows = batch * t_pad

    meta = jnp.broadcast_to(meta_tokens[None].astype(x.dtype), (batch, N_META, d))
    h = jnp.concatenate([meta, x, jnp.zeros((batch, t_pad - t_real, d), x.dtype)], axis=1)
    h = h.reshape(rows, d)
    cos128, sin128 = _rope_tables(t_pad)
    steps = _attention_steps(batch, t_pad)

    dsa_qkv = N_HEADS * HEAD_DIM
    idx_cols = IDX_HEADS * HEAD_DIM
    for i in range(depth):
        j = i // 2
        if i % 2 == 0:
            lambda_init = 0.8 - 0.6 * math.exp(-0.3 * i)
            q_t, k, v_t = _project(h, da_norm[j], da_w_qkv[j].astype(BF16), cos128, sin128, t_pad)
            lam_rows = jnp.zeros((8, LANES), F32).at[0:4, 0:HEAD_DIM].set(
                jnp.stack([da_lambda_q1[j], da_lambda_k1[j], da_lambda_q2[j], da_lambda_k2[j]]).astype(F32))
            o = _diff_attention(q_t, k, v_t, lam_rows, da_subln[j].reshape(LANES, 1).astype(F32),
                                steps, lambda_init)
            h = _out_project(h, o, da_w_o[j].astype(BF16))
        else:
            w_in = dsa_w_in[j]
            n_main = 3 * dsa_qkv + idx_cols
            w_tail = jnp.zeros((d, LANES), w_in.dtype).at[:, :w_in.shape[1] - n_main].set(w_in[:, n_main:])
            gk = jnp.ones((1, LANES), F32).at[0, :HEAD_DIM].set(dsa_idx_k_norm[j].astype(F32))
            q_t, k, v_t, qi_t, kidx, wi_t = _project(
                h, dsa_norm[j], w_in[:, :n_main].astype(BF16), cos128, sin128, t_pad,
                w_tail=w_tail.astype(BF16), gk=gk)
            topk = min(TOPK_MAX, seq // 4)
            o = _dsa_attention(q_t, k, v_t, qi_t, wi_t, kidx, steps, topk, t_pad)
            h = _out_project(h, o, dsa_w_o[j].astype(BF16))
        h = _ffn(h, ffn_norm[i], _interleave_gate_val(ffn_w_up[i]).astype(BF16),
                 _interleave_gate_val(ffn_conv_w[i]).astype(F32),
                 _interleave_gate_val(ffn_conv_b[i]).reshape(1, -1).astype(F32),
                 ffn_w_down[i].astype(BF16), final_norm, t_pad, final=(i == depth - 1))
    return h.reshape(batch, t_pad, d)[:, N_META:N_META + seq]
```

```python
import functools
import math

import numpy as np
import jax
import jax.numpy as jnp
from jax import lax
from jax.experimental import pallas as pl
from jax.experimental.pallas import tpu as pltpu

D_MODEL = 1024
N_META = 16
ROPE_THETA = 10000.0
RMS_EPS = 1e-6
HEAD_DIM = 64
N_PAIRS = 8
N_HEADS = 2 * N_PAIRS
IDX_HEADS = 8
IDX_PAIRS = IDX_HEADS // 2
TOPK_MAX = 256
BIG_SCORE = 1e30
LOG2_E = 1.4426950408889634
FFN_HIDDEN = 2816
CONV_WIDTH = 3

LANES = 128
BF16_ROWS = 16
ROW_TILE = 512
Q_TILE = 256
K_TILE = 512
K_SUBS = 3
FFN_TILE = 256
HALO = BF16_ROWS
VMEM_LIMIT = 56 * 1024 * 1024

F32 = jnp.float32
BF16 = jnp.bfloat16
INT_MIN = -2147483648
KEY_NEG_INF = INT_MIN + 0x7FFFFF
BF16_MIN_NORMAL_BITS = 0x0080
F32_MIN_NORMAL = 1.1754943508222875e-38


def _padded_len(t):
    unit = math.lcm(ROW_TILE, K_SUBS * K_TILE, Q_TILE)
    return ((t + unit - 1) // unit) * unit


def _rms(x, gain):
    ms = jnp.mean(x * x, axis=-1, keepdims=True)
    return x * lax.rsqrt(ms + RMS_EPS) * gain


def _rope_lanes(y, cos, sin, lo):
    sw = jnp.where(lo, pltpu.roll(y, LANES - HEAD_DIM // 2, 1), pltpu.roll(y, HEAD_DIM // 2, 1))
    return y * cos + sw * sin


def _proj_kernel(x_ref, g_ref, w_ref, cos_ref, sin_ref, *rest, has_idx):
    if has_idx:
        wt_ref, gk_ref, qt_ref, k_ref, vt_ref, qit_ref, kidx_ref, wit_ref = rest
    else:
        qt_ref, k_ref, vt_ref = rest
    xh = _rms(x_ref[...], g_ref[...]).astype(BF16)
    cos = cos_ref[...]
    sin = sin_ref[...]
    lane = lax.broadcasted_iota(jnp.int32, cos.shape, 1)
    lo = (lane & (HEAD_DIM // 2)) == 0
    n_groups = (3 * N_PAIRS + IDX_PAIRS) if has_idx else 3 * N_PAIRS
    cw = 4 * LANES
    ones = jnp.ones((BF16_ROWS, ROW_TILE), BF16)
    for c in range(n_groups // 4):
        y = jnp.dot(xh, w_ref[:, c * cw:(c + 1) * cw], preferred_element_type=F32)
        for s in range(4):
            g = c * 4 + s
            yg = y[:, s * LANES:(s + 1) * LANES]
            if g < N_PAIRS:
                yg = _rope_lanes(yg, cos, sin, lo) * (HEAD_DIM ** -0.5 * LOG2_E)
                qt_ref[g] = yg.T.astype(BF16)
            elif g < 2 * N_PAIRS:
                k_ref[g - N_PAIRS] = _rope_lanes(yg, cos, sin, lo).astype(BF16)
            elif g < 3 * N_PAIRS:
                j = g - 2 * N_PAIRS
                vt_ref[j, 0, 0:2 * HEAD_DIM, :] = yg.T.astype(BF16)
                vt_ref[j, 0, 2 * HEAD_DIM:, :] = ones
            else:
                yg = _rope_lanes(yg, cos, sin, lo) * (HEAD_DIM ** -0.5)
                qit_ref[g - 3 * N_PAIRS] = yg.T.astype(BF16)
    if has_idx:
        t = jnp.dot(xh, wt_ref[...], preferred_element_type=F32)
        is_k = lane < HEAD_DIM
        ms = jnp.sum(jnp.where(is_k, t * t, 0.0), axis=-1, keepdims=True) * (1.0 / HEAD_DIM)
        kn = t * lax.rsqrt(ms + RMS_EPS) * gk_ref[...]
        kn = _rope_lanes(kn, cos, sin, lo)
        kidx_ref[...] = jnp.where(is_k, kn, pltpu.roll(kn, HEAD_DIM, 1)).astype(BF16)
        wit_ref[...] = (t * (IDX_HEADS ** -0.5)).T[HEAD_DIM:HEAD_DIM + IDX_HEADS, :]


def _project(h, gain, w, cos, sin, t_pad, w_tail=None, gk=None):
    assert ROW_TILE == K_TILE
    rows = h.shape[0]
    blocks_per_batch = t_pad // ROW_TILE
    has_idx = w_tail is not None
    vrows = 2 * HEAD_DIM + BF16_ROWS
    in_specs = [
        pl.BlockSpec((ROW_TILE, D_MODEL), lambda i: (i, 0)),
        pl.BlockSpec((1, D_MODEL), lambda i: (0, 0)),
        pl.BlockSpec(w.shape, lambda i: (0, 0)),
        pl.BlockSpec((ROW_TILE, LANES), lambda i: (i % blocks_per_batch, 0)),
        pl.BlockSpec((ROW_TILE, LANES), lambda i: (i % blocks_per_batch, 0)),
    ]
    out_shape = [jax.ShapeDtypeStruct((N_PAIRS, LANES, rows), BF16),
                 jax.ShapeDtypeStruct((N_PAIRS, rows, LANES), BF16),
                 jax.ShapeDtypeStruct((N_PAIRS, rows // K_TILE, vrows, K_TILE), BF16)]
    out_specs = [pl.BlockSpec((N_PAIRS, LANES, ROW_TILE), lambda i: (0, 0, i)),
                 pl.BlockSpec((N_PAIRS, ROW_TILE, LANES), lambda i: (0, i, 0)),
                 pl.BlockSpec((N_PAIRS, 1, vrows, K_TILE), lambda i: (0, i, 0, 0))]
    args = [h, gain.reshape(1, D_MODEL), w, cos, sin]
    if has_idx:
        in_specs += [pl.BlockSpec(w_tail.shape, lambda i: (0, 0)),
                     pl.BlockSpec((1, LANES), lambda i: (0, 0))]
        out_shape += [jax.ShapeDtypeStruct((IDX_PAIRS, LANES, rows), BF16),
                      jax.ShapeDtypeStruct((rows, LANES), BF16),
                      jax.ShapeDtypeStruct((IDX_HEADS, rows), F32)]
        out_specs += [pl.BlockSpec((IDX_PAIRS, LANES, ROW_TILE), lambda i: (0, 0, i)),
                      pl.BlockSpec((ROW_TILE, LANES), lambda i: (i, 0)),
                      pl.BlockSpec((IDX_HEADS, ROW_TILE), lambda i: (0, i))]
        args += [w_tail, gk]
    return pl.pallas_call(
        functools.partial(_proj_kernel, has_idx=has_idx),
        grid=(rows // ROW_TILE,),
        in_specs=in_specs,
        out_specs=out_specs,
        out_shape=out_shape,
        compiler_params=pltpu.CompilerParams(dimension_semantics=("arbitrary",),
                                             vmem_limit_bytes=VMEM_LIMIT),
        name="proj_idx" if has_idx else "proj",
    )(*args)


def _oproj_kernel(h_ref, o_ref, w_ref, out_ref):
    o = jnp.concatenate([o_ref[j] for j in range(N_PAIRS)], axis=1)
    out_ref[...] = h_ref[...] + jnp.dot(o, w_ref[...], preferred_element_type=F32)


def _out_project(h, o, w):
    rows = h.shape[0]
    return pl.pallas_call(
        _oproj_kernel,
        grid=(rows // ROW_TILE,),
        in_specs=[pl.BlockSpec((ROW_TILE, D_MODEL), lambda i: (i, 0)),
                  pl.BlockSpec((N_PAIRS, ROW_TILE, LANES), lambda i: (0, i, 0)),
                  pl.BlockSpec(w.shape, lambda i: (0, 0))],
        out_specs=pl.BlockSpec((ROW_TILE, D_MODEL), lambda i: (i, 0)),
        out_shape=jax.ShapeDtypeStruct(h.shape, F32),
        compiler_params=pltpu.CompilerParams(dimension_semantics=("arbitrary",),
                                             vmem_limit_bytes=VMEM_LIMIT),
        name="oproj",
    )(h, o, w)


def _ffn_kernel(xp_ref, x_ref, g_ref, wup_ref, cw_ref, cb_ref, wdn_ref, fg_ref, o_ref, u_sc, a_sc,
                *, blocks_per_batch, final):
    i = pl.program_id(0)
    x = x_ref[...]
    xe = jnp.concatenate([xp_ref[...], x], axis=0)
    xh = _rms(xe, g_ref[...])
    row = lax.broadcasted_iota(jnp.int32, (HALO + ROW_TILE, 1), 0)
    keep = jnp.logical_or(row >= HALO, i % blocks_per_batch != 0)
    xh = jnp.where(keep, xh, 0.0).astype(BF16)
    cw2 = 2 * FFN_TILE
    n_chunks = FFN_HIDDEN // FFN_TILE

    def up(c):
        return jnp.dot(xh, wup_ref[:, c * cw2:(c + 1) * cw2], preferred_element_type=F32)

    u_sc[0] = up(0)
    for c in range(n_chunks):
        cur = c % 2
        after = None
        if c + 1 < n_chunks:
            u_sc[1 - cur] = up(c + 1)
            after = _zero_after(u_sc[1 - cur, 0:8, 0:FFN_TILE])
        cwc = cw_ref[:, c * cw2:(c + 1) * cw2]
        conv = cb_ref[:, c * cw2:(c + 1) * cw2]
        for j in range(CONV_WIDTH):
            conv = conv + cwc[j:j + 1, :] * u_sc[cur, pl.ds(HALO - (CONV_WIDTH - 1) + j, ROW_TILE), :]
        gate = conv[:, :FFN_TILE]
        val = conv[:, FFN_TILE:]
        a = gate * jax.nn.sigmoid(gate) * val
        if after is not None:
            cut = ROW_TILE // 2
            a = jnp.concatenate([a[:cut], a[cut:] + after], axis=0)
        a_sc[:, c * FFN_TILE:(c + 1) * FFN_TILE] = a.astype(BF16)
    acc = jnp.dot(a_sc[...], wdn_ref[...], preferred_element_type=F32)
    y = x + acc
    if final:
        y = _rms(y, fg_ref[...])
    o_ref[...] = y


def _ffn(h, gain, w_up, conv_w, conv_b, w_down, final_gain, t_pad, final):
    rows = h.shape[0]
    blocks_per_batch = t_pad // ROW_TILE
    halo_blocks = ROW_TILE // HALO
    return pl.pallas_call(
        functools.partial(_ffn_kernel, blocks_per_batch=blocks_per_batch, final=final),
        grid=(rows // ROW_TILE,),
        in_specs=[pl.BlockSpec((HALO, D_MODEL), lambda i: (jnp.maximum(i * halo_blocks - 1, 0), 0)),
                  pl.BlockSpec((ROW_TILE, D_MODEL), lambda i: (i, 0)),
                  pl.BlockSpec((1, D_MODEL), lambda i: (0, 0)),
                  pl.BlockSpec(w_up.shape, lambda i: (0, 0)),
                  pl.BlockSpec(conv_w.shape, lambda i: (0, 0)),
                  pl.BlockSpec(conv_b.shape, lambda i: (0, 0)),
                  pl.BlockSpec(w_down.shape, lambda i: (0, 0)),
                  pl.BlockSpec((1, D_MODEL), lambda i: (0, 0))],
        out_specs=pl.BlockSpec((ROW_TILE, D_MODEL), lambda i: (i, 0)),
        out_shape=jax.ShapeDtypeStruct(h.shape, F32),
        scratch_shapes=[pltpu.VMEM((2, HALO + ROW_TILE, 2 * FFN_TILE), F32),
                        pltpu.VMEM((ROW_TILE, FFN_HIDDEN), BF16)],
        compiler_params=pltpu.CompilerParams(dimension_semantics=("arbitrary",),
                                             vmem_limit_bytes=VMEM_LIMIT),
        name="ffn_final" if final else "ffn",
    )(h, h, gain.reshape(1, D_MODEL), w_up, conv_w, conv_b, w_down, final_gain.reshape(1, D_MODEL))


def _attention_steps(batch, t_pad):
    nq = t_pad // Q_TILE
    nk = t_pad // (K_SUBS * K_TILE)
    cols = {k: [] for k in ("qrow", "krow", "qloc", "kloc", "nkb", "nsub", "full", "first", "last", "b")}
    for b in range(batch):
        for qi in range(nq):
            n_tiles = ((qi + 1) * Q_TILE + K_TILE - 1) // K_TILE
            n = (n_tiles + K_SUBS - 1) // K_SUBS
            for ki in range(n):
                cols["qrow"].append(b * nq + qi)
                cols["krow"].append(b * nk + ki)
                cols["qloc"].append(qi)
                cols["kloc"].append(ki)
                cols["nkb"].append(n_tiles)
                cols["nsub"].append(min(K_SUBS, n_tiles - ki * K_SUBS))
                cols["full"].append(int((ki + 1) * K_SUBS < n_tiles))
                cols["first"].append(int(ki == 0))
                cols["last"].append(int(ki == n - 1))
                cols["b"].append(b)
    return {k: jnp.asarray(np.asarray(v, np.int32)) for k, v in cols.items()}


def _half_select(x_t, first_half):
    row = lax.broadcasted_iota(jnp.int32, x_t.shape, 0)
    keep = (row < HEAD_DIM) if first_half else (row >= HEAD_DIM)
    return jnp.where(keep, x_t, jnp.zeros_like(x_t))


COUNT_UNROLL = 4


def _fori_unrolled(n, body, init):
    def group(i, carry):
        for u in range(COUNT_UNROLL):
            carry = body(COUNT_UNROLL * i + u, carry)
        return carry

    groups = n // COUNT_UNROLL if isinstance(n, int) else lax.div(n, jnp.int32(COUNT_UNROLL))
    return lax.fori_loop(COUNT_UNROLL * groups, n, body, lax.fori_loop(0, groups, group, init))


def _zero_after(x):
    w = lax.bitcast_convert_type(x, jnp.uint32)
    w = lax.shift_right_logical(lax.shift_right_logical(w, jnp.uint32(16)), jnp.uint32(16))
    return lax.bitcast_convert_type(w, F32)[0:1, :]


def _flash_update(s_t, v_t, h, m_sc, acc_sc, after=None):
    m_prev = m_sc[h]
    m_new = jnp.maximum(m_prev, jnp.max(s_t, axis=0, keepdims=True))
    alpha = jnp.exp2(m_prev - m_new)
    if after is None:
        p_t = jnp.exp2((s_t - m_new).astype(BF16))
    else:
        cut = s_t.shape[0] // 2
        p_t = jnp.concatenate([jnp.exp2((s_t[:cut] - m_new).astype(BF16)),
                               jnp.exp2((s_t[cut:] - (m_new + after)).astype(BF16))], axis=0)
    acc_sc[h] = alpha * acc_sc[h] + jnp.dot(v_t, p_t, preferred_element_type=F32)
    m_sc[h] = m_new


def _both_halves(x_t):
    return jnp.concatenate([_half_select(x_t, True), _half_select(x_t, False)], axis=1)


def _attention_sweep(qt_ref, k_ref, vt_ref, subs, bias_of, m_sc, acc_sc):
    def scores(sub, j):
        start = sub * K_TILE
        if not isinstance(sub, int):
            start = pl.multiple_of(start, K_TILE)
        return jnp.dot(k_ref[j, pl.ds(start, K_TILE), :], _both_halves(qt_ref[j]), preferred_element_type=F32)

    items = [(sub, j) for sub in subs for j in range(N_PAIRS)]
    biases = {}
    ahead = 2
    pending = [scores(*item) for item in items[:ahead]]
    for i, (sub, j) in enumerate(items):
        s_t = pending.pop(0)
        tile_key = sub if isinstance(sub, int) else "traced"
        if tile_key not in biases:
            biases[tile_key] = bias_of(sub)
        bias = biases[tile_key]
        if bias is not None and bias.shape[1] == Q_TILE:
            s_t = jnp.concatenate([s_t[:, :Q_TILE] + bias, s_t[:, Q_TILE:] + bias], axis=1)
        elif bias is not None:
            s_t = s_t + bias
        after = None
        if i + ahead < len(items):
            pending.append(scores(*items[i + ahead]))
            after = _zero_after(pending[-1][0:8, :])
        _flash_update(s_t, vt_ref[j, sub], j, m_sc, acc_sc, after=after)


def _diff_attn_kernel(qrow, krow, qloc, kloc, nsub, full, first, last,
                      qt_ref, k_ref, vt_ref, lam_ref, subln_ref, o_ref, m_sc, acc_sc,
                      *, lambda_init):
    p = pl.program_id(0)

    @pl.when(first[p] == 1)
    def _():
        m_sc[...] = jnp.full(m_sc.shape, -jnp.inf, F32)
        acc_sc[...] = jnp.zeros(acc_sc.shape, F32)

    def key_tile(sub, carry):
        tile = kloc[p] * K_SUBS + sub
        on_diagonal = tile * K_TILE + (K_TILE - 1) > qloc[p] * Q_TILE

        @pl.when(on_diagonal)
        def _():
            kpos = tile * K_TILE + lax.broadcasted_iota(jnp.int32, (K_TILE, 2 * Q_TILE), 0)
            qcol = lax.broadcasted_iota(jnp.int32, (K_TILE, 2 * Q_TILE), 1)
            qpos = qloc[p] * Q_TILE + jnp.where(qcol >= Q_TILE, qcol - Q_TILE, qcol)
            bias = jnp.where(kpos <= qpos, 0.0, -jnp.inf).astype(F32)
            _attention_sweep(qt_ref, k_ref, vt_ref, (sub,), lambda _: bias, m_sc, acc_sc)

        @pl.when(jnp.logical_not(on_diagonal))
        def _():
            _attention_sweep(qt_ref, k_ref, vt_ref, (sub,), lambda _: None, m_sc, acc_sc)

        return carry

    @pl.when(full[p] == 1)
    def _():
        _attention_sweep(qt_ref, k_ref, vt_ref, tuple(range(K_SUBS)), lambda _: None, m_sc, acc_sc)

    @pl.when(full[p] == 0)
    def _():
        lax.fori_loop(0, nsub[p], key_tile, 0)

    @pl.when(last[p] == 1)
    def _():
        lam_rows = lam_ref[...]
        lam = (jnp.exp(jnp.sum(lam_rows[0:1] * lam_rows[1:2], axis=-1, keepdims=True))
               - jnp.exp(jnp.sum(lam_rows[2:3] * lam_rows[3:4], axis=-1, keepdims=True))
               + lambda_init)
        vdim = 2 * HEAD_DIM
        for j in range(N_PAIRS):
            a = acc_sc[j]
            a1 = a[:, :Q_TILE]
            a2 = a[:, Q_TILE:]
            o = a1[:vdim] / a1[vdim:vdim + 1] - lam * (a2[:vdim] / a2[vdim:vdim + 1])
            ms = jnp.mean(o * o, axis=0, keepdims=True)
            o = o * lax.rsqrt(ms + RMS_EPS) * subln_ref[...] * (1.0 - lambda_init)
            o_ref[j] = o.T.astype(BF16)


def _diff_attention(q_t, k, v_t, lam_rows, subln_col, steps, lambda_init):
    rows = k.shape[1]
    n_steps = steps["qrow"].shape[0]
    vrows = v_t.shape[2]

    def im(fn):
        return lambda p, qr, kr, ql, kl, ns, fu, f, l: fn(p, qr, kr)

    grid_spec = pltpu.PrefetchScalarGridSpec(
        num_scalar_prefetch=8,
        grid=(n_steps,),
        in_specs=[
            pl.BlockSpec((N_PAIRS, LANES, Q_TILE), im(lambda p, qr, kr: (0, 0, qr[p]))),
            pl.BlockSpec((N_PAIRS, K_SUBS * K_TILE, LANES), im(lambda p, qr, kr: (0, kr[p], 0))),
            pl.BlockSpec((N_PAIRS, K_SUBS, vrows, K_TILE), im(lambda p, qr, kr: (0, kr[p], 0, 0))),
            pl.BlockSpec((8, LANES), im(lambda p, qr, kr: (0, 0))),
            pl.BlockSpec((LANES, 1), im(lambda p, qr, kr: (0, 0))),
        ],
        out_specs=pl.BlockSpec((N_PAIRS, Q_TILE, LANES), im(lambda p, qr, kr: (0, qr[p], 0))),
        scratch_shapes=[pltpu.VMEM((N_PAIRS, 1, 2 * Q_TILE), F32),
                        pltpu.VMEM((N_PAIRS, vrows, 2 * Q_TILE), F32)],
    )
    return pl.pallas_call(
        functools.partial(_diff_attn_kernel, lambda_init=lambda_init),
        grid_spec=grid_spec,
        out_shape=jax.ShapeDtypeStruct((N_PAIRS, rows, LANES), BF16),
        compiler_params=pltpu.CompilerParams(dimension_semantics=("arbitrary",),
                                             vmem_limit_bytes=VMEM_LIMIT),
        name="diff_attn",
    )(steps["qrow"], steps["krow"], steps["qloc"], steps["kloc"], steps["nsub"], steps["full"],
      steps["first"], steps["last"], q_t, k, v_t, lam_rows, subln_col)


def _dsa_kernel(qrow, krow, qloc, kloc, nkb, nsub, full, first, last, bidx,
                qt_ref, k_ref, vt_ref, qit_ref, wit_ref, kidx_ref, o_ref,
                key_sc, coarse_sc, thr_sc, m_sc, acc_sc, *, topk):
    p = pl.program_id(0)

    @pl.when(first[p] == 1)
    def _():
        m_sc[...] = jnp.full(m_sc.shape, -jnp.inf, F32)
        acc_sc[...] = jnp.zeros(acc_sc.shape, F32)
        n = nkb[p]
        wi = wit_ref[...]
        qpos = qloc[p] * Q_TILE + lax.broadcasted_iota(jnp.int32, (K_TILE, Q_TILE), 1)
        krow_iota = lax.broadcasted_iota(jnp.int32, (K_TILE, Q_TILE), 0)

        def score_chunk(c, carry):
            kt = kidx_ref[pl.ds(pl.multiple_of(c * K_TILE, K_TILE), K_TILE), :]
            sc = jnp.zeros((K_TILE, Q_TILE), F32)
            for jp in range(IDX_PAIRS):
                qi_t = qit_ref[jp]
                for cc in range(2):
                    raw = jnp.dot(kt, _half_select(qi_t, cc == 0), preferred_element_type=F32)
                    hh = 2 * jp + cc
                    sc = sc + jnp.maximum(raw, 0.0) * wi[hh:hh + 1, :]
            kpos = c * K_TILE + krow_iota
            sc = jnp.where(kpos < N_META, BIG_SCORE, sc)
            sc = jnp.where(kpos <= qpos, sc, -jnp.inf)
            sc = jnp.where(jnp.abs(sc) < F32_MIN_NORMAL, 0.0, sc)
            bits = lax.bitcast_convert_type(sc, jnp.int32)
            key_sc[c] = bits ^ ((bits >> 31) & 0x7FFFFFFF)
            coarse_sc[c] = lax.bitcast_convert_type(bits & jnp.int32(-65536), F32).astype(BF16)
            return carry

        def score_pair(i, carry):
            return score_chunk(2 * i + 1, score_chunk(2 * i, carry))

        pairs = lax.shift_right_logical(n, 1)
        lax.fori_loop(2 * pairs, n, score_chunk, lax.fori_loop(0, pairs, score_pair, 0))

        kf = float(topk)
        q1 = qloc[p] * Q_TILE + lax.broadcasted_iota(jnp.int32, (1, Q_TILE), 1)
        settled0 = (q1 < topk).astype(jnp.int32)
        coarse_bits = 16

        def coarse_body(b, state):
            lo, settled = state
            cand = lo + jnp.left_shift(jnp.int32(1), 31 - b)
            top = cand >> 16
            top = jnp.where(jnp.logical_and(top > 0, top < BF16_MIN_NORMAL_BITS), BF16_MIN_NORMAL_BITS, top)
            cbits = jnp.left_shift(top ^ ((top >> 31) & 0x7FFF), 16)
            cand16 = lax.bitcast_convert_type(cbits, F32).astype(BF16)

            def count_chunk(c, cnt16):
                hit = jnp.where(coarse_sc[c] >= cand16, jnp.ones((), BF16), jnp.zeros((), BF16))
                parts = [hit[i * BF16_ROWS:(i + 1) * BF16_ROWS] for i in range(K_TILE // BF16_ROWS)]
                while len(parts) > 1:
                    parts = [parts[i] + parts[i + 1] for i in range(0, len(parts), 2)]
                return cnt16 + parts[0].astype(F32)

            cnt16 = _fori_unrolled(n, count_chunk, jnp.zeros((BF16_ROWS, Q_TILE), F32))
            cnt = jnp.sum(cnt16, axis=0, keepdims=True)
            return jnp.where(cnt >= kf, cand, lo), jnp.where(cnt == kf, 1, settled)

        lo, settled = lax.fori_loop(0, coarse_bits, coarse_body,
                                    (jnp.full((1, Q_TILE), INT_MIN, jnp.int32), settled0))

        def unsettled(state):
            b, _, settled = state
            return jnp.logical_and(b < 32, jnp.min(settled) == 0)

        def count_where(hit_of):
            def count_chunk(c, cnt8):
                hit = jnp.where(hit_of(c, key_sc[c]), 1.0, 0.0).reshape(8, K_TILE // 64, 8, Q_TILE)
                return cnt8 + jnp.sum(jnp.sum(hit, axis=1), axis=0)

            cnt8 = _fori_unrolled(n, count_chunk, jnp.zeros((8, Q_TILE), F32))
            return jnp.sum(cnt8, axis=0, keepdims=True)

        def fine_body(state):
            b, lo, settled = state
            cand = lo + jnp.left_shift(jnp.int32(1), 31 - b)
            cnt = count_where(lambda c, keys: keys >= cand)
            return b + 1, jnp.where(cnt >= kf, cand, lo), jnp.where(cnt == kf, 1, settled)

        _, thr, settled = lax.while_loop(unsettled, fine_body, (jnp.int32(coarse_bits), lo, settled))
        thr = jnp.maximum(thr, KEY_NEG_INF + 1)
        thr_sc[...] = thr

        @pl.when(jnp.min(settled) == 0)
        def _():
            surplus = count_where(lambda c, keys: keys >= thr) - kf
            kk = lax.broadcasted_iota(jnp.int32, (K_TILE, K_TILE), 0)
            kc = lax.broadcasted_iota(jnp.int32, (K_TILE, K_TILE), 1)
            later_or_same = jnp.where(kc >= kk, 1.0, 0.0).astype(BF16)

            def demote(i, seen):
                c = n - 1 - i
                keys = key_sc[c]
                tie = keys == thr
                from_end = seen + jnp.dot(later_or_same, jnp.where(tie, 1.0, 0.0).astype(BF16),
                                          preferred_element_type=F32)
                drop = jnp.logical_and(tie, from_end <= surplus)
                key_sc[c] = jnp.where(drop, keys - 1, keys)
                return from_end[0:1, :]

            lax.fori_loop(0, n, demote, jnp.zeros((1, Q_TILE), F32))

    def mask_bias(sub):
        sel = key_sc[kloc[p] * K_SUBS + sub] >= thr_sc[...]
        return jnp.where(sel, 0.0, -jnp.inf).astype(F32)

    def key_tile(sub, carry):
        _attention_sweep(qt_ref, k_ref, vt_ref, (sub,), mask_bias, m_sc, acc_sc)
        return carry

    @pl.when(full[p] == 1)
    def _():
        _attention_sweep(qt_ref, k_ref, vt_ref, tuple(range(K_SUBS)), mask_bias, m_sc, acc_sc)

    @pl.when(full[p] == 0)
    def _():
        lax.fori_loop(0, nsub[p], key_tile, 0)

    @pl.when(last[p] == 1)
    def _():
        ones_row = 2 * HEAD_DIM
        for j in range(N_PAIRS):
            a = acc_sc[j]
            o = jnp.concatenate([a[:HEAD_DIM, :Q_TILE] / a[ones_row:ones_row + 1, :Q_TILE],
                                 a[HEAD_DIM:ones_row, Q_TILE:] / a[ones_row:ones_row + 1, Q_TILE:]],
                                axis=0)
            o_ref[j] = o.T.astype(BF16)


def _dsa_attention(q_t, k, v_t, qi_t, wi_t, kidx, steps, topk, t_pad):
    rows = k.shape[1]
    n_steps = steps["qrow"].shape[0]
    nkb_total = t_pad // K_TILE
    vrows = v_t.shape[2]

    def im(fn):
        return lambda p, qr, kr, ql, kl, nk, ns, fu, f, l, b: fn(p, qr, kr, b)

    grid_spec = pltpu.PrefetchScalarGridSpec(
        num_scalar_prefetch=10,
        grid=(n_steps,),
        in_specs=[
            pl.BlockSpec((N_PAIRS, LANES, Q_TILE), im(lambda p, qr, kr, b: (0, 0, qr[p]))),
            pl.BlockSpec((N_PAIRS, K_SUBS * K_TILE, LANES), im(lambda p, qr, kr, b: (0, kr[p], 0))),
            pl.BlockSpec((N_PAIRS, K_SUBS, vrows, K_TILE), im(lambda p, qr, kr, b: (0, kr[p], 0, 0))),
            pl.BlockSpec((IDX_PAIRS, LANES, Q_TILE), im(lambda p, qr, kr, b: (0, 0, qr[p]))),
            pl.BlockSpec((IDX_HEADS, Q_TILE), im(lambda p, qr, kr, b: (0, qr[p]))),
            pl.BlockSpec((t_pad, LANES), im(lambda p, qr, kr, b: (b[p], 0))),
        ],
        out_specs=pl.BlockSpec((N_PAIRS, Q_TILE, LANES), im(lambda p, qr, kr, b: (0, qr[p], 0))),
        scratch_shapes=[pltpu.VMEM((nkb_total, K_TILE, Q_TILE), jnp.int32),
                        pltpu.VMEM((nkb_total, K_TILE, Q_TILE), BF16),
                        pltpu.VMEM((1, Q_TILE), jnp.int32),
                        pltpu.VMEM((N_PAIRS, 1, 2 * Q_TILE), F32),
                        pltpu.VMEM((N_PAIRS, vrows, 2 * Q_TILE), F32)],
    )
    return pl.pallas_call(
        functools.partial(_dsa_kernel, topk=topk),
        grid_spec=grid_spec,
        out_shape=jax.ShapeDtypeStruct((N_PAIRS, rows, LANES), BF16),
        compiler_params=pltpu.CompilerParams(dimension_semantics=("arbitrary",),
                                             vmem_limit_bytes=VMEM_LIMIT),
        name="dsa_attn",
    )(steps["qrow"], steps["krow"], steps["qloc"], steps["kloc"], steps["nkb"], steps["nsub"],
      steps["full"], steps["first"], steps["last"], steps["b"], q_t, k, v_t, qi_t, wi_t, kidx)


def _rope_tables(t_pad):
    inv = ROPE_THETA ** (-jnp.arange(0, HEAD_DIM, 2, dtype=F32) / HEAD_DIM)
    ang = jnp.arange(t_pad, dtype=F32)[:, None] * inv[None, :]
    cos, sin = jnp.cos(ang), jnp.sin(ang)
    cos128 = jnp.tile(cos, (1, LANES // (HEAD_DIM // 2)))
    sin128 = jnp.tile(jnp.concatenate([-sin, sin], axis=1), (1, LANES // HEAD_DIM))
    return cos128, sin128


def _interleave_gate_val(a):
    lead = a.shape[:-1]
    a = a.reshape(lead + (2, FFN_HIDDEN // FFN_TILE, FFN_TILE))
    a = jnp.swapaxes(a, -3, -2)
    return a.reshape(lead + (2 * FFN_HIDDEN,))


def kernel(x, meta_tokens, da_norm, da_w_qkv, da_lambda_q1, da_lambda_k1, da_lambda_q2, da_lambda_k2, da_subln, da_w_o, dsa_norm, dsa_w_in, dsa_idx_k_norm, dsa_w_o, ffn_norm, ffn_w_up, ffn_conv_w, ffn_conv_b, ffn_w_down, final_norm):
    batch, seq, d = x.shape
    assert d == D_MODEL
    depth = ffn_norm.shape[0]
    t_real = seq + N_META
    t_pad = _padded_len(t_real)
    assert t_pad % (K_SUBS * K_TILE) == 0 and t_pad % Q_TILE == 0
    rows = batch * t_pad

    meta = jnp.broadcast_to(meta_tokens[None].astype(x.dtype), (batch, N_META, d))
    h = jnp.concatenate([meta, x, jnp.zeros((batch, t_pad - t_real, d), x.dtype)], axis=1)
    h = h.reshape(rows, d)
    cos128, sin128 = _rope_tables(t_pad)
    steps = _attention_steps(batch, t_pad)

    dsa_qkv = N_HEADS * HEAD_DIM
    idx_cols = IDX_HEADS * HEAD_DIM
    for i in range(depth):
        j = i // 2
        if i % 2 == 0:
            lambda_init = 0.8 - 0.6 * math.exp(-0.3 * i)
            q_t, k, v_t = _project(h, da_norm[j], da_w_qkv[j].astype(BF16), cos128, sin128, t_pad)
            lam_rows = jnp.zeros((8, LANES), F32).at[0:4, 0:HEAD_DIM].set(
                jnp.stack([da_lambda_q1[j], da_lambda_k1[j], da_lambda_q2[j], da_lambda_k2[j]]).astype(F32))
            o = _diff_attention(q_t, k, v_t, lam_rows, da_subln[j].reshape(LANES, 1).astype(F32),
                                steps, lambda_init)
            h = _out_project(h, o, da_w_o[j].astype(BF16))
        else:
            w_in = dsa_w_in[j]
            n_main = 3 * dsa_qkv + idx_cols
            w_tail = jnp.zeros((d, LANES), w_in.dtype).at[:, :w_in.shape[1] - n_main].set(w_in[:, n_main:])
            gk = jnp.ones((1, LANES), F32).at[0, :HEAD_DIM].set(dsa_idx_k_norm[j].astype(F32))
            q_t, k, v_t, qi_t, kidx, wi_t = _project(
                h, dsa_norm[j], w_in[:, :n_main].astype(BF16), cos128, sin128, t_pad,
                w_tail=w_tail.astype(BF16), gk=gk)
            topk = min(TOPK_MAX, seq // 4)
            o = _dsa_attention(q_t, k, v_t, qi_t, wi_t, kidx, steps, topk, t_pad)
            h = _out_project(h, o, dsa_w_o[j].astype(BF16))
        h = _ffn(h, ffn_norm[i], _interleave_gate_val(ffn_w_up[i]).astype(BF16),
                 _interleave_gate_val(ffn_conv_w[i]).astype(F32),
                 _interleave_gate_val(ffn_conv_b[i]).reshape(1, -1).astype(F32),
                 ffn_w_down[i].astype(BF16), final_norm, t_pad, final=(i == depth - 1))
    return h.reshape(batch, t_pad, d)[:, N_META:N_META + seq]
```
